```python
import math
import jax, jax.numpy as jnp
from jax import lax
import numpy as np

D_MODEL = 1024
BATCH = 16
SEQ = 256
DEPTH = 2
DEC_BATCH = 8
DEC_SEQ = 4096
PAST_LEN = 256

GRID_W = 64
BLOCK = 128
WINDOW = 128
H_A = 6
KV_A = 2
HD_A = 64
GQ_A = H_A // KV_A
W_A = H_A * HD_A
H_B = 4
HD_B = 64
DC_B = HD_B // 2
W_B = H_B * HD_B
SSM_CH = 16
W_C = D_MODEL - W_A - W_B
G_C = W_C // SSM_CH
P_C = 64
DT_MIN = 0.001
DT_MAX = 0.1
MIX = W_A + W_B + W_C
_KVW = KV_A * HD_A
IN_COLS = W_A + 2 * _KVW + 3 * W_B + W_C
IN_SPLITS = (W_A, W_A + _KVW, W_A + 2 * _KVW, W_A + 2 * _KVW + W_B,
             W_A + 2 * _KVW + 2 * W_B, W_A + 2 * _KVW + 3 * W_B)
N_EXPERTS = 64
TOP_K = 6
F_EXP = 128
F_SHARED = 256
N_EXP_GROUPS = 8
TOPK_GROUPS = 4
ROUTED_SCALE = 2.5
MOE_BLOCK = 128
ROPE_BASE = 10000.0
EPS = 1e-6
NEG = -1e30
F32 = jnp.float32

kernel_name = 'hybrid_diffusion_prefix_trunk_step'


def rmsnorm(x, g):
    xf = x.astype(F32)
    y = xf * lax.rsqrt(jnp.mean(xf * xf, axis=-1, keepdims=True) + EPS)
    return (y * g.astype(F32)).astype(x.dtype)


def _rope_1d(x, pos):
    h = x.shape[-1]
    freqs = ROPE_BASE ** (-jnp.arange(0, h, 2, dtype=F32) / h)
    ang = pos[:, None] * freqs[None, :]
    cos, sin = jnp.cos(ang), jnp.sin(ang)
    x1, x2 = x[..., : h // 2], x[..., h // 2:]
    return jnp.concatenate([x1 * cos - x2 * sin, x2 * cos + x1 * sin], axis=-1)


def axial_rope(x):
    L, d = x.shape[-2], x.shape[-1]
    rows = L // GRID_W
    row = jnp.repeat(jnp.arange(rows, dtype=F32), GRID_W)
    col = jnp.tile(jnp.arange(GRID_W, dtype=F32), rows)
    xf = x.astype(F32)
    out = jnp.concatenate([_rope_1d(xf[..., : d // 2], row), _rope_1d(xf[..., d // 2:], col)], axis=-1)
    return out.astype(x.dtype)


def sink_softmax(parts, sink):
    m = sink
    for s in parts:
        m = jnp.maximum(m, jnp.max(s, axis=-1, keepdims=True))
    es = [jnp.exp(s - m) for s in parts]
    den = jnp.exp(sink - m)
    for e in es:
        den = den + jnp.sum(e, axis=-1, keepdims=True)
    return [e / den for e in es]


def map_query_blocks(fn, q, axis, out_axis):
    L = q.shape[axis]
    qb = q.reshape(q.shape[:axis] + (L // BLOCK, BLOCK) + q.shape[axis + 1:])
    out = jnp.moveaxis(lax.map(fn, jnp.moveaxis(qb, axis, 0)), 0, out_axis)
    return out.reshape(out.shape[:out_axis] + (L,) + out.shape[out_axis + 2:])


def gqa_sink_context(q, k, v, sink):
    scale = q.shape[-1] ** -0.5
    sk = sink.astype(F32)[None, :, :, None, None]

    def block(qb):
        s = jnp.einsum('bkgqd,bkcd->bkgqc', qb, k).astype(F32) * scale
        (p,) = sink_softmax([s], sk)
        return jnp.einsum('bkgqc,bkcd->bkgqd', p.astype(v.dtype), v)

    return map_query_blocks(block, q, 3, 3)


def gqa_sink_latent(q, k, v, k_ctx, v_ctx, sink):
    B_, KV, GQ, L, hd = q.shape
    nb = L // BLOCK
    scale = hd ** -0.5
    qb = q.reshape(B_, KV, GQ, nb, BLOCK, hd)

    def band(t):
        tp = jnp.pad(t, ((0, 0), (0, 0), (BLOCK, BLOCK), (0, 0))).reshape(B_, KV, nb + 2, BLOCK, hd)
        return jnp.concatenate([tp[:, :, :-2], tp[:, :, 1:-1], tp[:, :, 2:]], axis=3)

    kw, vw = band(k), band(v)
    qi = jnp.arange(BLOCK)[:, None]
    kj = jnp.arange(3 * BLOCK)[None, :]
    kpos = (jnp.arange(nb)[:, None, None] - 1) * BLOCK + kj[None]
    mask = (jnp.abs(kj - BLOCK - qi) <= WINDOW)[None] & (kpos >= 0) & (kpos < L)
    s_w = jnp.einsum('bkgnqd,bknjd->bkgnqj', qb, kw).astype(F32) * scale
    s_w = jnp.where(mask, s_w, NEG)
    s_c = jnp.einsum('bkgnqd,bkcd->bkgnqc', qb, k_ctx).astype(F32) * scale
    sk = sink.astype(F32)[None, :, :, None, None, None]
    p_w, p_c = sink_softmax([s_w, s_c], sk)
    o = (jnp.einsum('bkgnqj,bknjd->bkgnqd', p_w.astype(v.dtype), vw)
         + jnp.einsum('bkgnqc,bkcd->bkgnqd', p_c.astype(v.dtype), v_ctx))
    return o.reshape(B_, KV, GQ, L, hd)


def diff_attention(q, k, v, lam, lam_init, subln_g):
    scale = q.shape[-1] ** -0.5

    def block(qb):
        s = jnp.einsum('bhcqd,bhckd->bhcqk', qb, k).astype(F32) * scale
        p = jax.nn.softmax(s, axis=-1)
        w = p[:, :, 0] - lam * p[:, :, 1]
        return jnp.einsum('bhqk,bhkd->bhqd', w.astype(v.dtype), v)

    o = map_query_blocks(block, q, 3, 2)
    return rmsnorm(o, subln_g) * (1.0 - lam_init)


def _linear_recurrence(e1, e2):
    a1, b1 = e1
    a2, b2 = e2
    return a1 * a2, a2 * b1 + b2


def s5_bidirectional(u, lp, h0):
    B_, L, _ = u.shape
    uf = u.astype(F32).reshape(B_, L, G_C, SSM_CH)
    uc = uf.astype(jnp.complex64)
    y = uf * lp['ssm_d'].astype(F32)
    finals = []
    for d, rev in enumerate((False, True)):
        lam = lax.complex(lp['ssm_lam_re'][d].astype(F32), lp['ssm_lam_im'][d].astype(F32))
        dt = jnp.exp(lp['ssm_log_dt'][d].astype(F32))[:, None]
        a_bar = jnp.exp(lam * dt)
        b_mat = lax.complex(lp['ssm_b_re'][d].astype(F32), lp['ssm_b_im'][d].astype(F32))
        c_mat = lax.complex(lp['ssm_c_re'][d].astype(F32), lp['ssm_c_im'][d].astype(F32))
        b_bar = ((a_bar - 1.0) / lam)[..., None] * b_mat
        bu = jnp.einsum('blgh,gph->blgp', uc, b_bar)
        a_el = jnp.broadcast_to(a_bar, (1, L, G_C, P_C))
        a_cum, hs = lax.associative_scan(_linear_recurrence, (a_el, bu), reverse=rev, axis=1)
        if h0 is not None:
            hs = hs + a_cum * h0[:, d, None]
        y = y + jnp.real(jnp.einsum('blgp,ghp->blgh', hs, c_mat))
        finals.append(hs[:, 0] if rev else hs[:, L - 1])
    y = y.reshape(B_, L, W_C)
    g = jax.nn.gelu(y)
    out = g * jax.nn.sigmoid(g @ lp['w_glu'].astype(F32))
    return out.astype(u.dtype), jnp.stack(finals, axis=1)


def mixing_sublayer(h, lp, l, ctx):
    B_, L, _ = h.shape
    qa, ka, va, qb, kb, vb, uc = jnp.split(h @ lp['w_in'], IN_SPLITS, axis=-1)
    latent = ctx is not None
    q_a = rmsnorm(qa.reshape(B_, L, H_A, HD_A), lp['q_norm_a']).transpose(0, 2, 1, 3)
    k_a = rmsnorm(ka.reshape(B_, L, KV_A, HD_A), lp['k_norm_a']).transpose(0, 2, 1, 3)
    v_a = va.reshape(B_, L, KV_A, HD_A).transpose(0, 2, 1, 3)
    sink = lp['sink_a'].reshape(KV_A, GQ_A)
    q_b = rmsnorm(qb.reshape(B_, L, H_B, 2, DC_B), lp['q_norm_b']).transpose(0, 2, 3, 1, 4)
    k_b = rmsnorm(kb.reshape(B_, L, H_B, 2, DC_B), lp['k_norm_b']).transpose(0, 2, 3, 1, 4)
    v_b = vb.reshape(B_, L, H_B, HD_B).transpose(0, 2, 1, 3)
    lam_init = 0.8 - 0.6 * math.exp(-0.3 * l)
    lv = lp['lam_b'].astype(F32)
    lam = jnp.exp(jnp.sum(lv[0] * lv[1])) - jnp.exp(jnp.sum(lv[2] * lv[3])) + lam_init
    if latent:
        ctx_ak, ctx_av, ctx_bk, ctx_bv, st_re, st_im = ctx
        q_a, k_a = axial_rope(q_a), axial_rope(k_a)
        q_b, k_b = axial_rope(q_b), axial_rope(k_b)
        o_a = gqa_sink_latent(q_a.reshape(B_, KV_A, GQ_A, L, HD_A), k_a, v_a, ctx_ak, ctx_av, sink)
        o_b = diff_attention(q_b, jnp.concatenate([k_b, ctx_bk], axis=3),
                             jnp.concatenate([v_b, ctx_bv], axis=2), lam, lam_init, lp['subln_b'])
        h0 = lax.complex(st_re.astype(F32), st_im.astype(F32))
    else:
        o_a = gqa_sink_context(q_a.reshape(B_, KV_A, GQ_A, L, HD_A), k_a, v_a, sink)
        o_b = diff_attention(q_b, k_b, v_b, lam, lam_init, lp['subln_b'])
        h0 = None
    o_c, finals = s5_bidirectional(uc, lp, h0)
    o_a = o_a.reshape(B_, H_A, L, HD_A).transpose(0, 2, 1, 3).reshape(B_, L, W_A)
    o_b = o_b.transpose(0, 2, 1, 3).reshape(B_, L, W_B)
    out = jnp.concatenate([o_a, o_b, o_c], axis=-1) @ lp['w_out']
    ctx_out = None if latent else (k_a, v_a, k_b, v_b, jnp.real(finals), jnp.imag(finals))
    return out, ctx_out


def moe(h, lp):
    B_, L, D = h.shape
    t = h.reshape(B_ * L, D)
    T = t.shape[0]
    scores = jax.nn.sigmoid((t @ lp['w_router']).astype(F32))
    biased = scores + lp['b_router'].astype(F32)
    per_group = N_EXPERTS // N_EXP_GROUPS
    gscore = jnp.sum(lax.top_k(biased.reshape(T, N_EXP_GROUPS, per_group), 2)[0], axis=-1)
    gidx = lax.top_k(gscore, TOPK_GROUPS)[1]
    gmask = jnp.any(gidx[..., None] == jnp.arange(N_EXP_GROUPS), axis=-2)
    emask = jnp.repeat(gmask, per_group, axis=-1)
    eidx = lax.top_k(jnp.where(emask, biased, -jnp.inf), TOP_K)[1]
    w = jnp.take_along_axis(scores, eidx, axis=-1)
    w = w / jnp.sum(w, axis=-1, keepdims=True) * ROUTED_SCALE
    gates = jnp.sum(jax.nn.one_hot(eidx, N_EXPERTS, dtype=F32) * w[..., None], axis=-2).astype(t.dtype)
    w_e1, w_e3, w_e2 = lp['w_e1'], lp['w_e3'], lp['w_e2']

    def expert_block(args):
        xb, gb = args
        a = jnp.einsum('td,edf->tef', xb, w_e1)
        b = jnp.einsum('td,edf->tef', xb, w_e3)
        return jnp.einsum('tef,efd->td', jax.nn.silu(a) * b * gb[..., None], w_e2)

    routed = lax.map(expert_block, (t.reshape(-1, MOE_BLOCK, D),
                                    gates.reshape(-1, MOE_BLOCK, N_EXPERTS))).reshape(T, D)
    shared = (jax.nn.silu(t @ lp['w_s1']) * (t @ lp['w_s3'])) @ lp['w_s2']
    return (routed + shared).reshape(B_, L, D)


def trunk_layer(x, mod, lp, l, ctx):
    sh1, sc1, g1, sh2, sc2, g2 = jnp.split(mod, 6, axis=-1)
    h = rmsnorm(x, lp['norm1_g']) * (1 + sc1) + sh1
    mix, ctx_out = mixing_sublayer(h, lp, l, ctx)
    x = x + g1 * mix
    h = rmsnorm(x, lp['norm2_g']) * (1 + sc2) + sh2
    x = x + g2 * moe(h, lp)
    return x, ctx_out


def setup_inputs(seed: int = 0) -> dict:
    key = jax.random.key(seed)
    ks = iter(jax.random.split(key, 48))

    def nrm(shape, scale=1.0):
        return scale * jax.random.normal(next(ks), shape, F32)

    def gain(shape):
        return 1.0 + nrm(shape, 0.02)

    D = D_MODEL
    n = jnp.arange(P_C, dtype=F32)
    return {
        'x_prompt': nrm((BATCH, SEQ, D)),
        'x_sample': nrm((DEC_BATCH, DEC_SEQ, D)),
        'cache_a_k': nrm((DEC_BATCH, DEPTH, KV_A, PAST_LEN, HD_A)),
        'cache_a_v': nrm((DEC_BATCH, DEPTH, KV_A, PAST_LEN, HD_A)),
        'cache_b_k': nrm((DEC_BATCH, DEPTH, H_B, 2, PAST_LEN, DC_B)),
        'cache_b_v': nrm((DEC_BATCH, DEPTH, H_B, PAST_LEN, HD_B)),
        'state_ssm_re': nrm((DEC_BATCH, DEPTH, 2, G_C, P_C), 0.5),
        'state_ssm_im': nrm((DEC_BATCH, DEPTH, 2, G_C, P_C), 0.5),
        'c': nrm((DEC_BATCH, D)),
        'c_ctx': nrm((D,)),
        'norm1_g': gain((DEPTH, D)),
        'norm2_g': gain((DEPTH, D)),
        'w_ada': nrm((DEPTH, D, 6 * D), 0.5 * D ** -0.5),
        'b_ada': nrm((DEPTH, 6 * D), 0.02),
        'w_in': nrm((DEPTH, D, IN_COLS), D ** -0.5),
        'q_norm_a': gain((DEPTH, HD_A)),
        'k_norm_a': gain((DEPTH, HD_A)),
        'sink_a': nrm((DEPTH, H_A), 0.5),
        'q_norm_b': gain((DEPTH, DC_B)),
        'k_norm_b': gain((DEPTH, DC_B)),
        'lam_b': nrm((DEPTH, 4, DC_B), 0.1),
        'subln_b': gain((DEPTH, HD_B)),
        'ssm_lam_re': -0.5 + nrm((DEPTH, 2, G_C, P_C), 0.01),
        'ssm_lam_im': math.pi * n + nrm((DEPTH, 2, G_C, P_C), 0.01),
        'ssm_log_dt': jax.random.uniform(next(ks), (DEPTH, 2, G_C), F32, math.log(DT_MIN), math.log(DT_MAX)),
        'ssm_b_re': nrm((DEPTH, 2, G_C, P_C, SSM_CH), (2 * SSM_CH) ** -0.5),
        'ssm_b_im': nrm((DEPTH, 2, G_C, P_C, SSM_CH), (2 * SSM_CH) ** -0.5),
        'ssm_c_re': nrm((DEPTH, 2, G_C, SSM_CH, P_C), (2 * P_C) ** -0.5),
        'ssm_c_im': nrm((DEPTH, 2, G_C, SSM_CH, P_C), (2 * P_C) ** -0.5),
        'ssm_d': nrm((DEPTH, G_C, SSM_CH)),
        'w_glu': nrm((DEPTH, W_C, W_C), W_C ** -0.5),
        'w_out': nrm((DEPTH, MIX, D), MIX ** -0.5),
        'w_router': nrm((DEPTH, D, N_EXPERTS), D ** -0.5),
        'b_router': nrm((DEPTH, N_EXPERTS), 0.01),
        'w_e1': nrm((DEPTH, N_EXPERTS, D, F_EXP), D ** -0.5),
        'w_e3': nrm((DEPTH, N_EXPERTS, D, F_EXP), D ** -0.5),
        'w_e2': nrm((DEPTH, N_EXPERTS, F_EXP, D), F_EXP ** -0.5),
        'w_s1': nrm((DEPTH, D, F_SHARED), D ** -0.5),
        'w_s3': nrm((DEPTH, D, F_SHARED), D ** -0.5),
        'w_s2': nrm((DEPTH, F_SHARED, D), F_SHARED ** -0.5),
    }


def reference(x_prompt, x_sample, cache_a_k, cache_a_v, cache_b_k, cache_b_v, state_ssm_re, state_ssm_im,
              c, c_ctx, norm1_g, norm2_g, w_ada, b_ada, w_in, q_norm_a, k_norm_a, sink_a,
              q_norm_b, k_norm_b, lam_b, subln_b, ssm_lam_re, ssm_lam_im, ssm_log_dt,
              ssm_b_re, ssm_b_im, ssm_c_re, ssm_c_im, ssm_d, w_glu, w_out,
              w_router, b_router, w_e1, w_e3, w_e2, w_s1, w_s3, w_s2):
    xp, xs = x_prompt, x_sample
    ak, av, bk, bv, sre, sim = [], [], [], [], [], []
    for l in range(DEPTH):
        lp = dict(norm1_g=norm1_g[l], norm2_g=norm2_g[l], w_in=w_in[l],
                  q_norm_a=q_norm_a[l], k_norm_a=k_norm_a[l], sink_a=sink_a[l],
                  q_norm_b=q_norm_b[l], k_norm_b=k_norm_b[l], lam_b=lam_b[l], subln_b=subln_b[l],
                  ssm_lam_re=ssm_lam_re[l], ssm_lam_im=ssm_lam_im[l], ssm_log_dt=ssm_log_dt[l],
                  ssm_b_re=ssm_b_re[l], ssm_b_im=ssm_b_im[l], ssm_c_re=ssm_c_re[l], ssm_c_im=ssm_c_im[l],
                  ssm_d=ssm_d[l], w_glu=w_glu[l], w_out=w_out[l],
                  w_router=w_router[l], b_router=b_router[l], w_e1=w_e1[l], w_e3=w_e3[l], w_e2=w_e2[l],
                  w_s1=w_s1[l], w_s3=w_s3[l], w_s2=w_s2[l])
        mod_ctx = (jax.nn.silu(c_ctx) @ w_ada[l] + b_ada[l])[None, None, :]
        mod_lat = (jax.nn.silu(c) @ w_ada[l] + b_ada[l])[:, None, :]
        xp, (k_a, v_a, k_b, v_b, s_re, s_im) = trunk_layer(xp, mod_ctx, lp, l, None)
        ak.append(k_a)
        av.append(v_a)
        bk.append(k_b)
        bv.append(v_b)
        sre.append(s_re)
        sim.append(s_im)
        cached = (cache_a_k[:, l], cache_a_v[:, l], cache_b_k[:, l], cache_b_v[:, l],
                  state_ssm_re[:, l], state_ssm_im[:, l])
        xs, _ = trunk_layer(xs, mod_lat, lp, l, cached)
    return (xp, xs, jnp.stack(ak, axis=1), jnp.stack(av, axis=1), jnp.stack(bk, axis=1),
            jnp.stack(bv, axis=1), jnp.stack(sre, axis=1), jnp.stack(sim, axis=1))
```

```python
import functools
import math

import jax
import jax.numpy as jnp
from jax import lax
from jax.experimental import pallas as pl
from jax.experimental.pallas import tpu as pltpu

F32 = jnp.float32
BF16 = jnp.bfloat16

D_MODEL = 1024
GRID_W = 64
BLOCK = 128
H_A, KV_A, HD_A = 6, 2, 64
GQ_A = H_A // KV_A
W_A = H_A * HD_A
H_B, HD_B = 4, 64
DC_B = HD_B // 2
W_B = H_B * HD_B
SSM_CH = 16
W_C = D_MODEL - W_A - W_B
G_C = W_C // SSM_CH
P_C = 64
N_EXPERTS, TOP_K, F_EXP, F_SHARED = 64, 6, 128, 256
N_EXP_GROUPS, TOPK_GROUPS = 8, 4
PER_GROUP = N_EXPERTS // N_EXP_GROUPS
ROUTED_SCALE = 2.5
ROPE_BASE = 10000.0
EPS = 1e-6
NEG = -1e30

LANES = 128
SSM_T = 16
N_PAIR = G_C // 2
QA_COLS = H_A * LANES
IN_COLS_P = QA_COLS + 2 * KV_A * HD_A + 3 * W_B + W_C
VMEM_LIMIT = 56 << 20


def _cparams(*sem):
    return pltpu.CompilerParams(dimension_semantics=sem, vmem_limit_bytes=VMEM_LIMIT)


def _dot(a, b):
    return jnp.dot(a, b, preferred_element_type=F32)


def _dot_nt(a, b):
    return lax.dot_general(a, b, (((1,), (1,)), ((), ())), preferred_element_type=F32)


def _split_bf16(x):
    hi = x.astype(BF16)
    lo = (x - hi.astype(F32)).astype(BF16)
    return hi, lo


def _mod_body(c_ref, w_ref, b_ref, o_ref):
    c = c_ref[...]
    s = c * jax.nn.sigmoid(c)
    s_hi, s_lo = _split_bf16(s)
    w_hi, w_lo = _split_bf16(w_ref[0])
    o_ref[0] = _dot(s_hi, w_hi) + _dot(s_lo, w_hi) + _dot(s_hi, w_lo) + b_ref[0]


def _modulation(cvec, w_ada, b_ada):
    depth, d, n = w_ada.shape
    rows = cvec.shape[0]
    tn = 768
    return pl.pallas_call(
        _mod_body,
        grid=(depth, n // tn),
        in_specs=[pl.BlockSpec((rows, d), lambda l, j: (0, 0)),
                  pl.BlockSpec((1, d, tn), lambda l, j: (l, 0, j)),
                  pl.BlockSpec((1, 1, tn), lambda l, j: (l, 0, j))],
        out_specs=pl.BlockSpec((1, rows, tn), lambda l, j: (l, 0, j)),
        out_shape=jax.ShapeDtypeStruct((depth, rows, n), F32),
        compiler_params=_cparams("parallel", "parallel"),
        name="adaln_mod",
    )(cvec, w_ada, b_ada.reshape(depth, 1, n))


def _inproj_body(*refs, latent):
    if latent:
        (x_ref, mod_ref, g1_ref, w_ref, gains_ref, s64_ref, s32_ref, ca_ref, sa_ref, cb_ref, sb_ref,
         qa_ref, ka_ref, va_ref, qb_ref, kb_ref, vb_ref, u_ref) = refs
    else:
        (x_ref, mod_ref, g1_ref, w_ref, gains_ref, s64_ref, s32_ref,
         qa_ref, ka_ref, va_ref, qb_ref, kb_ref, vb_ref, u_ref) = refs
    d = D_MODEL
    x = x_ref[0]
    mod = mod_ref[0]
    xn = x * lax.rsqrt(jnp.mean(x * x, axis=-1, keepdims=True) + EPS) * g1_ref[...]
    h = xn * (1.0 + mod[:, d:2 * d]) + mod[:, 0:d]
    acc = _dot(h.astype(BF16), w_ref[...])

    tm = x.shape[0]
    lane = lax.broadcasted_iota(jnp.int32, (tm, LANES), 1)
    first_a = (lane % 32) < 16
    first_b = (lane % 16) < 8

    def normed(xb, seg_ref, inv_n, gain):
        hi, lo = _split_bf16(xb * xb)
        ss = _dot(hi, seg_ref[...]) + _dot(lo, seg_ref[...])
        return xb * lax.rsqrt(ss * inv_n + EPS) * gain

    def rope_a(y):
        if not latent:
            return y
        sw = jnp.where(first_a, pltpu.roll(y, LANES - 16, 1), pltpu.roll(y, 16, 1))
        return y * ca_ref[...] + sw * sa_ref[...]

    def rope_b(y):
        if not latent:
            return y
        sw = jnp.where(first_b, pltpu.roll(y, LANES - 8, 1), pltpu.roll(y, 8, 1))
        return y * cb_ref[...] + sw * sb_ref[...]

    gains = gains_ref[...]
    off = 0
    for b in range(H_A):
        y = normed(acc[:, off:off + LANES], s64_ref, 1.0 / HD_A, gains[0:1])
        qa_ref[0, :, b * LANES:(b + 1) * LANES] = rope_a(y).astype(qa_ref.dtype)
        off += LANES
    y = normed(acc[:, off:off + LANES], s64_ref, 1.0 / HD_A, gains[1:2])
    ka_ref[0] = rope_a(y).astype(ka_ref.dtype)
    off += LANES
    va_ref[0] = acc[:, off:off + LANES].astype(va_ref.dtype)
    off += LANES
    for b in range(W_B // LANES):
        y = normed(acc[:, off:off + LANES], s32_ref, 1.0 / DC_B, gains[2:3])
        qb_ref[0, :, b * LANES:(b + 1) * LANES] = rope_b(y).astype(qb_ref.dtype)
        off += LANES
    for b in range(W_B // LANES):
        y = normed(acc[:, off:off + LANES], s32_ref, 1.0 / DC_B, gains[3:4])
        kb_ref[0, :, b * LANES:(b + 1) * LANES] = rope_b(y).astype(kb_ref.dtype)
        off += LANES
    vb_ref[0] = acc[:, off:off + W_B].astype(vb_ref.dtype)
    off += W_B
    u_ref[0] = acc[:, off:off + W_C].astype(u_ref.dtype)


def _inproj(x, mod, g1, w_in_p, gains, seg64, seg32, rope, kv_dtype):
    bsz, seq, d = x.shape
    latent = rope is not None
    tm = 512 if seq % 512 == 0 else 256
    bm = mod.shape[0]
    mod_idx = (lambda b, i: (b, 0, 0)) if bm > 1 else (lambda b, i: (0, 0, 0))
    const2 = lambda b, i: (0, 0)
    tok = lambda w: pl.BlockSpec((1, tm, w), lambda b, i: (b, i, 0))
    in_specs = [tok(d),
                pl.BlockSpec((1, 1, 6 * d), mod_idx),
                pl.BlockSpec((1, d), const2),
                pl.BlockSpec((d, IN_COLS_P), const2),
                pl.BlockSpec((4, LANES), const2),
                pl.BlockSpec((LANES, LANES), const2),
                pl.BlockSpec((LANES, LANES), const2)]
    args = [x, mod, g1, w_in_p, gains, seg64, seg32]
    if latent:
        in_specs += [pl.BlockSpec((tm, LANES), lambda b, i: (i, 0))] * 4
        args += list(rope)
    widths = (QA_COLS, KV_A * HD_A, KV_A * HD_A, W_B, W_B, W_B, W_C)
    dtypes = (BF16, kv_dtype, kv_dtype, BF16, kv_dtype, kv_dtype, BF16)
    return pl.pallas_call(
        functools.partial(_inproj_body, latent=latent),
        grid=(bsz, seq // tm),
        in_specs=in_specs,
        out_specs=[tok(w) for w in widths],
        out_shape=[jax.ShapeDtypeStruct((bsz, seq, w), dt) for w, dt in zip(widths, dtypes)],
        compiler_params=_cparams("parallel", "parallel"),
        name="inproj_latent" if latent else "inproj_ctx",
    )(*args)


def _attn_a_body(sink_ref, q_ref, *refs, latent, nblk):
    o_ref = refs[-1]
    nk = (len(refs) - 1) // 2
    kcat = jnp.concatenate([r[0].astype(BF16) for r in refs[:nk]], axis=0)
    vcat = jnp.concatenate([r[0].astype(BF16) for r in refs[nk:2 * nk]], axis=0)
    rows = GQ_A * BLOCK
    cols = kcat.shape[0]
    rowi = lax.broadcasted_iota(jnp.int32, (rows, 1), 0)
    if latent:
        i = pl.program_id(1)
        r = lax.broadcasted_iota(jnp.int32, (rows, cols), 0) & (BLOCK - 1)
        c = lax.broadcasted_iota(jnp.int32, (rows, cols), 1)
        p_off = jnp.where(i > 0, 0, 2 * BLOCK)
        n_off = jnp.where(i < nblk - 1, 0, 2 * BLOCK)
        prev_ok = (c >= r + p_off) | (c >= BLOCK)
        next_ok = ((c - 2 * BLOCK + n_off) <= r) | (c < 2 * BLOCK) | (c >= 3 * BLOCK)
        valid = prev_ok & next_ok
    lane = lax.broadcasted_iota(jnp.int32, (BLOCK, LANES), 1)
    heads = []
    for j in range(KV_A):
        q3 = jnp.concatenate([q_ref[0, :, (GQ_A * j + g) * LANES:(GQ_A * j + g + 1) * LANES]
                              for g in range(GQ_A)], axis=0)
        s = _dot_nt(q3, kcat)
        if latent:
            s = jnp.where(valid, s, NEG)
        sink = jnp.where(rowi < BLOCK, sink_ref[GQ_A * j],
                         jnp.where(rowi < 2 * BLOCK, sink_ref[GQ_A * j + 1], sink_ref[GQ_A * j + 2]))
        m = jnp.maximum(jnp.max(s, axis=-1, keepdims=True), sink)
        e = jnp.exp(s - m)
        den = jnp.sum(e, axis=-1, keepdims=True) + jnp.exp(sink - m)
        o = _dot(e.astype(BF16), vcat) / den
        for g in range(GQ_A):
            heads.append((j, o[g * BLOCK:(g + 1) * BLOCK]))
    for blk in range(H_A // 2):
        (j0, o0), (j1, o1) = heads[2 * blk], heads[2 * blk + 1]
        lo = o0 if j0 == 0 else pltpu.roll(o0, HD_A, 1)
        hi = o1 if j1 == 1 else pltpu.roll(o1, HD_A, 1)
        o_ref[0, :, blk * LANES:(blk + 1) * LANES] = jnp.where(lane < HD_A, lo, hi).astype(o_ref.dtype)


def _attn_a(qa, ka, va, sink, ctx_kv):
    bsz, seq, _ = qa.shape
    nblk = seq // BLOCK
    latent = ctx_kv is not None
    kvw = KV_A * HD_A
    if latent:
        past = ctx_kv[0].shape[1]
        band = [pl.BlockSpec((1, BLOCK, kvw), lambda b, i: (b, jnp.maximum(i - 1, 0), 0)),
                pl.BlockSpec((1, BLOCK, kvw), lambda b, i: (b, i, 0)),
                pl.BlockSpec((1, BLOCK, kvw), lambda b, i: (b, jnp.minimum(i + 1, nblk - 1), 0)),
                pl.BlockSpec((1, past, kvw), lambda b, i: (b, 0, 0))]
        kv_specs = band + band
        kv_args = [ka, ka, ka, ctx_kv[0], va, va, va, ctx_kv[1]]
    else:
        kv_specs = [pl.BlockSpec((1, seq, kvw), lambda b, i: (b, 0, 0))] * 2
        kv_args = [ka, va]
    return pl.pallas_call(
        functools.partial(_attn_a_body, latent=latent, nblk=nblk),
        grid=(bsz, nblk),
        in_specs=[pl.BlockSpec(memory_space=pltpu.SMEM),
                  pl.BlockSpec((1, BLOCK, QA_COLS), lambda b, i: (b, i, 0))] + kv_specs,
        out_specs=pl.BlockSpec((1, BLOCK, W_A), lambda b, i: (b, i, 0)),
        out_shape=jax.ShapeDtypeStruct((bsz, seq, W_A), BF16),
        compiler_params=_cparams("parallel", "parallel"),
        name="attn_a_latent" if latent else "attn_a_ctx",
    )(sink, qa, *kv_args)


def _attn_b_body(lam_ref, gain_ref, q_ref, *refs, part_lens, lam_init, kc):
    npart = len(part_lens)
    k_refs, v_refs = refs[:npart], refs[npart:2 * npart]
    o_ref, s_scr = refs[2 * npart], refs[2 * npart + 1]
    lv = lam_ref[...]
    lam = (jnp.exp(jnp.sum(lv[0:1] * lv[1:2], axis=-1, keepdims=True))
           - jnp.exp(jnp.sum(lv[2:3] * lv[3:4], axis=-1, keepdims=True)) + lam_init)
    q = q_ref[0]
    tq = q.shape[0]
    lane_q = lax.broadcasted_iota(jnp.int32, (tq, LANES), 1)

    def chunks():
        base = 0
        for p, plen in enumerate(part_lens):
            step = min(kc, plen)
            yield p, base, step, plen // step
            base += plen // step

    total = jnp.zeros((tq, LANES), F32)
    for h in range(2):
        comps = []
        for c in range(2):
            lo = h * HD_B + c * DC_B
            qm = jnp.where((lane_q >= lo) & (lane_q < lo + DC_B), q, jnp.zeros_like(q))

            m = jnp.full((tq, 1), -jnp.inf, F32)
            for p, off, step, n in chunks():
                def score(t, m, p=p, off=off, step=step):
                    start = pl.multiple_of(t * step, step)
                    s = _dot_nt(qm, k_refs[p][0, pl.ds(start, step), :].astype(BF16))
                    s_scr[off + t, :, 0:step] = s
                    return jnp.maximum(m, jnp.max(s, axis=-1, keepdims=True))
                m = lax.fori_loop(0, n, score, m)

            acc = jnp.zeros((tq, LANES), F32)
            den = jnp.zeros((tq, 1), F32)
            for p, off, step, n in chunks():
                lane_v = lax.broadcasted_iota(jnp.int32, (step, LANES), 1)
                vmask = (lane_v >= h * HD_B) & (lane_v < (h + 1) * HD_B)

                def pv(t, carry, p=p, off=off, step=step, vmask=vmask):
                    acc, den = carry
                    start = pl.multiple_of(t * step, step)
                    e = jnp.exp(s_scr[off + t, :, 0:step] - m)
                    v = v_refs[p][0, pl.ds(start, step), :].astype(BF16)
                    v = jnp.where(vmask, v, jnp.zeros_like(v))
                    return acc + _dot(e.astype(BF16), v), den + jnp.sum(e, axis=-1, keepdims=True)
                acc, den = lax.fori_loop(0, n, pv, (acc, den))
            comps.append(acc / den)
        total = total + comps[0] - lam * comps[1]
    sq = total * total
    ss_lo = jnp.sum(jnp.where(lane_q < HD_B, sq, 0.0), axis=-1, keepdims=True)
    ss_hi = jnp.sum(jnp.where(lane_q >= HD_B, sq, 0.0), axis=-1, keepdims=True)
    rinv = jnp.where(lane_q < HD_B, lax.rsqrt(ss_lo * (1.0 / HD_B) + EPS), lax.rsqrt(ss_hi * (1.0 / HD_B) + EPS))
    o_ref[0] = (total * rinv * gain_ref[...] * (1.0 - lam_init)).astype(o_ref.dtype)


def _attn_b(qb, k_parts, v_parts, lam_b, gain, lam_init):
    bsz, seq, _ = qb.shape
    tq = 256
    part_lens = tuple(k.shape[1] for k in k_parts)
    kv_specs = [pl.BlockSpec((1, n, LANES), lambda b, hp, i: (b, 0, hp)) for n in part_lens]
    kc = 512
    return pl.pallas_call(
        functools.partial(_attn_b_body, part_lens=part_lens, lam_init=lam_init, kc=kc),
        grid=(bsz, W_B // LANES, seq // tq),
        in_specs=[pl.BlockSpec((4, DC_B), lambda b, hp, i: (0, 0)),
                  pl.BlockSpec((1, LANES), lambda b, hp, i: (0, 0)),
                  pl.BlockSpec((1, tq, LANES), lambda b, hp, i: (b, i, hp))] + kv_specs + kv_specs,
        out_specs=pl.BlockSpec((1, tq, LANES), lambda b, hp, i: (b, i, hp)),
        out_shape=jax.ShapeDtypeStruct((bsz, seq, W_B), BF16),
        scratch_shapes=[pltpu.VMEM((sum(n // min(kc, n) for n in part_lens), tq, kc), F32)],
        compiler_params=_cparams("parallel", "parallel", "parallel"),
        name="attn_b_latent" if len(k_parts) > 1 else "attn_b_ctx",
    )(lam_b, gain, qb, *k_parts, *v_parts)


def _ssm_body(u_ref, m_ref, g_ref, cc_ref, a_ref, h0_ref, y_ref, fin_ref, s_scr, h_scr, *, nb, nc):
    u = u_ref[0]
    s_scr[...] = _dot(u, g_ref[0])
    a = a_ref[0]
    afr, afi, abr, abi = (jnp.broadcast_to(a[k:k + 1], (nb, LANES)) for k in range(4))
    h0 = h0_ref[0]
    col = lambda k: slice(k * LANES, (k + 1) * LANES)

    def step(c, carry):
        fr, fi, br, bi = carry
        rf = pl.ds(pl.multiple_of(c * nb, nb), nb)
        rb = pl.ds(pl.multiple_of((nc - 1 - c) * nb, nb), nb)
        h_scr[rf, col(0)] = fr
        h_scr[rf, col(1)] = fi
        h_scr[rb, col(2)] = br
        h_scr[rb, col(3)] = bi
        nfr = afr * fr - afi * fi + s_scr[rf, col(0)]
        nfi = afr * fi + afi * fr + s_scr[rf, col(1)]
        nbr = abr * br - abi * bi + s_scr[rb, col(2)]
        nbi = abr * bi + abi * br + s_scr[rb, col(3)]
        return nfr, nfi, nbr, nbi

    fin = lax.fori_loop(0, nc, step, tuple(h0[:, col(k)] for k in range(4)))
    for k in range(4):
        fin_ref[0, :, col(k)] = fin[k]
    y = _dot(u, m_ref[0]) + _dot(h_scr[...].astype(BF16), cc_ref[0])
    y_ref[0] = y.astype(y_ref.dtype)


def _ssm(u_rows, mats, h0, nb):
    npair, rows, w = u_rows.shape
    nc = rows // nb
    mat_spec = pl.BlockSpec((1, w, w), lambda p: (p, 0, 0))
    return pl.pallas_call(
        functools.partial(_ssm_body, nb=nb, nc=nc),
        grid=(npair,),
        in_specs=[pl.BlockSpec((1, rows, w), lambda p: (p, 0, 0)), mat_spec, mat_spec, mat_spec,
                  pl.BlockSpec((1, 4, LANES), lambda p: (p, 0, 0)),
                  pl.BlockSpec((1, nb, w), lambda p: (p, 0, 0))],
        out_specs=[pl.BlockSpec((1, rows, w), lambda p: (p, 0, 0)),
                   pl.BlockSpec((1, nb, w), lambda p: (p, 0, 0))],
        out_shape=[jax.ShapeDtypeStruct((npair, rows, w), BF16),
                   jax.ShapeDtypeStruct((npair, nb, w), F32)],
        scratch_shapes=[pltpu.VMEM((rows, w), F32), pltpu.VMEM((rows, w), F32)],
        compiler_params=_cparams("parallel"),
        name="ssm_scan",
    )(u_rows, mats["m"], mats["g"], mats["cc"], mats["a16"], h0)


def _ssm_matrices(lp):
    t = SSM_T
    ks = jnp.arange(t + 1, dtype=F32)
    dirs = []
    for d in range(2):
        lam = lax.complex(lp["ssm_lam_re"][d].astype(F32), lp["ssm_lam_im"][d].astype(F32))
        dt = jnp.exp(lp["ssm_log_dt"][d].astype(F32))[:, None]
        a_bar = jnp.exp(lam * dt)
        b_bar = ((a_bar - 1.0) / lam)[..., None] * lax.complex(lp["ssm_b_re"][d].astype(F32),
                                                               lp["ssm_b_im"][d].astype(F32))
        c_mat = lax.complex(lp["ssm_c_re"][d].astype(F32), lp["ssm_c_im"][d].astype(F32))
        pw = jnp.exp((lam * dt)[None] * ks[:, None, None].astype(jnp.complex64))
        kern = jnp.real(jnp.einsum("gop,kgp,gpi->gkoi", c_mat, pw[:t], b_bar))
        dirs.append((pw, b_bar, c_mat, kern))
    (pw_f, bb_f, cm_f, k_f), (pw_b, bb_b, cm_b, k_b) = dirs
    s_idx = jnp.arange(t)[:, None]
    t_idx = jnp.arange(t)[None, :]
    lag_f = jnp.clip(t_idx - s_idx, 0, t - 1)
    lag_b = jnp.clip(s_idx - t_idx, 0, t - 1)
    m5 = (jnp.where((t_idx >= s_idx)[None, :, :, None, None], k_f[:, lag_f], 0.0)
          + jnp.where((s_idx >= t_idx)[None, :, :, None, None], k_b[:, lag_b], 0.0))
    eye_t = jnp.eye(t, dtype=F32)
    eye_c = jnp.eye(SSM_CH, dtype=F32)
    m5 = m5 + eye_t[None, :, :, None, None] * eye_c[None, None, None] * lp["ssm_d"].astype(F32)[:, None, None, :, None]
    m_g = m5.transpose(0, 1, 4, 2, 3).reshape(G_C, t * SSM_CH, t * SSM_CH)
    g_f = pw_f[t - 1 - jnp.arange(t)].transpose(1, 0, 2)[:, :, None, :] * bb_f.transpose(0, 2, 1)[:, None]
    g_b = pw_b[jnp.arange(t)].transpose(1, 0, 2)[:, :, None, :] * bb_b.transpose(0, 2, 1)[:, None]
    g_g = jnp.stack([jnp.real(g_f), jnp.imag(g_f), jnp.real(g_b), jnp.imag(g_b)], axis=3)
    z_f = cm_f.transpose(0, 2, 1)[:, :, None, :] * pw_f[1 + jnp.arange(t)].transpose(1, 2, 0)[:, :, :, None]
    z_b = cm_b.transpose(0, 2, 1)[:, :, None, :] * pw_b[t - jnp.arange(t)].transpose(1, 2, 0)[:, :, :, None]
    cc_g = jnp.stack([jnp.real(z_f), -jnp.imag(z_f), jnp.real(z_b), -jnp.imag(z_b)], axis=1)
    eye2 = jnp.eye(2, dtype=F32)
    tw = t * SSM_CH
    m_p = jnp.einsum("pgab,gh->pgahb", m_g.reshape(N_PAIR, 2, tw, tw), eye2).reshape(N_PAIR, 2 * tw, 2 * tw)
    g_p = jnp.einsum("pgawq,gh->pgawhq", g_g.reshape(N_PAIR, 2, tw, 4, P_C), eye2).reshape(N_PAIR, 2 * tw, 8 * P_C)
    cc_p = jnp.einsum("pgwqb,gh->pwgqhb", cc_g.reshape(N_PAIR, 2, 4, P_C, tw), eye2).reshape(N_PAIR, 8 * P_C, 2 * tw)
    a16 = jnp.stack([jnp.real(pw_f[t]), jnp.imag(pw_f[t]), jnp.real(pw_b[t]), jnp.imag(pw_b[t])], axis=0)
    a16 = a16.reshape(4, N_PAIR, 2 * P_C).transpose(1, 0, 2)
    return dict(m=m_p.astype(BF16), g=g_p.astype(BF16), cc=cc_p.astype(BF16), a16=a16)


def _ssm_rows(u):
    bsz, seq, _ = u.shape
    nc = seq // SSM_T
    r = u.reshape(bsz, nc, SSM_T, N_PAIR, 2, SSM_CH).transpose(3, 1, 0, 4, 2, 5)
    return r.reshape(N_PAIR, nc * bsz, 2 * SSM_T * SSM_CH)


def _ssm_unrows(y, bsz):
    npair, rows, _ = y.shape
    nc = rows // bsz
    r = y.reshape(npair, nc, bsz, 2, SSM_T, SSM_CH).transpose(2, 1, 4, 0, 3, 5)
    return r.reshape(bsz, nc * SSM_T, W_C)


def _ssm_state_rows(s_re, s_im):
    bsz = s_re.shape[0]
    parts = [s_re[:, 0], s_im[:, 0], s_re[:, 1], s_im[:, 1]]
    st = jnp.stack([p.reshape(bsz, N_PAIR, 2 * P_C) for p in parts], axis=2)
    return st.transpose(1, 0, 2, 3).reshape(N_PAIR, bsz, 8 * P_C).astype(F32)


def _ssm_state_unrows(fin):
    npair, bsz, _ = fin.shape
    st = fin.reshape(npair, bsz, 4, 2, P_C).transpose(1, 2, 0, 3, 4).reshape(bsz, 4, G_C, P_C)
    return jnp.stack([st[:, 0], st[:, 2]], axis=1), jnp.stack([st[:, 1], st[:, 3]], axis=1)


def _route(scores, bias):
    tm = scores.shape[1]
    biased = scores + bias
    iota8 = lax.broadcasted_iota(jnp.int32, (PER_GROUP, tm), 0)
    grp = [biased[PER_GROUP * g:PER_GROUP * (g + 1)] for g in range(N_EXP_GROUPS)]
    gscore = []
    for v in grp:
        m1 = jnp.max(v, axis=0, keepdims=True)
        first = jnp.min(jnp.where(v == m1, iota8, PER_GROUP), axis=0, keepdims=True)
        m2 = jnp.max(jnp.where(iota8 == first, -jnp.inf, v), axis=0, keepdims=True)
        gscore.append(m1 + m2)
    masked = []
    for g in range(N_EXP_GROUPS):
        rank = jnp.zeros((1, tm), jnp.int32)
        for o in range(N_EXP_GROUPS):
            if o == g:
                continue
            ahead = (gscore[o] >= gscore[g]) if o < g else (gscore[o] > gscore[g])
            rank = rank + jnp.where(ahead, 1, 0)
        masked.append(jnp.where(rank < TOPK_GROUPS, grp[g], -jnp.inf))
    chosen = [None] * N_EXP_GROUPS
    for _ in range(TOP_K):
        best = masked[0]
        for v in masked[1:]:
            best = jnp.maximum(best, v)
        best = jnp.max(best, axis=0, keepdims=True)
        first = jnp.full((1, tm), N_EXPERTS, jnp.int32)
        for g, v in enumerate(masked):
            cand = jnp.min(jnp.where(v == best, iota8 + PER_GROUP * g, N_EXPERTS), axis=0, keepdims=True)
            first = jnp.minimum(first, cand)
        for g in range(N_EXP_GROUPS):
            hit = (iota8 + PER_GROUP * g) == first
            chosen[g] = hit if chosen[g] is None else (chosen[g] | hit)
            masked[g] = jnp.where(hit, -jnp.inf, masked[g])
    w = [jnp.where(chosen[g], scores[PER_GROUP * g:PER_GROUP * (g + 1)], 0.0) for g in range(N_EXP_GROUPS)]
    wsum = w[0]
    for v in w[1:]:
        wsum = wsum + v
    wsum = jnp.sum(wsum, axis=0, keepdims=True)
    return jnp.concatenate([v / wsum * ROUTED_SCALE for v in w], axis=0)


def _post_body(x_ref, oa_ref, ob_ref, y_ref, mod_ref, wglu_ref, wout_ref, g2_ref, wrh_ref, wrl_ref, br_ref,
               x1_ref, h2_ref, gt_ref):
    d = D_MODEL
    g = jax.nn.gelu(y_ref[0].astype(F32))
    oc = g * jax.nn.sigmoid(_dot(g.astype(BF16), wglu_ref[...]))
    mix = (_dot(oa_ref[0], wout_ref[0:W_A]) + _dot(ob_ref[0], wout_ref[W_A:W_A + W_B])
           + _dot(oc.astype(BF16), wout_ref[W_A + W_B:]))
    mod = mod_ref[0]
    x1 = x_ref[0] + mod[:, 2 * d:3 * d] * mix
    x1_ref[0] = x1
    xn = x1 * lax.rsqrt(jnp.mean(x1 * x1, axis=-1, keepdims=True) + EPS) * g2_ref[...]
    h2 = xn * (1.0 + mod[:, 4 * d:5 * d]) + mod[:, 3 * d:4 * d]
    h_hi, h_lo = _split_bf16(h2)
    h2_ref[0] = h_hi
    logits = _dot_nt(wrh_ref[...], h_hi) + _dot_nt(wrh_ref[...], h_lo) + _dot_nt(wrl_ref[...], h_hi)
    gt_ref[0] = _route(jax.nn.sigmoid(logits), br_ref[...])


def _post_mix(x, oa, ob, y, mod, w_glu, w_out, g2, wr_hi, wr_lo, b_r):
    bsz, seq, d = x.shape
    tm = 256
    bm = mod.shape[0]
    mod_idx = (lambda b, i: (b, 0, 0)) if bm > 1 else (lambda b, i: (0, 0, 0))
    const2 = lambda b, i: (0, 0)
    tok = lambda w: pl.BlockSpec((1, tm, w), lambda b, i: (b, i, 0))
    return pl.pallas_call(
        _post_body,
        grid=(bsz, seq // tm),
        in_specs=[tok(d), tok(W_A), tok(W_B), tok(W_C),
                  pl.BlockSpec((1, 1, 6 * d), mod_idx),
                  pl.BlockSpec((W_C, W_C), const2),
                  pl.BlockSpec((d, d), const2),
                  pl.BlockSpec((1, d), const2),
                  pl.BlockSpec((N_EXPERTS, d), const2),
                  pl.BlockSpec((N_EXPERTS, d), const2),
                  pl.BlockSpec((N_EXPERTS, 1), const2)],
        out_specs=[tok(d), tok(d), pl.BlockSpec((1, N_EXPERTS, tm), lambda b, i: (b, 0, i))],
        out_shape=[jax.ShapeDtypeStruct((bsz, seq, d), F32),
                   jax.ShapeDtypeStruct((bsz, seq, d), BF16),
                   jax.ShapeDtypeStruct((bsz, N_EXPERTS, seq), F32)],
        compiler_params=_cparams("parallel", "parallel"),
        name="post_mix",
    )(x, oa, ob, y, mod, w_glu, w_out, g2, wr_hi, wr_lo, b_r)


def _moe_body(x1_ref, h_ref, gate_ref, g2_ref, w1_ref, w3_ref, w2_ref, ex_ref, s1_ref, s3_ref, s2_ref,
              o_ref, acc_ref):
    j = pl.program_id(1)
    h = h_ref[...]

    @pl.when(j == 0)
    def _():
        a = _dot(h, s1_ref[...])
        acc_ref[...] = _dot((a * jax.nn.sigmoid(a) * _dot(h, s3_ref[...])).astype(BF16), s2_ref[...])

    a = _dot(h, w1_ref[...])
    b = _dot(h, w3_ref[...])
    g_hi, g_lo = _split_bf16(gate_ref[...])
    gexp = _dot(g_hi, ex_ref[...]) + _dot(g_lo, ex_ref[...])
    hid = a * jax.nn.sigmoid(a) * b * gexp
    acc_ref[...] += _dot(hid.astype(BF16), w2_ref[...])

    @pl.when(j == pl.num_programs(1) - 1)
    def _():
        o_ref[...] = x1_ref[...] + g2_ref[0] * acc_ref[...]


def _moe(x1, h2, gates, mod, seq, w1, w3, w2, expand, ws1, ws3, ws2):
    tokens, d = x1.shape
    bm = mod.shape[0]
    span = seq if bm > 1 else tokens
    tm = next(t for t in (1024, 512, 256) if span % t == 0)
    per_b = seq // tm if bm > 1 else 1
    mod_idx = (lambda i, j: (i // per_b, 0, 5)) if bm > 1 else (lambda i, j: (0, 0, 5))
    fc = 512
    hidden = w1.shape[1]
    const2 = lambda i, j: (0, 0)
    return pl.pallas_call(
        _moe_body,
        grid=(tokens // tm, hidden // fc),
        in_specs=[pl.BlockSpec((tm, d), lambda i, j: (i, 0)),
                  pl.BlockSpec((tm, d), lambda i, j: (i, 0)),
                  pl.BlockSpec((tm, N_EXPERTS), lambda i, j: (i, 0)),
                  pl.BlockSpec((1, 1, d), mod_idx),
                  pl.BlockSpec((d, fc), lambda i, j: (0, j)),
                  pl.BlockSpec((d, fc), lambda i, j: (0, j)),
                  pl.BlockSpec((fc, d), lambda i, j: (j, 0)),
                  pl.BlockSpec((N_EXPERTS, fc), lambda i, j: (0, j)),
                  pl.BlockSpec((d, F_SHARED), const2),
                  pl.BlockSpec((d, F_SHARED), const2),
                  pl.BlockSpec((F_SHARED, d), const2)],
        out_specs=pl.BlockSpec((tm, d), lambda i, j: (i, 0)),
        out_shape=jax.ShapeDtypeStruct((tokens, d), F32),
        scratch_shapes=[pltpu.VMEM((tm, d), F32)],
        compiler_params=_cparams("parallel", "arbitrary"),
        name="moe",
    )(x1, h2, gates, mod, w1, w3, w2, expand, ws1, ws3, ws2)


def _rope_tables(seq):
    pos = jnp.arange(seq)
    row = (pos // GRID_W).astype(F32)[:, None]
    colp = (pos % GRID_W).astype(F32)[:, None]
    lane = jnp.arange(LANES)

    def table(width):
        half, quarter = width // 2, width // 4
        i = lane % width
        freq = ROPE_BASE ** (-(2.0 * (i % quarter).astype(F32)) / half)
        ang = jnp.where((i // half) == 0, row, colp) * freq[None, :]
        sign = jnp.where((i % half) < quarter, -1.0, 1.0)
        return jnp.cos(ang), jnp.sin(ang) * sign[None, :]

    ca, sa = table(HD_A)
    cb, sb = table(DC_B)
    return ca, sa, cb, sb


def _prep_layer(p, l):
    d = D_MODEL
    w_in = p["w_in"][l]
    kvw = KV_A * HD_A
    qa = w_in[:, :W_A].reshape(d, H_A, HD_A)
    qa_pad = jnp.zeros((d, H_A, 2, HD_A), w_in.dtype)
    for h in range(H_A):
        qa_pad = qa_pad.at[:, h, h // GQ_A].set(qa[:, h])
    w_in_p = jnp.concatenate([qa_pad.reshape(d, QA_COLS), w_in[:, W_A:]], axis=1).astype(BF16)
    gains = jnp.stack([jnp.tile(p["q_norm_a"][l], LANES // HD_A) * (HD_A ** -0.5),
                       jnp.tile(p["k_norm_a"][l], LANES // HD_A),
                       jnp.tile(p["q_norm_b"][l], LANES // DC_B) * (DC_B ** -0.5),
                       jnp.tile(p["k_norm_b"][l], LANES // DC_B)], axis=0).astype(F32)
    lp = {k: p[k][l] for k in ("ssm_lam_re", "ssm_lam_im", "ssm_log_dt", "ssm_b_re", "ssm_b_im",
                               "ssm_c_re", "ssm_c_im", "ssm_d")}
    wr_hi, wr_lo = _split_bf16(p["w_router"][l].T.astype(F32))
    hidden = N_EXPERTS * F_EXP
    return dict(
        w_in_p=w_in_p, gains=gains,
        g1=p["norm1_g"][l].reshape(1, d).astype(F32), g2=p["norm2_g"][l].reshape(1, d).astype(F32),
        sink=p["sink_a"][l].astype(F32), lam_b=p["lam_b"][l].astype(F32),
        subln=jnp.tile(p["subln_b"][l], LANES // HD_B).reshape(1, LANES).astype(F32),
        lam_init=0.8 - 0.6 * math.exp(-0.3 * l),
        ssm=_ssm_matrices(lp),
        w_glu=p["w_glu"][l].astype(BF16), w_out=p["w_out"][l].astype(BF16),
        wr_hi=wr_hi, wr_lo=wr_lo, b_r=p["b_router"][l].reshape(N_EXPERTS, 1).astype(F32),
        w1=p["w_e1"][l].transpose(1, 0, 2).reshape(d, hidden).astype(BF16),
        w3=p["w_e3"][l].transpose(1, 0, 2).reshape(d, hidden).astype(BF16),
        w2=p["w_e2"][l].reshape(hidden, d).astype(BF16),
        ws1=p["w_s1"][l].astype(BF16), ws3=p["w_s3"][l].astype(BF16), ws2=p["w_s2"][l].astype(BF16),
    )


def _trunk_layer(x, mod, lw, consts, ctx):
    bsz, seq, d = x.shape
    latent = ctx is not None
    rope = consts["rope"] if latent else None
    kv_dtype = BF16 if latent else F32
    qa, ka, va, qb, kb, vb, u = _inproj(x, mod, lw["g1"], lw["w_in_p"], lw["gains"],
                                        consts["seg64"], consts["seg32"], rope, kv_dtype)
    if latent:
        oa = _attn_a(qa, ka, va, lw["sink"], (ctx["ak"], ctx["av"]))
        ob = _attn_b(qb, [kb, ctx["bk"]], [vb, ctx["bv"]], lw["lam_b"], lw["subln"], lw["lam_init"])
        h0 = ctx["h0"]
    else:
        oa = _attn_a(qa, ka, va, lw["sink"], None)
        ob = _attn_b(qb, [kb], [vb], lw["lam_b"], lw["subln"], lw["lam_init"])
        h0 = jnp.zeros((N_PAIR, bsz, 8 * P_C), F32)
    y_rows, fin = _ssm(_ssm_rows(u), lw["ssm"], h0, bsz)
    y = _ssm_unrows(y_rows, bsz)
    x1, h2, gates_t = _post_mix(x, oa, ob, y, mod, lw["w_glu"], lw["w_out"], lw["g2"],
                                lw["wr_hi"], lw["wr_lo"], lw["b_r"])
    gates = gates_t.transpose(0, 2, 1).reshape(bsz * seq, N_EXPERTS)
    out = _moe(x1.reshape(bsz * seq, d), h2.reshape(bsz * seq, d), gates, mod, seq,
               lw["w1"], lw["w3"], lw["w2"], consts["expand"], lw["ws1"], lw["ws3"], lw["ws2"])
    return out.reshape(bsz, seq, d), (ka, va, kb, vb, fin)


def kernel(x_prompt, x_sample, cache_a_k, cache_a_v, cache_b_k, cache_b_v, state_ssm_re, state_ssm_im, c, c_ctx, norm1_g, norm2_g, w_ada, b_ada, w_in, q_norm_a, k_norm_a, sink_a, q_norm_b, k_norm_b, lam_b, subln_b, ssm_lam_re, ssm_lam_im, ssm_log_dt, ssm_b_re, ssm_b_im, ssm_c_re, ssm_c_im, ssm_d, w_glu, w_out, w_router, b_router, w_e1, w_e3, w_e2, w_s1, w_s3, w_s2):
    p = dict(norm1_g=norm1_g, norm2_g=norm2_g, w_in=w_in, q_norm_a=q_norm_a, k_norm_a=k_norm_a, sink_a=sink_a,
             q_norm_b=q_norm_b, k_norm_b=k_norm_b, lam_b=lam_b, subln_b=subln_b,
             ssm_lam_re=ssm_lam_re, ssm_lam_im=ssm_lam_im, ssm_log_dt=ssm_log_dt, ssm_b_re=ssm_b_re,
             ssm_b_im=ssm_b_im, ssm_c_re=ssm_c_re, ssm_c_im=ssm_c_im, ssm_d=ssm_d, w_glu=w_glu, w_out=w_out,
             w_router=w_router, b_router=b_router, w_e1=w_e1, w_e3=w_e3, w_e2=w_e2,
             w_s1=w_s1, w_s3=w_s3, w_s2=w_s2)
    depth = w_in.shape[0]
    bsz, seq, d = x_prompt.shape
    dbsz, dseq, _ = x_sample.shape
    past = cache_a_k.shape[3]

    mod_rows = 16
    cvec = jnp.concatenate([c.astype(F32), c_ctx.astype(F32)[None],
                            jnp.zeros((mod_rows - dbsz - 1, d), F32)], axis=0)
    mods = _modulation(cvec, w_ada.astype(F32), b_ada.astype(F32))

    lane = jnp.arange(LANES)
    hidden = N_EXPERTS * F_EXP
    consts = dict(
        rope=_rope_tables(dseq),
        seg64=(lane[:, None] // HD_A == lane[None, :] // HD_A).astype(BF16),
        seg32=(lane[:, None] // DC_B == lane[None, :] // DC_B).astype(BF16),
        expand=(jnp.arange(N_EXPERTS)[:, None] == jnp.arange(hidden)[None, :] // F_EXP).astype(BF16),
    )

    xp, xs = x_prompt, x_sample
    ak, av, bk, bv, sre, sim = [], [], [], [], [], []
    for l in range(depth):
        lw = _prep_layer(p, l)
        mod_lat = mods[l, :dbsz][:, None, :]
        mod_ctx = mods[l, dbsz:dbsz + 1][:, None, :]
        xp, (k_a, v_a, k_b, v_b, fin) = _trunk_layer(xp, mod_ctx, lw, consts, None)
        ak.append(k_a.reshape(bsz, seq, KV_A, HD_A).transpose(0, 2, 1, 3))
        av.append(v_a.reshape(bsz, seq, KV_A, HD_A).transpose(0, 2, 1, 3))
        bk.append(k_b.reshape(bsz, seq, H_B, 2, DC_B).transpose(0, 2, 3, 1, 4))
        bv.append(v_b.reshape(bsz, seq, H_B, HD_B).transpose(0, 2, 1, 3))
        f_re, f_im = _ssm_state_unrows(fin)
        sre.append(f_re)
        sim.append(f_im)
        ctx = dict(
            ak=cache_a_k[:, l].transpose(0, 2, 1, 3).reshape(dbsz, past, KV_A * HD_A).astype(BF16),
            av=cache_a_v[:, l].transpose(0, 2, 1, 3).reshape(dbsz, past, KV_A * HD_A).astype(BF16),
            bk=cache_b_k[:, l].transpose(0, 3, 1, 2, 4).reshape(dbsz, past, W_B).astype(BF16),
            bv=cache_b_v[:, l].transpose(0, 2, 1, 3).reshape(dbsz, past, W_B).astype(BF16),
            h0=_ssm_state_rows(state_ssm_re[:, l], state_ssm_im[:, l]),
        )
        xs, _ = _trunk_layer(xs, mod_lat, lw, consts, ctx)
    return (xp, xs, jnp.stack(ak, axis=1), jnp.stack(av, axis=1), jnp.stack(bk, axis=1),
            jnp.stack(bv, axis=1), jnp.stack(sre, axis=1), jnp.stack(sim, axis=1))
```

```python
import functools
import math

import jax
import jax.numpy as jnp
from jax import lax
from jax.experimental import pallas as pl
from jax.experimental.pallas import tpu as pltpu

F32 = jnp.float32
BF16 = jnp.bfloat16

D_MODEL = 1024
GRID_W = 64
BLOCK = 128
H_A, KV_A, HD_A = 6, 2, 64
GQ_A = H_A // KV_A
W_A = H_A * HD_A
H_B, HD_B = 4, 64
DC_B = HD_B // 2
W_B = H_B * HD_B
SSM_CH = 16
W_C = D_MODEL - W_A - W_B
G_C = W_C // SSM_CH
P_C = 64
N_EXPERTS, TOP_K, F_EXP, F_SHARED = 64, 6, 128, 256
N_EXP_GROUPS, TOPK_GROUPS = 8, 4
PER_GROUP = N_EXPERTS // N_EXP_GROUPS
ROUTED_SCALE = 2.5
ROPE_BASE = 10000.0
EPS = 1e-6
NEG = -1e30
LOG2E = 1.4426950408889634

LANES = 128
SSM_T = 16
N_PAIR = G_C // 2
SSM_ROW = 2 * SSM_T * SSM_CH
QA_COLS = H_A * LANES
IN_COLS_P = QA_COLS + 2 * KV_A * HD_A + 3 * W_B + W_C
VMEM_LIMIT = 56 << 20


def _cparams(*sem):
    return pltpu.CompilerParams(dimension_semantics=sem, vmem_limit_bytes=VMEM_LIMIT)


def _dot(a, b):
    return jnp.dot(a, b, preferred_element_type=F32)


def _dot_nt(a, b):
    return lax.dot_general(a, b, (((1,), (1,)), ((), ())), preferred_element_type=F32)


def _split_bf16(x):
    hi = x.astype(BF16)
    lo = (x - hi.astype(F32)).astype(BF16)
    return hi, lo


def _mod_body(c_ref, w_ref, b_ref, o_ref):
    c = c_ref[...]
    s = c * jax.nn.sigmoid(c)
    s_hi, s_lo = _split_bf16(s)
    w_hi, w_lo = _split_bf16(w_ref[0])
    o_ref[0] = _dot(s_hi, w_hi) + _dot(s_lo, w_hi) + _dot(s_hi, w_lo) + b_ref[0]


def _modulation(cvec, w_ada, b_ada):
    depth, d, n = w_ada.shape
    rows = cvec.shape[0]
    tn = 768
    return pl.pallas_call(
        _mod_body,
        grid=(depth, n // tn),
        in_specs=[pl.BlockSpec((rows, d), lambda l, j: (0, 0)),
                  pl.BlockSpec((1, d, tn), lambda l, j: (l, 0, j)),
                  pl.BlockSpec((1, 1, tn), lambda l, j: (l, 0, j))],
        out_specs=pl.BlockSpec((1, rows, tn), lambda l, j: (l, 0, j)),
        out_shape=jax.ShapeDtypeStruct((depth, rows, n), F32),
        compiler_params=_cparams("parallel", "parallel"),
        name="adaln_mod",
    )(cvec, w_ada, b_ada.reshape(depth, 1, n))


def _inproj_body(*refs, latent):
    if latent:
        (x_ref, mod_ref, g1_ref, w_ref, gains_ref, s64_ref, s32_ref, ca_ref, sa_ref, cb_ref, sb_ref,
         qa_ref, ka_ref, va_ref, qb_ref, kb_ref, vb_ref, u_ref, u_scr) = refs
    else:
        (x_ref, mod_ref, g1_ref, w_ref, gains_ref, s64_ref, s32_ref,
         qa_ref, ka_ref, va_ref, qb_ref, kb_ref, vb_ref, u_ref, u_scr) = refs
    d = D_MODEL
    x = x_ref[0]
    mod = mod_ref[0]
    xn = x * lax.rsqrt(jnp.mean(x * x, axis=-1, keepdims=True) + EPS) * g1_ref[...]
    h = xn * (1.0 + mod[:, d:2 * d]) + mod[:, 0:d]
    acc = _dot(h.astype(BF16), w_ref[...])

    tm = x.shape[0]
    lane = lax.broadcasted_iota(jnp.int32, (tm, LANES), 1)
    first_a = (lane % 32) < 16
    first_b = (lane % 16) < 8

    def normed(xb, seg_ref, inv_n, gain):
        hi, lo = _split_bf16(xb * xb)
        ss = _dot(hi, seg_ref[...]) + _dot(lo, seg_ref[...])
        return xb * lax.rsqrt(ss * inv_n + EPS) * gain

    def rope_a(y):
        if not latent:
            return y
        sw = jnp.where(first_a, pltpu.roll(y, LANES - 16, 1), pltpu.roll(y, 16, 1))
        return y * ca_ref[...] + sw * sa_ref[...]

    def rope_b(y):
        if not latent:
            return y
        sw = jnp.where(first_b, pltpu.roll(y, LANES - 8, 1), pltpu.roll(y, 8, 1))
        return y * cb_ref[...] + sw * sb_ref[...]

    gains = gains_ref[...]
    off = 0
    for b in range(H_A):
        y = normed(acc[:, off:off + LANES], s64_ref, 1.0 / HD_A, gains[0:1])
        qa_ref[0, :, b * LANES:(b + 1) * LANES] = rope_a(y).astype(qa_ref.dtype)
        off += LANES
    y = normed(acc[:, off:off + LANES], s64_ref, 1.0 / HD_A, gains[1:2])
    ka_ref[0] = rope_a(y).astype(ka_ref.dtype)
    off += LANES
    va_ref[0] = acc[:, off:off + LANES].astype(va_ref.dtype)
    off += LANES
    for b in range(W_B // LANES):
        y = normed(acc[:, off:off + LANES], s32_ref, 1.0 / DC_B, gains[2:3])
        qb_ref[0, :, b * LANES:(b + 1) * LANES] = rope_b(y).astype(qb_ref.dtype)
        off += LANES
    for b in range(W_B // LANES):
        y = normed(acc[:, off:off + LANES], s32_ref, 1.0 / DC_B, gains[3:4])
        kb_ref[0, :, b * LANES:(b + 1) * LANES] = rope_b(y).astype(kb_ref.dtype)
        off += LANES
    vb_ref[0] = acc[:, off:off + W_B].astype(vb_ref.dtype)
    off += W_B
    for blk in range(W_C // LANES):
        u_scr[blk] = acc[:, off + blk * LANES:off + (blk + 1) * LANES]
    pw = 2 * SSM_CH
    for t in range(SSM_T):
        for blk in range(W_C // LANES):
            xt = u_scr[blk, pl.ds(t, tm // SSM_T, stride=SSM_T), :]
            for pp in range(LANES // pw):
                u_ref[blk * (LANES // pw) + pp, :, t * pw:(t + 1) * pw] = xt[:, pp * pw:(pp + 1) * pw].astype(u_ref.dtype)


def _inproj(x, mod, g1, w_in_p, gains, seg64, seg32, rope, kv_dtype):
    bsz, seq, d = x.shape
    latent = rope is not None
    tm = 512 if seq % 512 == 0 else 256
    bm = mod.shape[0]
    mod_idx = (lambda b, i: (b, 0, 0)) if bm > 1 else (lambda b, i: (0, 0, 0))
    const2 = lambda b, i: (0, 0)
    tok = lambda w: pl.BlockSpec((1, tm, w), lambda b, i: (b, i, 0))
    in_specs = [tok(d),
                pl.BlockSpec((1, 1, 6 * d), mod_idx),
                pl.BlockSpec((1, d), const2),
                pl.BlockSpec((d, IN_COLS_P), const2),
                pl.BlockSpec((4, LANES), const2),
                pl.BlockSpec((LANES, LANES), const2),
                pl.BlockSpec((LANES, LANES), const2)]
    args = [x, mod, g1, w_in_p, gains, seg64, seg32]
    if latent:
        in_specs += [pl.BlockSpec((tm, LANES), lambda b, i: (i, 0))] * 4
        args += list(rope)
    widths = (QA_COLS, KV_A * HD_A, KV_A * HD_A, W_B, W_B, W_B)
    dtypes = (BF16, kv_dtype, kv_dtype, BF16, kv_dtype, kv_dtype)
    nt = seq // tm
    rows = tm // SSM_T
    u_spec = pl.BlockSpec((N_PAIR, rows, SSM_ROW), lambda b, i: (0, b * nt + i, 0))
    u_shape = jax.ShapeDtypeStruct((N_PAIR, bsz * seq // SSM_T, SSM_ROW), BF16)
    return pl.pallas_call(
        functools.partial(_inproj_body, latent=latent),
        grid=(bsz, nt),
        in_specs=in_specs,
        out_specs=[tok(w) for w in widths] + [u_spec],
        out_shape=[jax.ShapeDtypeStruct((bsz, seq, w), dt) for w, dt in zip(widths, dtypes)] + [u_shape],
        scratch_shapes=[pltpu.VMEM((W_C // LANES, tm, LANES), F32)],
        compiler_params=_cparams("parallel", "parallel"),
        name="inproj_latent" if latent else "inproj_ctx",
    )(*args)


def _attn_a_body(sink_ref, q_ref, *refs, latent, nblk):
    o_ref = refs[-1]
    nk = (len(refs) - 1) // 2
    kcat = jnp.concatenate([r[0].astype(BF16) for r in refs[:nk]], axis=0)
    vcat = jnp.concatenate([r[0].astype(BF16) for r in refs[nk:2 * nk]], axis=0)
    rows = GQ_A * BLOCK
    cols = kcat.shape[0]
    rowi = lax.broadcasted_iota(jnp.int32, (rows, 1), 0)
    if latent:
        i = pl.program_id(1)
        r = lax.broadcasted_iota(jnp.int32, (rows, cols), 0) & (BLOCK - 1)
        c = lax.broadcasted_iota(jnp.int32, (rows, cols), 1)
        p_off = jnp.where(i > 0, 0, 2 * BLOCK)
        n_off = jnp.where(i < nblk - 1, 0, 2 * BLOCK)
        prev_ok = (c >= r + p_off) | (c >= BLOCK)
        next_ok = ((c - 2 * BLOCK + n_off) <= r) | (c < 2 * BLOCK) | (c >= 3 * BLOCK)
        valid = prev_ok & next_ok
    lane = lax.broadcasted_iota(jnp.int32, (BLOCK, LANES), 1)
    heads = []
    for j in range(KV_A):
        q3 = jnp.concatenate([q_ref[0, :, (GQ_A * j + g) * LANES:(GQ_A * j + g + 1) * LANES]
                              for g in range(GQ_A)], axis=0)
        s = _dot_nt(q3, kcat)
        if latent:
            s = jnp.where(valid, s, NEG)
        sink = jnp.where(rowi < BLOCK, sink_ref[GQ_A * j],
                         jnp.where(rowi < 2 * BLOCK, sink_ref[GQ_A * j + 1], sink_ref[GQ_A * j + 2]))
        m = jnp.maximum(jnp.max(s, axis=-1, keepdims=True), sink)
        e = jnp.exp(s - m)
        den = jnp.sum(e, axis=-1, keepdims=True) + jnp.exp(sink - m)
        o = _dot(e.astype(BF16), vcat) / den
        for g in range(GQ_A):
            heads.append((j, o[g * BLOCK:(g + 1) * BLOCK]))
    for blk in range(H_A // 2):
        (j0, o0), (j1, o1) = heads[2 * blk], heads[2 * blk + 1]
        lo = o0 if j0 == 0 else pltpu.roll(o0, HD_A, 1)
        hi = o1 if j1 == 1 else pltpu.roll(o1, HD_A, 1)
        o_ref[0, :, blk * LANES:(blk + 1) * LANES] = jnp.where(lane < HD_A, lo, hi).astype(o_ref.dtype)


def _attn_a(qa, ka, va, sink, ctx_kv):
    bsz, seq, _ = qa.shape
    nblk = seq // BLOCK
    latent = ctx_kv is not None
    kvw = KV_A * HD_A
    if latent:
        past = ctx_kv[0].shape[1]
        band = [pl.BlockSpec((1, BLOCK, kvw), lambda b, i: (b, jnp.maximum(i - 1, 0), 0)),
                pl.BlockSpec((1, BLOCK, kvw), lambda b, i: (b, i, 0)),
                pl.BlockSpec((1, BLOCK, kvw), lambda b, i: (b, jnp.minimum(i + 1, nblk - 1), 0)),
                pl.BlockSpec((1, past, kvw), lambda b, i: (b, 0, 0))]
        kv_specs = band + band
        kv_args = [ka, ka, ka, ctx_kv[0], va, va, va, ctx_kv[1]]
    else:
        kv_specs = [pl.BlockSpec((1, seq, kvw), lambda b, i: (b, 0, 0))] * 2
        kv_args = [ka, va]
    return pl.pallas_call(
        functools.partial(_attn_a_body, latent=latent, nblk=nblk),
        grid=(bsz, nblk),
        in_specs=[pl.BlockSpec(memory_space=pltpu.SMEM),
                  pl.BlockSpec((1, BLOCK, QA_COLS), lambda b, i: (b, i, 0))] + kv_specs,
        out_specs=pl.BlockSpec((1, BLOCK, W_A), lambda b, i: (b, i, 0)),
        out_shape=jax.ShapeDtypeStruct((bsz, seq, W_A), BF16),
        compiler_params=_cparams("parallel", "parallel"),
        name="attn_a_latent" if latent else "attn_a_ctx",
    )(sink, qa, *kv_args)


def _attn_b_body(lam_ref, gain_ref, q_ref, *refs, part_lens, lam_init, kc):
    npart = len(part_lens)
    k_refs, v_refs = refs[:npart], refs[npart:2 * npart]
    o_ref, s_scr, vm_scr = refs[2 * npart:]
    tq = q_ref.shape[1]
    chunks = []
    col = 0
    for p, plen in enumerate(part_lens):
        step = min(kc, plen)
        for start in range(0, plen, step):
            chunks.append((p, start, col, step))
            col += step

    @pl.when(pl.program_id(2) == 0)
    def _():
        off = 0
        for p, plen in enumerate(part_lens):
            v = v_refs[p][0].astype(BF16)
            lane_v = lax.broadcasted_iota(jnp.int32, (plen, LANES), 1)
            for h in range(2):
                own = (lane_v >= h * HD_B) & (lane_v < (h + 1) * HD_B)
                ones = jnp.where(lane_v == (1 - h) * HD_B, 1.0, 0.0).astype(BF16)
                vm_scr[h, off:off + plen, :] = jnp.where(own, v, ones)
            off += plen

    lv = lam_ref[...]
    lam = (jnp.exp(jnp.sum(lv[0:1] * lv[1:2], axis=-1, keepdims=True))
           - jnp.exp(jnp.sum(lv[2:3] * lv[3:4], axis=-1, keepdims=True)) + lam_init)
    q = q_ref[0]
    lane_q = lax.broadcasted_iota(jnp.int32, (tq, LANES), 1)
    lane_2 = lax.broadcasted_iota(jnp.int32, (2 * tq, LANES), 1)
    total = jnp.zeros((tq, LANES), F32)
    for h in range(2):
        q2 = jnp.concatenate(
            [jnp.where((lane_q >= h * HD_B + c * DC_B) & (lane_q < h * HD_B + (c + 1) * DC_B), q, jnp.zeros_like(q))
             for c in range(2)], axis=0)
        macc = None
        for p, start, col, step in chunks:
            s = _dot_nt(q2, k_refs[p][0, start:start + step, :].astype(BF16))
            s_scr[:, col:col + step] = s
            for j in range(step // LANES):
                t = s[:, j * LANES:(j + 1) * LANES]
                macc = t if macc is None else jnp.maximum(macc, t)
        m = jnp.max(macc, axis=-1, keepdims=True)
        acc = jnp.zeros((2 * tq, LANES), F32)
        for p, start, col, step in chunks:
            e = jnp.exp2(s_scr[:, col:col + step] - m).astype(BF16)
            acc = acc + _dot(e, vm_scr[h, col:col + step, :])
        den = jnp.sum(jnp.where(lane_2 == (1 - h) * HD_B, acc, 0.0), axis=-1, keepdims=True)
        o2 = acc / den
        own = (lane_q >= h * HD_B) & (lane_q < (h + 1) * HD_B)
        total = total + jnp.where(own, o2[:tq] - lam * o2[tq:], 0.0)
    sq = total * total
    ss_lo = jnp.sum(jnp.where(lane_q < HD_B, sq, 0.0), axis=-1, keepdims=True)
    ss_hi = jnp.sum(jnp.where(lane_q >= HD_B, sq, 0.0), axis=-1, keepdims=True)
    rinv = jnp.where(lane_q < HD_B, lax.rsqrt(ss_lo * (1.0 / HD_B) + EPS), lax.rsqrt(ss_hi * (1.0 / HD_B) + EPS))
    o_ref[0] = (total * rinv * gain_ref[...] * (1.0 - lam_init)).astype(o_ref.dtype)


def _attn_b(qb, k_parts, v_parts, lam_b, gain, lam_init):
    bsz, seq, _ = qb.shape
    tq = 256
    part_lens = tuple(k.shape[1] for k in k_parts)
    lk = sum(part_lens)
    kv_specs = [pl.BlockSpec((1, n, LANES), lambda b, hp, i: (b, 0, hp)) for n in part_lens]
    return pl.pallas_call(
        functools.partial(_attn_b_body, part_lens=part_lens, lam_init=lam_init, kc=512),
        grid=(bsz, W_B // LANES, seq // tq),
        in_specs=[pl.BlockSpec((4, DC_B), lambda b, hp, i: (0, 0)),
                  pl.BlockSpec((1, LANES), lambda b, hp, i: (0, 0)),
                  pl.BlockSpec((1, tq, LANES), lambda b, hp, i: (b, i, hp))] + kv_specs + kv_specs,
        out_specs=pl.BlockSpec((1, tq, LANES), lambda b, hp, i: (b, i, hp)),
        out_shape=jax.ShapeDtypeStruct((bsz, seq, W_B), BF16),
        scratch_shapes=[pltpu.VMEM((2 * tq, lk), F32), pltpu.VMEM((2, lk, LANES), BF16)],
        compiler_params=_cparams("parallel", "parallel", "arbitrary"),
        name="attn_b_latent" if len(k_parts) > 1 else "attn_b_ctx",
    )(lam_b, gain, qb, *k_parts, *v_parts)


def _ssm_body(u_ref, m_ref, g_ref, cc_ref, a_ref, h0_ref, y_ref, fin_ref, s_scr, h_scr, *, nb, nc):
    u = u_ref[0]
    col = lambda k: slice(k * LANES, (k + 1) * LANES)
    s = _dot(u, g_ref[0])
    for k in range(4):
        s_scr[k] = s[:, col(k)]
    a = a_ref[0]
    afr, afi, abr, abi = (jnp.broadcast_to(a[k:k + 1], (nb, LANES)) for k in range(4))
    h0 = h0_ref[0]

    def step(c, carry):
        fr, fi, br, bi = carry
        rf = pl.ds(c, nb, stride=nc)
        rb = pl.ds(nc - 1 - c, nb, stride=nc)
        h_scr[0, rf, :] = fr
        h_scr[1, rf, :] = fi
        h_scr[2, rb, :] = br
        h_scr[3, rb, :] = bi
        nfr = afr * fr - afi * fi + s_scr[0, rf, :]
        nfi = afr * fi + afi * fr + s_scr[1, rf, :]
        nbr = abr * br - abi * bi + s_scr[2, rb, :]
        nbi = abr * bi + abi * br + s_scr[3, rb, :]
        return nfr, nfi, nbr, nbi

    fin = lax.fori_loop(0, nc, step, tuple(h0[:, col(k)] for k in range(4)))
    for k in range(4):
        fin_ref[0, :, col(k)] = fin[k]
    hin = jnp.concatenate([h_scr[k] for k in range(4)], axis=1).astype(BF16)
    y = _dot(u, m_ref[0]) + _dot(hin, cc_ref[0])
    y_ref[0] = y.astype(y_ref.dtype)


def _ssm(u_rows, mats, h0, nb):
    npair, rows, w = u_rows.shape
    nc = rows // nb
    mat_spec = pl.BlockSpec((1, w, w), lambda p: (p, 0, 0))
    return pl.pallas_call(
        functools.partial(_ssm_body, nb=nb, nc=nc),
        grid=(npair,),
        in_specs=[pl.BlockSpec((1, rows, w), lambda p: (p, 0, 0)), mat_spec, mat_spec, mat_spec,
                  pl.BlockSpec((1, 4, LANES), lambda p: (p, 0, 0)),
                  pl.BlockSpec((1, nb, w), lambda p: (p, 0, 0))],
        out_specs=[pl.BlockSpec((1, rows, w), lambda p: (p, 0, 0)),
                   pl.BlockSpec((1, nb, w), lambda p: (p, 0, 0))],
        out_shape=[jax.ShapeDtypeStruct((npair, rows, w), BF16),
                   jax.ShapeDtypeStruct((npair, nb, w), F32)],
        scratch_shapes=[pltpu.VMEM((4, rows, LANES), F32), pltpu.VMEM((4, rows, LANES), F32)],
        compiler_params=_cparams("parallel"),
        name="ssm_scan",
    )(u_rows, mats["m"], mats["g"], mats["cc"], mats["a16"], h0)


def _ssm_matrices(lp):
    t = SSM_T
    ks = jnp.arange(t + 1, dtype=F32)
    dirs = []
    for d in range(2):
        lam = lax.complex(lp["ssm_lam_re"][d].astype(F32), lp["ssm_lam_im"][d].astype(F32))
        dt = jnp.exp(lp["ssm_log_dt"][d].astype(F32))[:, None]
        a_bar = jnp.exp(lam * dt)
        b_bar = ((a_bar - 1.0) / lam)[..., None] * lax.complex(lp["ssm_b_re"][d].astype(F32),
                                                               lp["ssm_b_im"][d].astype(F32))
        c_mat = lax.complex(lp["ssm_c_re"][d].astype(F32), lp["ssm_c_im"][d].astype(F32))
        pw = jnp.exp((lam * dt)[None] * ks[:, None, None].astype(jnp.complex64))
        kern = jnp.real(jnp.einsum("gop,kgp,gpi->gkoi", c_mat, pw[:t], b_bar))
        dirs.append((pw, b_bar, c_mat, kern))
    (pw_f, bb_f, cm_f, k_f), (pw_b, bb_b, cm_b, k_b) = dirs
    s_idx = jnp.arange(t)[:, None]
    t_idx = jnp.arange(t)[None, :]
    lag_f = jnp.clip(t_idx - s_idx, 0, t - 1)
    lag_b = jnp.clip(s_idx - t_idx, 0, t - 1)
    m5 = (jnp.where((t_idx >= s_idx)[None, :, :, None, None], k_f[:, lag_f], 0.0)
          + jnp.where((s_idx >= t_idx)[None, :, :, None, None], k_b[:, lag_b], 0.0))
    eye_t = jnp.eye(t, dtype=F32)
    eye_c = jnp.eye(SSM_CH, dtype=F32)
    m5 = m5 + eye_t[None, :, :, None, None] * eye_c[None, None, None] * lp["ssm_d"].astype(F32)[:, None, None, :, None]
    m_g = m5.transpose(0, 1, 4, 2, 3).reshape(G_C, t * SSM_CH, t * SSM_CH)
    g_f = pw_f[t - 1 - jnp.arange(t)].transpose(1, 0, 2)[:, :, None, :] * bb_f.transpose(0, 2, 1)[:, None]
    g_b = pw_b[jnp.arange(t)].transpose(1, 0, 2)[:, :, None, :] * bb_b.transpose(0, 2, 1)[:, None]
    g_g = jnp.stack([jnp.real(g_f), jnp.imag(g_f), jnp.real(g_b), jnp.imag(g_b)], axis=3)
    z_f = cm_f.transpose(0, 2, 1)[:, :, None, :] * pw_f[1 + jnp.arange(t)].transpose(1, 2, 0)[:, :, :, None]
    z_b = cm_b.transpose(0, 2, 1)[:, :, None, :] * pw_b[t - jnp.arange(t)].transpose(1, 2, 0)[:, :, :, None]
    cc_g = jnp.stack([jnp.real(z_f), -jnp.imag(z_f), jnp.real(z_b), -jnp.imag(z_b)], axis=1)
    eye2 = jnp.eye(2, dtype=F32)
    ch = SSM_CH
    m_p = jnp.einsum("pgsitc,gh->psgithc", m_g.reshape(N_PAIR, 2, t, ch, t, ch), eye2).reshape(N_PAIR, SSM_ROW, SSM_ROW)
    g_p = jnp.einsum("pgsiwq,gh->psgiwhq", g_g.reshape(N_PAIR, 2, t, ch, 4, P_C), eye2).reshape(N_PAIR, SSM_ROW, 8 * P_C)
    cc_p = jnp.einsum("pgwqtc,gh->pwgqthc", cc_g.reshape(N_PAIR, 2, 4, P_C, t, ch), eye2).reshape(N_PAIR, 8 * P_C, SSM_ROW)
    a16 = jnp.stack([jnp.real(pw_f[t]), jnp.imag(pw_f[t]), jnp.real(pw_b[t]), jnp.imag(pw_b[t])], axis=0)
    a16 = a16.reshape(4, N_PAIR, 2 * P_C).transpose(1, 0, 2)
    return dict(m=m_p.astype(BF16), g=g_p.astype(BF16), cc=cc_p.astype(BF16), a16=a16)


def _ssm_state_rows(s_re, s_im):
    bsz = s_re.shape[0]
    parts = [s_re[:, 0], s_im[:, 0], s_re[:, 1], s_im[:, 1]]
    st = jnp.stack([p.reshape(bsz, N_PAIR, 2 * P_C) for p in parts], axis=2)
    return st.transpose(1, 0, 2, 3).reshape(N_PAIR, bsz, 8 * P_C).astype(F32)


def _ssm_state_unrows(fin):
    npair, bsz, _ = fin.shape
    st = fin.reshape(npair, bsz, 4, 2, P_C).transpose(1, 2, 0, 3, 4).reshape(bsz, 4, G_C, P_C)
    return jnp.stack([st[:, 0], st[:, 2]], axis=1), jnp.stack([st[:, 1], st[:, 3]], axis=1)


def _route(scores, bias):
    tm = scores.shape[1]
    biased = scores + bias
    iota8 = lax.broadcasted_iota(jnp.int32, (PER_GROUP, tm), 0)
    grp = [biased[PER_GROUP * g:PER_GROUP * (g + 1)] for g in range(N_EXP_GROUPS)]
    gscore = []
    for v in grp:
        m1 = jnp.max(v, axis=0, keepdims=True)
        first = jnp.min(jnp.where(v == m1, iota8, PER_GROUP), axis=0, keepdims=True)
        m2 = jnp.max(jnp.where(iota8 == first, -jnp.inf, v), axis=0, keepdims=True)
        gscore.append(m1 + m2)
    masked = []
    for g in range(N_EXP_GROUPS):
        rank = jnp.zeros((1, tm), jnp.int32)
        for o in range(N_EXP_GROUPS):
            if o == g:
                continue
            ahead = (gscore[o] >= gscore[g]) if o < g else (gscore[o] > gscore[g])
            rank = rank + jnp.where(ahead, 1, 0)
        masked.append(jnp.where(rank < TOPK_GROUPS, grp[g], -jnp.inf))
    chosen = [None] * N_EXP_GROUPS
    for _ in range(TOP_K):
        best = masked[0]
        for v in masked[1:]:
            best = jnp.maximum(best, v)
        best = jnp.max(best, axis=0, keepdims=True)
        first = jnp.full((1, tm), N_EXPERTS, jnp.int32)
        for g, v in enumerate(masked):
            cand = jnp.min(jnp.where(v == best, iota8 + PER_GROUP * g, N_EXPERTS), axis=0, keepdims=True)
            first = jnp.minimum(first, cand)
        for g in range(N_EXP_GROUPS):
            hit = (iota8 + PER_GROUP * g) == first
            chosen[g] = hit if chosen[g] is None else (chosen[g] | hit)
            masked[g] = jnp.where(hit, -jnp.inf, masked[g])
    w = [jnp.where(chosen[g], scores[PER_GROUP * g:PER_GROUP * (g + 1)], 0.0) for g in range(N_EXP_GROUPS)]
    wsum = w[0]
    for v in w[1:]:
        wsum = wsum + v
    wsum = jnp.sum(wsum, axis=0, keepdims=True)
    return jnp.concatenate([v / wsum * ROUTED_SCALE for v in w], axis=0)


def _post_body(x_ref, oa_ref, ob_ref, y_ref, mod_ref, wglu_ref, wout_ref, g2_ref, wrh_ref, wrl_ref, br_ref,
               x1_ref, h2_ref, gt_ref, y_scr):
    d = D_MODEL
    tm = x_ref.shape[1]
    pw = 2 * SSM_CH
    for t in range(SSM_T):
        for blk in range(W_C // LANES):
            piece = jnp.concatenate([y_ref[blk * (LANES // pw) + pp, :, t * pw:(t + 1) * pw].astype(F32)
                                     for pp in range(LANES // pw)], axis=1)
            y_scr[blk, pl.ds(t, tm // SSM_T, stride=SSM_T), :] = piece
    g = jax.nn.gelu(jnp.concatenate([y_scr[blk] for blk in range(W_C // LANES)], axis=1))
    oc = g * jax.nn.sigmoid(_dot(g.astype(BF16), wglu_ref[...]))
    mix = (_dot(oa_ref[0], wout_ref[0:W_A]) + _dot(ob_ref[0], wout_ref[W_A:W_A + W_B])
           + _dot(oc.astype(BF16), wout_ref[W_A + W_B:]))
    mod = mod_ref[0]
    x1 = x_ref[0] + mod[:, 2 * d:3 * d] * mix
    x1_ref[0] = x1
    xn = x1 * lax.rsqrt(jnp.mean(x1 * x1, axis=-1, keepdims=True) + EPS) * g2_ref[...]
    h2 = xn * (1.0 + mod[:, 4 * d:5 * d]) + mod[:, 3 * d:4 * d]
    h_hi, h_lo = _split_bf16(h2)
    h2_ref[0] = h_hi
    logits = _dot_nt(wrh_ref[...], h_hi) + _dot_nt(wrh_ref[...], h_lo) + _dot_nt(wrl_ref[...], h_hi)
    gt_ref[0] = _route(jax.nn.sigmoid(logits), br_ref[...])


def _post_mix(x, oa, ob, y, mod, w_glu, w_out, g2, wr_hi, wr_lo, b_r):
    bsz, seq, d = x.shape
    tm = 256
    bm = mod.shape[0]
    mod_idx = (lambda b, i: (b, 0, 0)) if bm > 1 else (lambda b, i: (0, 0, 0))
    const2 = lambda b, i: (0, 0)
    tok = lambda w: pl.BlockSpec((1, tm, w), lambda b, i: (b, i, 0))
    nt = seq // tm
    return pl.pallas_call(
        _post_body,
        grid=(bsz, nt),
        in_specs=[tok(d), tok(W_A), tok(W_B),
                  pl.BlockSpec((N_PAIR, tm // SSM_T, SSM_ROW), lambda b, i: (0, b * nt + i, 0)),
                  pl.BlockSpec((1, 1, 6 * d), mod_idx),
                  pl.BlockSpec((W_C, W_C), const2),
                  pl.BlockSpec((d, d), const2),
                  pl.BlockSpec((1, d), const2),
                  pl.BlockSpec((N_EXPERTS, d), const2),
                  pl.BlockSpec((N_EXPERTS, d), const2),
                  pl.BlockSpec((N_EXPERTS, 1), const2)],
        out_specs=[tok(d), tok(d), pl.BlockSpec((1, N_EXPERTS, tm), lambda b, i: (b, 0, i))],
        out_shape=[jax.ShapeDtypeStruct((bsz, seq, d), F32),
                   jax.ShapeDtypeStruct((bsz, seq, d), BF16),
                   jax.ShapeDtypeStruct((bsz, N_EXPERTS, seq), F32)],
        scratch_shapes=[pltpu.VMEM((W_C // LANES, tm, LANES), F32)],
        compiler_params=_cparams("parallel", "parallel"),
        name="post_mix",
    )(x, oa, ob, y, mod, w_glu, w_out, g2, wr_hi, wr_lo, b_r)


def _moe_body(x1_ref, h_ref, gate_ref, g2_ref, w1_ref, w3_ref, w2_ref, ex_ref, s1_ref, s3_ref, s2_ref,
              o_ref, acc_ref):
    j = pl.program_id(1)
    h = h_ref[...]

    @pl.when(j == 0)
    def _():
        a = _dot(h, s1_ref[...])
        acc_ref[...] = _dot((a * jax.nn.sigmoid(a) * _dot(h, s3_ref[...])).astype(BF16), s2_ref[...])

    ne = w1_ref.shape[0]
    a = _dot(h, jnp.concatenate([w1_ref[e] for e in range(ne)], axis=1))
    b = _dot(h, jnp.concatenate([w3_ref[e] for e in range(ne)], axis=1))
    gexp = _dot(jnp.concatenate(_split_bf16(gate_ref[...]), axis=1), ex_ref[...])
    hid = a * jax.nn.sigmoid(a) * b * gexp
    acc_ref[...] += _dot(hid.astype(BF16), w2_ref[...])

    @pl.when(j == pl.num_programs(1) - 1)
    def _():
        o_ref[...] = x1_ref[...] + g2_ref[0] * acc_ref[...]


def _moe(x1, h2, gates, mod, seq, w1, w3, w2, expand, ws1, ws3, ws2):
    tokens, d = x1.shape
    bm = mod.shape[0]
    span = seq if bm > 1 else tokens
    tm = next(t for t in (1024, 512, 256) if span % t == 0)
    per_b = seq // tm if bm > 1 else 1
    mod_idx = (lambda i, j: (i // per_b, 0, 5)) if bm > 1 else (lambda i, j: (0, 0, 5))
    ne = 4
    fc = ne * F_EXP
    hidden = w2.shape[0]
    const2 = lambda i, j: (0, 0)
    return pl.pallas_call(
        _moe_body,
        grid=(tokens // tm, hidden // fc),
        in_specs=[pl.BlockSpec((tm, d), lambda i, j: (i, 0)),
                  pl.BlockSpec((tm, d), lambda i, j: (i, 0)),
                  pl.BlockSpec((tm, N_EXPERTS), lambda i, j: (i, 0)),
                  pl.BlockSpec((1, 1, d), mod_idx),
                  pl.BlockSpec((ne, d, F_EXP), lambda i, j: (j, 0, 0)),
                  pl.BlockSpec((ne, d, F_EXP), lambda i, j: (j, 0, 0)),
                  pl.BlockSpec((fc, d), lambda i, j: (j, 0)),
                  pl.BlockSpec((2 * N_EXPERTS, fc), lambda i, j: (0, j)),
                  pl.BlockSpec((d, F_SHARED), const2),
                  pl.BlockSpec((d, F_SHARED), const2),
                  pl.BlockSpec((F_SHARED, d), const2)],
        out_specs=pl.BlockSpec((tm, d), lambda i, j: (i, 0)),
        out_shape=jax.ShapeDtypeStruct((tokens, d), F32),
        scratch_shapes=[pltpu.VMEM((tm, d), F32)],
        compiler_params=_cparams("parallel", "arbitrary"),
        name="moe",
    )(x1, h2, gates, mod, w1, w3, w2, expand, ws1, ws3, ws2)


def _rope_tables(seq):
    pos = jnp.arange(seq)
    row = (pos // GRID_W).astype(F32)[:, None]
    colp = (pos % GRID_W).astype(F32)[:, None]
    lane = jnp.arange(LANES)

    def table(width):
        half, quarter = width // 2, width // 4
        i = lane % width
        freq = ROPE_BASE ** (-(2.0 * (i % quarter).astype(F32)) / half)
        ang = jnp.where((i // half) == 0, row, colp) * freq[None, :]
        sign = jnp.where((i % half) < quarter, -1.0, 1.0)
        return jnp.cos(ang), jnp.sin(ang) * sign[None, :]

    ca, sa = table(HD_A)
    cb, sb = table(DC_B)
    return ca, sa, cb, sb


def _prep_layer(p, l):
    d = D_MODEL
    w_in = p["w_in"][l]
    kvw = KV_A * HD_A
    qa = w_in[:, :W_A].reshape(d, H_A, HD_A)
    qa_pad = jnp.zeros((d, H_A, 2, HD_A), w_in.dtype)
    for h in range(H_A):
        qa_pad = qa_pad.at[:, h, h // GQ_A].set(qa[:, h])
    w_in_p = jnp.concatenate([qa_pad.reshape(d, QA_COLS), w_in[:, W_A:]], axis=1).astype(BF16)
    gains = jnp.stack([jnp.tile(p["q_norm_a"][l], LANES // HD_A) * (HD_A ** -0.5),
                       jnp.tile(p["k_norm_a"][l], LANES // HD_A),
                       jnp.tile(p["q_norm_b"][l], LANES // DC_B) * (DC_B ** -0.5 * LOG2E),
                       jnp.tile(p["k_norm_b"][l], LANES // DC_B)], axis=0).astype(F32)
    lp = {k: p[k][l] for k in ("ssm_lam_re", "ssm_lam_im", "ssm_log_dt", "ssm_b_re", "ssm_b_im",
                               "ssm_c_re", "ssm_c_im", "ssm_d")}
    wr_hi, wr_lo = _split_bf16(p["w_router"][l].T.astype(F32))
    hidden = N_EXPERTS * F_EXP
    return dict(
        w_in_p=w_in_p, gains=gains,
        g1=p["norm1_g"][l].reshape(1, d).astype(F32), g2=p["norm2_g"][l].reshape(1, d).astype(F32),
        sink=p["sink_a"][l].astype(F32), lam_b=p["lam_b"][l].astype(F32),
        subln=jnp.tile(p["subln_b"][l], LANES // HD_B).reshape(1, LANES).astype(F32),
        lam_init=0.8 - 0.6 * math.exp(-0.3 * l),
        ssm=_ssm_matrices(lp),
        w_glu=p["w_glu"][l].astype(BF16), w_out=p["w_out"][l].astype(BF16),
        wr_hi=wr_hi, wr_lo=wr_lo, b_r=p["b_router"][l].reshape(N_EXPERTS, 1).astype(F32),
        w1=p["w_e1"][l].astype(BF16), w3=p["w_e3"][l].astype(BF16),
        w2=p["w_e2"][l].reshape(hidden, d).astype(BF16),
        ws1=p["w_s1"][l].astype(BF16), ws3=p["w_s3"][l].astype(BF16), ws2=p["w_s2"][l].astype(BF16),
    )


def _trunk_layer(x, mod, lw, consts, ctx):
    bsz, seq, d = x.shape
    latent = ctx is not None
    rope = consts["rope"] if latent else None
    kv_dtype = BF16 if latent else F32
    qa, ka, va, qb, kb, vb, u = _inproj(x, mod, lw["g1"], lw["w_in_p"], lw["gains"],
                                        consts["seg64"], consts["seg32"], rope, kv_dtype)
    if latent:
        oa = _attn_a(qa, ka, va, lw["sink"], (ctx["ak"], ctx["av"]))
        ob = _attn_b(qb, [kb, ctx["bk"]], [vb, ctx["bv"]], lw["lam_b"], lw["subln"], lw["lam_init"])
        h0 = ctx["h0"]
    else:
        oa = _attn_a(qa, ka, va, lw["sink"], None)
        ob = _attn_b(qb, [kb], [vb], lw["lam_b"], lw["subln"], lw["lam_init"])
        h0 = jnp.zeros((N_PAIR, bsz, 8 * P_C), F32)
    y_rows, fin = _ssm(u, lw["ssm"], h0, bsz)
    x1, h2, gates_t = _post_mix(x, oa, ob, y_rows, mod, lw["w_glu"], lw["w_out"], lw["g2"],
                                lw["wr_hi"], lw["wr_lo"], lw["b_r"])
    gates = gates_t.transpose(0, 2, 1).reshape(bsz * seq, N_EXPERTS)
    out = _moe(x1.reshape(bsz * seq, d), h2.reshape(bsz * seq, d), gates, mod, seq,
               lw["w1"], lw["w3"], lw["w2"], consts["expand"], lw["ws1"], lw["ws3"], lw["ws2"])
    return out.reshape(bsz, seq, d), (ka, va, kb, vb, fin)


def kernel(x_prompt, x_sample, cache_a_k, cache_a_v, cache_b_k, cache_b_v, state_ssm_re, state_ssm_im, c, c_ctx, norm1_g, norm2_g, w_ada, b_ada, w_in, q_norm_a, k_norm_a, sink_a, q_norm_b, k_norm_b, lam_b, subln_b, ssm_lam_re, ssm_lam_im, ssm_log_dt, ssm_b_re, ssm_b_im, ssm_c_re, ssm_c_im, ssm_d, w_glu, w_out, w_router, b_router, w_e1, w_e3, w_e2, w_s1, w_s3, w_s2):
    p = dict(norm1_g=norm1_g, norm2_g=norm2_g, w_in=w_in, q_norm_a=q_norm_a, k_norm_a=k_norm_a, sink_a=sink_a,
             q_norm_b=q_norm_b, k_norm_b=k_norm_b, lam_b=lam_b, subln_b=subln_b,
             ssm_lam_re=ssm_lam_re, ssm_lam_im=ssm_lam_im, ssm_log_dt=ssm_log_dt, ssm_b_re=ssm_b_re,
             ssm_b_im=ssm_b_im, ssm_c_re=ssm_c_re, ssm_c_im=ssm_c_im, ssm_d=ssm_d, w_glu=w_glu, w_out=w_out,
             w_router=w_router, b_router=b_router, w_e1=w_e1, w_e3=w_e3, w_e2=w_e2,
             w_s1=w_s1, w_s3=w_s3, w_s2=w_s2)
    depth = w_in.shape[0]
    bsz, seq, d = x_prompt.shape
    dbsz, dseq, _ = x_sample.shape
    past = cache_a_k.shape[3]

    mod_rows = 16
    cvec = jnp.concatenate([c.astype(F32), c_ctx.astype(F32)[None],
                            jnp.zeros((mod_rows - dbsz - 1, d), F32)], axis=0)
    mods = _modulation(cvec, w_ada.astype(F32), b_ada.astype(F32))

    lane = jnp.arange(LANES)
    hidden = N_EXPERTS * F_EXP
    consts = dict(
        rope=_rope_tables(dseq),
        seg64=(lane[:, None] // HD_A == lane[None, :] // HD_A).astype(BF16),
        seg32=(lane[:, None] // DC_B == lane[None, :] // DC_B).astype(BF16),
        expand=(jnp.arange(2 * N_EXPERTS)[:, None] % N_EXPERTS == jnp.arange(hidden)[None, :] // F_EXP).astype(BF16),
    )

    xp, xs = x_prompt, x_sample
    ak, av, bk, bv, sre, sim = [], [], [], [], [], []
    for l in range(depth):
        lw = _prep_layer(p, l)
        mod_lat = mods[l, :dbsz][:, None, :]
        mod_ctx = mods[l, dbsz:dbsz + 1][:, None, :]
        xp, (k_a, v_a, k_b, v_b, fin) = _trunk_layer(xp, mod_ctx, lw, consts, None)
        ak.append(k_a.reshape(bsz, seq, KV_A, HD_A).transpose(0, 2, 1, 3))
        av.append(v_a.reshape(bsz, seq, KV_A, HD_A).transpose(0, 2, 1, 3))
        bk.append(k_b.reshape(bsz, seq, H_B, 2, DC_B).transpose(0, 2, 3, 1, 4))
        bv.append(v_b.reshape(bsz, seq, H_B, HD_B).transpose(0, 2, 1, 3))
        f_re, f_im = _ssm_state_unrows(fin)
        sre.append(f_re)
        sim.append(f_im)
        ctx = dict(
            ak=cache_a_k[:, l].transpose(0, 2, 1, 3).reshape(dbsz, past, KV_A * HD_A).astype(BF16),
            av=cache_a_v[:, l].transpose(0, 2, 1, 3).reshape(dbsz, past, KV_A * HD_A).astype(BF16),
            bk=cache_b_k[:, l].transpose(0, 3, 1, 2, 4).reshape(dbsz, past, W_B).astype(BF16),
            bv=cache_b_v[:, l].transpose(0, 2, 1, 3).reshape(dbsz, past, W_B).astype(BF16),
            h0=_ssm_state_rows(state_ssm_re[:, l], state_ssm_im[:, l]),
        )
        xs, _ = _trunk_layer(xs, mod_lat, lw, consts, ctx)
    return (xp, xs, jnp.stack(ak, axis=1), jnp.stack(av, axis=1), jnp.stack(bk, axis=1),
            jnp.stack(bv, axis=1), jnp.stack(sre, axis=1), jnp.stack(sim, axis=1))
```

```python
import functools
import math

import jax
import jax.numpy as jnp
from jax import lax
from jax.experimental import pallas as pl
from jax.experimental.pallas import tpu as pltpu

F32 = jnp.float32
BF16 = jnp.bfloat16

D_MODEL = 1024
GRID_W = 64
BLOCK = 128
H_A, KV_A, HD_A = 6, 2, 64
GQ_A = H_A // KV_A
W_A = H_A * HD_A
H_B, HD_B = 4, 64
DC_B = HD_B // 2
W_B = H_B * HD_B
SSM_CH = 16
W_C = D_MODEL - W_A - W_B
G_C = W_C // SSM_CH
P_C = 64
N_EXPERTS, TOP_K, F_EXP, F_SHARED = 64, 6, 128, 256
N_EXP_GROUPS, TOPK_GROUPS = 8, 4
PER_GROUP = N_EXPERTS // N_EXP_GROUPS
ROUTED_SCALE = 2.5
ROPE_BASE = 10000.0
EPS = 1e-6
NEG = -1e30
LOG2E = 1.4426950408889634

LANES = 128
SSM_T = 16
N_PAIR = G_C // 2
SSM_ROW = 2 * SSM_T * SSM_CH
QA_COLS = H_A * LANES
IN_COLS_P = QA_COLS + 2 * KV_A * HD_A + 3 * W_B + W_C
VMEM_LIMIT = 56 << 20


def _cparams(*sem):
    return pltpu.CompilerParams(dimension_semantics=sem, vmem_limit_bytes=VMEM_LIMIT)


def _dot(a, b):
    return jnp.dot(a, b, preferred_element_type=F32)


def _dot_nt(a, b):
    return lax.dot_general(a, b, (((1,), (1,)), ((), ())), preferred_element_type=F32)


def _split_bf16(x):
    hi = x.astype(BF16)
    lo = (x - hi.astype(F32)).astype(BF16)
    return hi, lo


def _mod_body(c_ref, w_ref, b_ref, o_ref):
    c = c_ref[...]
    s = c * jax.nn.sigmoid(c)
    s_hi, s_lo = _split_bf16(s)
    w_hi, w_lo = _split_bf16(w_ref[0])
    o_ref[0] = _dot(s_hi, w_hi) + _dot(s_lo, w_hi) + _dot(s_hi, w_lo) + b_ref[0]


def _modulation(cvec, w_ada, b_ada):
    depth, d, n = w_ada.shape
    rows = cvec.shape[0]
    tn = 768
    return pl.pallas_call(
        _mod_body,
        grid=(depth, n // tn),
        in_specs=[pl.BlockSpec((rows, d), lambda l, j: (0, 0)),
                  pl.BlockSpec((1, d, tn), lambda l, j: (l, 0, j)),
                  pl.BlockSpec((1, 1, tn), lambda l, j: (l, 0, j))],
        out_specs=pl.BlockSpec((1, rows, tn), lambda l, j: (l, 0, j)),
        out_shape=jax.ShapeDtypeStruct((depth, rows, n), F32),
        compiler_params=_cparams("parallel", "parallel"),
        name="adaln_mod",
    )(cvec, w_ada, b_ada.reshape(depth, 1, n))


def _inproj_body(*refs, latent):
    if latent:
        (x_ref, mod_ref, g1_ref, w_ref, gains_ref, s64_ref, s32_ref, ca_ref, sa_ref, cb_ref, sb_ref,
         qa_ref, ka_ref, va_ref, qb_ref, kb_ref, vb_ref, u_ref, u_scr) = refs
    else:
        (x_ref, mod_ref, g1_ref, w_ref, gains_ref, s64_ref, s32_ref,
         qa_ref, ka_ref, va_ref, qb_ref, kb_ref, vb_ref, u_ref, u_scr) = refs
    d = D_MODEL
    x = x_ref[0]
    mod = mod_ref[0]
    xn = x * lax.rsqrt(jnp.mean(x * x, axis=-1, keepdims=True) + EPS) * g1_ref[...]
    h = xn * (1.0 + mod[:, d:2 * d]) + mod[:, 0:d]
    acc = _dot(h.astype(BF16), w_ref[...])

    tm = x.shape[0]
    lane = lax.broadcasted_iota(jnp.int32, (tm, LANES), 1)
    first_a = (lane % 32) < 16
    first_b = (lane % 16) < 8

    def normed(xb, seg_ref, inv_n, gain):
        hi, lo = _split_bf16(xb * xb)
        ss = _dot(hi, seg_ref[...]) + _dot(lo, seg_ref[...])
        return xb * lax.rsqrt(ss * inv_n + EPS) * gain

    def rope_a(y):
        if not latent:
            return y
        sw = jnp.where(first_a, pltpu.roll(y, LANES - 16, 1), pltpu.roll(y, 16, 1))
        return y * ca_ref[...] + sw * sa_ref[...]

    def rope_b(y):
        if not latent:
            return y
        sw = jnp.where(first_b, pltpu.roll(y, LANES - 8, 1), pltpu.roll(y, 8, 1))
        return y * cb_ref[...] + sw * sb_ref[...]

    gains = gains_ref[...]
    off = 0
    for b in range(H_A):
        y = normed(acc[:, off:off + LANES], s64_ref, 1.0 / HD_A, gains[0:1])
        qa_ref[0, :, b * LANES:(b + 1) * LANES] = rope_a(y).astype(qa_ref.dtype)
        off += LANES
    y = normed(acc[:, off:off + LANES], s64_ref, 1.0 / HD_A, gains[1:2])
    ka_ref[0] = rope_a(y).astype(ka_ref.dtype)
    off += LANES
    va_ref[0] = acc[:, off:off + LANES].astype(va_ref.dtype)
    off += LANES
    for b in range(W_B // LANES):
        y = normed(acc[:, off:off + LANES], s32_ref, 1.0 / DC_B, gains[2:3])
        qb_ref[0, :, b * LANES:(b + 1) * LANES] = rope_b(y).astype(qb_ref.dtype)
        off += LANES
    for b in range(W_B // LANES):
        y = normed(acc[:, off:off + LANES], s32_ref, 1.0 / DC_B, gains[3:4])
        kb_ref[0, :, b * LANES:(b + 1) * LANES] = rope_b(y).astype(kb_ref.dtype)
        off += LANES
    vb_ref[0] = acc[:, off:off + W_B].astype(vb_ref.dtype)
    off += W_B
    for blk in range(W_C // LANES):
        u_scr[blk] = acc[:, off + blk * LANES:off + (blk + 1) * LANES]
    pw = 2 * SSM_CH
    for t in range(SSM_T):
        for blk in range(W_C // LANES):
            xt = u_scr[blk, pl.ds(t, tm // SSM_T, stride=SSM_T), :]
            for pp in range(LANES // pw):
                u_ref[blk * (LANES // pw) + pp, :, t * pw:(t + 1) * pw] = xt[:, pp * pw:(pp + 1) * pw].astype(u_ref.dtype)


def _inproj(x, mod, g1, w_in_p, gains, seg64, seg32, rope, kv_dtype):
    bsz, seq, d = x.shape
    latent = rope is not None
    tm = 512 if seq % 512 == 0 else 256
    bm = mod.shape[0]
    mod_idx = (lambda b, i: (b, 0, 0)) if bm > 1 else (lambda b, i: (0, 0, 0))
    const2 = lambda b, i: (0, 0)
    tok = lambda w: pl.BlockSpec((1, tm, w), lambda b, i: (b, i, 0))
    in_specs = [tok(d),
                pl.BlockSpec((1, 1, 6 * d), mod_idx),
                pl.BlockSpec((1, d), const2),
                pl.BlockSpec((d, IN_COLS_P), const2),
                pl.BlockSpec((4, LANES), const2),
                pl.BlockSpec((LANES, LANES), const2),
                pl.BlockSpec((LANES, LANES), const2)]
    args = [x, mod, g1, w_in_p, gains, seg64, seg32]
    if latent:
        in_specs += [pl.BlockSpec((tm, LANES), lambda b, i: (i, 0))] * 4
        args += list(rope)
    widths = (QA_COLS, KV_A * HD_A, KV_A * HD_A, W_B, W_B, W_B)
    dtypes = (BF16, kv_dtype, kv_dtype, BF16, kv_dtype, kv_dtype)
    nt = seq // tm
    rows = tm // SSM_T
    u_spec = pl.BlockSpec((N_PAIR, rows, SSM_ROW), lambda b, i: (0, b * nt + i, 0))
    u_shape = jax.ShapeDtypeStruct((N_PAIR, bsz * seq // SSM_T, SSM_ROW), BF16)
    return pl.pallas_call(
        functools.partial(_inproj_body, latent=latent),
        grid=(bsz, nt),
        in_specs=in_specs,
        out_specs=[tok(w) for w in widths] + [u_spec],
        out_shape=[jax.ShapeDtypeStruct((bsz, seq, w), dt) for w, dt in zip(widths, dtypes)] + [u_shape],
        scratch_shapes=[pltpu.VMEM((W_C // LANES, tm, LANES), F32)],
        compiler_params=_cparams("parallel", "parallel"),
        name="inproj_latent" if latent else "inproj_ctx",
    )(*args)


def _attn_a_body(sink_ref, q_ref, *refs, latent, nblk):
    o_ref = refs[-1]
    nk = (len(refs) - 1) // 2
    kcat = jnp.concatenate([r[0].astype(BF16) for r in refs[:nk]], axis=0)
    vcat = jnp.concatenate([r[0].astype(BF16) for r in refs[nk:2 * nk]], axis=0)
    rows = GQ_A * BLOCK
    cols = kcat.shape[0]
    rowi = lax.broadcasted_iota(jnp.int32, (rows, 1), 0)
    if latent:
        i = pl.program_id(1)
        r = lax.broadcasted_iota(jnp.int32, (rows, cols), 0) & (BLOCK - 1)
        c = lax.broadcasted_iota(jnp.int32, (rows, cols), 1)
        p_off = jnp.where(i > 0, 0, 2 * BLOCK)
        n_off = jnp.where(i < nblk - 1, 0, 2 * BLOCK)
        prev_ok = (c >= r + p_off) | (c >= BLOCK)
        next_ok = ((c - 2 * BLOCK + n_off) <= r) | (c < 2 * BLOCK) | (c >= 3 * BLOCK)
        valid = prev_ok & next_ok
    lane = lax.broadcasted_iota(jnp.int32, (BLOCK, LANES), 1)
    heads = []
    for j in range(KV_A):
        q3 = jnp.concatenate([q_ref[0, :, (GQ_A * j + g) * LANES:(GQ_A * j + g + 1) * LANES]
                              for g in range(GQ_A)], axis=0)
        s = _dot_nt(q3, kcat)
        if latent:
            s = jnp.where(valid, s, NEG)
        sink = jnp.where(rowi < BLOCK, sink_ref[GQ_A * j],
                         jnp.where(rowi < 2 * BLOCK, sink_ref[GQ_A * j + 1], sink_ref[GQ_A * j + 2]))
        m = jnp.maximum(jnp.max(s, axis=-1, keepdims=True), sink)
        e = jnp.exp(s - m)
        den = jnp.sum(e, axis=-1, keepdims=True) + jnp.exp(sink - m)
        o = _dot(e.astype(BF16), vcat) / den
        for g in range(GQ_A):
            heads.append((j, o[g * BLOCK:(g + 1) * BLOCK]))
    for blk in range(H_A // 2):
        (j0, o0), (j1, o1) = heads[2 * blk], heads[2 * blk + 1]
        lo = o0 if j0 == 0 else pltpu.roll(o0, HD_A, 1)
        hi = o1 if j1 == 1 else pltpu.roll(o1, HD_A, 1)
        o_ref[0, :, blk * LANES:(blk + 1) * LANES] = jnp.where(lane < HD_A, lo, hi).astype(o_ref.dtype)


def _attn_a(qa, ka, va, sink, ctx_kv):
    bsz, seq, _ = qa.shape
    nblk = seq // BLOCK
    latent = ctx_kv is not None
    kvw = KV_A * HD_A
    if latent:
        past = ctx_kv[0].shape[1]
        band = [pl.BlockSpec((1, BLOCK, kvw), lambda b, i: (b, jnp.maximum(i - 1, 0), 0)),
                pl.BlockSpec((1, BLOCK, kvw), lambda b, i: (b, i, 0)),
                pl.BlockSpec((1, BLOCK, kvw), lambda b, i: (b, jnp.minimum(i + 1, nblk - 1), 0)),
                pl.BlockSpec((1, past, kvw), lambda b, i: (b, 0, 0))]
        kv_specs = band + band
        kv_args = [ka, ka, ka, ctx_kv[0], va, va, va, ctx_kv[1]]
    else:
        kv_specs = [pl.BlockSpec((1, seq, kvw), lambda b, i: (b, 0, 0))] * 2
        kv_args = [ka, va]
    return pl.pallas_call(
        functools.partial(_attn_a_body, latent=latent, nblk=nblk),
        grid=(bsz, nblk),
        in_specs=[pl.BlockSpec(memory_space=pltpu.SMEM),
                  pl.BlockSpec((1, BLOCK, QA_COLS), lambda b, i: (b, i, 0))] + kv_specs,
        out_specs=pl.BlockSpec((1, BLOCK, W_A), lambda b, i: (b, i, 0)),
        out_shape=jax.ShapeDtypeStruct((bsz, seq, W_A), BF16),
        compiler_params=_cparams("parallel", "parallel"),
        name="attn_a_latent" if latent else "attn_a_ctx",
    )(sink, qa, *kv_args)


def _attn_b_body(lam_ref, gain_ref, q_ref, *refs, part_lens, lam_init, kc):
    npart = len(part_lens)
    k_refs, v_refs = refs[:npart], refs[npart:2 * npart]
    o_ref, s_scr, vm_scr = refs[2 * npart:]
    tq = q_ref.shape[1]
    chunks = []
    col = 0
    for p, plen in enumerate(part_lens):
        step = min(kc, plen)
        for start in range(0, plen, step):
            chunks.append((p, start, col, step))
            col += step

    @pl.when(pl.program_id(2) == 0)
    def _():
        off = 0
        for p, plen in enumerate(part_lens):
            v = v_refs[p][0].astype(BF16)
            lane_v = lax.broadcasted_iota(jnp.int32, (plen, LANES), 1)
            for h in range(2):
                own = (lane_v >= h * HD_B) & (lane_v < (h + 1) * HD_B)
                ones = jnp.where(lane_v == (1 - h) * HD_B, 1.0, 0.0).astype(BF16)
                vm_scr[h, off:off + plen, :] = jnp.where(own, v, ones)
            off += plen

    lv = lam_ref[...]
    lam = (jnp.exp(jnp.sum(lv[0:1] * lv[1:2], axis=-1, keepdims=True))
           - jnp.exp(jnp.sum(lv[2:3] * lv[3:4], axis=-1, keepdims=True)) + lam_init)
    q = q_ref[0]
    lane_q = lax.broadcasted_iota(jnp.int32, (tq, LANES), 1)
    total = jnp.zeros((tq, LANES), F32)
    for h in range(2):
        qc = [jnp.where((lane_q >= h * HD_B + c * DC_B) & (lane_q < h * HD_B + (c + 1) * DC_B), q, jnp.zeros_like(q))
              for c in range(2)]
        rows = [slice(c * tq, (c + 1) * tq) for c in range(2)]
        macc = [None, None]
        for p, start, col, step in chunks:
            kch = k_refs[p][0, start:start + step, :].astype(BF16)
            for c in range(2):
                s = _dot_nt(qc[c], kch)
                s_scr[rows[c], col:col + step] = s
                for j in range(step // LANES):
                    t = s[:, j * LANES:(j + 1) * LANES]
                    macc[c] = t if macc[c] is None else jnp.maximum(macc[c], t)
        m = [jnp.max(macc[c], axis=-1, keepdims=True) for c in range(2)]
        acc = [jnp.zeros((tq, LANES), F32) for _ in range(2)]
        for p, start, col, step in chunks:
            vch = vm_scr[h, col:col + step, :]
            for c in range(2):
                e = jnp.exp2(s_scr[rows[c], col:col + step] - m[c]).astype(BF16)
                acc[c] = acc[c] + _dot(e, vch)
        o2 = [acc[c] / jnp.sum(jnp.where(lane_q == (1 - h) * HD_B, acc[c], 0.0), axis=-1, keepdims=True)
              for c in range(2)]
        own = (lane_q >= h * HD_B) & (lane_q < (h + 1) * HD_B)
        total = total + jnp.where(own, o2[0] - lam * o2[1], 0.0)
    sq = total * total
    ss_lo = jnp.sum(jnp.where(lane_q < HD_B, sq, 0.0), axis=-1, keepdims=True)
    ss_hi = jnp.sum(jnp.where(lane_q >= HD_B, sq, 0.0), axis=-1, keepdims=True)
    rinv = jnp.where(lane_q < HD_B, lax.rsqrt(ss_lo * (1.0 / HD_B) + EPS), lax.rsqrt(ss_hi * (1.0 / HD_B) + EPS))
    o_ref[0] = (total * rinv * gain_ref[...] * (1.0 - lam_init)).astype(o_ref.dtype)


def _attn_b(qb, k_parts, v_parts, lam_b, gain, lam_init):
    bsz, seq, _ = qb.shape
    tq = 256
    part_lens = tuple(k.shape[1] for k in k_parts)
    lk = sum(part_lens)
    kv_specs = [pl.BlockSpec((1, n, LANES), lambda b, hp, i: (b, 0, hp)) for n in part_lens]
    return pl.pallas_call(
        functools.partial(_attn_b_body, part_lens=part_lens, lam_init=lam_init, kc=512),
        grid=(bsz, W_B // LANES, seq // tq),
        in_specs=[pl.BlockSpec((4, DC_B), lambda b, hp, i: (0, 0)),
                  pl.BlockSpec((1, LANES), lambda b, hp, i: (0, 0)),
                  pl.BlockSpec((1, tq, LANES), lambda b, hp, i: (b, i, hp))] + kv_specs + kv_specs,
        out_specs=pl.BlockSpec((1, tq, LANES), lambda b, hp, i: (b, i, hp)),
        out_shape=jax.ShapeDtypeStruct((bsz, seq, W_B), BF16),
        scratch_shapes=[pltpu.VMEM((2 * tq, lk), F32), pltpu.VMEM((2, lk, LANES), BF16)],
        compiler_params=_cparams("parallel", "parallel", "arbitrary"),
        name="attn_b_latent" if len(k_parts) > 1 else "attn_b_ctx",
    )(lam_b, gain, qb, *k_parts, *v_parts)


def _ssm_body(u_ref, m_ref, g_ref, cc_ref, a_ref, h0_ref, y_ref, fin_ref, s_scr, h_scr, *, nb, nc):
    u = u_ref[0]
    col = lambda k: slice(k * LANES, (k + 1) * LANES)
    s = _dot(u, g_ref[0])
    for k in range(4):
        s_scr[k] = s[:, col(k)]
    a = a_ref[0]
    afr, afi, abr, abi = (jnp.broadcast_to(a[k:k + 1], (nb, LANES)) for k in range(4))
    h0 = h0_ref[0]

    def step(c, carry):
        fr, fi, br, bi = carry
        rf = pl.ds(c, nb, stride=nc)
        rb = pl.ds(nc - 1 - c, nb, stride=nc)
        h_scr[0, rf, :] = fr
        h_scr[1, rf, :] = fi
        h_scr[2, rb, :] = br
        h_scr[3, rb, :] = bi
        nfr = afr * fr - afi * fi + s_scr[0, rf, :]
        nfi = afr * fi + afi * fr + s_scr[1, rf, :]
        nbr = abr * br - abi * bi + s_scr[2, rb, :]
        nbi = abr * bi + abi * br + s_scr[3, rb, :]
        return nfr, nfi, nbr, nbi

    fin = lax.fori_loop(0, nc, step, tuple(h0[:, col(k)] for k in range(4)))
    for k in range(4):
        fin_ref[0, :, col(k)] = fin[k]
    hin = jnp.concatenate([h_scr[k] for k in range(4)], axis=1).astype(BF16)
    y = _dot(u, m_ref[0]) + _dot(hin, cc_ref[0])
    y_ref[0] = y.astype(y_ref.dtype)


def _ssm(u_rows, mats, h0, nb):
    npair, rows, w = u_rows.shape
    nc = rows // nb
    mat_spec = pl.BlockSpec((1, w, w), lambda p: (p, 0, 0))
    return pl.pallas_call(
        functools.partial(_ssm_body, nb=nb, nc=nc),
        grid=(npair,),
        in_specs=[pl.BlockSpec((1, rows, w), lambda p: (p, 0, 0)), mat_spec, mat_spec, mat_spec,
                  pl.BlockSpec((1, 4, LANES), lambda p: (p, 0, 0)),
                  pl.BlockSpec((1, nb, w), lambda p: (p, 0, 0))],
        out_specs=[pl.BlockSpec((1, rows, w), lambda p: (p, 0, 0)),
                   pl.BlockSpec((1, nb, w), lambda p: (p, 0, 0))],
        out_shape=[jax.ShapeDtypeStruct((npair, rows, w), BF16),
                   jax.ShapeDtypeStruct((npair, nb, w), F32)],
        scratch_shapes=[pltpu.VMEM((4, rows, LANES), F32), pltpu.VMEM((4, rows, LANES), F32)],
        compiler_params=_cparams("parallel"),
        name="ssm_scan",
    )(u_rows, mats["m"], mats["g"], mats["cc"], mats["a16"], h0)


def _ssm_matrices(lp):
    t = SSM_T
    ks = jnp.arange(t + 1, dtype=F32)
    dirs = []
    for d in range(2):
        lam = lax.complex(lp["ssm_lam_re"][d].astype(F32), lp["ssm_lam_im"][d].astype(F32))
        dt = jnp.exp(lp["ssm_log_dt"][d].astype(F32))[:, None]
        a_bar = jnp.exp(lam * dt)
        b_bar = ((a_bar - 1.0) / lam)[..., None] * lax.complex(lp["ssm_b_re"][d].astype(F32),
                                                               lp["ssm_b_im"][d].astype(F32))
        c_mat = lax.complex(lp["ssm_c_re"][d].astype(F32), lp["ssm_c_im"][d].astype(F32))
        pw = jnp.exp((lam * dt)[None] * ks[:, None, None].astype(jnp.complex64))
        kern = jnp.real(jnp.einsum("gop,kgp,gpi->gkoi", c_mat, pw[:t], b_bar))
        dirs.append((pw, b_bar, c_mat, kern))
    (pw_f, bb_f, cm_f, k_f), (pw_b, bb_b, cm_b, k_b) = dirs
    s_idx = jnp.arange(t)[:, None]
    t_idx = jnp.arange(t)[None, :]
    lag_f = jnp.clip(t_idx - s_idx, 0, t - 1)
    lag_b = jnp.clip(s_idx - t_idx, 0, t - 1)
    m5 = (jnp.where((t_idx >= s_idx)[None, :, :, None, None], k_f[:, lag_f], 0.0)
          + jnp.where((s_idx >= t_idx)[None, :, :, None, None], k_b[:, lag_b], 0.0))
    eye_t = jnp.eye(t, dtype=F32)
    eye_c = jnp.eye(SSM_CH, dtype=F32)
    m5 = m5 + eye_t[None, :, :, None, None] * eye_c[None, None, None] * lp["ssm_d"].astype(F32)[:, None, None, :, None]
    m_g = m5.transpose(0, 1, 4, 2, 3).reshape(G_C, t * SSM_CH, t * SSM_CH)
    g_f = pw_f[t - 1 - jnp.arange(t)].transpose(1, 0, 2)[:, :, None, :] * bb_f.transpose(0, 2, 1)[:, None]
    g_b = pw_b[jnp.arange(t)].transpose(1, 0, 2)[:, :, None, :] * bb_b.transpose(0, 2, 1)[:, None]
    g_g = jnp.stack([jnp.real(g_f), jnp.imag(g_f), jnp.real(g_b), jnp.imag(g_b)], axis=3)
    z_f = cm_f.transpose(0, 2, 1)[:, :, None, :] * pw_f[1 + jnp.arange(t)].transpose(1, 2, 0)[:, :, :, None]
    z_b = cm_b.transpose(0, 2, 1)[:, :, None, :] * pw_b[t - jnp.arange(t)].transpose(1, 2, 0)[:, :, :, None]
    cc_g = jnp.stack([jnp.real(z_f), -jnp.imag(z_f), jnp.real(z_b), -jnp.imag(z_b)], axis=1)
    eye2 = jnp.eye(2, dtype=F32)
    ch = SSM_CH
    m_p = jnp.einsum("pgsitc,gh->psgithc", m_g.reshape(N_PAIR, 2, t, ch, t, ch), eye2).reshape(N_PAIR, SSM_ROW, SSM_ROW)
    g_p = jnp.einsum("pgsiwq,gh->psgiwhq", g_g.reshape(N_PAIR, 2, t, ch, 4, P_C), eye2).reshape(N_PAIR, SSM_ROW, 8 * P_C)
    cc_p = jnp.einsum("pgwqtc,gh->pwgqthc", cc_g.reshape(N_PAIR, 2, 4, P_C, t, ch), eye2).reshape(N_PAIR, 8 * P_C, SSM_ROW)
    a16 = jnp.stack([jnp.real(pw_f[t]), jnp.imag(pw_f[t]), jnp.real(pw_b[t]), jnp.imag(pw_b[t])], axis=0)
    a16 = a16.reshape(4, N_PAIR, 2 * P_C).transpose(1, 0, 2)
    return dict(m=m_p.astype(BF16), g=g_p.astype(BF16), cc=cc_p.astype(BF16), a16=a16)


def _ssm_state_rows(s_re, s_im):
    bsz = s_re.shape[0]
    parts = [s_re[:, 0], s_im[:, 0], s_re[:, 1], s_im[:, 1]]
    st = jnp.stack([p.reshape(bsz, N_PAIR, 2 * P_C) for p in parts], axis=2)
    return st.transpose(1, 0, 2, 3).reshape(N_PAIR, bsz, 8 * P_C).astype(F32)


def _ssm_state_unrows(fin):
    npair, bsz, _ = fin.shape
    st = fin.reshape(npair, bsz, 4, 2, P_C).transpose(1, 2, 0, 3, 4).reshape(bsz, 4, G_C, P_C)
    return jnp.stack([st[:, 0], st[:, 2]], axis=1), jnp.stack([st[:, 1], st[:, 3]], axis=1)


def _route(scores, bias):
    tm = scores.shape[1]
    biased = scores + bias
    iota8 = lax.broadcasted_iota(jnp.int32, (PER_GROUP, tm), 0)
    grp = [biased[PER_GROUP * g:PER_GROUP * (g + 1)] for g in range(N_EXP_GROUPS)]
    gscore = []
    for v in grp:
        m1 = jnp.max(v, axis=0, keepdims=True)
        first = jnp.min(jnp.where(v == m1, iota8, PER_GROUP), axis=0, keepdims=True)
        m2 = jnp.max(jnp.where(iota8 == first, -jnp.inf, v), axis=0, keepdims=True)
        gscore.append(m1 + m2)
    masked = []
    for g in range(N_EXP_GROUPS):
        rank = jnp.zeros((1, tm), jnp.int32)
        for o in range(N_EXP_GROUPS):
            if o == g:
                continue
            ahead = (gscore[o] >= gscore[g]) if o < g else (gscore[o] > gscore[g])
            rank = rank + jnp.where(ahead, 1, 0)
        masked.append(jnp.where(rank < TOPK_GROUPS, grp[g], -jnp.inf))
    chosen = [None] * N_EXP_GROUPS
    for _ in range(TOP_K):
        best = masked[0]
        for v in masked[1:]:
            best = jnp.maximum(best, v)
        best = jnp.max(best, axis=0, keepdims=True)
        first = jnp.full((1, tm), N_EXPERTS, jnp.int32)
        for g, v in enumerate(masked):
            cand = jnp.min(jnp.where(v == best, iota8 + PER_GROUP * g, N_EXPERTS), axis=0, keepdims=True)
            first = jnp.minimum(first, cand)
        for g in range(N_EXP_GROUPS):
            hit = (iota8 + PER_GROUP * g) == first
            chosen[g] = hit if chosen[g] is None else (chosen[g] | hit)
            masked[g] = jnp.where(hit, -jnp.inf, masked[g])
    w = [jnp.where(chosen[g], scores[PER_GROUP * g:PER_GROUP * (g + 1)], 0.0) for g in range(N_EXP_GROUPS)]
    wsum = w[0]
    for v in w[1:]:
        wsum = wsum + v
    wsum = jnp.sum(wsum, axis=0, keepdims=True)
    return jnp.concatenate([v / wsum * ROUTED_SCALE for v in w], axis=0)


def _post_body(x_ref, oa_ref, ob_ref, y_ref, mod_ref, wglu_ref, wout_ref, g2_ref, wrh_ref, wrl_ref, br_ref,
               x1_ref, h2_ref, gate_ref, y_scr):
    d = D_MODEL
    tm = x_ref.shape[1]
    pw = 2 * SSM_CH
    for t in range(SSM_T):
        for blk in range(W_C // LANES):
            piece = jnp.concatenate([y_ref[blk * (LANES // pw) + pp, :, t * pw:(t + 1) * pw].astype(F32)
                                     for pp in range(LANES // pw)], axis=1)
            y_scr[blk, pl.ds(t, tm // SSM_T, stride=SSM_T), :] = piece
    g = jax.nn.gelu(jnp.concatenate([y_scr[blk] for blk in range(W_C // LANES)], axis=1))
    oc = g * jax.nn.sigmoid(_dot(g.astype(BF16), wglu_ref[...]))
    mix = (_dot(oa_ref[0], wout_ref[0:W_A]) + _dot(ob_ref[0], wout_ref[W_A:W_A + W_B])
           + _dot(oc.astype(BF16), wout_ref[W_A + W_B:]))
    mod = mod_ref[0]
    x1 = x_ref[0] + mod[:, 2 * d:3 * d] * mix
    x1_ref[0] = x1
    xn = x1 * lax.rsqrt(jnp.mean(x1 * x1, axis=-1, keepdims=True) + EPS) * g2_ref[...]
    h2 = xn * (1.0 + mod[:, 4 * d:5 * d]) + mod[:, 3 * d:4 * d]
    h_hi, h_lo = _split_bf16(h2)
    h2_ref[0] = h_hi
    logits = _dot_nt(wrh_ref[...], h_hi) + _dot_nt(wrh_ref[...], h_lo) + _dot_nt(wrl_ref[...], h_hi)
    gate_ref[0] = _route(jax.nn.sigmoid(logits), br_ref[...]).T


def _post_mix(x, oa, ob, y, mod, w_glu, w_out, g2, wr_hi, wr_lo, b_r):
    bsz, seq, d = x.shape
    tm = 256
    bm = mod.shape[0]
    mod_idx = (lambda b, i: (b, 0, 0)) if bm > 1 else (lambda b, i: (0, 0, 0))
    const2 = lambda b, i: (0, 0)
    tok = lambda w: pl.BlockSpec((1, tm, w), lambda b, i: (b, i, 0))
    nt = seq // tm
    return pl.pallas_call(
        _post_body,
        grid=(bsz, nt),
        in_specs=[tok(d), tok(W_A), tok(W_B),
                  pl.BlockSpec((N_PAIR, tm // SSM_T, SSM_ROW), lambda b, i: (0, b * nt + i, 0)),
                  pl.BlockSpec((1, 1, 6 * d), mod_idx),
                  pl.BlockSpec((W_C, W_C), const2),
                  pl.BlockSpec((d, d), const2),
                  pl.BlockSpec((1, d), const2),
                  pl.BlockSpec((N_EXPERTS, d), const2),
                  pl.BlockSpec((N_EXPERTS, d), const2),
                  pl.BlockSpec((N_EXPERTS, 1), const2)],
        out_specs=[tok(d), tok(d), tok(N_EXPERTS)],
        out_shape=[jax.ShapeDtypeStruct((bsz, seq, d), F32),
                   jax.ShapeDtypeStruct((bsz, seq, d), BF16),
                   jax.ShapeDtypeStruct((bsz, seq, N_EXPERTS), F32)],
        scratch_shapes=[pltpu.VMEM((W_C // LANES, tm, LANES), F32)],
        compiler_params=_cparams("parallel", "parallel"),
        name="post_mix",
    )(x, oa, ob, y, mod, w_glu, w_out, g2, wr_hi, wr_lo, b_r)


def _moe_body(x1_ref, h_ref, gate_ref, g2_ref, w1_ref, w3_ref, w2_ref, ex_ref, s1_ref, s3_ref, s2_ref,
              o_ref, acc_ref):
    j = pl.program_id(1)
    h = h_ref[...]

    @pl.when(j == 0)
    def _():
        a = _dot(h, s1_ref[...])
        acc_ref[...] = _dot((a * jax.nn.sigmoid(a) * _dot(h, s3_ref[...])).astype(BF16), s2_ref[...])

    ne = w1_ref.shape[0]
    a = _dot(h, jnp.concatenate([w1_ref[e] for e in range(ne)], axis=1))
    b = _dot(h, jnp.concatenate([w3_ref[e] for e in range(ne)], axis=1))
    gexp = _dot(jnp.concatenate(_split_bf16(gate_ref[...]), axis=1), ex_ref[...])
    hid = a * jax.nn.sigmoid(a) * b * gexp
    acc_ref[...] += _dot(hid.astype(BF16), w2_ref[...])

    @pl.when(j == pl.num_programs(1) - 1)
    def _():
        o_ref[...] = x1_ref[...] + g2_ref[0] * acc_ref[...]


def _moe(x1, h2, gates, mod, seq, w1, w3, w2, expand, ws1, ws3, ws2):
    tokens, d = x1.shape
    bm = mod.shape[0]
    span = seq if bm > 1 else tokens
    tm = next(t for t in (1024, 512, 256) if span % t == 0)
    per_b = seq // tm if bm > 1 else 1
    mod_idx = (lambda i, j: (i // per_b, 0, 5)) if bm > 1 else (lambda i, j: (0, 0, 5))
    ne = 4
    fc = ne * F_EXP
    hidden = w2.shape[0]
    const2 = lambda i, j: (0, 0)
    return pl.pallas_call(
        _moe_body,
        grid=(tokens // tm, hidden // fc),
        in_specs=[pl.BlockSpec((tm, d), lambda i, j: (i, 0)),
                  pl.BlockSpec((tm, d), lambda i, j: (i, 0)),
                  pl.BlockSpec((tm, N_EXPERTS), lambda i, j: (i, 0)),
                  pl.BlockSpec((1, 1, d), mod_idx),
                  pl.BlockSpec((ne, d, F_EXP), lambda i, j: (j, 0, 0)),
                  pl.BlockSpec((ne, d, F_EXP), lambda i, j: (j, 0, 0)),
                  pl.BlockSpec((fc, d), lambda i, j: (j, 0)),
                  pl.BlockSpec((2 * N_EXPERTS, fc), lambda i, j: (0, j)),
                  pl.BlockSpec((d, F_SHARED), const2),
                  pl.BlockSpec((d, F_SHARED), const2),
                  pl.BlockSpec((F_SHARED, d), const2)],
        out_specs=pl.BlockSpec((tm, d), lambda i, j: (i, 0)),
        out_shape=jax.ShapeDtypeStruct((tokens, d), F32),
        scratch_shapes=[pltpu.VMEM((tm, d), F32)],
        compiler_params=_cparams("parallel", "arbitrary"),
        name="moe",
    )(x1, h2, gates, mod, w1, w3, w2, expand, ws1, ws3, ws2)


def _rope_tables(seq):
    pos = jnp.arange(seq)
    row = (pos // GRID_W).astype(F32)[:, None]
    colp = (pos % GRID_W).astype(F32)[:, None]
    lane = jnp.arange(LANES)

    def table(width):
        half, quarter = width // 2, width // 4
        i = lane % width
        freq = ROPE_BASE ** (-(2.0 * (i % quarter).astype(F32)) / half)
        ang = jnp.where((i // half) == 0, row, colp) * freq[None, :]
        sign = jnp.where((i % half) < quarter, -1.0, 1.0)
        return jnp.cos(ang), jnp.sin(ang) * sign[None, :]

    ca, sa = table(HD_A)
    cb, sb = table(DC_B)
    return ca, sa, cb, sb


def _prep_layer(p, l):
    d = D_MODEL
    w_in = p["w_in"][l]
    kvw = KV_A * HD_A
    qa = w_in[:, :W_A].reshape(d, H_A, HD_A)
    qa_pad = jnp.zeros((d, H_A, 2, HD_A), w_in.dtype)
    for h in range(H_A):
        qa_pad = qa_pad.at[:, h, h // GQ_A].set(qa[:, h])
    w_in_p = jnp.concatenate([qa_pad.reshape(d, QA_COLS), w_in[:, W_A:]], axis=1).astype(BF16)
    gains = jnp.stack([jnp.tile(p["q_norm_a"][l], LANES // HD_A) * (HD_A ** -0.5),
                       jnp.tile(p["k_norm_a"][l], LANES // HD_A),
                       jnp.tile(p["q_norm_b"][l], LANES // DC_B) * (DC_B ** -0.5 * LOG2E),
                       jnp.tile(p["k_norm_b"][l], LANES // DC_B)], axis=0).astype(F32)
    lp = {k: p[k][l] for k in ("ssm_lam_re", "ssm_lam_im", "ssm_log_dt", "ssm_b_re", "ssm_b_im",
                               "ssm_c_re", "ssm_c_im", "ssm_d")}
    wr_hi, wr_lo = _split_bf16(p["w_router"][l].T.astype(F32))
    hidden = N_EXPERTS * F_EXP
    return dict(
        w_in_p=w_in_p, gains=gains,
        g1=p["norm1_g"][l].reshape(1, d).astype(F32), g2=p["norm2_g"][l].reshape(1, d).astype(F32),
        sink=p["sink_a"][l].astype(F32), lam_b=p["lam_b"][l].astype(F32),
        subln=jnp.tile(p["subln_b"][l], LANES // HD_B).reshape(1, LANES).astype(F32),
        lam_init=0.8 - 0.6 * math.exp(-0.3 * l),
        ssm=_ssm_matrices(lp),
        w_glu=p["w_glu"][l].astype(BF16), w_out=p["w_out"][l].astype(BF16),
        wr_hi=wr_hi, wr_lo=wr_lo, b_r=p["b_router"][l].reshape(N_EXPERTS, 1).astype(F32),
        w1=p["w_e1"][l].astype(BF16), w3=p["w_e3"][l].astype(BF16),
        w2=p["w_e2"][l].reshape(hidden, d).astype(BF16),
        ws1=p["w_s1"][l].astype(BF16), ws3=p["w_s3"][l].astype(BF16), ws2=p["w_s2"][l].astype(BF16),
    )


def _trunk_layer(x, mod, lw, consts, ctx):
    bsz, seq, d = x.shape
    latent = ctx is not None
    rope = consts["rope"] if latent else None
    kv_dtype = BF16 if latent else F32
    qa, ka, va, qb, kb, vb, u = _inproj(x, mod, lw["g1"], lw["w_in_p"], lw["gains"],
                                        consts["seg64"], consts["seg32"], rope, kv_dtype)
    if latent:
        oa = _attn_a(qa, ka, va, lw["sink"], (ctx["ak"], ctx["av"]))
        ob = _attn_b(qb, [kb, ctx["bk"]], [vb, ctx["bv"]], lw["lam_b"], lw["subln"], lw["lam_init"])
        h0 = ctx["h0"]
    else:
        oa = _attn_a(qa, ka, va, lw["sink"], None)
        ob = _attn_b(qb, [kb], [vb], lw["lam_b"], lw["subln"], lw["lam_init"])
        h0 = jnp.zeros((N_PAIR, bsz, 8 * P_C), F32)
    y_rows, fin = _ssm(u, lw["ssm"], h0, bsz)
    x1, h2, gates = _post_mix(x, oa, ob, y_rows, mod, lw["w_glu"], lw["w_out"], lw["g2"],
                                lw["wr_hi"], lw["wr_lo"], lw["b_r"])
    out = _moe(x1.reshape(bsz * seq, d), h2.reshape(bsz * seq, d), gates.reshape(bsz * seq, N_EXPERTS), mod, seq,
               lw["w1"], lw["w3"], lw["w2"], consts["expand"], lw["ws1"], lw["ws3"], lw["ws2"])
    return out.reshape(bsz, seq, d), (ka, va, kb, vb, fin)


def kernel(x_prompt, x_sample, cache_a_k, cache_a_v, cache_b_k, cache_b_v, state_ssm_re, state_ssm_im, c, c_ctx, norm1_g, norm2_g, w_ada, b_ada, w_in, q_norm_a, k_norm_a, sink_a, q_norm_b, k_norm_b, lam_b, subln_b, ssm_lam_re, ssm_lam_im, ssm_log_dt, ssm_b_re, ssm_b_im, ssm_c_re, ssm_c_im, ssm_d, w_glu, w_out, w_router, b_router, w_e1, w_e3, w_e2, w_s1, w_s3, w_s2):
    p = dict(norm1_g=norm1_g, norm2_g=norm2_g, w_in=w_in, q_norm_a=q_norm_a, k_norm_a=k_norm_a, sink_a=sink_a,
             q_norm_b=q_norm_b, k_norm_b=k_norm_b, lam_b=lam_b, subln_b=subln_b,
             ssm_lam_re=ssm_lam_re, ssm_lam_im=ssm_lam_im, ssm_log_dt=ssm_log_dt, ssm_b_re=ssm_b_re,
             ssm_b_im=ssm_b_im, ssm_c_re=ssm_c_re, ssm_c_im=ssm_c_im, ssm_d=ssm_d, w_glu=w_glu, w_out=w_out,
             w_router=w_router, b_router=b_router, w_e1=w_e1, w_e3=w_e3, w_e2=w_e2,
             w_s1=w_s1, w_s3=w_s3, w_s2=w_s2)
    depth = w_in.shape[0]
    bsz, seq, d = x_prompt.shape
    dbsz, dseq, _ = x_sample.shape
    past = cache_a_k.shape[3]

    mod_rows = 16
    cvec = jnp.concatenate([c.astype(F32), c_ctx.astype(F32)[None],
                            jnp.zeros((mod_rows - dbsz - 1, d), F32)], axis=0)
    mods = _modulation(cvec, w_ada.astype(F32), b_ada.astype(F32))

    lane = jnp.arange(LANES)
    hidden = N_EXPERTS * F_EXP
    consts = dict(
        rope=_rope_tables(dseq),
        seg64=(lane[:, None] // HD_A == lane[None, :] // HD_A).astype(BF16),
        seg32=(lane[:, None] // DC_B == lane[None, :] // DC_B).astype(BF16),
        expand=(jnp.arange(2 * N_EXPERTS)[:, None] % N_EXPERTS == jnp.arange(hidden)[None, :] // F_EXP).astype(BF16),
    )

    xp, xs = x_prompt, x_sample
    ak, av, bk, bv, sre, sim = [], [], [], [], [], []
    for l in range(depth):
        lw = _prep_layer(p, l)
        mod_lat = mods[l, :dbsz][:, None, :]
        mod_ctx = mods[l, dbsz:dbsz + 1][:, None, :]
        xp, (k_a, v_a, k_b, v_b, fin) = _trunk_layer(xp, mod_ctx, lw, consts, None)
        ak.append(k_a.reshape(bsz, seq, KV_A, HD_A).transpose(0, 2, 1, 3))
        av.append(v_a.reshape(bsz, seq, KV_A, HD_A).transpose(0, 2, 1, 3))
        bk.append(k_b.reshape(bsz, seq, H_B, 2, DC_B).transpose(0, 2, 3, 1, 4))
        bv.append(v_b.reshape(bsz, seq, H_B, HD_B).transpose(0, 2, 1, 3))
        f_re, f_im = _ssm_state_unrows(fin)
        sre.append(f_re)
        sim.append(f_im)
        ctx = dict(
            ak=cache_a_k[:, l].transpose(0, 2, 1, 3).reshape(dbsz, past, KV_A * HD_A).astype(BF16),
            av=cache_a_v[:, l].transpose(0, 2, 1, 3).reshape(dbsz, past, KV_A * HD_A).astype(BF16),
            bk=cache_b_k[:, l].transpose(0, 3, 1, 2, 4).reshape(dbsz, past, W_B).astype(BF16),
            bv=cache_b_v[:, l].transpose(0, 2, 1, 3).reshape(dbsz, past, W_B).astype(BF16),
            h0=_ssm_state_rows(state_ssm_re[:, l], state_ssm_im[:, l]),
        )
        xs, _ = _trunk_layer(xs, mod_lat, lw, consts, ctx)
    return (xp, xs, jnp.stack(ak, axis=1), jnp.stack(av, axis=1), jnp.stack(bk, axis=1),
            jnp.stack(bv, axis=1), jnp.stack(sre, axis=1), jnp.stack(sim, axis=1))
```

```python
import functools
import math

import jax
import jax.numpy as jnp
from jax import lax
from jax.experimental import pallas as pl
from jax.experimental.pallas import tpu as pltpu

F32 = jnp.float32
BF16 = jnp.bfloat16

D_MODEL = 1024
GRID_W = 64
BLOCK = 128
H_A, KV_A, HD_A = 6, 2, 64
GQ_A = H_A // KV_A
W_A = H_A * HD_A
H_B, HD_B = 4, 64
DC_B = HD_B // 2
W_B = H_B * HD_B
SSM_CH = 16
W_C = D_MODEL - W_A - W_B
G_C = W_C // SSM_CH
P_C = 64
N_EXPERTS, TOP_K, F_EXP, F_SHARED = 64, 6, 128, 256
N_EXP_GROUPS, TOPK_GROUPS = 8, 4
PER_GROUP = N_EXPERTS // N_EXP_GROUPS
ROUTED_SCALE = 2.5
ROPE_BASE = 10000.0
EPS = 1e-6
NEG = -1e30
LOG2E = 1.4426950408889634

LANES = 128
SSM_T = 16
N_PAIR = G_C // 2
SSM_ROW = 2 * SSM_T * SSM_CH
QA_COLS = H_A * LANES
IN_COLS_P = QA_COLS + 2 * KV_A * HD_A + 3 * W_B + W_C
VMEM_LIMIT = 56 << 20


def _cparams(*sem):
    return pltpu.CompilerParams(dimension_semantics=sem, vmem_limit_bytes=VMEM_LIMIT)


def _dot(a, b):
    return jnp.dot(a, b, preferred_element_type=F32)


def _dot_nt(a, b):
    return lax.dot_general(a, b, (((1,), (1,)), ((), ())), preferred_element_type=F32)


def _split_bf16(x):
    hi = x.astype(BF16)
    lo = (x - hi.astype(F32)).astype(BF16)
    return hi, lo


def _mod_body(c_ref, w_ref, b_ref, o_ref):
    c = c_ref[...]
    s = c * jax.nn.sigmoid(c)
    s_hi, s_lo = _split_bf16(s)
    w_hi, w_lo = _split_bf16(w_ref[0])
    o_ref[0] = _dot(s_hi, w_hi) + _dot(s_lo, w_hi) + _dot(s_hi, w_lo) + b_ref[0]


def _modulation(cvec, w_ada, b_ada):
    depth, d, n = w_ada.shape
    rows = cvec.shape[0]
    tn = 768
    return pl.pallas_call(
        _mod_body,
        grid=(depth, n // tn),
        in_specs=[pl.BlockSpec((rows, d), lambda l, j: (0, 0)),
                  pl.BlockSpec((1, d, tn), lambda l, j: (l, 0, j)),
                  pl.BlockSpec((1, 1, tn), lambda l, j: (l, 0, j))],
        out_specs=pl.BlockSpec((1, rows, tn), lambda l, j: (l, 0, j)),
        out_shape=jax.ShapeDtypeStruct((depth, rows, n), F32),
        compiler_params=_cparams("parallel", "parallel"),
        name="adaln_mod",
    )(cvec, w_ada, b_ada.reshape(depth, 1, n))


def _inproj_body(*refs, latent):
    if latent:
        (x_ref, mod_ref, g1_ref, w_ref, gains_ref, s64_ref, s32_ref, ca_ref, sa_ref, cb_ref, sb_ref,
         qa_ref, ka_ref, va_ref, qb_ref, kb_ref, vb_ref, u_ref, u_scr) = refs
    else:
        (x_ref, mod_ref, g1_ref, w_ref, gains_ref, s64_ref, s32_ref,
         qa_ref, ka_ref, va_ref, qb_ref, kb_ref, vb_ref, u_ref, u_scr) = refs
    d = D_MODEL
    x = x_ref[0]
    mod = mod_ref[0]
    xn = x * lax.rsqrt(jnp.mean(x * x, axis=-1, keepdims=True) + EPS) * g1_ref[...]
    h = xn * (1.0 + mod[:, d:2 * d]) + mod[:, 0:d]
    acc = _dot(h.astype(BF16), w_ref[...])

    tm = x.shape[0]
    lane = lax.broadcasted_iota(jnp.int32, (tm, LANES), 1)
    first_a = (lane % 32) < 16
    first_b = (lane % 16) < 8

    def normed(xb, seg_ref, inv_n, gain):
        ss = _dot((xb * xb).astype(BF16), seg_ref[...])
        return xb * lax.rsqrt(ss * inv_n + EPS) * gain

    def rope_a(y):
        if not latent:
            return y
        sw = jnp.where(first_a, pltpu.roll(y, LANES - 16, 1), pltpu.roll(y, 16, 1))
        return y * ca_ref[...] + sw * sa_ref[...]

    def rope_b(y):
        if not latent:
            return y
        sw = jnp.where(first_b, pltpu.roll(y, LANES - 8, 1), pltpu.roll(y, 8, 1))
        return y * cb_ref[...] + sw * sb_ref[...]

    gains = gains_ref[...]
    off = 0
    for b in range(H_A):
        y = normed(acc[:, off:off + LANES], s64_ref, 1.0 / HD_A, gains[0:1])
        qa_ref[0, :, b * LANES:(b + 1) * LANES] = rope_a(y).astype(qa_ref.dtype)
        off += LANES
    y = normed(acc[:, off:off + LANES], s64_ref, 1.0 / HD_A, gains[1:2])
    ka_ref[0] = rope_a(y).astype(ka_ref.dtype)
    off += LANES
    va_ref[0] = acc[:, off:off + LANES].astype(va_ref.dtype)
    off += LANES
    for b in range(W_B // LANES):
        y = normed(acc[:, off:off + LANES], s32_ref, 1.0 / DC_B, gains[2:3])
        qb_ref[0, :, b * LANES:(b + 1) * LANES] = rope_b(y).astype(qb_ref.dtype)
        off += LANES
    for b in range(W_B // LANES):
        y = normed(acc[:, off:off + LANES], s32_ref, 1.0 / DC_B, gains[3:4])
        kb_ref[0, :, b * LANES:(b + 1) * LANES] = rope_b(y).astype(kb_ref.dtype)
        off += LANES
    vb_ref[0] = acc[:, off:off + W_B].astype(vb_ref.dtype)
    off += W_B
    for blk in range(W_C // LANES):
        u_scr[blk] = acc[:, off + blk * LANES:off + (blk + 1) * LANES]
    pw = 2 * SSM_CH
    for t in range(SSM_T):
        for blk in range(W_C // LANES):
            xt = u_scr[blk, pl.ds(t, tm // SSM_T, stride=SSM_T), :]
            for pp in range(LANES // pw):
                u_ref[blk * (LANES // pw) + pp, :, t * pw:(t + 1) * pw] = xt[:, pp * pw:(pp + 1) * pw].astype(u_ref.dtype)


def _inproj(x, mod, g1, w_in_p, gains, seg64, seg32, rope, kv_dtype):
    bsz, seq, d = x.shape
    latent = rope is not None
    tm = 512 if seq % 512 == 0 else 256
    bm = mod.shape[0]
    mod_idx = (lambda b, i: (b, 0, 0)) if bm > 1 else (lambda b, i: (0, 0, 0))
    const2 = lambda b, i: (0, 0)
    tok = lambda w: pl.BlockSpec((1, tm, w), lambda b, i: (b, i, 0))
    in_specs = [tok(d),
                pl.BlockSpec((1, 1, 6 * d), mod_idx),
                pl.BlockSpec((1, d), const2),
                pl.BlockSpec((d, IN_COLS_P), const2),
                pl.BlockSpec((4, LANES), const2),
                pl.BlockSpec((LANES, LANES), const2),
                pl.BlockSpec((LANES, LANES), const2)]
    args = [x, mod, g1, w_in_p, gains, seg64, seg32]
    if latent:
        in_specs += [pl.BlockSpec((tm, LANES), lambda b, i: (i, 0))] * 4
        args += list(rope)
    widths = (QA_COLS, KV_A * HD_A, KV_A * HD_A, W_B, W_B, W_B)
    dtypes = (BF16, kv_dtype, kv_dtype, BF16, kv_dtype, kv_dtype)
    nt = seq // tm
    rows = tm // SSM_T
    u_spec = pl.BlockSpec((N_PAIR, rows, SSM_ROW), lambda b, i: (0, b * nt + i, 0))
    u_shape = jax.ShapeDtypeStruct((N_PAIR, bsz * seq // SSM_T, SSM_ROW), BF16)
    return pl.pallas_call(
        functools.partial(_inproj_body, latent=latent),
        grid=(bsz, nt),
        in_specs=in_specs,
        out_specs=[tok(w) for w in widths] + [u_spec],
        out_shape=[jax.ShapeDtypeStruct((bsz, seq, w), dt) for w, dt in zip(widths, dtypes)] + [u_shape],
        scratch_shapes=[pltpu.VMEM((W_C // LANES, tm, LANES), F32)],
        compiler_params=_cparams("parallel", "parallel"),
        name="inproj_latent" if latent else "inproj_ctx",
    )(*args)


def _attn_a_body(sink_ref, q_ref, *refs, latent, nblk):
    o_ref = refs[-1]
    nk = (len(refs) - 1) // 2
    kcat = jnp.concatenate([r[0].astype(BF16) for r in refs[:nk]], axis=0)
    vcat = jnp.concatenate([r[0].astype(BF16) for r in refs[nk:2 * nk]], axis=0)
    rows = GQ_A * BLOCK
    cols = kcat.shape[0]
    rowi = lax.broadcasted_iota(jnp.int32, (rows, 1), 0)
    if latent:
        i = pl.program_id(1)
        r = lax.broadcasted_iota(jnp.int32, (rows, cols), 0) & (BLOCK - 1)
        c = lax.broadcasted_iota(jnp.int32, (rows, cols), 1)
        p_off = jnp.where(i > 0, 0, 2 * BLOCK)
        n_off = jnp.where(i < nblk - 1, 0, 2 * BLOCK)
        prev_ok = (c >= r + p_off) | (c >= BLOCK)
        next_ok = ((c - 2 * BLOCK + n_off) <= r) | (c < 2 * BLOCK) | (c >= 3 * BLOCK)
        valid = prev_ok & next_ok
    lane = lax.broadcasted_iota(jnp.int32, (BLOCK, LANES), 1)
    heads = []
    for j in range(KV_A):
        q3 = jnp.concatenate([q_ref[0, :, (GQ_A * j + g) * LANES:(GQ_A * j + g + 1) * LANES]
                              for g in range(GQ_A)], axis=0)
        s = _dot_nt(q3, kcat)
        if latent:
            s = jnp.where(valid, s, NEG)
        sink = jnp.where(rowi < BLOCK, sink_ref[GQ_A * j],
                         jnp.where(rowi < 2 * BLOCK, sink_ref[GQ_A * j + 1], sink_ref[GQ_A * j + 2]))
        m = jnp.maximum(jnp.max(s, axis=-1, keepdims=True), sink)
        e = jnp.exp(s - m)
        den = jnp.sum(e, axis=-1, keepdims=True) + jnp.exp(sink - m)
        o = _dot(e.astype(BF16), vcat) / den
        for g in range(GQ_A):
            heads.append((j, o[g * BLOCK:(g + 1) * BLOCK]))
    for blk in range(H_A // 2):
        (j0, o0), (j1, o1) = heads[2 * blk], heads[2 * blk + 1]
        lo = o0 if j0 == 0 else pltpu.roll(o0, HD_A, 1)
        hi = o1 if j1 == 1 else pltpu.roll(o1, HD_A, 1)
        o_ref[0, :, blk * LANES:(blk + 1) * LANES] = jnp.where(lane < HD_A, lo, hi).astype(o_ref.dtype)


def _attn_a(qa, ka, va, sink, ctx_kv):
    bsz, seq, _ = qa.shape
    nblk = seq // BLOCK
    latent = ctx_kv is not None
    kvw = KV_A * HD_A
    if latent:
        past = ctx_kv[0].shape[1]
        band = [pl.BlockSpec((1, BLOCK, kvw), lambda b, i: (b, jnp.maximum(i - 1, 0), 0)),
                pl.BlockSpec((1, BLOCK, kvw), lambda b, i: (b, i, 0)),
                pl.BlockSpec((1, BLOCK, kvw), lambda b, i: (b, jnp.minimum(i + 1, nblk - 1), 0)),
                pl.BlockSpec((1, past, kvw), lambda b, i: (b, 0, 0))]
        kv_specs = band + band
        kv_args = [ka, ka, ka, ctx_kv[0], va, va, va, ctx_kv[1]]
    else:
        kv_specs = [pl.BlockSpec((1, seq, kvw), lambda b, i: (b, 0, 0))] * 2
        kv_args = [ka, va]
    return pl.pallas_call(
        functools.partial(_attn_a_body, latent=latent, nblk=nblk),
        grid=(bsz, nblk),
        in_specs=[pl.BlockSpec(memory_space=pltpu.SMEM),
                  pl.BlockSpec((1, BLOCK, QA_COLS), lambda b, i: (b, i, 0))] + kv_specs,
        out_specs=pl.BlockSpec((1, BLOCK, W_A), lambda b, i: (b, i, 0)),
        out_shape=jax.ShapeDtypeStruct((bsz, seq, W_A), BF16),
        compiler_params=_cparams("parallel", "parallel"),
        name="attn_a_latent" if latent else "attn_a_ctx",
    )(sink, qa, *kv_args)


def _attn_b_body(lam_ref, gain_ref, q_ref, *refs, part_lens, lam_init, kc):
    npart = len(part_lens)
    k_refs, v_refs = refs[:npart], refs[npart:2 * npart]
    o_ref, s_scr, vm_scr = refs[2 * npart:]
    tq = q_ref.shape[1]
    chunks = []
    col = 0
    for p, plen in enumerate(part_lens):
        step = min(kc, plen)
        for start in range(0, plen, step):
            chunks.append((p, start, col, step))
            col += step

    @pl.when(pl.program_id(2) == 0)
    def _():
        off = 0
        for p, plen in enumerate(part_lens):
            v = v_refs[p][0].astype(BF16)
            lane_v = lax.broadcasted_iota(jnp.int32, (plen, LANES), 1)
            for h in range(2):
                own = (lane_v >= h * HD_B) & (lane_v < (h + 1) * HD_B)
                ones = jnp.where(lane_v == (1 - h) * HD_B, 1.0, 0.0).astype(BF16)
                vm_scr[h, off:off + plen, :] = jnp.where(own, v, ones)
            off += plen

    lv = lam_ref[...]
    lam = (jnp.exp(jnp.sum(lv[0:1] * lv[1:2], axis=-1, keepdims=True))
           - jnp.exp(jnp.sum(lv[2:3] * lv[3:4], axis=-1, keepdims=True)) + lam_init)
    q = q_ref[0]
    lane_q = lax.broadcasted_iota(jnp.int32, (tq, LANES), 1)
    total = jnp.zeros((tq, LANES), F32)
    for h in range(2):
        qc = [jnp.where((lane_q >= h * HD_B + c * DC_B) & (lane_q < h * HD_B + (c + 1) * DC_B), q, jnp.zeros_like(q))
              for c in range(2)]
        rows = [slice(c * tq, (c + 1) * tq) for c in range(2)]
        macc = [None, None]
        for p, start, col, step in chunks:
            kch = k_refs[p][0, start:start + step, :].astype(BF16)
            for c in range(2):
                s = _dot_nt(qc[c], kch)
                s_scr[rows[c], col:col + step] = s
                for j in range(step // LANES):
                    t = s[:, j * LANES:(j + 1) * LANES]
                    macc[c] = t if macc[c] is None else jnp.maximum(macc[c], t)
        m = [jnp.max(macc[c], axis=-1, keepdims=True) for c in range(2)]
        acc = [jnp.zeros((tq, LANES), F32) for _ in range(2)]
        for p, start, col, step in chunks:
            vch = vm_scr[h, col:col + step, :]
            for c in range(2):
                e = jnp.exp2(s_scr[rows[c], col:col + step] - m[c]).astype(BF16)
                acc[c] = acc[c] + _dot(e, vch)
        o2 = [acc[c] / jnp.sum(jnp.where(lane_q == (1 - h) * HD_B, acc[c], 0.0), axis=-1, keepdims=True)
              for c in range(2)]
        own = (lane_q >= h * HD_B) & (lane_q < (h + 1) * HD_B)
        total = total + jnp.where(own, o2[0] - lam * o2[1], 0.0)
    sq = total * total
    ss_lo = jnp.sum(jnp.where(lane_q < HD_B, sq, 0.0), axis=-1, keepdims=True)
    ss_hi = jnp.sum(jnp.where(lane_q >= HD_B, sq, 0.0), axis=-1, keepdims=True)
    rinv = jnp.where(lane_q < HD_B, lax.rsqrt(ss_lo * (1.0 / HD_B) + EPS), lax.rsqrt(ss_hi * (1.0 / HD_B) + EPS))
    o_ref[0] = (total * rinv * gain_ref[...] * (1.0 - lam_init)).astype(o_ref.dtype)


def _attn_b(qb, k_parts, v_parts, lam_b, gain, lam_init):
    bsz, seq, _ = qb.shape
    tq = 256
    part_lens = tuple(k.shape[1] for k in k_parts)
    lk = sum(part_lens)
    kv_specs = [pl.BlockSpec((1, n, LANES), lambda b, hp, i: (b, 0, hp)) for n in part_lens]
    return pl.pallas_call(
        functools.partial(_attn_b_body, part_lens=part_lens, lam_init=lam_init, kc=512),
        grid=(bsz, W_B // LANES, seq // tq),
        in_specs=[pl.BlockSpec((4, DC_B), lambda b, hp, i: (0, 0)),
                  pl.BlockSpec((1, LANES), lambda b, hp, i: (0, 0)),
                  pl.BlockSpec((1, tq, LANES), lambda b, hp, i: (b, i, hp))] + kv_specs + kv_specs,
        out_specs=pl.BlockSpec((1, tq, LANES), lambda b, hp, i: (b, i, hp)),
        out_shape=jax.ShapeDtypeStruct((bsz, seq, W_B), BF16),
        scratch_shapes=[pltpu.VMEM((2 * tq, lk), F32), pltpu.VMEM((2, lk, LANES), BF16)],
        compiler_params=_cparams("parallel", "parallel", "arbitrary"),
        name="attn_b_latent" if len(k_parts) > 1 else "attn_b_ctx",
    )(lam_b, gain, qb, *k_parts, *v_parts)


def _ssm_body(u_ref, m_ref, g_ref, cc_ref, a_ref, h0_ref, y_ref, fin_ref, s_scr, h_scr, *, nb, nc):
    u = u_ref[0]
    col = lambda k: slice(k * LANES, (k + 1) * LANES)
    s = _dot(u, g_ref[0])
    for k in range(4):
        s_scr[k] = s[:, col(k)]
    a = a_ref[0]
    afr, afi, abr, abi = (jnp.broadcast_to(a[k:k + 1], (nb, LANES)) for k in range(4))
    h0 = h0_ref[0]

    def step(c, carry):
        fr, fi, br, bi = carry
        rf = pl.ds(c, nb, stride=nc)
        rb = pl.ds(nc - 1 - c, nb, stride=nc)
        h_scr[0, rf, :] = fr
        h_scr[1, rf, :] = fi
        h_scr[2, rb, :] = br
        h_scr[3, rb, :] = bi
        nfr = afr * fr - afi * fi + s_scr[0, rf, :]
        nfi = afr * fi + afi * fr + s_scr[1, rf, :]
        nbr = abr * br - abi * bi + s_scr[2, rb, :]
        nbi = abr * bi + abi * br + s_scr[3, rb, :]
        return nfr, nfi, nbr, nbi

    fin = lax.fori_loop(0, nc, step, tuple(h0[:, col(k)] for k in range(4)))
    for k in range(4):
        fin_ref[0, :, col(k)] = fin[k]
    hin = jnp.concatenate([h_scr[k] for k in range(4)], axis=1).astype(BF16)
    y = _dot(u, m_ref[0]) + _dot(hin, cc_ref[0])
    y_ref[0] = y.astype(y_ref.dtype)


def _ssm(u_rows, mats, h0, nb):
    npair, rows, w = u_rows.shape
    nc = rows // nb
    mat_spec = pl.BlockSpec((1, w, w), lambda p: (p, 0, 0))
    return pl.pallas_call(
        functools.partial(_ssm_body, nb=nb, nc=nc),
        grid=(npair,),
        in_specs=[pl.BlockSpec((1, rows, w), lambda p: (p, 0, 0)), mat_spec, mat_spec, mat_spec,
                  pl.BlockSpec((1, 4, LANES), lambda p: (p, 0, 0)),
                  pl.BlockSpec((1, nb, w), lambda p: (p, 0, 0))],
        out_specs=[pl.BlockSpec((1, rows, w), lambda p: (p, 0, 0)),
                   pl.BlockSpec((1, nb, w), lambda p: (p, 0, 0))],
        out_shape=[jax.ShapeDtypeStruct((npair, rows, w), BF16),
                   jax.ShapeDtypeStruct((npair, nb, w), F32)],
        scratch_shapes=[pltpu.VMEM((4, rows, LANES), F32), pltpu.VMEM((4, rows, LANES), F32)],
        compiler_params=_cparams("parallel"),
        name="ssm_scan",
    )(u_rows, mats["m"], mats["g"], mats["cc"], mats["a16"], h0)


def _ssm_matrices(lp):
    t = SSM_T
    ks = jnp.arange(t + 1, dtype=F32)
    dirs = []
    for d in range(2):
        lam = lax.complex(lp["ssm_lam_re"][d].astype(F32), lp["ssm_lam_im"][d].astype(F32))
        dt = jnp.exp(lp["ssm_log_dt"][d].astype(F32))[:, None]
        a_bar = jnp.exp(lam * dt)
        b_bar = ((a_bar - 1.0) / lam)[..., None] * lax.complex(lp["ssm_b_re"][d].astype(F32),
                                                               lp["ssm_b_im"][d].astype(F32))
        c_mat = lax.complex(lp["ssm_c_re"][d].astype(F32), lp["ssm_c_im"][d].astype(F32))
        pw = jnp.exp((lam * dt)[None] * ks[:, None, None].astype(jnp.complex64))
        kern = jnp.real(jnp.einsum("gop,kgp,gpi->gkoi", c_mat, pw[:t], b_bar))
        dirs.append((pw, b_bar, c_mat, kern))
    (pw_f, bb_f, cm_f, k_f), (pw_b, bb_b, cm_b, k_b) = dirs
    eye2 = jnp.eye(2, dtype=F32)
    ch, pw2 = SSM_CH, 2 * SSM_CH
    hi = lax.Precision.HIGHEST

    def pair_bd(x):
        r, c = x.shape[1:]
        return jnp.einsum("pgrc,gh->pgrhc", x.reshape(N_PAIR, 2, r, c), eye2.astype(x.dtype)).reshape(N_PAIR, 2 * r, 2 * c)

    def pair_vec(x):
        return x.reshape(x.shape[0], N_PAIR, 2 * P_C).transpose(1, 0, 2)

    def lag_blocks(kern):
        x = kern.transpose(1, 0, 3, 2).reshape(t * G_C, ch, ch)
        return jnp.stack([pair_bd(x[l * G_C:(l + 1) * G_C]) for l in range(t)], axis=1)
    kp_f, kp_b = lag_blocks(k_f), lag_blocks(k_b)
    d_blk = pair_bd(lp["ssm_d"].astype(F32)[:, :, None] * jnp.eye(ch, dtype=F32)[None])
    center = (kp_f[:, 0] + kp_b[:, 0] + d_blk)[:, None]
    band = jnp.concatenate([kp_b[:, :0:-1], center, kp_f[:, 1:]], axis=1)
    band = band.transpose(0, 2, 1, 3).reshape(N_PAIR, pw2, (2 * t - 1) * pw2)
    m_p = jnp.concatenate([band[:, :, (t - 1 - s) * pw2:(t - 1 - s) * pw2 + SSM_ROW] for s in range(t)], axis=1)

    def inject(pw_sel, b_bar):
        x1 = jnp.repeat(pair_vec(pw_sel), pw2, axis=1)
        x2 = jnp.tile(pair_bd(b_bar.transpose(0, 2, 1)), (1, t, 1))
        return x1 * x2
    g_f = inject(pw_f[t - 1 - jnp.arange(t)], bb_f)
    g_b = inject(pw_b[jnp.arange(t)], bb_b)
    g_p = jnp.concatenate([jnp.real(g_f), jnp.imag(g_f), jnp.real(g_b), jnp.imag(g_b)], axis=2)

    lane = jnp.arange(SSM_ROW)
    exp_t = (jnp.arange(t)[:, None] == lane[None, :] // pw2).astype(F32)
    exp_c = (jnp.arange(pw2)[:, None] == lane[None, :] % pw2).astype(F32)

    def widen(x, e):
        f = lambda v: jnp.einsum("pqk,kx->pqx", v, e, precision=hi)
        return lax.complex(f(jnp.real(x)), f(jnp.imag(x)))

    def readout(pw_sel, c_mat):
        y1 = widen(pair_vec(pw_sel).transpose(0, 2, 1), exp_t)
        y2 = widen(pair_bd(c_mat.transpose(0, 2, 1)), exp_c)
        return y1 * y2
    z_f = readout(pw_f[1 + jnp.arange(t)], cm_f)
    z_b = readout(pw_b[t - jnp.arange(t)], cm_b)
    cc_p = jnp.concatenate([jnp.real(z_f), -jnp.imag(z_f), jnp.real(z_b), -jnp.imag(z_b)], axis=1)
    a16 = jnp.stack([jnp.real(pw_f[t]), jnp.imag(pw_f[t]), jnp.real(pw_b[t]), jnp.imag(pw_b[t])], axis=0)
    a16 = a16.reshape(4, N_PAIR, 2 * P_C).transpose(1, 0, 2)
    return dict(m=m_p.astype(BF16), g=g_p.astype(BF16), cc=cc_p.astype(BF16), a16=a16)


def _ssm_state_rows(s_re, s_im):
    bsz = s_re.shape[0]
    parts = [s_re[:, 0], s_im[:, 0], s_re[:, 1], s_im[:, 1]]
    st = jnp.stack([p.reshape(bsz, N_PAIR, 2 * P_C) for p in parts], axis=2)
    return st.transpose(1, 0, 2, 3).reshape(N_PAIR, bsz, 8 * P_C).astype(F32)


def _ssm_state_unrows(fin):
    npair, bsz, _ = fin.shape
    st = fin.reshape(npair, bsz, 4, 2, P_C).transpose(1, 2, 0, 3, 4).reshape(bsz, 4, G_C, P_C)
    return jnp.stack([st[:, 0], st[:, 2]], axis=1), jnp.stack([st[:, 1], st[:, 3]], axis=1)


def _route(scores, bias):
    tm = scores.shape[1]
    biased = scores + bias
    iota8 = lax.broadcasted_iota(jnp.int32, (PER_GROUP, tm), 0)
    grp = [biased[PER_GROUP * g:PER_GROUP * (g + 1)] for g in range(N_EXP_GROUPS)]
    gscore = []
    for v in grp:
        m1 = jnp.max(v, axis=0, keepdims=True)
        first = jnp.min(jnp.where(v == m1, iota8, PER_GROUP), axis=0, keepdims=True)
        m2 = jnp.max(jnp.where(iota8 == first, -jnp.inf, v), axis=0, keepdims=True)
        gscore.append(m1 + m2)
    masked = []
    for g in range(N_EXP_GROUPS):
        rank = jnp.zeros((1, tm), jnp.int32)
        for o in range(N_EXP_GROUPS):
            if o == g:
                continue
            ahead = (gscore[o] >= gscore[g]) if o < g else (gscore[o] > gscore[g])
            rank = rank + jnp.where(ahead, 1, 0)
        masked.append(jnp.where(rank < TOPK_GROUPS, grp[g], -jnp.inf))
    chosen = [None] * N_EXP_GROUPS
    for _ in range(TOP_K):
        best = masked[0]
        for v in masked[1:]:
            best = jnp.maximum(best, v)
        best = jnp.max(best, axis=0, keepdims=True)
        first = jnp.full((1, tm), N_EXPERTS, jnp.int32)
        for g, v in enumerate(masked):
            cand = jnp.min(jnp.where(v == best, iota8 + PER_GROUP * g, N_EXPERTS), axis=0, keepdims=True)
            first = jnp.minimum(first, cand)
        for g in range(N_EXP_GROUPS):
            hit = (iota8 + PER_GROUP * g) == first
            chosen[g] = hit if chosen[g] is None else (chosen[g] | hit)
            masked[g] = jnp.where(hit, -jnp.inf, masked[g])
    w = [jnp.where(chosen[g], scores[PER_GROUP * g:PER_GROUP * (g + 1)], 0.0) for g in range(N_EXP_GROUPS)]
    wsum = w[0]
    for v in w[1:]:
        wsum = wsum + v
    wsum = jnp.sum(wsum, axis=0, keepdims=True)
    return jnp.concatenate([v / wsum * ROUTED_SCALE for v in w], axis=0)


def _post_body(x_ref, oa_ref, ob_ref, y_ref, mod_ref, wglu_ref, wout_ref, g2_ref, wrh_ref, wrl_ref, br_ref,
               x1_ref, h2_ref, gate_ref, y_scr):
    d = D_MODEL
    tm = x_ref.shape[1]
    pw = 2 * SSM_CH
    for t in range(SSM_T):
        for blk in range(W_C // LANES):
            piece = jnp.concatenate([y_ref[blk * (LANES // pw) + pp, :, t * pw:(t + 1) * pw].astype(F32)
                                     for pp in range(LANES // pw)], axis=1)
            y_scr[blk, pl.ds(t, tm // SSM_T, stride=SSM_T), :] = piece
    g = jax.nn.gelu(jnp.concatenate([y_scr[blk] for blk in range(W_C // LANES)], axis=1))
    oc = g * jax.nn.sigmoid(_dot(g.astype(BF16), wglu_ref[...]))
    mix = (_dot(oa_ref[0], wout_ref[0:W_A]) + _dot(ob_ref[0], wout_ref[W_A:W_A + W_B])
           + _dot(oc.astype(BF16), wout_ref[W_A + W_B:]))
    mod = mod_ref[0]
    x1 = x_ref[0] + mod[:, 2 * d:3 * d] * mix
    x1_ref[0] = x1
    xn = x1 * lax.rsqrt(jnp.mean(x1 * x1, axis=-1, keepdims=True) + EPS) * g2_ref[...]
    h2 = xn * (1.0 + mod[:, 4 * d:5 * d]) + mod[:, 3 * d:4 * d]
    h_hi, h_lo = _split_bf16(h2)
    h2_ref[0] = h_hi
    logits = _dot_nt(wrh_ref[...], h_hi) + _dot_nt(wrh_ref[...], h_lo) + _dot_nt(wrl_ref[...], h_hi)
    gate_ref[0] = _route(jax.nn.sigmoid(logits), br_ref[...]).T


def _post_mix(x, oa, ob, y, mod, w_glu, w_out, g2, wr_hi, wr_lo, b_r):
    bsz, seq, d = x.shape
    tm = 512 if seq % 512 == 0 else 256
    bm = mod.shape[0]
    mod_idx = (lambda b, i: (b, 0, 0)) if bm > 1 else (lambda b, i: (0, 0, 0))
    const2 = lambda b, i: (0, 0)
    tok = lambda w: pl.BlockSpec((1, tm, w), lambda b, i: (b, i, 0))
    nt = seq // tm
    return pl.pallas_call(
        _post_body,
        grid=(bsz, nt),
        in_specs=[tok(d), tok(W_A), tok(W_B),
                  pl.BlockSpec((N_PAIR, tm // SSM_T, SSM_ROW), lambda b, i: (0, b * nt + i, 0)),
                  pl.BlockSpec((1, 1, 6 * d), mod_idx),
                  pl.BlockSpec((W_C, W_C), const2),
                  pl.BlockSpec((d, d), const2),
                  pl.BlockSpec((1, d), const2),
                  pl.BlockSpec((N_EXPERTS, d), const2),
                  pl.BlockSpec((N_EXPERTS, d), const2),
                  pl.BlockSpec((N_EXPERTS, 1), const2)],
        out_specs=[tok(d), tok(d), tok(N_EXPERTS)],
        out_shape=[jax.ShapeDtypeStruct((bsz, seq, d), F32),
                   jax.ShapeDtypeStruct((bsz, seq, d), BF16),
                   jax.ShapeDtypeStruct((bsz, seq, N_EXPERTS), F32)],
        scratch_shapes=[pltpu.VMEM((W_C // LANES, tm, LANES), F32)],
        compiler_params=_cparams("parallel", "parallel"),
        name="post_mix",
    )(x, oa, ob, y, mod, w_glu, w_out, g2, wr_hi, wr_lo, b_r)


def _moe_body(x1_ref, h_ref, gate_ref, g2_ref, w1_ref, w3_ref, w2_ref, ex_ref, s1_ref, s3_ref, s2_ref,
              o_ref, acc_ref):
    j = pl.program_id(1)
    h = h_ref[...]

    @pl.when(j == 0)
    def _():
        a = _dot(h, s1_ref[...])
        acc_ref[...] = _dot((a * jax.nn.sigmoid(a) * _dot(h, s3_ref[...])).astype(BF16), s2_ref[...])

    ne = w1_ref.shape[0]
    a = _dot(h, jnp.concatenate([w1_ref[e] for e in range(ne)], axis=1))
    b = _dot(h, jnp.concatenate([w3_ref[e] for e in range(ne)], axis=1))
    gexp = _dot(jnp.concatenate(_split_bf16(gate_ref[...]), axis=1), ex_ref[...])
    hid = a * jax.nn.sigmoid(a) * b * gexp
    acc_ref[...] += _dot(hid.astype(BF16), w2_ref[...])

    @pl.when(j == pl.num_programs(1) - 1)
    def _():
        o_ref[...] = x1_ref[...] + g2_ref[0] * acc_ref[...]


def _moe(x1, h2, gates, mod, seq, w1, w3, w2, expand, ws1, ws3, ws2):
    tokens, d = x1.shape
    bm = mod.shape[0]
    span = seq if bm > 1 else tokens
    tm = next(t for t in (1024, 512, 256) if span % t == 0)
    per_b = seq // tm if bm > 1 else 1
    mod_idx = (lambda i, j: (i // per_b, 0, 5)) if bm > 1 else (lambda i, j: (0, 0, 5))
    ne = 4
    fc = ne * F_EXP
    hidden = w2.shape[0]
    const2 = lambda i, j: (0, 0)
    return pl.pallas_call(
        _moe_body,
        grid=(tokens // tm, hidden // fc),
        in_specs=[pl.BlockSpec((tm, d), lambda i, j: (i, 0)),
                  pl.BlockSpec((tm, d), lambda i, j: (i, 0)),
                  pl.BlockSpec((tm, N_EXPERTS), lambda i, j: (i, 0)),
                  pl.BlockSpec((1, 1, d), mod_idx),
                  pl.BlockSpec((ne, d, F_EXP), lambda i, j: (j, 0, 0)),
                  pl.BlockSpec((ne, d, F_EXP), lambda i, j: (j, 0, 0)),
                  pl.BlockSpec((fc, d), lambda i, j: (j, 0)),
                  pl.BlockSpec((2 * N_EXPERTS, fc), lambda i, j: (0, j)),
                  pl.BlockSpec((d, F_SHARED), const2),
                  pl.BlockSpec((d, F_SHARED), const2),
                  pl.BlockSpec((F_SHARED, d), const2)],
        out_specs=pl.BlockSpec((tm, d), lambda i, j: (i, 0)),
        out_shape=jax.ShapeDtypeStruct((tokens, d), F32),
        scratch_shapes=[pltpu.VMEM((tm, d), F32)],
        compiler_params=_cparams("parallel", "arbitrary"),
        name="moe",
    )(x1, h2, gates, mod, w1, w3, w2, expand, ws1, ws3, ws2)


def _rope_tables(seq):
    pos = jnp.arange(seq)
    row = (pos // GRID_W).astype(F32)[:, None]
    colp = (pos % GRID_W).astype(F32)[:, None]
    lane = jnp.arange(LANES)

    def table(width):
        half, quarter = width // 2, width // 4
        i = lane % width
        freq = ROPE_BASE ** (-(2.0 * (i % quarter).astype(F32)) / half)
        ang = jnp.where((i // half) == 0, row, colp) * freq[None, :]
        sign = jnp.where((i % half) < quarter, -1.0, 1.0)
        return jnp.cos(ang), jnp.sin(ang) * sign[None, :]

    ca, sa = table(HD_A)
    cb, sb = table(DC_B)
    return ca, sa, cb, sb


def _prep_layer(p, l):
    d = D_MODEL
    w_in = p["w_in"][l]
    kvw = KV_A * HD_A
    qa = w_in[:, :W_A].reshape(d, H_A, HD_A)
    qa_pad = jnp.zeros((d, H_A, 2, HD_A), w_in.dtype)
    for h in range(H_A):
        qa_pad = qa_pad.at[:, h, h // GQ_A].set(qa[:, h])
    w_in_p = jnp.concatenate([qa_pad.reshape(d, QA_COLS), w_in[:, W_A:]], axis=1).astype(BF16)
    gains = jnp.stack([jnp.tile(p["q_norm_a"][l], LANES // HD_A) * (HD_A ** -0.5),
                       jnp.tile(p["k_norm_a"][l], LANES // HD_A),
                       jnp.tile(p["q_norm_b"][l], LANES // DC_B) * (DC_B ** -0.5 * LOG2E),
                       jnp.tile(p["k_norm_b"][l], LANES // DC_B)], axis=0).astype(F32)
    lp = {k: p[k][l] for k in ("ssm_lam_re", "ssm_lam_im", "ssm_log_dt", "ssm_b_re", "ssm_b_im",
                               "ssm_c_re", "ssm_c_im", "ssm_d")}
    wr_hi, wr_lo = _split_bf16(p["w_router"][l].T.astype(F32))
    hidden = N_EXPERTS * F_EXP
    return dict(
        w_in_p=w_in_p, gains=gains,
        g1=p["norm1_g"][l].reshape(1, d).astype(F32), g2=p["norm2_g"][l].reshape(1, d).astype(F32),
        sink=p["sink_a"][l].astype(F32), lam_b=p["lam_b"][l].astype(F32),
        subln=jnp.tile(p["subln_b"][l], LANES // HD_B).reshape(1, LANES).astype(F32),
        lam_init=0.8 - 0.6 * math.exp(-0.3 * l),
        ssm=_ssm_matrices(lp),
        w_glu=p["w_glu"][l].astype(BF16), w_out=p["w_out"][l].astype(BF16),
        wr_hi=wr_hi, wr_lo=wr_lo, b_r=p["b_router"][l].reshape(N_EXPERTS, 1).astype(F32),
        w1=p["w_e1"][l].astype(BF16), w3=p["w_e3"][l].astype(BF16),
        w2=p["w_e2"][l].reshape(hidden, d).astype(BF16),
        ws1=p["w_s1"][l].astype(BF16), ws3=p["w_s3"][l].astype(BF16), ws2=p["w_s2"][l].astype(BF16),
    )


def _trunk_layer(x, mod, lw, consts, ctx):
    bsz, seq, d = x.shape
    latent = ctx is not None
    rope = consts["rope"] if latent else None
    kv_dtype = BF16 if latent else F32
    qa, ka, va, qb, kb, vb, u = _inproj(x, mod, lw["g1"], lw["w_in_p"], lw["gains"],
                                        consts["seg64"], consts["seg32"], rope, kv_dtype)
    if latent:
        oa = _attn_a(qa, ka, va, lw["sink"], (ctx["ak"], ctx["av"]))
        ob = _attn_b(qb, [kb, ctx["bk"]], [vb, ctx["bv"]], lw["lam_b"], lw["subln"], lw["lam_init"])
        h0 = ctx["h0"]
    else:
        oa = _attn_a(qa, ka, va, lw["sink"], None)
        ob = _attn_b(qb, [kb], [vb], lw["lam_b"], lw["subln"], lw["lam_init"])
        h0 = jnp.zeros((N_PAIR, bsz, 8 * P_C), F32)
    y_rows, fin = _ssm(u, lw["ssm"], h0, bsz)
    x1, h2, gates = _post_mix(x, oa, ob, y_rows, mod, lw["w_glu"], lw["w_out"], lw["g2"],
                                lw["wr_hi"], lw["wr_lo"], lw["b_r"])
    out = _moe(x1.reshape(bsz * seq, d), h2.reshape(bsz * seq, d), gates.reshape(bsz * seq, N_EXPERTS), mod, seq,
               lw["w1"], lw["w3"], lw["w2"], consts["expand"], lw["ws1"], lw["ws3"], lw["ws2"])
    return out.reshape(bsz, seq, d), (ka, va, kb, vb, fin)


def kernel(x_prompt, x_sample, cache_a_k, cache_a_v, cache_b_k, cache_b_v, state_ssm_re, state_ssm_im, c, c_ctx, norm1_g, norm2_g, w_ada, b_ada, w_in, q_norm_a, k_norm_a, sink_a, q_norm_b, k_norm_b, lam_b, subln_b, ssm_lam_re, ssm_lam_im, ssm_log_dt, ssm_b_re, ssm_b_im, ssm_c_re, ssm_c_im, ssm_d, w_glu, w_out, w_router, b_router, w_e1, w_e3, w_e2, w_s1, w_s3, w_s2):
    p = dict(norm1_g=norm1_g, norm2_g=norm2_g, w_in=w_in, q_norm_a=q_norm_a, k_norm_a=k_norm_a, sink_a=sink_a,
             q_norm_b=q_norm_b, k_norm_b=k_norm_b, lam_b=lam_b, subln_b=subln_b,
             ssm_lam_re=ssm_lam_re, ssm_lam_im=ssm_lam_im, ssm_log_dt=ssm_log_dt, ssm_b_re=ssm_b_re,
             ssm_b_im=ssm_b_im, ssm_c_re=ssm_c_re, ssm_c_im=ssm_c_im, ssm_d=ssm_d, w_glu=w_glu, w_out=w_out,
             w_router=w_router, b_router=b_router, w_e1=w_e1, w_e3=w_e3, w_e2=w_e2,
             w_s1=w_s1, w_s3=w_s3, w_s2=w_s2)
    depth = w_in.shape[0]
    bsz, seq, d = x_prompt.shape
    dbsz, dseq, _ = x_sample.shape
    past = cache_a_k.shape[3]

    mod_rows = 16
    cvec = jnp.concatenate([c.astype(F32), c_ctx.astype(F32)[None],
                            jnp.zeros((mod_rows - dbsz - 1, d), F32)], axis=0)
    mods = _modulation(cvec, w_ada.astype(F32), b_ada.astype(F32))

    lane = jnp.arange(LANES)
    hidden = N_EXPERTS * F_EXP
    consts = dict(
        rope=_rope_tables(dseq),
        seg64=(lane[:, None] // HD_A == lane[None, :] // HD_A).astype(BF16),
        seg32=(lane[:, None] // DC_B == lane[None, :] // DC_B).astype(BF16),
        expand=(jnp.arange(2 * N_EXPERTS)[:, None] % N_EXPERTS == jnp.arange(hidden)[None, :] // F_EXP).astype(BF16),
    )

    xp, xs = x_prompt, x_sample
    ak, av, bk, bv, sre, sim = [], [], [], [], [], []
    for l in range(depth):
        lw = _prep_layer(p, l)
        mod_lat = mods[l, :dbsz][:, None, :]
        mod_ctx = mods[l, dbsz:dbsz + 1][:, None, :]
        xp, (k_a, v_a, k_b, v_b, fin) = _trunk_layer(xp, mod_ctx, lw, consts, None)
        ak.append(k_a.reshape(bsz, seq, KV_A, HD_A).transpose(0, 2, 1, 3))
        av.append(v_a.reshape(bsz, seq, KV_A, HD_A).transpose(0, 2, 1, 3))
        bk.append(k_b.reshape(bsz, seq, H_B, 2, DC_B).transpose(0, 2, 3, 1, 4))
        bv.append(v_b.reshape(bsz, seq, H_B, HD_B).transpose(0, 2, 1, 3))
        f_re, f_im = _ssm_state_unrows(fin)
        sre.append(f_re)
        sim.append(f_im)
        ctx = dict(
            ak=cache_a_k[:, l].transpose(0, 2, 1, 3).reshape(dbsz, past, KV_A * HD_A).astype(BF16),
            av=cache_a_v[:, l].transpose(0, 2, 1, 3).reshape(dbsz, past, KV_A * HD_A).astype(BF16),
            bk=cache_b_k[:, l].transpose(0, 3, 1, 2, 4).reshape(dbsz, past, W_B).astype(BF16),
            bv=cache_b_v[:, l].transpose(0, 2, 1, 3).reshape(dbsz, past, W_B).astype(BF16),
            h0=_ssm_state_rows(state_ssm_re[:, l], state_ssm_im[:, l]),
        )
        xs, _ = _trunk_layer(xs, mod_lat, lw, consts, ctx)
    return (xp, xs, jnp.stack(ak, axis=1), jnp.stack(av, axis=1), jnp.stack(bk, axis=1),
            jnp.stack(bv, axis=1), jnp.stack(sre, axis=1), jnp.stack(sim, axis=1))
```

```python
import functools
import math

import jax
import jax.numpy as jnp
from jax import lax
from jax.experimental import pallas as pl
from jax.experimental.pallas import tpu as pltpu

F32 = jnp.float32
BF16 = jnp.bfloat16
F8 = jnp.float8_e4m3fn
F8_RANGE = 384.0
F8_TINY = 1e-30

D_MODEL = 1024
GRID_W = 64
BLOCK = 128
H_A, KV_A, HD_A = 6, 2, 64
GQ_A = H_A // KV_A
W_A = H_A * HD_A
H_B, HD_B = 4, 64
DC_B = HD_B // 2
W_B = H_B * HD_B
SSM_CH = 16
W_C = D_MODEL - W_A - W_B
G_C = W_C // SSM_CH
P_C = 64
N_EXPERTS, TOP_K, F_EXP, F_SHARED = 64, 6, 128, 256
N_EXP_GROUPS, TOPK_GROUPS = 8, 4
PER_GROUP = N_EXPERTS // N_EXP_GROUPS
ROUTED_SCALE = 2.5
ROPE_BASE = 10000.0
EPS = 1e-6
NEG = -1e30
LOG2E = 1.4426950408889634

LANES = 128
SSM_T = 16
N_PAIR = G_C // 2
SSM_ROW = 2 * SSM_T * SSM_CH
QA_COLS = H_A * LANES
IN_COLS_P = QA_COLS + 2 * KV_A * HD_A + 3 * W_B + W_C
VMEM_LIMIT = 56 << 20


def _cparams(*sem):
    return pltpu.CompilerParams(dimension_semantics=sem, vmem_limit_bytes=VMEM_LIMIT)


def _dot(a, b):
    return jnp.dot(a, b, preferred_element_type=F32)


def _dot_nt(a, b):
    return lax.dot_general(a, b, (((1,), (1,)), ((), ())), preferred_element_type=F32)


def _split_bf16(x):
    hi = x.astype(BF16)
    lo = (x - hi.astype(F32)).astype(BF16)
    return hi, lo


def _mod_body(c_ref, w_ref, b_ref, o_ref):
    c = c_ref[...]
    s = c * jax.nn.sigmoid(c)
    s_hi, s_lo = _split_bf16(s)
    w_hi, w_lo = _split_bf16(w_ref[0])
    o_ref[0] = _dot(s_hi, w_hi) + _dot(s_lo, w_hi) + _dot(s_hi, w_lo) + b_ref[0]


def _modulation(cvec, w_ada, b_ada):
    depth, d, n = w_ada.shape
    rows = cvec.shape[0]
    tn = 768
    return pl.pallas_call(
        _mod_body,
        grid=(depth, n // tn),
        in_specs=[pl.BlockSpec((rows, d), lambda l, j: (0, 0)),
                  pl.BlockSpec((1, d, tn), lambda l, j: (l, 0, j)),
                  pl.BlockSpec((1, 1, tn), lambda l, j: (l, 0, j))],
        out_specs=pl.BlockSpec((1, rows, tn), lambda l, j: (l, 0, j)),
        out_shape=jax.ShapeDtypeStruct((depth, rows, n), F32),
        compiler_params=_cparams("parallel", "parallel"),
        name="adaln_mod",
    )(cvec, w_ada, b_ada.reshape(depth, 1, n))


def _inproj_body(*refs, latent):
    if latent:
        (x_ref, mod_ref, g1_ref, w_ref, gains_ref, s64_ref, s32_ref, ca_ref, sa_ref, cb_ref, sb_ref,
         qa_ref, ka_ref, va_ref, qb_ref, kb_ref, vb_ref, u_ref, u_scr) = refs
    else:
        (x_ref, mod_ref, g1_ref, w_ref, gains_ref, s64_ref, s32_ref,
         qa_ref, ka_ref, va_ref, qb_ref, kb_ref, vb_ref, u_ref, u_scr) = refs
    d = D_MODEL
    x = x_ref[0]
    mod = mod_ref[0]
    xn = x * lax.rsqrt(jnp.mean(x * x, axis=-1, keepdims=True) + EPS) * g1_ref[...]
    h = xn * (1.0 + mod[:, d:2 * d]) + mod[:, 0:d]
    acc = _dot(h.astype(BF16), w_ref[...])

    tm = x.shape[0]
    lane = lax.broadcasted_iota(jnp.int32, (tm, LANES), 1)
    first_a = (lane % 32) < 16
    first_b = (lane % 16) < 8

    def normed(xb, seg_ref, inv_n, gain):
        ss = _dot((xb * xb).astype(BF16), seg_ref[...])
        return xb * lax.rsqrt(ss * inv_n + EPS) * gain

    def rope_a(y):
        if not latent:
            return y
        sw = jnp.where(first_a, pltpu.roll(y, LANES - 16, 1), pltpu.roll(y, 16, 1))
        return y * ca_ref[...] + sw * sa_ref[...]

    def rope_b(y):
        if not latent:
            return y
        sw = jnp.where(first_b, pltpu.roll(y, LANES - 8, 1), pltpu.roll(y, 8, 1))
        return y * cb_ref[...] + sw * sb_ref[...]

    gains = gains_ref[...]
    off = 0
    for b in range(H_A):
        y = normed(acc[:, off:off + LANES], s64_ref, 1.0 / HD_A, gains[0:1])
        qa_ref[0, :, b * LANES:(b + 1) * LANES] = rope_a(y).astype(qa_ref.dtype)
        off += LANES
    y = normed(acc[:, off:off + LANES], s64_ref, 1.0 / HD_A, gains[1:2])
    ka_ref[0] = rope_a(y).astype(ka_ref.dtype)
    off += LANES
    va_ref[0] = acc[:, off:off + LANES].astype(va_ref.dtype)
    off += LANES
    for b in range(W_B // LANES):
        y = normed(acc[:, off:off + LANES], s32_ref, 1.0 / DC_B, gains[2:3])
        qb_ref[0, :, b * LANES:(b + 1) * LANES] = rope_b(y).astype(qb_ref.dtype)
        off += LANES
    for b in range(W_B // LANES):
        y = normed(acc[:, off:off + LANES], s32_ref, 1.0 / DC_B, gains[3:4])
        kb_ref[0, :, b * LANES:(b + 1) * LANES] = rope_b(y).astype(kb_ref.dtype)
        off += LANES
    vb_ref[0] = acc[:, off:off + W_B].astype(vb_ref.dtype)
    off += W_B
    for blk in range(W_C // LANES):
        u_scr[blk] = acc[:, off + blk * LANES:off + (blk + 1) * LANES]
    pw = 2 * SSM_CH
    for t in range(SSM_T):
        for blk in range(W_C // LANES):
            xt = u_scr[blk, pl.ds(t, tm // SSM_T, stride=SSM_T), :]
            for pp in range(LANES // pw):
                u_ref[blk * (LANES // pw) + pp, :, t * pw:(t + 1) * pw] = xt[:, pp * pw:(pp + 1) * pw].astype(u_ref.dtype)


def _inproj(x, mod, g1, w_in_p, gains, seg64, seg32, rope, kv_dtype):
    bsz, seq, d = x.shape
    latent = rope is not None
    tm = 512 if seq % 512 == 0 else 256
    bm = mod.shape[0]
    mod_idx = (lambda b, i: (b, 0, 0)) if bm > 1 else (lambda b, i: (0, 0, 0))
    const2 = lambda b, i: (0, 0)
    tok = lambda w: pl.BlockSpec((1, tm, w), lambda b, i: (b, i, 0))
    in_specs = [tok(d),
                pl.BlockSpec((1, 1, 6 * d), mod_idx),
                pl.BlockSpec((1, d), const2),
                pl.BlockSpec((d, IN_COLS_P), const2),
                pl.BlockSpec((4, LANES), const2),
                pl.BlockSpec((LANES, LANES), const2),
                pl.BlockSpec((LANES, LANES), const2)]
    args = [x, mod, g1, w_in_p, gains, seg64, seg32]
    if latent:
        in_specs += [pl.BlockSpec((tm, LANES), lambda b, i: (i, 0))] * 4
        args += list(rope)
    widths = (QA_COLS, KV_A * HD_A, KV_A * HD_A, W_B, W_B, W_B)
    dtypes = (BF16, kv_dtype, kv_dtype, BF16, kv_dtype, kv_dtype)
    nt = seq // tm
    rows = tm // SSM_T
    u_spec = pl.BlockSpec((N_PAIR, rows, SSM_ROW), lambda b, i: (0, b * nt + i, 0))
    u_shape = jax.ShapeDtypeStruct((N_PAIR, bsz * seq // SSM_T, SSM_ROW), BF16)
    return pl.pallas_call(
        functools.partial(_inproj_body, latent=latent),
        grid=(bsz, nt),
        in_specs=in_specs,
        out_specs=[tok(w) for w in widths] + [u_spec],
        out_shape=[jax.ShapeDtypeStruct((bsz, seq, w), dt) for w, dt in zip(widths, dtypes)] + [u_shape],
        scratch_shapes=[pltpu.VMEM((W_C // LANES, tm, LANES), F32)],
        compiler_params=_cparams("parallel", "parallel"),
        name="inproj_latent" if latent else "inproj_ctx",
    )(*args)


def _attn_a_body(sink_ref, q_ref, *refs, latent, nblk):
    o_ref = refs[-1]
    nk = (len(refs) - 1) // 2
    kcat = jnp.concatenate([r[0].astype(BF16) for r in refs[:nk]], axis=0)
    vcat = jnp.concatenate([r[0].astype(BF16) for r in refs[nk:2 * nk]], axis=0)
    rows = GQ_A * BLOCK
    cols = kcat.shape[0]
    rowi = lax.broadcasted_iota(jnp.int32, (rows, 1), 0)
    if latent:
        i = pl.program_id(1)
        r = lax.broadcasted_iota(jnp.int32, (rows, cols), 0) & (BLOCK - 1)
        c = lax.broadcasted_iota(jnp.int32, (rows, cols), 1)
        p_off = jnp.where(i > 0, 0, 2 * BLOCK)
        n_off = jnp.where(i < nblk - 1, 0, 2 * BLOCK)
        prev_ok = (c >= r + p_off) | (c >= BLOCK)
        next_ok = ((c - 2 * BLOCK + n_off) <= r) | (c < 2 * BLOCK) | (c >= 3 * BLOCK)
        valid = prev_ok & next_ok
    lane = lax.broadcasted_iota(jnp.int32, (BLOCK, LANES), 1)
    heads = []
    for j in range(KV_A):
        q3 = jnp.concatenate([q_ref[0, :, (GQ_A * j + g) * LANES:(GQ_A * j + g + 1) * LANES]
                              for g in range(GQ_A)], axis=0)
        s = _dot_nt(q3, kcat)
        if latent:
            s = jnp.where(valid, s, NEG)
        sink = jnp.where(rowi < BLOCK, sink_ref[GQ_A * j],
                         jnp.where(rowi < 2 * BLOCK, sink_ref[GQ_A * j + 1], sink_ref[GQ_A * j + 2]))
        m = jnp.maximum(jnp.max(s, axis=-1, keepdims=True), sink)
        e = jnp.exp(s - m)
        den = jnp.sum(e, axis=-1, keepdims=True) + jnp.exp(sink - m)
        o = _dot(e.astype(BF16), vcat) / den
        for g in range(GQ_A):
            heads.append((j, o[g * BLOCK:(g + 1) * BLOCK]))
    for blk in range(H_A // 2):
        (j0, o0), (j1, o1) = heads[2 * blk], heads[2 * blk + 1]
        lo = o0 if j0 == 0 else pltpu.roll(o0, HD_A, 1)
        hi = o1 if j1 == 1 else pltpu.roll(o1, HD_A, 1)
        o_ref[0, :, blk * LANES:(blk + 1) * LANES] = jnp.where(lane < HD_A, lo, hi).astype(o_ref.dtype)


def _attn_a(qa, ka, va, sink, ctx_kv):
    bsz, seq, _ = qa.shape
    nblk = seq // BLOCK
    latent = ctx_kv is not None
    kvw = KV_A * HD_A
    if latent:
        past = ctx_kv[0].shape[1]
        band = [pl.BlockSpec((1, BLOCK, kvw), lambda b, i: (b, jnp.maximum(i - 1, 0), 0)),
                pl.BlockSpec((1, BLOCK, kvw), lambda b, i: (b, i, 0)),
                pl.BlockSpec((1, BLOCK, kvw), lambda b, i: (b, jnp.minimum(i + 1, nblk - 1), 0)),
                pl.BlockSpec((1, past, kvw), lambda b, i: (b, 0, 0))]
        kv_specs = band + band
        kv_args = [ka, ka, ka, ctx_kv[0], va, va, va, ctx_kv[1]]
    else:
        kv_specs = [pl.BlockSpec((1, seq, kvw), lambda b, i: (b, 0, 0))] * 2
        kv_args = [ka, va]
    return pl.pallas_call(
        functools.partial(_attn_a_body, latent=latent, nblk=nblk),
        grid=(bsz, nblk),
        in_specs=[pl.BlockSpec(memory_space=pltpu.SMEM),
                  pl.BlockSpec((1, BLOCK, QA_COLS), lambda b, i: (b, i, 0))] + kv_specs,
        out_specs=pl.BlockSpec((1, BLOCK, W_A), lambda b, i: (b, i, 0)),
        out_shape=jax.ShapeDtypeStruct((bsz, seq, W_A), BF16),
        compiler_params=_cparams("parallel", "parallel"),
        name="attn_a_latent" if latent else "attn_a_ctx",
    )(sink, qa, *kv_args)


def _attn_b_body(lam_ref, gain_ref, q_ref, *refs, part_lens, lam_init, kc):
    npart = len(part_lens)
    k_refs, v_refs = refs[:npart], refs[npart:2 * npart]
    o_ref, s_scr, vm_scr = refs[2 * npart:]
    tq = q_ref.shape[1]
    chunks = []
    col = 0
    for p, plen in enumerate(part_lens):
        step = min(kc, plen)
        for start in range(0, plen, step):
            chunks.append((p, start, col, step))
            col += step

    @pl.when(pl.program_id(2) == 0)
    def _():
        off = 0
        for p, plen in enumerate(part_lens):
            v = v_refs[p][0].astype(BF16)
            lane_v = lax.broadcasted_iota(jnp.int32, (plen, LANES), 1)
            for h in range(2):
                own = (lane_v >= h * HD_B) & (lane_v < (h + 1) * HD_B)
                ones = jnp.where(lane_v == (1 - h) * HD_B, 1.0, 0.0).astype(BF16)
                vm_scr[h, off:off + plen, :] = jnp.where(own, v, ones)
            off += plen

    lv = lam_ref[...]
    lam = (jnp.exp(jnp.sum(lv[0:1] * lv[1:2], axis=-1, keepdims=True))
           - jnp.exp(jnp.sum(lv[2:3] * lv[3:4], axis=-1, keepdims=True)) + lam_init)
    q = q_ref[0]
    lane_q = lax.broadcasted_iota(jnp.int32, (tq, LANES), 1)
    total = jnp.zeros((tq, LANES), F32)
    for h in range(2):
        qc = [jnp.where((lane_q >= h * HD_B + c * DC_B) & (lane_q < h * HD_B + (c + 1) * DC_B), q, jnp.zeros_like(q))
              for c in range(2)]
        rows = [slice(c * tq, (c + 1) * tq) for c in range(2)]
        macc = [None, None]
        for p, start, col, step in chunks:
            kch = k_refs[p][0, start:start + step, :].astype(BF16)
            for c in range(2):
                s = _dot_nt(qc[c], kch)
                s_scr[rows[c], col:col + step] = s
                for j in range(step // LANES):
                    t = s[:, j * LANES:(j + 1) * LANES]
                    macc[c] = t if macc[c] is None else jnp.maximum(macc[c], t)
        m = [jnp.max(macc[c], axis=-1, keepdims=True) for c in range(2)]
        acc = [jnp.zeros((tq, LANES), F32) for _ in range(2)]
        for p, start, col, step in chunks:
            vch = vm_scr[h, col:col + step, :]
            for c in range(2):
                e = jnp.exp2(s_scr[rows[c], col:col + step] - m[c]).astype(BF16)
                acc[c] = acc[c] + _dot(e, vch)
        o2 = [acc[c] / jnp.sum(jnp.where(lane_q == (1 - h) * HD_B, acc[c], 0.0), axis=-1, keepdims=True)
              for c in range(2)]
        own = (lane_q >= h * HD_B) & (lane_q < (h + 1) * HD_B)
        total = total + jnp.where(own, o2[0] - lam * o2[1], 0.0)
    sq = total * total
    ss_lo = jnp.sum(jnp.where(lane_q < HD_B, sq, 0.0), axis=-1, keepdims=True)
    ss_hi = jnp.sum(jnp.where(lane_q >= HD_B, sq, 0.0), axis=-1, keepdims=True)
    rinv = jnp.where(lane_q < HD_B, lax.rsqrt(ss_lo * (1.0 / HD_B) + EPS), lax.rsqrt(ss_hi * (1.0 / HD_B) + EPS))
    o_ref[0] = (total * rinv * gain_ref[...] * (1.0 - lam_init)).astype(o_ref.dtype)


def _attn_b(qb, k_parts, v_parts, lam_b, gain, lam_init):
    bsz, seq, _ = qb.shape
    tq = 256
    part_lens = tuple(k.shape[1] for k in k_parts)
    lk = sum(part_lens)
    kv_specs = [pl.BlockSpec((1, n, LANES), lambda b, hp, i: (b, 0, hp)) for n in part_lens]
    return pl.pallas_call(
        functools.partial(_attn_b_body, part_lens=part_lens, lam_init=lam_init, kc=512),
        grid=(bsz, W_B // LANES, seq // tq),
        in_specs=[pl.BlockSpec((4, DC_B), lambda b, hp, i: (0, 0)),
                  pl.BlockSpec((1, LANES), lambda b, hp, i: (0, 0)),
                  pl.BlockSpec((1, tq, LANES), lambda b, hp, i: (b, i, hp))] + kv_specs + kv_specs,
        out_specs=pl.BlockSpec((1, tq, LANES), lambda b, hp, i: (b, i, hp)),
        out_shape=jax.ShapeDtypeStruct((bsz, seq, W_B), BF16),
        scratch_shapes=[pltpu.VMEM((2 * tq, lk), F32), pltpu.VMEM((2, lk, LANES), BF16)],
        compiler_params=_cparams("parallel", "parallel", "arbitrary"),
        name="attn_b_latent" if len(k_parts) > 1 else "attn_b_ctx",
    )(lam_b, gain, qb, *k_parts, *v_parts)


def _ssm_body(u_ref, m_ref, g_ref, cc_ref, a_ref, h0_ref, y_ref, fin_ref, s_scr, h_scr, *, nb, nc):
    u = u_ref[0]
    col = lambda k: slice(k * LANES, (k + 1) * LANES)
    s = _dot(u, g_ref[0])
    for k in range(4):
        s_scr[k] = s[:, col(k)]
    a = a_ref[0]
    afr, afi, abr, abi = (jnp.broadcast_to(a[k:k + 1], (nb, LANES)) for k in range(4))
    h0 = h0_ref[0]

    def step(c, carry):
        fr, fi, br, bi = carry
        rf = pl.ds(c, nb, stride=nc)
        rb = pl.ds(nc - 1 - c, nb, stride=nc)
        h_scr[0, rf, :] = fr
        h_scr[1, rf, :] = fi
        h_scr[2, rb, :] = br
        h_scr[3, rb, :] = bi
        nfr = afr * fr - afi * fi + s_scr[0, rf, :]
        nfi = afr * fi + afi * fr + s_scr[1, rf, :]
        nbr = abr * br - abi * bi + s_scr[2, rb, :]
        nbi = abr * bi + abi * br + s_scr[3, rb, :]
        return nfr, nfi, nbr, nbi

    fin = lax.fori_loop(0, nc, step, tuple(h0[:, col(k)] for k in range(4)))
    for k in range(4):
        fin_ref[0, :, col(k)] = fin[k]
    hin = jnp.concatenate([h_scr[k] for k in range(4)], axis=1).astype(BF16)
    y = _dot(u, m_ref[0]) + _dot(hin, cc_ref[0])
    y_ref[0] = y.astype(y_ref.dtype)


def _ssm(u_rows, mats, h0, nb):
    npair, rows, w = u_rows.shape
    nc = rows // nb
    mat_spec = pl.BlockSpec((1, w, w), lambda p: (p, 0, 0))
    return pl.pallas_call(
        functools.partial(_ssm_body, nb=nb, nc=nc),
        grid=(npair,),
        in_specs=[pl.BlockSpec((1, rows, w), lambda p: (p, 0, 0)), mat_spec, mat_spec, mat_spec,
                  pl.BlockSpec((1, 4, LANES), lambda p: (p, 0, 0)),
                  pl.BlockSpec((1, nb, w), lambda p: (p, 0, 0))],
        out_specs=[pl.BlockSpec((1, rows, w), lambda p: (p, 0, 0)),
                   pl.BlockSpec((1, nb, w), lambda p: (p, 0, 0))],
        out_shape=[jax.ShapeDtypeStruct((npair, rows, w), BF16),
                   jax.ShapeDtypeStruct((npair, nb, w), F32)],
        scratch_shapes=[pltpu.VMEM((4, rows, LANES), F32), pltpu.VMEM((4, rows, LANES), F32)],
        compiler_params=_cparams("parallel"),
        name="ssm_scan",
    )(u_rows, mats["m"], mats["g"], mats["cc"], mats["a16"], h0)


def _ssm_matrices(lp):
    t = SSM_T
    ks = jnp.arange(t + 1, dtype=F32)
    dirs = []
    for d in range(2):
        lam = lax.complex(lp["ssm_lam_re"][d].astype(F32), lp["ssm_lam_im"][d].astype(F32))
        dt = jnp.exp(lp["ssm_log_dt"][d].astype(F32))[:, None]
        a_bar = jnp.exp(lam * dt)
        b_bar = ((a_bar - 1.0) / lam)[..., None] * lax.complex(lp["ssm_b_re"][d].astype(F32),
                                                               lp["ssm_b_im"][d].astype(F32))
        c_mat = lax.complex(lp["ssm_c_re"][d].astype(F32), lp["ssm_c_im"][d].astype(F32))
        pw = jnp.exp((lam * dt)[None] * ks[:, None, None].astype(jnp.complex64))
        kern = jnp.real(jnp.einsum("gop,kgp,gpi->gkoi", c_mat, pw[:t], b_bar))
        dirs.append((pw, b_bar, c_mat, kern))
    (pw_f, bb_f, cm_f, k_f), (pw_b, bb_b, cm_b, k_b) = dirs
    eye2 = jnp.eye(2, dtype=F32)
    ch, pw2 = SSM_CH, 2 * SSM_CH
    hi = lax.Precision.HIGHEST

    def pair_bd(x):
        r, c = x.shape[1:]
        return jnp.einsum("pgrc,gh->pgrhc", x.reshape(N_PAIR, 2, r, c), eye2.astype(x.dtype)).reshape(N_PAIR, 2 * r, 2 * c)

    def pair_vec(x):
        return x.reshape(x.shape[0], N_PAIR, 2 * P_C).transpose(1, 0, 2)

    def lag_blocks(kern):
        x = kern.transpose(1, 0, 3, 2).reshape(t * G_C, ch, ch)
        return jnp.stack([pair_bd(x[l * G_C:(l + 1) * G_C]) for l in range(t)], axis=1)
    kp_f, kp_b = lag_blocks(k_f), lag_blocks(k_b)
    d_blk = pair_bd(lp["ssm_d"].astype(F32)[:, :, None] * jnp.eye(ch, dtype=F32)[None])
    center = (kp_f[:, 0] + kp_b[:, 0] + d_blk)[:, None]
    band = jnp.concatenate([kp_b[:, :0:-1], center, kp_f[:, 1:]], axis=1)
    band = band.transpose(0, 2, 1, 3).reshape(N_PAIR, pw2, (2 * t - 1) * pw2)
    m_p = jnp.concatenate([band[:, :, (t - 1 - s) * pw2:(t - 1 - s) * pw2 + SSM_ROW] for s in range(t)], axis=1)

    def inject(pw_sel, b_bar):
        x1 = jnp.repeat(pair_vec(pw_sel), pw2, axis=1)
        x2 = jnp.tile(pair_bd(b_bar.transpose(0, 2, 1)), (1, t, 1))
        return x1 * x2
    g_f = inject(pw_f[t - 1 - jnp.arange(t)], bb_f)
    g_b = inject(pw_b[jnp.arange(t)], bb_b)
    g_p = jnp.concatenate([jnp.real(g_f), jnp.imag(g_f), jnp.real(g_b), jnp.imag(g_b)], axis=2)

    lane = jnp.arange(SSM_ROW)
    exp_t = (jnp.arange(t)[:, None] == lane[None, :] // pw2).astype(F32)
    exp_c = (jnp.arange(pw2)[:, None] == lane[None, :] % pw2).astype(F32)

    def widen(x, e):
        f = lambda v: jnp.einsum("pqk,kx->pqx", v, e, precision=hi)
        return lax.complex(f(jnp.real(x)), f(jnp.imag(x)))

    def readout(pw_sel, c_mat):
        y1 = widen(pair_vec(pw_sel).transpose(0, 2, 1), exp_t)
        y2 = widen(pair_bd(c_mat.transpose(0, 2, 1)), exp_c)
        return y1 * y2
    z_f = readout(pw_f[1 + jnp.arange(t)], cm_f)
    z_b = readout(pw_b[t - jnp.arange(t)], cm_b)
    cc_p = jnp.concatenate([jnp.real(z_f), -jnp.imag(z_f), jnp.real(z_b), -jnp.imag(z_b)], axis=1)
    a16 = jnp.stack([jnp.real(pw_f[t]), jnp.imag(pw_f[t]), jnp.real(pw_b[t]), jnp.imag(pw_b[t])], axis=0)
    a16 = a16.reshape(4, N_PAIR, 2 * P_C).transpose(1, 0, 2)
    return dict(m=m_p.astype(BF16), g=g_p.astype(BF16), cc=cc_p.astype(BF16), a16=a16)


def _ssm_state_rows(s_re, s_im):
    bsz = s_re.shape[0]
    parts = [s_re[:, 0], s_im[:, 0], s_re[:, 1], s_im[:, 1]]
    st = jnp.stack([p.reshape(bsz, N_PAIR, 2 * P_C) for p in parts], axis=2)
    return st.transpose(1, 0, 2, 3).reshape(N_PAIR, bsz, 8 * P_C).astype(F32)


def _ssm_state_unrows(fin):
    npair, bsz, _ = fin.shape
    st = fin.reshape(npair, bsz, 4, 2, P_C).transpose(1, 2, 0, 3, 4).reshape(bsz, 4, G_C, P_C)
    return jnp.stack([st[:, 0], st[:, 2]], axis=1), jnp.stack([st[:, 1], st[:, 3]], axis=1)


def _route(scores, bias):
    tm = scores.shape[1]
    biased = scores + bias
    iota8 = lax.broadcasted_iota(jnp.int32, (PER_GROUP, tm), 0)
    grp = [biased[PER_GROUP * g:PER_GROUP * (g + 1)] for g in range(N_EXP_GROUPS)]
    gscore = []
    for v in grp:
        m1 = jnp.max(v, axis=0, keepdims=True)
        first = jnp.min(jnp.where(v == m1, iota8, PER_GROUP), axis=0, keepdims=True)
        m2 = jnp.max(jnp.where(iota8 == first, -jnp.inf, v), axis=0, keepdims=True)
        gscore.append(m1 + m2)
    masked = []
    for g in range(N_EXP_GROUPS):
        rank = jnp.zeros((1, tm), jnp.int32)
        for o in range(N_EXP_GROUPS):
            if o == g:
                continue
            ahead = (gscore[o] >= gscore[g]) if o < g else (gscore[o] > gscore[g])
            rank = rank + jnp.where(ahead, 1, 0)
        masked.append(jnp.where(rank < TOPK_GROUPS, grp[g], -jnp.inf))
    chosen = [None] * N_EXP_GROUPS
    for _ in range(TOP_K):
        best = masked[0]
        for v in masked[1:]:
            best = jnp.maximum(best, v)
        best = jnp.max(best, axis=0, keepdims=True)
        first = jnp.full((1, tm), N_EXPERTS, jnp.int32)
        for g, v in enumerate(masked):
            cand = jnp.min(jnp.where(v == best, iota8 + PER_GROUP * g, N_EXPERTS), axis=0, keepdims=True)
            first = jnp.minimum(first, cand)
        for g in range(N_EXP_GROUPS):
            hit = (iota8 + PER_GROUP * g) == first
            chosen[g] = hit if chosen[g] is None else (chosen[g] | hit)
            masked[g] = jnp.where(hit, -jnp.inf, masked[g])
    w = [jnp.where(chosen[g], scores[PER_GROUP * g:PER_GROUP * (g + 1)], 0.0) for g in range(N_EXP_GROUPS)]
    wsum = w[0]
    for v in w[1:]:
        wsum = wsum + v
    wsum = jnp.sum(wsum, axis=0, keepdims=True)
    return jnp.concatenate([v / wsum * ROUTED_SCALE for v in w], axis=0)


def _post_body(x_ref, oa_ref, ob_ref, y_ref, mod_ref, wglu_ref, wout_ref, g2_ref, wrh_ref, wrl_ref, br_ref,
               x1_ref, h2_ref, gate_ref, y_scr):
    d = D_MODEL
    tm = x_ref.shape[1]
    pw = 2 * SSM_CH
    for t in range(SSM_T):
        for blk in range(W_C // LANES):
            piece = jnp.concatenate([y_ref[blk * (LANES // pw) + pp, :, t * pw:(t + 1) * pw].astype(F32)
                                     for pp in range(LANES // pw)], axis=1)
            y_scr[blk, pl.ds(t, tm // SSM_T, stride=SSM_T), :] = piece
    g = jax.nn.gelu(jnp.concatenate([y_scr[blk] for blk in range(W_C // LANES)], axis=1))
    oc = g * jax.nn.sigmoid(_dot(g.astype(BF16), wglu_ref[...]))
    mix = (_dot(oa_ref[0], wout_ref[0:W_A]) + _dot(ob_ref[0], wout_ref[W_A:W_A + W_B])
           + _dot(oc.astype(BF16), wout_ref[W_A + W_B:]))
    mod = mod_ref[0]
    x1 = x_ref[0] + mod[:, 2 * d:3 * d] * mix
    x1_ref[0] = x1
    xn = x1 * lax.rsqrt(jnp.mean(x1 * x1, axis=-1, keepdims=True) + EPS) * g2_ref[...]
    h2 = xn * (1.0 + mod[:, 4 * d:5 * d]) + mod[:, 3 * d:4 * d]
    h_hi, h_lo = _split_bf16(h2)
    h2_ref[0] = h_hi
    logits = _dot_nt(wrh_ref[...], h_hi) + _dot_nt(wrh_ref[...], h_lo) + _dot_nt(wrl_ref[...], h_hi)
    gate_ref[0] = _route(jax.nn.sigmoid(logits), br_ref[...]).T


def _post_mix(x, oa, ob, y, mod, w_glu, w_out, g2, wr_hi, wr_lo, b_r):
    bsz, seq, d = x.shape
    tm = 512 if seq % 512 == 0 else 256
    bm = mod.shape[0]
    mod_idx = (lambda b, i: (b, 0, 0)) if bm > 1 else (lambda b, i: (0, 0, 0))
    const2 = lambda b, i: (0, 0)
    tok = lambda w: pl.BlockSpec((1, tm, w), lambda b, i: (b, i, 0))
    nt = seq // tm
    return pl.pallas_call(
        _post_body,
        grid=(bsz, nt),
        in_specs=[tok(d), tok(W_A), tok(W_B),
                  pl.BlockSpec((N_PAIR, tm // SSM_T, SSM_ROW), lambda b, i: (0, b * nt + i, 0)),
                  pl.BlockSpec((1, 1, 6 * d), mod_idx),
                  pl.BlockSpec((W_C, W_C), const2),
                  pl.BlockSpec((d, d), const2),
                  pl.BlockSpec((1, d), const2),
                  pl.BlockSpec((N_EXPERTS, d), const2),
                  pl.BlockSpec((N_EXPERTS, d), const2),
                  pl.BlockSpec((N_EXPERTS, 1), const2)],
        out_specs=[tok(d), tok(d), tok(N_EXPERTS)],
        out_shape=[jax.ShapeDtypeStruct((bsz, seq, d), F32),
                   jax.ShapeDtypeStruct((bsz, seq, d), BF16),
                   jax.ShapeDtypeStruct((bsz, seq, N_EXPERTS), F32)],
        scratch_shapes=[pltpu.VMEM((W_C // LANES, tm, LANES), F32)],
        compiler_params=_cparams("parallel", "parallel"),
        name="post_mix",
    )(x, oa, ob, y, mod, w_glu, w_out, g2, wr_hi, wr_lo, b_r)


def _moe_body(x1_ref, h_ref, gate_ref, g2_ref, w1_ref, w3_ref, w2_ref, c1_ref, c32_ref, ex_ref,
              s1_ref, s3_ref, s2_ref, o_ref, acc_ref, h8_ref, hs_ref):
    j = pl.program_id(1)

    @pl.when(j == 0)
    def _():
        h = h_ref[...]
        a = _dot(h, s1_ref[...])
        acc_ref[...] = _dot((a * jax.nn.sigmoid(a) * _dot(h, s3_ref[...])).astype(BF16), s2_ref[...])
        hf = h.astype(F32)
        sc = jnp.maximum(jnp.max(jnp.abs(hf), axis=-1, keepdims=True), F8_TINY) * (1.0 / F8_RANGE)
        hs_ref[...] = sc
        h8_ref[...] = (hf * (1.0 / sc)).astype(F8)

    ne = w1_ref.shape[0]
    h8 = h8_ref[...]
    hs = hs_ref[...]
    a = _dot(h8, jnp.concatenate([w1_ref[e] for e in range(ne)], axis=1)) * c1_ref[...] * hs
    b = _dot(h8, jnp.concatenate([w3_ref[e] for e in range(ne)], axis=1))
    gexp = _dot(jnp.concatenate(_split_bf16(gate_ref[...]), axis=1), ex_ref[...])
    hid = a * jax.nn.sigmoid(a) * b * gexp * c32_ref[...]
    sc = jnp.maximum(jnp.max(jnp.abs(hid), axis=-1, keepdims=True), F8_TINY) * (1.0 / F8_RANGE)
    acc_ref[...] += _dot((hid * (1.0 / sc)).astype(F8), w2_ref[...]) * (sc * hs)

    @pl.when(j == pl.num_programs(1) - 1)
    def _():
        o_ref[...] = x1_ref[...] + g2_ref[0] * acc_ref[...]


def _moe(x1, h2, gates, mod, seq, ew, expand, ws1, ws3, ws2):
    tokens, d = x1.shape
    bm = mod.shape[0]
    span = seq if bm > 1 else tokens
    tm = next(t for t in (1024, 512, 256) if span % t == 0)
    per_b = seq // tm if bm > 1 else 1
    mod_idx = (lambda i, j: (i // per_b, 0, 5)) if bm > 1 else (lambda i, j: (0, 0, 5))
    ne = 4
    fc = ne * F_EXP
    hidden = ew["w2"].shape[0]
    const2 = lambda i, j: (0, 0)
    chunk_row = pl.BlockSpec((1, fc), lambda i, j: (0, j))
    return pl.pallas_call(
        _moe_body,
        grid=(tokens // tm, hidden // fc),
        in_specs=[pl.BlockSpec((tm, d), lambda i, j: (i, 0)),
                  pl.BlockSpec((tm, d), lambda i, j: (i, 0)),
                  pl.BlockSpec((tm, N_EXPERTS), lambda i, j: (i, 0)),
                  pl.BlockSpec((1, 1, d), mod_idx),
                  pl.BlockSpec((ne, d, F_EXP), lambda i, j: (j, 0, 0)),
                  pl.BlockSpec((ne, d, F_EXP), lambda i, j: (j, 0, 0)),
                  pl.BlockSpec((fc, d), lambda i, j: (j, 0)),
                  chunk_row, chunk_row,
                  pl.BlockSpec((2 * N_EXPERTS, fc), lambda i, j: (0, j)),
                  pl.BlockSpec((d, F_SHARED), const2),
                  pl.BlockSpec((d, F_SHARED), const2),
                  pl.BlockSpec((F_SHARED, d), const2)],
        out_specs=pl.BlockSpec((tm, d), lambda i, j: (i, 0)),
        out_shape=jax.ShapeDtypeStruct((tokens, d), F32),
        scratch_shapes=[pltpu.VMEM((tm, d), F32), pltpu.VMEM((tm, d), F8), pltpu.VMEM((tm, 1), F32)],
        compiler_params=_cparams("parallel", "arbitrary"),
        name="moe",
    )(x1, h2, gates, mod, ew["w1"], ew["w3"], ew["w2"], ew["c1"], ew["c32"], expand, ws1, ws3, ws2)


def _expert_fp8(w):
    sc = jnp.maximum(jnp.max(jnp.abs(w), axis=(1, 2)), F8_TINY) * (1.0 / F8_RANGE)
    return (w / sc[:, None, None]).astype(F8), jnp.repeat(sc, F_EXP)[None, :].astype(F32)


def _prep_experts(p, l):
    w1, c1 = _expert_fp8(p["w_e1"][l].astype(F32))
    w3, c3 = _expert_fp8(p["w_e3"][l].astype(F32))
    w2, c2 = _expert_fp8(p["w_e2"][l].astype(F32))
    return dict(w1=w1, w3=w3, w2=w2.reshape(N_EXPERTS * F_EXP, D_MODEL), c1=c1, c32=c3 * c2)


def _rope_tables(seq):
    pos = jnp.arange(seq)
    row = (pos // GRID_W).astype(F32)[:, None]
    colp = (pos % GRID_W).astype(F32)[:, None]
    lane = jnp.arange(LANES)

    def table(width):
        half, quarter = width // 2, width // 4
        i = lane % width
        freq = ROPE_BASE ** (-(2.0 * (i % quarter).astype(F32)) / half)
        ang = jnp.where((i // half) == 0, row, colp) * freq[None, :]
        sign = jnp.where((i % half) < quarter, -1.0, 1.0)
        return jnp.cos(ang), jnp.sin(ang) * sign[None, :]

    ca, sa = table(HD_A)
    cb, sb = table(DC_B)
    return ca, sa, cb, sb


def _prep_layer(p, l):
    d = D_MODEL
    w_in = p["w_in"][l]
    kvw = KV_A * HD_A
    qa = w_in[:, :W_A].reshape(d, H_A, HD_A)
    qa_pad = jnp.zeros((d, H_A, 2, HD_A), w_in.dtype)
    for h in range(H_A):
        qa_pad = qa_pad.at[:, h, h // GQ_A].set(qa[:, h])
    w_in_p = jnp.concatenate([qa_pad.reshape(d, QA_COLS), w_in[:, W_A:]], axis=1).astype(BF16)
    gains = jnp.stack([jnp.tile(p["q_norm_a"][l], LANES // HD_A) * (HD_A ** -0.5),
                       jnp.tile(p["k_norm_a"][l], LANES // HD_A),
                       jnp.tile(p["q_norm_b"][l], LANES // DC_B) * (DC_B ** -0.5 * LOG2E),
                       jnp.tile(p["k_norm_b"][l], LANES // DC_B)], axis=0).astype(F32)
    lp = {k: p[k][l] for k in ("ssm_lam_re", "ssm_lam_im", "ssm_log_dt", "ssm_b_re", "ssm_b_im",
                               "ssm_c_re", "ssm_c_im", "ssm_d")}
    wr_hi, wr_lo = _split_bf16(p["w_router"][l].T.astype(F32))
    hidden = N_EXPERTS * F_EXP
    return dict(
        w_in_p=w_in_p, gains=gains,
        g1=p["norm1_g"][l].reshape(1, d).astype(F32), g2=p["norm2_g"][l].reshape(1, d).astype(F32),
        sink=p["sink_a"][l].astype(F32), lam_b=p["lam_b"][l].astype(F32),
        subln=jnp.tile(p["subln_b"][l], LANES // HD_B).reshape(1, LANES).astype(F32),
        lam_init=0.8 - 0.6 * math.exp(-0.3 * l),
        ssm=_ssm_matrices(lp),
        w_glu=p["w_glu"][l].astype(BF16), w_out=p["w_out"][l].astype(BF16),
        wr_hi=wr_hi, wr_lo=wr_lo, b_r=p["b_router"][l].reshape(N_EXPERTS, 1).astype(F32),
        experts=_prep_experts(p, l),
        ws1=p["w_s1"][l].astype(BF16), ws3=p["w_s3"][l].astype(BF16), ws2=p["w_s2"][l].astype(BF16),
    )


def _trunk_layer(x, mod, lw, consts, ctx):
    bsz, seq, d = x.shape
    latent = ctx is not None
    rope = consts["rope"] if latent else None
    kv_dtype = BF16 if latent else F32
    qa, ka, va, qb, kb, vb, u = _inproj(x, mod, lw["g1"], lw["w_in_p"], lw["gains"],
                                        consts["seg64"], consts["seg32"], rope, kv_dtype)
    if latent:
        oa = _attn_a(qa, ka, va, lw["sink"], (ctx["ak"], ctx["av"]))
        ob = _attn_b(qb, [kb, ctx["bk"]], [vb, ctx["bv"]], lw["lam_b"], lw["subln"], lw["lam_init"])
        h0 = ctx["h0"]
    else:
        oa = _attn_a(qa, ka, va, lw["sink"], None)
        ob = _attn_b(qb, [kb], [vb], lw["lam_b"], lw["subln"], lw["lam_init"])
        h0 = jnp.zeros((N_PAIR, bsz, 8 * P_C), F32)
    y_rows, fin = _ssm(u, lw["ssm"], h0, bsz)
    x1, h2, gates = _post_mix(x, oa, ob, y_rows, mod, lw["w_glu"], lw["w_out"], lw["g2"],
                                lw["wr_hi"], lw["wr_lo"], lw["b_r"])
    out = _moe(x1.reshape(bsz * seq, d), h2.reshape(bsz * seq, d), gates.reshape(bsz * seq, N_EXPERTS), mod, seq,
               lw["experts"], consts["expand"], lw["ws1"], lw["ws3"], lw["ws2"])
    return out.reshape(bsz, seq, d), (ka, va, kb, vb, fin)


def kernel(x_prompt, x_sample, cache_a_k, cache_a_v, cache_b_k, cache_b_v, state_ssm_re, state_ssm_im, c, c_ctx, norm1_g, norm2_g, w_ada, b_ada, w_in, q_norm_a, k_norm_a, sink_a, q_norm_b, k_norm_b, lam_b, subln_b, ssm_lam_re, ssm_lam_im, ssm_log_dt, ssm_b_re, ssm_b_im, ssm_c_re, ssm_c_im, ssm_d, w_glu, w_out, w_router, b_router, w_e1, w_e3, w_e2, w_s1, w_s3, w_s2):
    p = dict(norm1_g=norm1_g, norm2_g=norm2_g, w_in=w_in, q_norm_a=q_norm_a, k_norm_a=k_norm_a, sink_a=sink_a,
             q_norm_b=q_norm_b, k_norm_b=k_norm_b, lam_b=lam_b, subln_b=subln_b,
             ssm_lam_re=ssm_lam_re, ssm_lam_im=ssm_lam_im, ssm_log_dt=ssm_log_dt, ssm_b_re=ssm_b_re,
             ssm_b_im=ssm_b_im, ssm_c_re=ssm_c_re, ssm_c_im=ssm_c_im, ssm_d=ssm_d, w_glu=w_glu, w_out=w_out,
             w_router=w_router, b_router=b_router, w_e1=w_e1, w_e3=w_e3, w_e2=w_e2,
             w_s1=w_s1, w_s3=w_s3, w_s2=w_s2)
    depth = w_in.shape[0]
    bsz, seq, d = x_prompt.shape
    dbsz, dseq, _ = x_sample.shape
    past = cache_a_k.shape[3]

    mod_rows = 16
    cvec = jnp.concatenate([c.astype(F32), c_ctx.astype(F32)[None],
                            jnp.zeros((mod_rows - dbsz - 1, d), F32)], axis=0)
    mods = _modulation(cvec, w_ada.astype(F32), b_ada.astype(F32))

    lane = jnp.arange(LANES)
    hidden = N_EXPERTS * F_EXP
    consts = dict(
        rope=_rope_tables(dseq),
        seg64=(lane[:, None] // HD_A == lane[None, :] // HD_A).astype(BF16),
        seg32=(lane[:, None] // DC_B == lane[None, :] // DC_B).astype(BF16),
        expand=(jnp.arange(2 * N_EXPERTS)[:, None] % N_EXPERTS == jnp.arange(hidden)[None, :] // F_EXP).astype(BF16),
    )

    xp, xs = x_prompt, x_sample
    ak, av, bk, bv, sre, sim = [], [], [], [], [], []
    for l in range(depth):
        lw = _prep_layer(p, l)
        mod_lat = mods[l, :dbsz][:, None, :]
        mod_ctx = mods[l, dbsz:dbsz + 1][:, None, :]
        xp, (k_a, v_a, k_b, v_b, fin) = _trunk_layer(xp, mod_ctx, lw, consts, None)
        ak.append(k_a.reshape(bsz, seq, KV_A, HD_A).transpose(0, 2, 1, 3))
        av.append(v_a.reshape(bsz, seq, KV_A, HD_A).transpose(0, 2, 1, 3))
        bk.append(k_b.reshape(bsz, seq, H_B, 2, DC_B).transpose(0, 2, 3, 1, 4))
        bv.append(v_b.reshape(bsz, seq, H_B, HD_B).transpose(0, 2, 1, 3))
        f_re, f_im = _ssm_state_unrows(fin)
        sre.append(f_re)
        sim.append(f_im)
        ctx = dict(
            ak=cache_a_k[:, l].transpose(0, 2, 1, 3).reshape(dbsz, past, KV_A * HD_A).astype(BF16),
            av=cache_a_v[:, l].transpose(0, 2, 1, 3).reshape(dbsz, past, KV_A * HD_A).astype(BF16),
            bk=cache_b_k[:, l].transpose(0, 3, 1, 2, 4).reshape(dbsz, past, W_B).astype(BF16),
            bv=cache_b_v[:, l].transpose(0, 2, 1, 3).reshape(dbsz, past, W_B).astype(BF16),
            h0=_ssm_state_rows(state_ssm_re[:, l], state_ssm_im[:, l]),
        )
        xs, _ = _trunk_layer(xs, mod_lat, lw, consts, ctx)
    return (xp, xs, jnp.stack(ak, axis=1), jnp.stack(av, axis=1), jnp.stack(bk, axis=1),
            jnp.stack(bv, axis=1), jnp.stack(sre, axis=1), jnp.stack(sim, axis=1))
```

```python
import functools
import math

import jax
import jax.numpy as jnp
from jax import lax
from jax.experimental import pallas as pl
from jax.experimental.pallas import tpu as pltpu

F32 = jnp.float32
BF16 = jnp.bfloat16
F8 = jnp.float8_e4m3fn
F8_RANGE = 384.0
F8_TINY = 1e-30

D_MODEL = 1024
GRID_W = 64
BLOCK = 128
H_A, KV_A, HD_A = 6, 2, 64
GQ_A = H_A // KV_A
W_A = H_A * HD_A
H_B, HD_B = 4, 64
DC_B = HD_B // 2
W_B = H_B * HD_B
SSM_CH = 16
W_C = D_MODEL - W_A - W_B
G_C = W_C // SSM_CH
P_C = 64
N_EXPERTS, TOP_K, F_EXP, F_SHARED = 64, 6, 128, 256
N_EXP_GROUPS, TOPK_GROUPS = 8, 4
PER_GROUP = N_EXPERTS // N_EXP_GROUPS
ROUTED_SCALE = 2.5
ROPE_BASE = 10000.0
EPS = 1e-6
NEG = -1e30
LOG2E = 1.4426950408889634

LANES = 128
SSM_T = 16
N_PAIR = G_C // 2
SSM_ROW = 2 * SSM_T * SSM_CH
QA_COLS = H_A * LANES
IN_COLS_P = QA_COLS + 2 * KV_A * HD_A + 3 * W_B + W_C
VMEM_LIMIT = 56 << 20


def _cparams(*sem):
    return pltpu.CompilerParams(dimension_semantics=sem, vmem_limit_bytes=VMEM_LIMIT)


def _dot(a, b):
    return jnp.dot(a, b, preferred_element_type=F32)


def _dot_nt(a, b):
    return lax.dot_general(a, b, (((1,), (1,)), ((), ())), preferred_element_type=F32)


def _split_bf16(x):
    hi = x.astype(BF16)
    lo = (x - hi.astype(F32)).astype(BF16)
    return hi, lo


def _mod_body(c_ref, w_ref, b_ref, o_ref):
    c = c_ref[...]
    s = c * jax.nn.sigmoid(c)
    s_hi, s_lo = _split_bf16(s)
    w_hi, w_lo = _split_bf16(w_ref[0])
    o_ref[0] = _dot(s_hi, w_hi) + _dot(s_lo, w_hi) + _dot(s_hi, w_lo) + b_ref[0]


def _modulation(cvec, w_ada, b_ada):
    depth, d, n = w_ada.shape
    rows = cvec.shape[0]
    tn = 768
    return pl.pallas_call(
        _mod_body,
        grid=(depth, n // tn),
        in_specs=[pl.BlockSpec((rows, d), lambda l, j: (0, 0)),
                  pl.BlockSpec((1, d, tn), lambda l, j: (l, 0, j)),
                  pl.BlockSpec((1, 1, tn), lambda l, j: (l, 0, j))],
        out_specs=pl.BlockSpec((1, rows, tn), lambda l, j: (l, 0, j)),
        out_shape=jax.ShapeDtypeStruct((depth, rows, n), F32),
        compiler_params=_cparams("parallel", "parallel"),
        name="adaln_mod",
    )(cvec, w_ada, b_ada.reshape(depth, 1, n))


def _inproj_body(*refs, latent):
    if latent:
        (x_ref, mod_ref, g1_ref, w_ref, gains_ref, s64_ref, s32_ref, ca_ref, sa_ref, cb_ref, sb_ref,
         qa_ref, ka_ref, va_ref, qb_ref, kb_ref, vb_ref, u_ref, u_scr) = refs
    else:
        (x_ref, mod_ref, g1_ref, w_ref, gains_ref, s64_ref, s32_ref,
         qa_ref, ka_ref, va_ref, qb_ref, kb_ref, vb_ref, u_ref, u_scr) = refs
    d = D_MODEL
    x = x_ref[0]
    mod = mod_ref[0]
    xn = x * lax.rsqrt(jnp.mean(x * x, axis=-1, keepdims=True) + EPS) * g1_ref[...]
    h = xn * (1.0 + mod[:, d:2 * d]) + mod[:, 0:d]
    acc = _dot(h.astype(BF16), w_ref[...])

    tm = x.shape[0]
    lane = lax.broadcasted_iota(jnp.int32, (tm, LANES), 1)
    first_a = (lane % 32) < 16
    first_b = (lane % 16) < 8

    def normed(xb, seg_ref, inv_n, gain):
        ss = _dot((xb * xb).astype(BF16), seg_ref[...])
        return xb * lax.rsqrt(ss * inv_n + EPS) * gain

    def rope_a(y):
        if not latent:
            return y
        sw = jnp.where(first_a, pltpu.roll(y, LANES - 16, 1), pltpu.roll(y, 16, 1))
        return y * ca_ref[...] + sw * sa_ref[...]

    def rope_b(y):
        if not latent:
            return y
        sw = jnp.where(first_b, pltpu.roll(y, LANES - 8, 1), pltpu.roll(y, 8, 1))
        return y * cb_ref[...] + sw * sb_ref[...]

    gains = gains_ref[...]
    off = 0
    for b in range(H_A):
        y = normed(acc[:, off:off + LANES], s64_ref, 1.0 / HD_A, gains[0:1])
        qa_ref[0, :, b * LANES:(b + 1) * LANES] = rope_a(y).astype(qa_ref.dtype)
        off += LANES
    y = normed(acc[:, off:off + LANES], s64_ref, 1.0 / HD_A, gains[1:2])
    ka_ref[0] = rope_a(y).astype(ka_ref.dtype)
    off += LANES
    va_ref[0] = acc[:, off:off + LANES].astype(va_ref.dtype)
    off += LANES
    for b in range(W_B // LANES):
        y = normed(acc[:, off:off + LANES], s32_ref, 1.0 / DC_B, gains[2:3])
        qb_ref[0, :, b * LANES:(b + 1) * LANES] = rope_b(y).astype(qb_ref.dtype)
        off += LANES
    for b in range(W_B // LANES):
        y = normed(acc[:, off:off + LANES], s32_ref, 1.0 / DC_B, gains[3:4])
        kb_ref[0, :, b * LANES:(b + 1) * LANES] = rope_b(y).astype(kb_ref.dtype)
        off += LANES
    vb_ref[0] = acc[:, off:off + W_B].astype(vb_ref.dtype)
    off += W_B
    for blk in range(W_C // LANES):
        u_scr[blk] = acc[:, off + blk * LANES:off + (blk + 1) * LANES]
    pw = 2 * SSM_CH
    for t in range(SSM_T):
        for blk in range(W_C // LANES):
            xt = u_scr[blk, pl.ds(t, tm // SSM_T, stride=SSM_T), :]
            for pp in range(LANES // pw):
                u_ref[blk * (LANES // pw) + pp, :, t * pw:(t + 1) * pw] = xt[:, pp * pw:(pp + 1) * pw].astype(u_ref.dtype)


def _inproj(x, mod, g1, w_in_p, gains, seg64, seg32, rope, kv_dtype):
    bsz, seq, d = x.shape
    latent = rope is not None
    tm = next(t for t in (1024, 512, 256) if seq % t == 0)
    bm = mod.shape[0]
    mod_idx = (lambda b, i: (b, 0, 0)) if bm > 1 else (lambda b, i: (0, 0, 0))
    const2 = lambda b, i: (0, 0)
    tok = lambda w: pl.BlockSpec((1, tm, w), lambda b, i: (b, i, 0))
    in_specs = [tok(d),
                pl.BlockSpec((1, 1, 6 * d), mod_idx),
                pl.BlockSpec((1, d), const2),
                pl.BlockSpec((d, IN_COLS_P), const2),
                pl.BlockSpec((4, LANES), const2),
                pl.BlockSpec((LANES, LANES), const2),
                pl.BlockSpec((LANES, LANES), const2)]
    args = [x, mod, g1, w_in_p, gains, seg64, seg32]
    if latent:
        in_specs += [pl.BlockSpec((tm, LANES), lambda b, i: (i, 0))] * 4
        args += list(rope)
    widths = (QA_COLS, KV_A * HD_A, KV_A * HD_A, W_B, W_B, W_B)
    dtypes = (BF16, kv_dtype, kv_dtype, BF16, kv_dtype, kv_dtype)
    nt = seq // tm
    rows = tm // SSM_T
    u_spec = pl.BlockSpec((N_PAIR, rows, SSM_ROW), lambda b, i: (0, b * nt + i, 0))
    u_shape = jax.ShapeDtypeStruct((N_PAIR, bsz * seq // SSM_T, SSM_ROW), BF16)
    return pl.pallas_call(
        functools.partial(_inproj_body, latent=latent),
        grid=(bsz, nt),
        in_specs=in_specs,
        out_specs=[tok(w) for w in widths] + [u_spec],
        out_shape=[jax.ShapeDtypeStruct((bsz, seq, w), dt) for w, dt in zip(widths, dtypes)] + [u_shape],
        scratch_shapes=[pltpu.VMEM((W_C // LANES, tm, LANES), F32)],
        compiler_params=_cparams("parallel", "parallel"),
        name="inproj_latent" if latent else "inproj_ctx",
    )(*args)


def _attn_a_body(sink_ref, q_ref, *refs, latent, nblk):
    o_ref = refs[-1]
    nk = (len(refs) - 1) // 2
    kcat = jnp.concatenate([r[0].astype(BF16) for r in refs[:nk]], axis=0)
    vcat = jnp.concatenate([r[0].astype(BF16) for r in refs[nk:2 * nk]], axis=0)
    rows = GQ_A * BLOCK
    cols = kcat.shape[0]
    rowi = lax.broadcasted_iota(jnp.int32, (rows, 1), 0)
    if latent:
        i = pl.program_id(1)
        r = lax.broadcasted_iota(jnp.int32, (rows, cols), 0) & (BLOCK - 1)
        c = lax.broadcasted_iota(jnp.int32, (rows, cols), 1)
        p_off = jnp.where(i > 0, 0, 2 * BLOCK)
        n_off = jnp.where(i < nblk - 1, 0, 2 * BLOCK)
        prev_ok = (c >= r + p_off) | (c >= BLOCK)
        next_ok = ((c - 2 * BLOCK + n_off) <= r) | (c < 2 * BLOCK) | (c >= 3 * BLOCK)
        valid = prev_ok & next_ok
    lane = lax.broadcasted_iota(jnp.int32, (BLOCK, LANES), 1)
    heads = []
    for j in range(KV_A):
        q3 = jnp.concatenate([q_ref[0, :, (GQ_A * j + g) * LANES:(GQ_A * j + g + 1) * LANES]
                              for g in range(GQ_A)], axis=0)
        s = _dot_nt(q3, kcat)
        if latent:
            s = jnp.where(valid, s, NEG)
        sink = jnp.where(rowi < BLOCK, sink_ref[GQ_A * j],
                         jnp.where(rowi < 2 * BLOCK, sink_ref[GQ_A * j + 1], sink_ref[GQ_A * j + 2]))
        m = jnp.maximum(jnp.max(s, axis=-1, keepdims=True), sink)
        e = jnp.exp(s - m)
        den = jnp.sum(e, axis=-1, keepdims=True) + jnp.exp(sink - m)
        o = _dot(e.astype(BF16), vcat) / den
        for g in range(GQ_A):
            heads.append((j, o[g * BLOCK:(g + 1) * BLOCK]))
    for blk in range(H_A // 2):
        (j0, o0), (j1, o1) = heads[2 * blk], heads[2 * blk + 1]
        lo = o0 if j0 == 0 else pltpu.roll(o0, HD_A, 1)
        hi = o1 if j1 == 1 else pltpu.roll(o1, HD_A, 1)
        o_ref[0, :, blk * LANES:(blk + 1) * LANES] = jnp.where(lane < HD_A, lo, hi).astype(o_ref.dtype)


def _attn_a(qa, ka, va, sink, ctx_kv):
    bsz, seq, _ = qa.shape
    nblk = seq // BLOCK
    latent = ctx_kv is not None
    kvw = KV_A * HD_A
    if latent:
        past = ctx_kv[0].shape[1]
        band = [pl.BlockSpec((1, BLOCK, kvw), lambda b, i: (b, jnp.maximum(i - 1, 0), 0)),
                pl.BlockSpec((1, BLOCK, kvw), lambda b, i: (b, i, 0)),
                pl.BlockSpec((1, BLOCK, kvw), lambda b, i: (b, jnp.minimum(i + 1, nblk - 1), 0)),
                pl.BlockSpec((1, past, kvw), lambda b, i: (b, 0, 0))]
        kv_specs = band + band
        kv_args = [ka, ka, ka, ctx_kv[0], va, va, va, ctx_kv[1]]
    else:
        kv_specs = [pl.BlockSpec((1, seq, kvw), lambda b, i: (b, 0, 0))] * 2
        kv_args = [ka, va]
    return pl.pallas_call(
        functools.partial(_attn_a_body, latent=latent, nblk=nblk),
        grid=(bsz, nblk),
        in_specs=[pl.BlockSpec(memory_space=pltpu.SMEM),
                  pl.BlockSpec((1, BLOCK, QA_COLS), lambda b, i: (b, i, 0))] + kv_specs,
        out_specs=pl.BlockSpec((1, BLOCK, W_A), lambda b, i: (b, i, 0)),
        out_shape=jax.ShapeDtypeStruct((bsz, seq, W_A), BF16),
        compiler_params=_cparams("parallel", "parallel"),
        name="attn_a_latent" if latent else "attn_a_ctx",
    )(sink, qa, *kv_args)


def _attn_b_body(lam_ref, gain_ref, q_ref, *refs, part_lens, lam_init, kc):
    npart = len(part_lens)
    k_refs, v_refs = refs[:npart], refs[npart:2 * npart]
    o_ref, s_scr, vm_scr = refs[2 * npart:]
    tq = q_ref.shape[1]
    chunks = []
    col = 0
    for p, plen in enumerate(part_lens):
        step = min(kc, plen)
        for start in range(0, plen, step):
            chunks.append((p, start, col, step))
            col += step

    @pl.when(pl.program_id(2) == 0)
    def _():
        off = 0
        for p, plen in enumerate(part_lens):
            v = v_refs[p][0].astype(BF16)
            lane_v = lax.broadcasted_iota(jnp.int32, (plen, LANES), 1)
            for h in range(2):
                own = (lane_v >= h * HD_B) & (lane_v < (h + 1) * HD_B)
                ones = jnp.where(lane_v == (1 - h) * HD_B, 1.0, 0.0).astype(BF16)
                vm_scr[h, off:off + plen, :] = jnp.where(own, v, ones)
            off += plen

    lv = lam_ref[...]
    lam = (jnp.exp(jnp.sum(lv[0:1] * lv[1:2], axis=-1, keepdims=True))
           - jnp.exp(jnp.sum(lv[2:3] * lv[3:4], axis=-1, keepdims=True)) + lam_init)
    q = q_ref[0]
    lane_q = lax.broadcasted_iota(jnp.int32, (tq, LANES), 1)
    total = jnp.zeros((tq, LANES), F32)
    for h in range(2):
        qc = [jnp.where((lane_q >= h * HD_B + c * DC_B) & (lane_q < h * HD_B + (c + 1) * DC_B), q, jnp.zeros_like(q))
              for c in range(2)]
        rows = [slice(c * tq, (c + 1) * tq) for c in range(2)]
        macc = [None, None]
        for p, start, col, step in chunks:
            kch = k_refs[p][0, start:start + step, :].astype(BF16)
            for c in range(2):
                s = _dot_nt(qc[c], kch)
                s_scr[rows[c], col:col + step] = s
                for j in range(step // LANES):
                    t = s[:, j * LANES:(j + 1) * LANES]
                    macc[c] = t if macc[c] is None else jnp.maximum(macc[c], t)
        m = [jnp.max(macc[c], axis=-1, keepdims=True) for c in range(2)]
        acc = [jnp.zeros((tq, LANES), F32) for _ in range(2)]
        for p, start, col, step in chunks:
            vch = vm_scr[h, col:col + step, :]
            for c in range(2):
                e = jnp.exp2(s_scr[rows[c], col:col + step] - m[c]).astype(BF16)
                acc[c] = acc[c] + _dot(e, vch)
        o2 = [acc[c] / jnp.sum(jnp.where(lane_q == (1 - h) * HD_B, acc[c], 0.0), axis=-1, keepdims=True)
              for c in range(2)]
        own = (lane_q >= h * HD_B) & (lane_q < (h + 1) * HD_B)
        total = total + jnp.where(own, o2[0] - lam * o2[1], 0.0)
    sq = total * total
    ss_lo = jnp.sum(jnp.where(lane_q < HD_B, sq, 0.0), axis=-1, keepdims=True)
    ss_hi = jnp.sum(jnp.where(lane_q >= HD_B, sq, 0.0), axis=-1, keepdims=True)
    rinv = jnp.where(lane_q < HD_B, lax.rsqrt(ss_lo * (1.0 / HD_B) + EPS), lax.rsqrt(ss_hi * (1.0 / HD_B) + EPS))
    o_ref[0] = (total * rinv * gain_ref[...] * (1.0 - lam_init)).astype(o_ref.dtype)


def _attn_b(qb, k_parts, v_parts, lam_b, gain, lam_init):
    bsz, seq, _ = qb.shape
    tq = 512 if seq % 512 == 0 else 256
    part_lens = tuple(k.shape[1] for k in k_parts)
    lk = sum(part_lens)
    kv_specs = [pl.BlockSpec((1, n, LANES), lambda b, hp, i: (b, 0, hp)) for n in part_lens]
    return pl.pallas_call(
        functools.partial(_attn_b_body, part_lens=part_lens, lam_init=lam_init, kc=512),
        grid=(bsz, W_B // LANES, seq // tq),
        in_specs=[pl.BlockSpec((4, DC_B), lambda b, hp, i: (0, 0)),
                  pl.BlockSpec((1, LANES), lambda b, hp, i: (0, 0)),
                  pl.BlockSpec((1, tq, LANES), lambda b, hp, i: (b, i, hp))] + kv_specs + kv_specs,
        out_specs=pl.BlockSpec((1, tq, LANES), lambda b, hp, i: (b, i, hp)),
        out_shape=jax.ShapeDtypeStruct((bsz, seq, W_B), BF16),
        scratch_shapes=[pltpu.VMEM((2 * tq, lk), F32), pltpu.VMEM((2, lk, LANES), BF16)],
        compiler_params=_cparams("parallel", "parallel", "arbitrary"),
        name="attn_b_latent" if len(k_parts) > 1 else "attn_b_ctx",
    )(lam_b, gain, qb, *k_parts, *v_parts)


def _ssm_body(u_ref, m_ref, g_ref, cc_ref, a_ref, h0_ref, y_ref, fin_ref, s_scr, h_scr, *, nb, nc):
    u = u_ref[0]
    col = lambda k: slice(k * LANES, (k + 1) * LANES)
    s = _dot(u, g_ref[0])
    for k in range(4):
        s_scr[k] = s[:, col(k)]
    a = a_ref[0]
    afr, afi, abr, abi = (jnp.broadcast_to(a[k:k + 1], (nb, LANES)) for k in range(4))
    h0 = h0_ref[0]

    def step(c, carry):
        fr, fi, br, bi = carry
        rf = pl.ds(c, nb, stride=nc)
        rb = pl.ds(nc - 1 - c, nb, stride=nc)
        h_scr[0, rf, :] = fr
        h_scr[1, rf, :] = fi
        h_scr[2, rb, :] = br
        h_scr[3, rb, :] = bi
        nfr = afr * fr - afi * fi + s_scr[0, rf, :]
        nfi = afr * fi + afi * fr + s_scr[1, rf, :]
        nbr = abr * br - abi * bi + s_scr[2, rb, :]
        nbi = abr * bi + abi * br + s_scr[3, rb, :]
        return nfr, nfi, nbr, nbi

    fin = lax.fori_loop(0, nc, step, tuple(h0[:, col(k)] for k in range(4)))
    for k in range(4):
        fin_ref[0, :, col(k)] = fin[k]
    hin = jnp.concatenate([h_scr[k] for k in range(4)], axis=1).astype(BF16)
    y = _dot(u, m_ref[0]) + _dot(hin, cc_ref[0])
    y_ref[0] = y.astype(y_ref.dtype)


def _ssm(u_rows, mats, h0, nb):
    npair, rows, w = u_rows.shape
    nc = rows // nb
    mat_spec = pl.BlockSpec((1, w, w), lambda p: (p, 0, 0))
    return pl.pallas_call(
        functools.partial(_ssm_body, nb=nb, nc=nc),
        grid=(npair,),
        in_specs=[pl.BlockSpec((1, rows, w), lambda p: (p, 0, 0)), mat_spec, mat_spec, mat_spec,
                  pl.BlockSpec((1, 4, LANES), lambda p: (p, 0, 0)),
                  pl.BlockSpec((1, nb, w), lambda p: (p, 0, 0))],
        out_specs=[pl.BlockSpec((1, rows, w), lambda p: (p, 0, 0)),
                   pl.BlockSpec((1, nb, w), lambda p: (p, 0, 0))],
        out_shape=[jax.ShapeDtypeStruct((npair, rows, w), BF16),
                   jax.ShapeDtypeStruct((npair, nb, w), F32)],
        scratch_shapes=[pltpu.VMEM((4, rows, LANES), F32), pltpu.VMEM((4, rows, LANES), F32)],
        compiler_params=_cparams("parallel"),
        name="ssm_scan",
    )(u_rows, mats["m"], mats["g"], mats["cc"], mats["a16"], h0)


def _ssm_matrices(lp):
    t = SSM_T
    ks = jnp.arange(t + 1, dtype=F32)
    dirs = []
    for d in range(2):
        lam = lax.complex(lp["ssm_lam_re"][d].astype(F32), lp["ssm_lam_im"][d].astype(F32))
        dt = jnp.exp(lp["ssm_log_dt"][d].astype(F32))[:, None]
        a_bar = jnp.exp(lam * dt)
        b_bar = ((a_bar - 1.0) / lam)[..., None] * lax.complex(lp["ssm_b_re"][d].astype(F32),
                                                               lp["ssm_b_im"][d].astype(F32))
        c_mat = lax.complex(lp["ssm_c_re"][d].astype(F32), lp["ssm_c_im"][d].astype(F32))
        pw = jnp.exp((lam * dt)[None] * ks[:, None, None].astype(jnp.complex64))
        kern = jnp.real(jnp.einsum("gop,kgp,gpi->gkoi", c_mat, pw[:t], b_bar))
        dirs.append((pw, b_bar, c_mat, kern))
    (pw_f, bb_f, cm_f, k_f), (pw_b, bb_b, cm_b, k_b) = dirs
    eye2 = jnp.eye(2, dtype=F32)
    ch, pw2 = SSM_CH, 2 * SSM_CH
    hi = lax.Precision.HIGHEST

    def pair_bd(x):
        r, c = x.shape[1:]
        return jnp.einsum("pgrc,gh->pgrhc", x.reshape(N_PAIR, 2, r, c), eye2.astype(x.dtype)).reshape(N_PAIR, 2 * r, 2 * c)

    def pair_vec(x):
        return x.reshape(x.shape[0], N_PAIR, 2 * P_C).transpose(1, 0, 2)

    def lag_blocks(kern):
        x = kern.transpose(1, 0, 3, 2).reshape(t * G_C, ch, ch)
        return jnp.stack([pair_bd(x[l * G_C:(l + 1) * G_C]) for l in range(t)], axis=1)
    kp_f, kp_b = lag_blocks(k_f), lag_blocks(k_b)
    d_blk = pair_bd(lp["ssm_d"].astype(F32)[:, :, None] * jnp.eye(ch, dtype=F32)[None])
    center = (kp_f[:, 0] + kp_b[:, 0] + d_blk)[:, None]
    band = jnp.concatenate([kp_b[:, :0:-1], center, kp_f[:, 1:]], axis=1)
    band = band.transpose(0, 2, 1, 3).reshape(N_PAIR, pw2, (2 * t - 1) * pw2)
    m_p = jnp.concatenate([band[:, :, (t - 1 - s) * pw2:(t - 1 - s) * pw2 + SSM_ROW] for s in range(t)], axis=1)

    def inject(pw_sel, b_bar):
        x1 = jnp.repeat(pair_vec(pw_sel), pw2, axis=1)
        x2 = jnp.tile(pair_bd(b_bar.transpose(0, 2, 1)), (1, t, 1))
        return x1 * x2
    g_f = inject(pw_f[t - 1 - jnp.arange(t)], bb_f)
    g_b = inject(pw_b[jnp.arange(t)], bb_b)
    g_p = jnp.concatenate([jnp.real(g_f), jnp.imag(g_f), jnp.real(g_b), jnp.imag(g_b)], axis=2)

    lane = jnp.arange(SSM_ROW)
    exp_t = (jnp.arange(t)[:, None] == lane[None, :] // pw2).astype(F32)
    exp_c = (jnp.arange(pw2)[:, None] == lane[None, :] % pw2).astype(F32)

    def widen(x, e):
        f = lambda v: jnp.einsum("pqk,kx->pqx", v, e, precision=hi)
        return lax.complex(f(jnp.real(x)), f(jnp.imag(x)))

    def readout(pw_sel, c_mat):
        y1 = widen(pair_vec(pw_sel).transpose(0, 2, 1), exp_t)
        y2 = widen(pair_bd(c_mat.transpose(0, 2, 1)), exp_c)
        return y1 * y2
    z_f = readout(pw_f[1 + jnp.arange(t)], cm_f)
    z_b = readout(pw_b[t - jnp.arange(t)], cm_b)
    cc_p = jnp.concatenate([jnp.real(z_f), -jnp.imag(z_f), jnp.real(z_b), -jnp.imag(z_b)], axis=1)
    a16 = jnp.stack([jnp.real(pw_f[t]), jnp.imag(pw_f[t]), jnp.real(pw_b[t]), jnp.imag(pw_b[t])], axis=0)
    a16 = a16.reshape(4, N_PAIR, 2 * P_C).transpose(1, 0, 2)
    return dict(m=m_p.astype(BF16), g=g_p.astype(BF16), cc=cc_p.astype(BF16), a16=a16)


def _ssm_state_rows(s_re, s_im):
    bsz = s_re.shape[0]
    parts = [s_re[:, 0], s_im[:, 0], s_re[:, 1], s_im[:, 1]]
    st = jnp.stack([p.reshape(bsz, N_PAIR, 2 * P_C) for p in parts], axis=2)
    return st.transpose(1, 0, 2, 3).reshape(N_PAIR, bsz, 8 * P_C).astype(F32)


def _ssm_state_unrows(fin):
    npair, bsz, _ = fin.shape
    st = fin.reshape(npair, bsz, 4, 2, P_C).transpose(1, 2, 0, 3, 4).reshape(bsz, 4, G_C, P_C)
    return jnp.stack([st[:, 0], st[:, 2]], axis=1), jnp.stack([st[:, 1], st[:, 3]], axis=1)


def _route(scores, bias):
    tm = scores.shape[1]
    biased = scores + bias
    iota8 = lax.broadcasted_iota(jnp.int32, (PER_GROUP, tm), 0)
    grp = [biased[PER_GROUP * g:PER_GROUP * (g + 1)] for g in range(N_EXP_GROUPS)]
    gscore = []
    for v in grp:
        m1 = jnp.max(v, axis=0, keepdims=True)
        first = jnp.min(jnp.where(v == m1, iota8, PER_GROUP), axis=0, keepdims=True)
        m2 = jnp.max(jnp.where(iota8 == first, -jnp.inf, v), axis=0, keepdims=True)
        gscore.append(m1 + m2)
    masked = []
    for g in range(N_EXP_GROUPS):
        rank = jnp.zeros((1, tm), jnp.int32)
        for o in range(N_EXP_GROUPS):
            if o == g:
                continue
            ahead = (gscore[o] >= gscore[g]) if o < g else (gscore[o] > gscore[g])
            rank = rank + jnp.where(ahead, 1, 0)
        masked.append(jnp.where(rank < TOPK_GROUPS, grp[g], -jnp.inf))
    chosen = [None] * N_EXP_GROUPS
    for _ in range(TOP_K):
        best = masked[0]
        for v in masked[1:]:
            best = jnp.maximum(best, v)
        best = jnp.max(best, axis=0, keepdims=True)
        first = jnp.full((1, tm), N_EXPERTS, jnp.int32)
        for g, v in enumerate(masked):
            cand = jnp.min(jnp.where(v == best, iota8 + PER_GROUP * g, N_EXPERTS), axis=0, keepdims=True)
            first = jnp.minimum(first, cand)
        for g in range(N_EXP_GROUPS):
            hit = (iota8 + PER_GROUP * g) == first
            chosen[g] = hit if chosen[g] is None else (chosen[g] | hit)
            masked[g] = jnp.where(hit, -jnp.inf, masked[g])
    w = [jnp.where(chosen[g], scores[PER_GROUP * g:PER_GROUP * (g + 1)], 0.0) for g in range(N_EXP_GROUPS)]
    wsum = w[0]
    for v in w[1:]:
        wsum = wsum + v
    wsum = jnp.sum(wsum, axis=0, keepdims=True)
    return jnp.concatenate([v / wsum * ROUTED_SCALE for v in w], axis=0)


def _post_body(x_ref, oa_ref, ob_ref, y_ref, mod_ref, wglu_ref, wout_ref, g2_ref, wrh_ref, wrl_ref, br_ref,
               x1_ref, h2_ref, gate_ref, y_scr):
    d = D_MODEL
    tm = x_ref.shape[1]
    pw = 2 * SSM_CH
    for t in range(SSM_T):
        for blk in range(W_C // LANES):
            piece = jnp.concatenate([y_ref[blk * (LANES // pw) + pp, :, t * pw:(t + 1) * pw].astype(F32)
                                     for pp in range(LANES // pw)], axis=1)
            y_scr[blk, pl.ds(t, tm // SSM_T, stride=SSM_T), :] = piece
    g = jax.nn.gelu(jnp.concatenate([y_scr[blk] for blk in range(W_C // LANES)], axis=1))
    oc = g * jax.nn.sigmoid(_dot(g.astype(BF16), wglu_ref[...]))
    mix = (_dot(oa_ref[0], wout_ref[0:W_A]) + _dot(ob_ref[0], wout_ref[W_A:W_A + W_B])
           + _dot(oc.astype(BF16), wout_ref[W_A + W_B:]))
    mod = mod_ref[0]
    x1 = x_ref[0] + mod[:, 2 * d:3 * d] * mix
    x1_ref[0] = x1
    xn = x1 * lax.rsqrt(jnp.mean(x1 * x1, axis=-1, keepdims=True) + EPS) * g2_ref[...]
    h2 = xn * (1.0 + mod[:, 4 * d:5 * d]) + mod[:, 3 * d:4 * d]
    h_hi, h_lo = _split_bf16(h2)
    h2_ref[0] = h_hi
    logits = _dot_nt(wrh_ref[...], h_hi) + _dot_nt(wrh_ref[...], h_lo) + _dot_nt(wrl_ref[...], h_hi)
    gate_ref[0] = _route(jax.nn.sigmoid(logits), br_ref[...]).T


def _post_mix(x, oa, ob, y, mod, w_glu, w_out, g2, wr_hi, wr_lo, b_r):
    bsz, seq, d = x.shape
    tm = 512 if seq % 512 == 0 else 256
    bm = mod.shape[0]
    mod_idx = (lambda b, i: (b, 0, 0)) if bm > 1 else (lambda b, i: (0, 0, 0))
    const2 = lambda b, i: (0, 0)
    tok = lambda w: pl.BlockSpec((1, tm, w), lambda b, i: (b, i, 0))
    nt = seq // tm
    return pl.pallas_call(
        _post_body,
        grid=(bsz, nt),
        in_specs=[tok(d), tok(W_A), tok(W_B),
                  pl.BlockSpec((N_PAIR, tm // SSM_T, SSM_ROW), lambda b, i: (0, b * nt + i, 0)),
                  pl.BlockSpec((1, 1, 6 * d), mod_idx),
                  pl.BlockSpec((W_C, W_C), const2),
                  pl.BlockSpec((d, d), const2),
                  pl.BlockSpec((1, d), const2),
                  pl.BlockSpec((N_EXPERTS, d), const2),
                  pl.BlockSpec((N_EXPERTS, d), const2),
                  pl.BlockSpec((N_EXPERTS, 1), const2)],
        out_specs=[tok(d), tok(d), tok(N_EXPERTS)],
        out_shape=[jax.ShapeDtypeStruct((bsz, seq, d), F32),
                   jax.ShapeDtypeStruct((bsz, seq, d), BF16),
                   jax.ShapeDtypeStruct((bsz, seq, N_EXPERTS), F32)],
        scratch_shapes=[pltpu.VMEM((W_C // LANES, tm, LANES), F32)],
        compiler_params=_cparams("parallel", "parallel"),
        name="post_mix",
    )(x, oa, ob, y, mod, w_glu, w_out, g2, wr_hi, wr_lo, b_r)


def _moe_body(x1_ref, h_ref, gate_ref, g2_ref, w1_ref, w3_ref, w2_ref, c1_ref, c32_ref, ex_ref,
              s1_ref, s3_ref, s2_ref, o_ref, acc_ref, h8_ref, hs_ref):
    j = pl.program_id(1)

    @pl.when(j == 0)
    def _():
        h = h_ref[...]
        a = _dot(h, s1_ref[...])
        acc_ref[...] = _dot((a * jax.nn.sigmoid(a) * _dot(h, s3_ref[...])).astype(BF16), s2_ref[...])
        hf = h.astype(F32)
        sc = jnp.maximum(jnp.max(jnp.abs(hf), axis=-1, keepdims=True), F8_TINY) * (1.0 / F8_RANGE)
        hs_ref[...] = sc
        h8_ref[...] = (hf * (1.0 / sc)).astype(F8)

    ne = w1_ref.shape[0]
    h8 = h8_ref[...]
    hs = hs_ref[...]
    a = _dot(h8, jnp.concatenate([w1_ref[e] for e in range(ne)], axis=1)) * c1_ref[...] * hs
    b = _dot(h8, jnp.concatenate([w3_ref[e] for e in range(ne)], axis=1))
    gexp = _dot(jnp.concatenate(_split_bf16(gate_ref[...]), axis=1), ex_ref[...])
    hid = a * jax.nn.sigmoid(a) * b * gexp * c32_ref[...]
    sc = jnp.maximum(jnp.max(jnp.abs(hid), axis=-1, keepdims=True), F8_TINY) * (1.0 / F8_RANGE)
    acc_ref[...] += _dot((hid * (1.0 / sc)).astype(F8), w2_ref[...]) * (sc * hs)

    @pl.when(j == pl.num_programs(1) - 1)
    def _():
        o_ref[...] = x1_ref[...] + g2_ref[0] * acc_ref[...]


def _moe(x1, h2, gates, mod, seq, ew, expand, ws1, ws3, ws2):
    tokens, d = x1.shape
    bm = mod.shape[0]
    span = seq if bm > 1 else tokens
    tm = next(t for t in (1024, 512, 256) if span % t == 0)
    per_b = seq // tm if bm > 1 else 1
    mod_idx = (lambda i, j: (i // per_b, 0, 5)) if bm > 1 else (lambda i, j: (0, 0, 5))
    ne = 4
    fc = ne * F_EXP
    hidden = ew["w2"].shape[0]
    const2 = lambda i, j: (0, 0)
    chunk_row = pl.BlockSpec((1, fc), lambda i, j: (0, j))
    return pl.pallas_call(
        _moe_body,
        grid=(tokens // tm, hidden // fc),
        in_specs=[pl.BlockSpec((tm, d), lambda i, j: (i, 0)),
                  pl.BlockSpec((tm, d), lambda i, j: (i, 0)),
                  pl.BlockSpec((tm, N_EXPERTS), lambda i, j: (i, 0)),
                  pl.BlockSpec((1, 1, d), mod_idx),
                  pl.BlockSpec((ne, d, F_EXP), lambda i, j: (j, 0, 0)),
                  pl.BlockSpec((ne, d, F_EXP), lambda i, j: (j, 0, 0)),
                  pl.BlockSpec((fc, d), lambda i, j: (j, 0)),
                  chunk_row, chunk_row,
                  pl.BlockSpec((2 * N_EXPERTS, fc), lambda i, j: (0, j)),
                  pl.BlockSpec((d, F_SHARED), const2),
                  pl.BlockSpec((d, F_SHARED), const2),
                  pl.BlockSpec((F_SHARED, d), const2)],
        out_specs=pl.BlockSpec((tm, d), lambda i, j: (i, 0)),
        out_shape=jax.ShapeDtypeStruct((tokens, d), F32),
        scratch_shapes=[pltpu.VMEM((tm, d), F32), pltpu.VMEM((tm, d), F8), pltpu.VMEM((tm, 1), F32)],
        compiler_params=_cparams("parallel", "arbitrary"),
        name="moe",
    )(x1, h2, gates, mod, ew["w1"], ew["w3"], ew["w2"], ew["c1"], ew["c32"], expand, ws1, ws3, ws2)


def _expert_fp8(w):
    sc = jnp.maximum(jnp.max(jnp.abs(w), axis=(1, 2)), F8_TINY) * (1.0 / F8_RANGE)
    return (w / sc[:, None, None]).astype(F8), jnp.repeat(sc, F_EXP)[None, :].astype(F32)


def _prep_experts(p, l):
    w1, c1 = _expert_fp8(p["w_e1"][l].astype(F32))
    w3, c3 = _expert_fp8(p["w_e3"][l].astype(F32))
    w2, c2 = _expert_fp8(p["w_e2"][l].astype(F32))
    return dict(w1=w1, w3=w3, w2=w2.reshape(N_EXPERTS * F_EXP, D_MODEL), c1=c1, c32=c3 * c2)


def _rope_tables(seq):
    pos = jnp.arange(seq)
    row = (pos // GRID_W).astype(F32)[:, None]
    colp = (pos % GRID_W).astype(F32)[:, None]
    lane = jnp.arange(LANES)

    def table(width):
        half, quarter = width // 2, width // 4
        i = lane % width
        freq = ROPE_BASE ** (-(2.0 * (i % quarter).astype(F32)) / half)
        ang = jnp.where((i // half) == 0, row, colp) * freq[None, :]
        sign = jnp.where((i % half) < quarter, -1.0, 1.0)
        return jnp.cos(ang), jnp.sin(ang) * sign[None, :]

    ca, sa = table(HD_A)
    cb, sb = table(DC_B)
    return ca, sa, cb, sb


def _prep_layer(p, l):
    d = D_MODEL
    w_in = p["w_in"][l]
    kvw = KV_A * HD_A
    qa = w_in[:, :W_A].reshape(d, H_A, HD_A)
    qa_pad = jnp.zeros((d, H_A, 2, HD_A), w_in.dtype)
    for h in range(H_A):
        qa_pad = qa_pad.at[:, h, h // GQ_A].set(qa[:, h])
    w_in_p = jnp.concatenate([qa_pad.reshape(d, QA_COLS), w_in[:, W_A:]], axis=1).astype(BF16)
    gains = jnp.stack([jnp.tile(p["q_norm_a"][l], LANES // HD_A) * (HD_A ** -0.5),
                       jnp.tile(p["k_norm_a"][l], LANES // HD_A),
                       jnp.tile(p["q_norm_b"][l], LANES // DC_B) * (DC_B ** -0.5 * LOG2E),
                       jnp.tile(p["k_norm_b"][l], LANES // DC_B)], axis=0).astype(F32)
    lp = {k: p[k][l] for k in ("ssm_lam_re", "ssm_lam_im", "ssm_log_dt", "ssm_b_re", "ssm_b_im",
                               "ssm_c_re", "ssm_c_im", "ssm_d")}
    wr_hi, wr_lo = _split_bf16(p["w_router"][l].T.astype(F32))
    hidden = N_EXPERTS * F_EXP
    return dict(
        w_in_p=w_in_p, gains=gains,
        g1=p["norm1_g"][l].reshape(1, d).astype(F32), g2=p["norm2_g"][l].reshape(1, d).astype(F32),
        sink=p["sink_a"][l].astype(F32), lam_b=p["lam_b"][l].astype(F32),
        subln=jnp.tile(p["subln_b"][l], LANES // HD_B).reshape(1, LANES).astype(F32),
        lam_init=0.8 - 0.6 * math.exp(-0.3 * l),
        ssm=_ssm_matrices(lp),
        w_glu=p["w_glu"][l].astype(BF16), w_out=p["w_out"][l].astype(BF16),
        wr_hi=wr_hi, wr_lo=wr_lo, b_r=p["b_router"][l].reshape(N_EXPERTS, 1).astype(F32),
        experts=_prep_experts(p, l),
        ws1=p["w_s1"][l].astype(BF16), ws3=p["w_s3"][l].astype(BF16), ws2=p["w_s2"][l].astype(BF16),
    )


def _trunk_layer(x, mod, lw, consts, ctx):
    bsz, seq, d = x.shape
    latent = ctx is not None
    rope = consts["rope"] if latent else None
    kv_dtype = BF16 if latent else F32
    qa, ka, va, qb, kb, vb, u = _inproj(x, mod, lw["g1"], lw["w_in_p"], lw["gains"],
                                        consts["seg64"], consts["seg32"], rope, kv_dtype)
    if latent:
        oa = _attn_a(qa, ka, va, lw["sink"], (ctx["ak"], ctx["av"]))
        ob = _attn_b(qb, [kb, ctx["bk"]], [vb, ctx["bv"]], lw["lam_b"], lw["subln"], lw["lam_init"])
        h0 = ctx["h0"]
    else:
        oa = _attn_a(qa, ka, va, lw["sink"], None)
        ob = _attn_b(qb, [kb], [vb], lw["lam_b"], lw["subln"], lw["lam_init"])
        h0 = jnp.zeros((N_PAIR, bsz, 8 * P_C), F32)
    y_rows, fin = _ssm(u, lw["ssm"], h0, bsz)
    x1, h2, gates = _post_mix(x, oa, ob, y_rows, mod, lw["w_glu"], lw["w_out"], lw["g2"],
                                lw["wr_hi"], lw["wr_lo"], lw["b_r"])
    out = _moe(x1.reshape(bsz * seq, d), h2.reshape(bsz * seq, d), gates.reshape(bsz * seq, N_EXPERTS), mod, seq,
               lw["experts"], consts["expand"], lw["ws1"], lw["ws3"], lw["ws2"])
    return out.reshape(bsz, seq, d), (ka, va, kb, vb, fin)


def kernel(x_prompt, x_sample, cache_a_k, cache_a_v, cache_b_k, cache_b_v, state_ssm_re, state_ssm_im, c, c_ctx, norm1_g, norm2_g, w_ada, b_ada, w_in, q_norm_a, k_norm_a, sink_a, q_norm_b, k_norm_b, lam_b, subln_b, ssm_lam_re, ssm_lam_im, ssm_log_dt, ssm_b_re, ssm_b_im, ssm_c_re, ssm_c_im, ssm_d, w_glu, w_out, w_router, b_router, w_e1, w_e3, w_e2, w_s1, w_s3, w_s2):
    p = dict(norm1_g=norm1_g, norm2_g=norm2_g, w_in=w_in, q_norm_a=q_norm_a, k_norm_a=k_norm_a, sink_a=sink_a,
             q_norm_b=q_norm_b, k_norm_b=k_norm_b, lam_b=lam_b, subln_b=subln_b,
             ssm_lam_re=ssm_lam_re, ssm_lam_im=ssm_lam_im, ssm_log_dt=ssm_log_dt, ssm_b_re=ssm_b_re,
             ssm_b_im=ssm_b_im, ssm_c_re=ssm_c_re, ssm_c_im=ssm_c_im, ssm_d=ssm_d, w_glu=w_glu, w_out=w_out,
             w_router=w_router, b_router=b_router, w_e1=w_e1, w_e3=w_e3, w_e2=w_e2,
             w_s1=w_s1, w_s3=w_s3, w_s2=w_s2)
    depth = w_in.shape[0]
    bsz, seq, d = x_prompt.shape
    dbsz, dseq, _ = x_sample.shape
    past = cache_a_k.shape[3]

    mod_rows = 16
    cvec = jnp.concatenate([c.astype(F32), c_ctx.astype(F32)[None],
                            jnp.zeros((mod_rows - dbsz - 1, d), F32)], axis=0)
    mods = _modulation(cvec, w_ada.astype(F32), b_ada.astype(F32))

    lane = jnp.arange(LANES)
    hidden = N_EXPERTS * F_EXP
    consts = dict(
        rope=_rope_tables(dseq),
        seg64=(lane[:, None] // HD_A == lane[None, :] // HD_A).astype(BF16),
        seg32=(lane[:, None] // DC_B == lane[None, :] // DC_B).astype(BF16),
        expand=(jnp.arange(2 * N_EXPERTS)[:, None] % N_EXPERTS == jnp.arange(hidden)[None, :] // F_EXP).astype(BF16),
    )

    xp, xs = x_prompt, x_sample
    ak, av, bk, bv, sre, sim = [], [], [], [], [], []
    for l in range(depth):
        lw = _prep_layer(p, l)
        mod_lat = mods[l, :dbsz][:, None, :]
        mod_ctx = mods[l, dbsz:dbsz + 1][:, None, :]
        xp, (k_a, v_a, k_b, v_b, fin) = _trunk_layer(xp, mod_ctx, lw, consts, None)
        ak.append(k_a.reshape(bsz, seq, KV_A, HD_A).transpose(0, 2, 1, 3))
        av.append(v_a.reshape(bsz, seq, KV_A, HD_A).transpose(0, 2, 1, 3))
        bk.append(k_b.reshape(bsz, seq, H_B, 2, DC_B).transpose(0, 2, 3, 1, 4))
        bv.append(v_b.reshape(bsz, seq, H_B, HD_B).transpose(0, 2, 1, 3))
        f_re, f_im = _ssm_state_unrows(fin)
        sre.append(f_re)
        sim.append(f_im)
        ctx = dict(
            ak=cache_a_k[:, l].transpose(0, 2, 1, 3).reshape(dbsz, past, KV_A * HD_A).astype(BF16),
            av=cache_a_v[:, l].transpose(0, 2, 1, 3).reshape(dbsz, past, KV_A * HD_A).astype(BF16),
            bk=cache_b_k[:, l].transpose(0, 3, 1, 2, 4).reshape(dbsz, past, W_B).astype(BF16),
            bv=cache_b_v[:, l].transpose(0, 2, 1, 3).reshape(dbsz, past, W_B).astype(BF16),
            h0=_ssm_state_rows(state_ssm_re[:, l], state_ssm_im[:, l]),
        )
        xs, _ = _trunk_layer(xs, mod_lat, lw, consts, ctx)
    return (xp, xs, jnp.stack(ak, axis=1), jnp.stack(av, axis=1), jnp.stack(bk, axis=1),
            jnp.stack(bv, axis=1), jnp.stack(sre, axis=1), jnp.stack(sim, axis=1))
```

```python
import functools
import math

import jax
import jax.numpy as jnp
from jax import lax
from jax.experimental import pallas as pl
from jax.experimental.pallas import tpu as pltpu

F32 = jnp.float32
BF16 = jnp.bfloat16
F8 = jnp.float8_e4m3fn
F8_RANGE = 384.0
F8_TINY = 1e-30

D_MODEL = 1024
GRID_W = 64
BLOCK = 128
H_A, KV_A, HD_A = 6, 2, 64
GQ_A = H_A // KV_A
W_A = H_A * HD_A
H_B, HD_B = 4, 64
DC_B = HD_B // 2
W_B = H_B * HD_B
SSM_CH = 16
W_C = D_MODEL - W_A - W_B
G_C = W_C // SSM_CH
P_C = 64
N_EXPERTS, TOP_K, F_EXP, F_SHARED = 64, 6, 128, 256
N_EXP_GROUPS, TOPK_GROUPS = 8, 4
PER_GROUP = N_EXPERTS // N_EXP_GROUPS
ROUTED_SCALE = 2.5
ROPE_BASE = 10000.0
EPS = 1e-6
NEG = -1e30
LOG2E = 1.4426950408889634

LANES = 128
SSM_T = 16
N_PAIR = G_C // 2
SSM_ROW = 2 * SSM_T * SSM_CH
QA_COLS = H_A * LANES
IN_COLS_P = QA_COLS + 2 * KV_A * HD_A + 3 * W_B + W_C
VMEM_LIMIT = 56 << 20


def _cparams(*sem):
    return pltpu.CompilerParams(dimension_semantics=sem, vmem_limit_bytes=VMEM_LIMIT)


def _dot(a, b):
    return jnp.dot(a, b, preferred_element_type=F32)


def _dot_nt(a, b):
    return lax.dot_general(a, b, (((1,), (1,)), ((), ())), preferred_element_type=F32)


def _split_bf16(x):
    hi = x.astype(BF16)
    lo = (x - hi.astype(F32)).astype(BF16)
    return hi, lo


def _mod_body(c_ref, w_ref, b_ref, o_ref):
    c = c_ref[...]
    s = c * jax.nn.sigmoid(c)
    s_hi, s_lo = _split_bf16(s)
    w_hi, w_lo = _split_bf16(w_ref[0])
    o_ref[0] = _dot(s_hi, w_hi) + _dot(s_lo, w_hi) + _dot(s_hi, w_lo) + b_ref[0]


def _modulation(cvec, w_ada, b_ada):
    depth, d, n = w_ada.shape
    rows = cvec.shape[0]
    tn = 768
    return pl.pallas_call(
        _mod_body,
        grid=(depth, n // tn),
        in_specs=[pl.BlockSpec((rows, d), lambda l, j: (0, 0)),
                  pl.BlockSpec((1, d, tn), lambda l, j: (l, 0, j)),
                  pl.BlockSpec((1, 1, tn), lambda l, j: (l, 0, j))],
        out_specs=pl.BlockSpec((1, rows, tn), lambda l, j: (l, 0, j)),
        out_shape=jax.ShapeDtypeStruct((depth, rows, n), F32),
        compiler_params=_cparams("parallel", "parallel"),
        name="adaln_mod",
    )(cvec, w_ada, b_ada.reshape(depth, 1, n))


def _inproj_body(*refs, latent):
    if latent:
        (x_ref, mod_ref, g1_ref, w_ref, gains_ref, s64_ref, s32_ref, ca_ref, sa_ref, cb_ref, sb_ref,
         qa_ref, ka_ref, va_ref, qb_ref, kb_ref, vb_ref, u_ref, u_scr) = refs
    else:
        (x_ref, mod_ref, g1_ref, w_ref, gains_ref, s64_ref, s32_ref,
         qa_ref, ka_ref, va_ref, qb_ref, kb_ref, vb_ref, u_ref, u_scr) = refs
    d = D_MODEL
    x = x_ref[0]
    mod = mod_ref[0]
    xn = x * lax.rsqrt(jnp.mean(x * x, axis=-1, keepdims=True) + EPS) * g1_ref[...]
    h = xn * (1.0 + mod[:, d:2 * d]) + mod[:, 0:d]
    acc = _dot(h.astype(BF16), w_ref[...])

    tm = x.shape[0]
    lane = lax.broadcasted_iota(jnp.int32, (tm, LANES), 1)
    first_a = (lane % 32) < 16
    first_b = (lane % 16) < 8

    def normed(xb, seg_ref, inv_n, gain):
        ss = _dot((xb * xb).astype(BF16), seg_ref[...])
        return xb * lax.rsqrt(ss * inv_n + EPS) * gain

    def rope_a(y):
        if not latent:
            return y
        sw = jnp.where(first_a, pltpu.roll(y, LANES - 16, 1), pltpu.roll(y, 16, 1))
        return y * ca_ref[...] + sw * sa_ref[...]

    def rope_b(y):
        if not latent:
            return y
        sw = jnp.where(first_b, pltpu.roll(y, LANES - 8, 1), pltpu.roll(y, 8, 1))
        return y * cb_ref[...] + sw * sb_ref[...]

    gains = gains_ref[...]
    off = 0
    for b in range(H_A):
        y = normed(acc[:, off:off + LANES], s64_ref, 1.0 / HD_A, gains[0:1])
        qa_ref[0, :, b * LANES:(b + 1) * LANES] = rope_a(y).astype(qa_ref.dtype)
        off += LANES
    y = normed(acc[:, off:off + LANES], s64_ref, 1.0 / HD_A, gains[1:2])
    ka_ref[0] = rope_a(y).astype(ka_ref.dtype)
    off += LANES
    va_ref[0] = acc[:, off:off + LANES].astype(va_ref.dtype)
    off += LANES
    for b in range(W_B // LANES):
        y = normed(acc[:, off:off + LANES], s32_ref, 1.0 / DC_B, gains[2:3])
        qb_ref[0, :, b * LANES:(b + 1) * LANES] = rope_b(y).astype(qb_ref.dtype)
        off += LANES
    for b in range(W_B // LANES):
        y = normed(acc[:, off:off + LANES], s32_ref, 1.0 / DC_B, gains[3:4])
        kb_ref[0, :, b * LANES:(b + 1) * LANES] = rope_b(y).astype(kb_ref.dtype)
        off += LANES
    vb_ref[0] = acc[:, off:off + W_B].astype(vb_ref.dtype)
    off += W_B
    for blk in range(W_C // LANES):
        u_scr[blk] = acc[:, off + blk * LANES:off + (blk + 1) * LANES]
    pw = 2 * SSM_CH
    for t in range(SSM_T):
        for blk in range(W_C // LANES):
            xt = u_scr[blk, pl.ds(t, tm // SSM_T, stride=SSM_T), :]
            for pp in range(LANES // pw):
                u_ref[blk * (LANES // pw) + pp, :, t * pw:(t + 1) * pw] = xt[:, pp * pw:(pp + 1) * pw].astype(u_ref.dtype)


def _inproj(x, mod, g1, w_in_p, gains, seg64, seg32, rope, kv_dtype):
    bsz, seq, d = x.shape
    latent = rope is not None
    tm = next(t for t in (1024, 512, 256) if seq % t == 0)
    bm = mod.shape[0]
    mod_idx = (lambda b, i: (b, 0, 0)) if bm > 1 else (lambda b, i: (0, 0, 0))
    const2 = lambda b, i: (0, 0)
    tok = lambda w: pl.BlockSpec((1, tm, w), lambda b, i: (b, i, 0))
    in_specs = [tok(d),
                pl.BlockSpec((1, 1, 6 * d), mod_idx),
                pl.BlockSpec((1, d), const2),
                pl.BlockSpec((d, IN_COLS_P), const2),
                pl.BlockSpec((4, LANES), const2),
                pl.BlockSpec((LANES, LANES), const2),
                pl.BlockSpec((LANES, LANES), const2)]
    args = [x, mod, g1, w_in_p, gains, seg64, seg32]
    if latent:
        in_specs += [pl.BlockSpec((tm, LANES), lambda b, i: (i, 0))] * 4
        args += list(rope)
    widths = (QA_COLS, KV_A * HD_A, KV_A * HD_A, W_B, W_B, W_B)
    dtypes = (BF16, kv_dtype, kv_dtype, BF16, kv_dtype, kv_dtype)
    nt = seq // tm
    rows = tm // SSM_T
    u_spec = pl.BlockSpec((N_PAIR, rows, SSM_ROW), lambda b, i: (0, b * nt + i, 0))
    u_shape = jax.ShapeDtypeStruct((N_PAIR, bsz * seq // SSM_T, SSM_ROW), BF16)
    return pl.pallas_call(
        functools.partial(_inproj_body, latent=latent),
        grid=(bsz, nt),
        in_specs=in_specs,
        out_specs=[tok(w) for w in widths] + [u_spec],
        out_shape=[jax.ShapeDtypeStruct((bsz, seq, w), dt) for w, dt in zip(widths, dtypes)] + [u_shape],
        scratch_shapes=[pltpu.VMEM((W_C // LANES, tm, LANES), F32)],
        compiler_params=_cparams("parallel", "parallel"),
        name="inproj_latent" if latent else "inproj_ctx",
    )(*args)


def _attn_a_body(sink_ref, q_ref, *refs, latent, nblk):
    o_ref = refs[-1]
    nk = (len(refs) - 1) // 2
    kcat = jnp.concatenate([r[0].astype(BF16) for r in refs[:nk]], axis=0)
    vcat = jnp.concatenate([r[0].astype(BF16) for r in refs[nk:2 * nk]], axis=0)
    rows = GQ_A * BLOCK
    cols = kcat.shape[0]
    rowi = lax.broadcasted_iota(jnp.int32, (rows, 1), 0)
    if latent:
        i = pl.program_id(1)
        r = lax.broadcasted_iota(jnp.int32, (rows, cols), 0) & (BLOCK - 1)
        c = lax.broadcasted_iota(jnp.int32, (rows, cols), 1)
        p_off = jnp.where(i > 0, 0, 2 * BLOCK)
        n_off = jnp.where(i < nblk - 1, 0, 2 * BLOCK)
        prev_ok = (c >= r + p_off) | (c >= BLOCK)
        next_ok = ((c - 2 * BLOCK + n_off) <= r) | (c < 2 * BLOCK) | (c >= 3 * BLOCK)
        valid = prev_ok & next_ok
    lane = lax.broadcasted_iota(jnp.int32, (BLOCK, LANES), 1)
    heads = []
    for j in range(KV_A):
        q3 = jnp.concatenate([q_ref[0, :, (GQ_A * j + g) * LANES:(GQ_A * j + g + 1) * LANES]
                              for g in range(GQ_A)], axis=0)
        s = _dot_nt(q3, kcat)
        if latent:
            s = jnp.where(valid, s, NEG)
        sink = jnp.where(rowi < BLOCK, sink_ref[GQ_A * j],
                         jnp.where(rowi < 2 * BLOCK, sink_ref[GQ_A * j + 1], sink_ref[GQ_A * j + 2]))
        m = jnp.maximum(jnp.max(s, axis=-1, keepdims=True), sink)
        e = jnp.exp(s - m)
        den = jnp.sum(e, axis=-1, keepdims=True) + jnp.exp(sink - m)
        o = _dot(e.astype(BF16), vcat) / den
        for g in range(GQ_A):
            heads.append((j, o[g * BLOCK:(g + 1) * BLOCK]))
    for blk in range(H_A // 2):
        (j0, o0), (j1, o1) = heads[2 * blk], heads[2 * blk + 1]
        lo = o0 if j0 == 0 else pltpu.roll(o0, HD_A, 1)
        hi = o1 if j1 == 1 else pltpu.roll(o1, HD_A, 1)
        o_ref[0, :, blk * LANES:(blk + 1) * LANES] = jnp.where(lane < HD_A, lo, hi).astype(o_ref.dtype)


def _attn_a(qa, ka, va, sink, ctx_kv):
    bsz, seq, _ = qa.shape
    nblk = seq // BLOCK
    latent = ctx_kv is not None
    kvw = KV_A * HD_A
    if latent:
        past = ctx_kv[0].shape[1]
        band = [pl.BlockSpec((1, BLOCK, kvw), lambda b, i: (b, jnp.maximum(i - 1, 0), 0)),
                pl.BlockSpec((1, BLOCK, kvw), lambda b, i: (b, i, 0)),
                pl.BlockSpec((1, BLOCK, kvw), lambda b, i: (b, jnp.minimum(i + 1, nblk - 1), 0)),
                pl.BlockSpec((1, past, kvw), lambda b, i: (b, 0, 0))]
        kv_specs = band + band
        kv_args = [ka, ka, ka, ctx_kv[0], va, va, va, ctx_kv[1]]
    else:
        kv_specs = [pl.BlockSpec((1, seq, kvw), lambda b, i: (b, 0, 0))] * 2
        kv_args = [ka, va]
    return pl.pallas_call(
        functools.partial(_attn_a_body, latent=latent, nblk=nblk),
        grid=(bsz, nblk),
        in_specs=[pl.BlockSpec(memory_space=pltpu.SMEM),
                  pl.BlockSpec((1, BLOCK, QA_COLS), lambda b, i: (b, i, 0))] + kv_specs,
        out_specs=pl.BlockSpec((1, BLOCK, W_A), lambda b, i: (b, i, 0)),
        out_shape=jax.ShapeDtypeStruct((bsz, seq, W_A), BF16),
        compiler_params=_cparams("parallel", "parallel"),
        name="attn_a_latent" if latent else "attn_a_ctx",
    )(sink, qa, *kv_args)


def _attn_b_body(lam_ref, gain_ref, q_ref, *refs, part_lens, lam_init, kc):
    npart = len(part_lens)
    k_refs, v_refs = refs[:npart], refs[npart:2 * npart]
    o_ref, s_scr, vm_scr = refs[2 * npart:]
    tq = q_ref.shape[1]
    chunks = []
    col = 0
    for p, plen in enumerate(part_lens):
        step = min(kc, plen)
        for start in range(0, plen, step):
            chunks.append((p, start, col, step))
            col += step

    @pl.when(pl.program_id(2) == 0)
    def _():
        off = 0
        for p, plen in enumerate(part_lens):
            v = v_refs[p][0].astype(BF16)
            lane_v = lax.broadcasted_iota(jnp.int32, (plen, LANES), 1)
            for h in range(2):
                own = (lane_v >= h * HD_B) & (lane_v < (h + 1) * HD_B)
                ones = jnp.where(lane_v == (1 - h) * HD_B, 1.0, 0.0).astype(BF16)
                vm_scr[h, off:off + plen, :] = jnp.where(own, v, ones)
            off += plen

    lv = lam_ref[...]
    lam = (jnp.exp(jnp.sum(lv[0:1] * lv[1:2], axis=-1, keepdims=True))
           - jnp.exp(jnp.sum(lv[2:3] * lv[3:4], axis=-1, keepdims=True)) + lam_init)
    q = q_ref[0]
    lane_q = lax.broadcasted_iota(jnp.int32, (tq, LANES), 1)
    total = jnp.zeros((tq, LANES), F32)
    for h in range(2):
        qc = [jnp.where((lane_q >= h * HD_B + c * DC_B) & (lane_q < h * HD_B + (c + 1) * DC_B), q, jnp.zeros_like(q))
              for c in range(2)]
        rows = [slice(c * tq, (c + 1) * tq) for c in range(2)]
        macc = [None, None]
        for p, start, col, step in chunks:
            kch = k_refs[p][0, start:start + step, :].astype(BF16)
            for c in range(2):
                s = _dot_nt(qc[c], kch)
                s_scr[rows[c], col:col + step] = s
                for j in range(step // LANES):
                    t = s[:, j * LANES:(j + 1) * LANES]
                    macc[c] = t if macc[c] is None else jnp.maximum(macc[c], t)
        m = [jnp.max(macc[c], axis=-1, keepdims=True) for c in range(2)]
        acc = [jnp.zeros((tq, LANES), F32) for _ in range(2)]
        for p, start, col, step in chunks:
            vch = vm_scr[h, col:col + step, :]
            for c in range(2):
                e = jnp.exp2(s_scr[rows[c], col:col + step] - m[c]).astype(BF16)
                acc[c] = acc[c] + _dot(e, vch)
        o2 = [acc[c] / jnp.sum(jnp.where(lane_q == (1 - h) * HD_B, acc[c], 0.0), axis=-1, keepdims=True)
              for c in range(2)]
        own = (lane_q >= h * HD_B) & (lane_q < (h + 1) * HD_B)
        total = total + jnp.where(own, o2[0] - lam * o2[1], 0.0)
    sq = total * total
    ss_lo = jnp.sum(jnp.where(lane_q < HD_B, sq, 0.0), axis=-1, keepdims=True)
    ss_hi = jnp.sum(jnp.where(lane_q >= HD_B, sq, 0.0), axis=-1, keepdims=True)
    rinv = jnp.where(lane_q < HD_B, lax.rsqrt(ss_lo * (1.0 / HD_B) + EPS), lax.rsqrt(ss_hi * (1.0 / HD_B) + EPS))
    o_ref[0] = (total * rinv * gain_ref[...] * (1.0 - lam_init)).astype(o_ref.dtype)


def _attn_b(qb, k_parts, v_parts, lam_b, gain, lam_init):
    bsz, seq, _ = qb.shape
    tq = 512 if seq % 512 == 0 else 256
    part_lens = tuple(k.shape[1] for k in k_parts)
    lk = sum(part_lens)
    kv_specs = [pl.BlockSpec((1, n, LANES), lambda b, hp, i: (b, 0, hp)) for n in part_lens]
    return pl.pallas_call(
        functools.partial(_attn_b_body, part_lens=part_lens, lam_init=lam_init, kc=512),
        grid=(bsz, W_B // LANES, seq // tq),
        in_specs=[pl.BlockSpec((4, DC_B), lambda b, hp, i: (0, 0)),
                  pl.BlockSpec((1, LANES), lambda b, hp, i: (0, 0)),
                  pl.BlockSpec((1, tq, LANES), lambda b, hp, i: (b, i, hp))] + kv_specs + kv_specs,
        out_specs=pl.BlockSpec((1, tq, LANES), lambda b, hp, i: (b, i, hp)),
        out_shape=jax.ShapeDtypeStruct((bsz, seq, W_B), BF16),
        scratch_shapes=[pltpu.VMEM((2 * tq, lk), F32), pltpu.VMEM((2, lk, LANES), BF16)],
        compiler_params=_cparams("parallel", "parallel", "arbitrary"),
        name="attn_b_latent" if len(k_parts) > 1 else "attn_b_ctx",
    )(lam_b, gain, qb, *k_parts, *v_parts)


def _ssm_body(u_ref, m_ref, g_ref, cc_ref, a_ref, h0_ref, y_ref, fin_ref, s_scr, h_scr, *, nb, nc):
    u = u_ref[0]
    col = lambda k: slice(k * LANES, (k + 1) * LANES)
    s = _dot(u, g_ref[0])
    for k in range(4):
        s_scr[k] = s[:, col(k)]
    a = a_ref[0]
    afr, afi, abr, abi = (jnp.broadcast_to(a[k:k + 1], (nb, LANES)) for k in range(4))
    h0 = h0_ref[0]

    def step(c, carry):
        fr, fi, br, bi = carry
        rf = pl.ds(c, nb, stride=nc)
        rb = pl.ds(nc - 1 - c, nb, stride=nc)
        h_scr[0, rf, :] = fr
        h_scr[1, rf, :] = fi
        h_scr[2, rb, :] = br
        h_scr[3, rb, :] = bi
        nfr = afr * fr - afi * fi + s_scr[0, rf, :]
        nfi = afr * fi + afi * fr + s_scr[1, rf, :]
        nbr = abr * br - abi * bi + s_scr[2, rb, :]
        nbi = abr * bi + abi * br + s_scr[3, rb, :]
        return nfr, nfi, nbr, nbi

    fin = lax.fori_loop(0, nc, step, tuple(h0[:, col(k)] for k in range(4)))
    for k in range(4):
        fin_ref[0, :, col(k)] = fin[k]
    hin = jnp.concatenate([h_scr[k] for k in range(4)], axis=1).astype(BF16)
    y = _dot(u, m_ref[0]) + _dot(hin, cc_ref[0])
    y_ref[0] = y.astype(y_ref.dtype)


def _ssm(u_rows, mats, h0, nb):
    npair, rows, w = u_rows.shape
    nc = rows // nb
    mat_spec = pl.BlockSpec((1, w, w), lambda p: (p, 0, 0))
    return pl.pallas_call(
        functools.partial(_ssm_body, nb=nb, nc=nc),
        grid=(npair,),
        in_specs=[pl.BlockSpec((1, rows, w), lambda p: (p, 0, 0)), mat_spec, mat_spec, mat_spec,
                  pl.BlockSpec((1, 4, LANES), lambda p: (p, 0, 0)),
                  pl.BlockSpec((1, nb, w), lambda p: (p, 0, 0))],
        out_specs=[pl.BlockSpec((1, rows, w), lambda p: (p, 0, 0)),
                   pl.BlockSpec((1, nb, w), lambda p: (p, 0, 0))],
        out_shape=[jax.ShapeDtypeStruct((npair, rows, w), BF16),
                   jax.ShapeDtypeStruct((npair, nb, w), F32)],
        scratch_shapes=[pltpu.VMEM((4, rows, LANES), F32), pltpu.VMEM((4, rows, LANES), F32)],
        compiler_params=_cparams("parallel"),
        name="ssm_scan",
    )(u_rows, mats["m"], mats["g"], mats["cc"], mats["a16"], h0)


def _ssm_matrices(lp):
    t = SSM_T
    ks = jnp.arange(t + 1, dtype=F32)
    dirs = []
    for d in range(2):
        lam = lax.complex(lp["ssm_lam_re"][d].astype(F32), lp["ssm_lam_im"][d].astype(F32))
        dt = jnp.exp(lp["ssm_log_dt"][d].astype(F32))[:, None]
        a_bar = jnp.exp(lam * dt)
        b_bar = ((a_bar - 1.0) / lam)[..., None] * lax.complex(lp["ssm_b_re"][d].astype(F32),
                                                               lp["ssm_b_im"][d].astype(F32))
        c_mat = lax.complex(lp["ssm_c_re"][d].astype(F32), lp["ssm_c_im"][d].astype(F32))
        pw = jnp.exp((lam * dt)[None] * ks[:, None, None].astype(jnp.complex64))
        kern = jnp.real(jnp.einsum("gop,kgp,gpi->gkoi", c_mat, pw[:t], b_bar))
        dirs.append((pw, b_bar, c_mat, kern))
    (pw_f, bb_f, cm_f, k_f), (pw_b, bb_b, cm_b, k_b) = dirs
    eye2 = jnp.eye(2, dtype=F32)
    ch, pw2 = SSM_CH, 2 * SSM_CH
    hi = lax.Precision.HIGHEST

    def pair_bd(x):
        r, c = x.shape[1:]
        return jnp.einsum("pgrc,gh->pgrhc", x.reshape(N_PAIR, 2, r, c), eye2.astype(x.dtype)).reshape(N_PAIR, 2 * r, 2 * c)

    def pair_vec(x):
        return x.reshape(x.shape[0], N_PAIR, 2 * P_C).transpose(1, 0, 2)

    def lag_blocks(kern):
        x = kern.transpose(0, 1, 3, 2).reshape(N_PAIR, 2, t, ch, ch)
        return jnp.einsum("pglic,gh->plgihc", x, eye2).reshape(N_PAIR, t, pw2, pw2)
    kp_f, kp_b = lag_blocks(k_f), lag_blocks(k_b)
    d_blk = pair_bd(lp["ssm_d"].astype(F32)[:, :, None] * jnp.eye(ch, dtype=F32)[None])
    center = (kp_f[:, 0] + kp_b[:, 0] + d_blk)[:, None]
    band = jnp.concatenate([kp_b[:, :0:-1], center, kp_f[:, 1:]], axis=1)
    band = band.transpose(0, 2, 1, 3).reshape(N_PAIR, pw2, (2 * t - 1) * pw2)
    m_p = jnp.concatenate([band[:, :, (t - 1 - s) * pw2:(t - 1 - s) * pw2 + SSM_ROW] for s in range(t)], axis=1)

    def inject(pw_sel, b_bar):
        x1 = jnp.repeat(pair_vec(pw_sel), pw2, axis=1)
        x2 = jnp.tile(pair_bd(b_bar.transpose(0, 2, 1)), (1, t, 1))
        return x1 * x2
    g_f = inject(pw_f[t - 1 - jnp.arange(t)], bb_f)
    g_b = inject(pw_b[jnp.arange(t)], bb_b)
    g_p = jnp.concatenate([jnp.real(g_f), jnp.imag(g_f), jnp.real(g_b), jnp.imag(g_b)], axis=2)

    lane = jnp.arange(SSM_ROW)
    exp_t = (jnp.arange(t)[:, None] == lane[None, :] // pw2).astype(F32)
    exp_c = (jnp.arange(pw2)[:, None] == lane[None, :] % pw2).astype(F32)

    def widen(x, e):
        f = lambda v: jnp.einsum("pqk,kx->pqx", v, e, precision=hi)
        return lax.complex(f(jnp.real(x)), f(jnp.imag(x)))

    def readout(pw_sel, c_mat):
        y1 = widen(pair_vec(pw_sel).transpose(0, 2, 1), exp_t)
        y2 = widen(pair_bd(c_mat.transpose(0, 2, 1)), exp_c)
        return y1 * y2
    z_f = readout(pw_f[1 + jnp.arange(t)], cm_f)
    z_b = readout(pw_b[t - jnp.arange(t)], cm_b)
    cc_p = jnp.concatenate([jnp.real(z_f), -jnp.imag(z_f), jnp.real(z_b), -jnp.imag(z_b)], axis=1)
    a16 = jnp.stack([jnp.real(pw_f[t]), jnp.imag(pw_f[t]), jnp.real(pw_b[t]), jnp.imag(pw_b[t])], axis=0)
    a16 = a16.reshape(4, N_PAIR, 2 * P_C).transpose(1, 0, 2)
    return dict(m=m_p.astype(BF16), g=g_p.astype(BF16), cc=cc_p.astype(BF16), a16=a16)


def _ssm_state_rows(s_re, s_im):
    bsz = s_re.shape[0]
    parts = [s_re[:, 0], s_im[:, 0], s_re[:, 1], s_im[:, 1]]
    st = jnp.stack([p.reshape(bsz, N_PAIR, 2 * P_C) for p in parts], axis=2)
    return st.transpose(1, 0, 2, 3).reshape(N_PAIR, bsz, 8 * P_C).astype(F32)


def _ssm_state_unrows(fin):
    npair, bsz, _ = fin.shape
    st = fin.reshape(npair, bsz, 4, 2, P_C).transpose(1, 2, 0, 3, 4).reshape(bsz, 4, G_C, P_C)
    return jnp.stack([st[:, 0], st[:, 2]], axis=1), jnp.stack([st[:, 1], st[:, 3]], axis=1)


def _route(scores, bias):
    tm = scores.shape[1]
    biased = scores + bias
    iota8 = lax.broadcasted_iota(jnp.int32, (PER_GROUP, tm), 0)
    grp = [biased[PER_GROUP * g:PER_GROUP * (g + 1)] for g in range(N_EXP_GROUPS)]
    gscore = []
    for v in grp:
        m1 = jnp.max(v, axis=0, keepdims=True)
        first = jnp.min(jnp.where(v == m1, iota8, PER_GROUP), axis=0, keepdims=True)
        m2 = jnp.max(jnp.where(iota8 == first, -jnp.inf, v), axis=0, keepdims=True)
        gscore.append(m1 + m2)
    masked = []
    for g in range(N_EXP_GROUPS):
        rank = jnp.zeros((1, tm), jnp.int32)
        for o in range(N_EXP_GROUPS):
            if o == g:
                continue
            ahead = (gscore[o] >= gscore[g]) if o < g else (gscore[o] > gscore[g])
            rank = rank + jnp.where(ahead, 1, 0)
        masked.append(jnp.where(rank < TOPK_GROUPS, grp[g], -jnp.inf))
    chosen = [None] * N_EXP_GROUPS
    for _ in range(TOP_K):
        best = masked[0]
        for v in masked[1:]:
            best = jnp.maximum(best, v)
        best = jnp.max(best, axis=0, keepdims=True)
        first = jnp.full((1, tm), N_EXPERTS, jnp.int32)
        for g, v in enumerate(masked):
            cand = jnp.min(jnp.where(v == best, iota8 + PER_GROUP * g, N_EXPERTS), axis=0, keepdims=True)
            first = jnp.minimum(first, cand)
        for g in range(N_EXP_GROUPS):
            hit = (iota8 + PER_GROUP * g) == first
            chosen[g] = hit if chosen[g] is None else (chosen[g] | hit)
            masked[g] = jnp.where(hit, -jnp.inf, masked[g])
    w = [jnp.where(chosen[g], scores[PER_GROUP * g:PER_GROUP * (g + 1)], 0.0) for g in range(N_EXP_GROUPS)]
    wsum = w[0]
    for v in w[1:]:
        wsum = wsum + v
    wsum = jnp.sum(wsum, axis=0, keepdims=True)
    return jnp.concatenate([v / wsum * ROUTED_SCALE for v in w], axis=0)


def _post_body(x_ref, oa_ref, ob_ref, y_ref, mod_ref, wglu_ref, wout_ref, g2_ref, wrh_ref, wrl_ref, br_ref,
               x1_ref, h2_ref, gate_ref, y_scr):
    d = D_MODEL
    tm = x_ref.shape[1]
    pw = 2 * SSM_CH
    for t in range(SSM_T):
        for blk in range(W_C // LANES):
            piece = jnp.concatenate([y_ref[blk * (LANES // pw) + pp, :, t * pw:(t + 1) * pw].astype(F32)
                                     for pp in range(LANES // pw)], axis=1)
            y_scr[blk, pl.ds(t, tm // SSM_T, stride=SSM_T), :] = piece
    g = jax.nn.gelu(jnp.concatenate([y_scr[blk] for blk in range(W_C // LANES)], axis=1))
    oc = g * jax.nn.sigmoid(_dot(g.astype(BF16), wglu_ref[...]))
    mix = (_dot(oa_ref[0], wout_ref[0:W_A]) + _dot(ob_ref[0], wout_ref[W_A:W_A + W_B])
           + _dot(oc.astype(BF16), wout_ref[W_A + W_B:]))
    mod = mod_ref[0]
    x1 = x_ref[0] + mod[:, 2 * d:3 * d] * mix
    x1_ref[0] = x1
    xn = x1 * lax.rsqrt(jnp.mean(x1 * x1, axis=-1, keepdims=True) + EPS) * g2_ref[...]
    h2 = xn * (1.0 + mod[:, 4 * d:5 * d]) + mod[:, 3 * d:4 * d]
    h_hi, h_lo = _split_bf16(h2)
    h2_ref[0] = h_hi
    logits = _dot_nt(wrh_ref[...], h_hi) + _dot_nt(wrh_ref[...], h_lo) + _dot_nt(wrl_ref[...], h_hi)
    gate_ref[0] = _route(jax.nn.sigmoid(logits), br_ref[...]).T


def _post_mix(x, oa, ob, y, mod, w_glu, w_out, g2, wr_hi, wr_lo, b_r):
    bsz, seq, d = x.shape
    tm = 512 if seq % 512 == 0 else 256
    bm = mod.shape[0]
    mod_idx = (lambda b, i: (b, 0, 0)) if bm > 1 else (lambda b, i: (0, 0, 0))
    const2 = lambda b, i: (0, 0)
    tok = lambda w: pl.BlockSpec((1, tm, w), lambda b, i: (b, i, 0))
    nt = seq // tm
    return pl.pallas_call(
        _post_body,
        grid=(bsz, nt),
        in_specs=[tok(d), tok(W_A), tok(W_B),
                  pl.BlockSpec((N_PAIR, tm // SSM_T, SSM_ROW), lambda b, i: (0, b * nt + i, 0)),
                  pl.BlockSpec((1, 1, 6 * d), mod_idx),
                  pl.BlockSpec((W_C, W_C), const2),
                  pl.BlockSpec((d, d), const2),
                  pl.BlockSpec((1, d), const2),
                  pl.BlockSpec((N_EXPERTS, d), const2),
                  pl.BlockSpec((N_EXPERTS, d), const2),
                  pl.BlockSpec((N_EXPERTS, 1), const2)],
        out_specs=[tok(d), tok(d), tok(N_EXPERTS)],
        out_shape=[jax.ShapeDtypeStruct((bsz, seq, d), F32),
                   jax.ShapeDtypeStruct((bsz, seq, d), BF16),
                   jax.ShapeDtypeStruct((bsz, seq, N_EXPERTS), F32)],
        scratch_shapes=[pltpu.VMEM((W_C // LANES, tm, LANES), F32)],
        compiler_params=_cparams("parallel", "parallel"),
        name="post_mix",
    )(x, oa, ob, y, mod, w_glu, w_out, g2, wr_hi, wr_lo, b_r)


def _moe_body(x1_ref, h_ref, gate_ref, g2_ref, w1_ref, w3_ref, w2_ref, c1_ref, c32_ref, ex_ref,
              s1_ref, s3_ref, s2_ref, o_ref, acc_ref, h8_ref, hs_ref):
    j = pl.program_id(1)

    @pl.when(j == 0)
    def _():
        h = h_ref[...]
        a = _dot(h, s1_ref[...])
        acc_ref[...] = _dot((a * jax.nn.sigmoid(a) * _dot(h, s3_ref[...])).astype(BF16), s2_ref[...])
        hf = h.astype(F32)
        sc = jnp.maximum(jnp.max(jnp.abs(hf), axis=-1, keepdims=True), F8_TINY) * (1.0 / F8_RANGE)
        hs_ref[...] = sc
        h8_ref[...] = (hf * (1.0 / sc)).astype(F8)

    ne = w1_ref.shape[0]
    h8 = h8_ref[...]
    hs = hs_ref[...]
    a = _dot(h8, jnp.concatenate([w1_ref[e] for e in range(ne)], axis=1)) * c1_ref[...] * hs
    b = _dot(h8, jnp.concatenate([w3_ref[e] for e in range(ne)], axis=1))
    gexp = _dot(jnp.concatenate(_split_bf16(gate_ref[...]), axis=1), ex_ref[...])
    hid = a * jax.nn.sigmoid(a) * b * gexp * c32_ref[...]
    sc = jnp.maximum(jnp.max(jnp.abs(hid), axis=-1, keepdims=True), F8_TINY) * (1.0 / F8_RANGE)
    acc_ref[...] += _dot((hid * (1.0 / sc)).astype(F8), w2_ref[...]) * (sc * hs)

    @pl.when(j == pl.num_programs(1) - 1)
    def _():
        o_ref[...] = x1_ref[...] + g2_ref[0] * acc_ref[...]


def _moe(x1, h2, gates, mod, seq, ew, expand, ws1, ws3, ws2):
    tokens, d = x1.shape
    bm = mod.shape[0]
    span = seq if bm > 1 else tokens
    tm = next(t for t in (1024, 512, 256) if span % t == 0)
    per_b = seq // tm if bm > 1 else 1
    mod_idx = (lambda i, j: (i // per_b, 0, 5)) if bm > 1 else (lambda i, j: (0, 0, 5))
    ne = 4
    fc = ne * F_EXP
    hidden = ew["w2"].shape[0]
    const2 = lambda i, j: (0, 0)
    chunk_row = pl.BlockSpec((1, fc), lambda i, j: (0, j))
    return pl.pallas_call(
        _moe_body,
        grid=(tokens // tm, hidden // fc),
        in_specs=[pl.BlockSpec((tm, d), lambda i, j: (i, 0)),
                  pl.BlockSpec((tm, d), lambda i, j: (i, 0)),
                  pl.BlockSpec((tm, N_EXPERTS), lambda i, j: (i, 0)),
                  pl.BlockSpec((1, 1, d), mod_idx),
                  pl.BlockSpec((ne, d, F_EXP), lambda i, j: (j, 0, 0)),
                  pl.BlockSpec((ne, d, F_EXP), lambda i, j: (j, 0, 0)),
                  pl.BlockSpec((fc, d), lambda i, j: (j, 0)),
                  chunk_row, chunk_row,
                  pl.BlockSpec((2 * N_EXPERTS, fc), lambda i, j: (0, j)),
                  pl.BlockSpec((d, F_SHARED), const2),
                  pl.BlockSpec((d, F_SHARED), const2),
                  pl.BlockSpec((F_SHARED, d), const2)],
        out_specs=pl.BlockSpec((tm, d), lambda i, j: (i, 0)),
        out_shape=jax.ShapeDtypeStruct((tokens, d), F32),
        scratch_shapes=[pltpu.VMEM((tm, d), F32), pltpu.VMEM((tm, d), F8), pltpu.VMEM((tm, 1), F32)],
        compiler_params=_cparams("parallel", "arbitrary"),
        name="moe",
    )(x1, h2, gates, mod, ew["w1"], ew["w3"], ew["w2"], ew["c1"], ew["c32"], expand, ws1, ws3, ws2)


def _expert_fp8(w):
    sc = jnp.maximum(jnp.max(jnp.abs(w), axis=(1, 2)), F8_TINY) * (1.0 / F8_RANGE)
    return (w / sc[:, None, None]).astype(F8), jnp.repeat(sc, F_EXP)[None, :].astype(F32)


def _prep_experts(p):
    w1, c1 = _expert_fp8(p["w_e1"].astype(F32))
    w3, c3 = _expert_fp8(p["w_e3"].astype(F32))
    w2, c2 = _expert_fp8(p["w_e2"].astype(F32))
    return dict(w1=w1, w3=w3, w2=w2.reshape(N_EXPERTS * F_EXP, D_MODEL), c1=c1, c32=c3 * c2)


def _rope_tables(seq):
    pos = jnp.arange(seq)
    row = (pos // GRID_W).astype(F32)[:, None]
    colp = (pos % GRID_W).astype(F32)[:, None]
    lane = jnp.arange(LANES)

    def table(width):
        half, quarter = width // 2, width // 4
        i = lane % width
        freq = ROPE_BASE ** (-(2.0 * (i % quarter).astype(F32)) / half)
        ang = jnp.where((i // half) == 0, row, colp) * freq[None, :]
        sign = jnp.where((i % half) < quarter, -1.0, 1.0)
        return jnp.cos(ang), jnp.sin(ang) * sign[None, :]

    ca, sa = table(HD_A)
    cb, sb = table(DC_B)
    return ca, sa, cb, sb


def _prep_layer(p):
    d = D_MODEL
    w_in = p["w_in"]
    place = (jnp.arange(H_A)[:, None] // GQ_A == jnp.arange(KV_A)[None, :]).astype(w_in.dtype)
    qa_pad = w_in[:, :W_A].reshape(d, H_A, 1, HD_A) * place[None, :, :, None]
    w_in_p = jnp.concatenate([qa_pad.reshape(d, QA_COLS), w_in[:, W_A:]], axis=1).astype(BF16)
    gains = jnp.stack([jnp.tile(p["q_norm_a"], LANES // HD_A) * (HD_A ** -0.5),
                       jnp.tile(p["k_norm_a"], LANES // HD_A),
                       jnp.tile(p["q_norm_b"], LANES // DC_B) * (DC_B ** -0.5 * LOG2E),
                       jnp.tile(p["k_norm_b"], LANES // DC_B)], axis=0).astype(F32)
    lp = {k: p[k] for k in ("ssm_lam_re", "ssm_lam_im", "ssm_log_dt", "ssm_b_re", "ssm_b_im",
                            "ssm_c_re", "ssm_c_im", "ssm_d")}
    wr_hi, wr_lo = _split_bf16(p["w_router"].T.astype(F32))
    return dict(
        w_in_p=w_in_p, gains=gains,
        g1=p["norm1_g"].reshape(1, d).astype(F32), g2=p["norm2_g"].reshape(1, d).astype(F32),
        sink=p["sink_a"].astype(F32), lam_b=p["lam_b"].astype(F32),
        subln=jnp.tile(p["subln_b"], LANES // HD_B).reshape(1, LANES).astype(F32),
        ssm=_ssm_matrices(lp),
        w_glu=p["w_glu"].astype(BF16), w_out=p["w_out"].astype(BF16),
        wr_hi=wr_hi, wr_lo=wr_lo, b_r=p["b_router"].reshape(N_EXPERTS, 1).astype(F32),
        experts=_prep_experts(p),
        ws1=p["w_s1"].astype(BF16), ws3=p["w_s3"].astype(BF16), ws2=p["w_s2"].astype(BF16),
    )


def _trunk_layer(x, mod, lw, consts, ctx):
    bsz, seq, d = x.shape
    latent = ctx is not None
    rope = consts["rope"] if latent else None
    kv_dtype = BF16 if latent else F32
    qa, ka, va, qb, kb, vb, u = _inproj(x, mod, lw["g1"], lw["w_in_p"], lw["gains"],
                                        consts["seg64"], consts["seg32"], rope, kv_dtype)
    if latent:
        oa = _attn_a(qa, ka, va, lw["sink"], (ctx["ak"], ctx["av"]))
        ob = _attn_b(qb, [kb, ctx["bk"]], [vb, ctx["bv"]], lw["lam_b"], lw["subln"], lw["lam_init"])
        h0 = ctx["h0"]
    else:
        oa = _attn_a(qa, ka, va, lw["sink"], None)
        ob = _attn_b(qb, [kb], [vb], lw["lam_b"], lw["subln"], lw["lam_init"])
        h0 = jnp.zeros((N_PAIR, bsz, 8 * P_C), F32)
    y_rows, fin = _ssm(u, lw["ssm"], h0, bsz)
    x1, h2, gates = _post_mix(x, oa, ob, y_rows, mod, lw["w_glu"], lw["w_out"], lw["g2"],
                                lw["wr_hi"], lw["wr_lo"], lw["b_r"])
    out = _moe(x1.reshape(bsz * seq, d), h2.reshape(bsz * seq, d), gates.reshape(bsz * seq, N_EXPERTS), mod, seq,
               lw["experts"], consts["expand"], lw["ws1"], lw["ws3"], lw["ws2"])
    return out.reshape(bsz, seq, d), (ka, va, kb, vb, fin)


def kernel(x_prompt, x_sample, cache_a_k, cache_a_v, cache_b_k, cache_b_v, state_ssm_re, state_ssm_im, c, c_ctx, norm1_g, norm2_g, w_ada, b_ada, w_in, q_norm_a, k_norm_a, sink_a, q_norm_b, k_norm_b, lam_b, subln_b, ssm_lam_re, ssm_lam_im, ssm_log_dt, ssm_b_re, ssm_b_im, ssm_c_re, ssm_c_im, ssm_d, w_glu, w_out, w_router, b_router, w_e1, w_e3, w_e2, w_s1, w_s3, w_s2):
    p = dict(norm1_g=norm1_g, norm2_g=norm2_g, w_in=w_in, q_norm_a=q_norm_a, k_norm_a=k_norm_a, sink_a=sink_a,
             q_norm_b=q_norm_b, k_norm_b=k_norm_b, lam_b=lam_b, subln_b=subln_b,
             ssm_lam_re=ssm_lam_re, ssm_lam_im=ssm_lam_im, ssm_log_dt=ssm_log_dt, ssm_b_re=ssm_b_re,
             ssm_b_im=ssm_b_im, ssm_c_re=ssm_c_re, ssm_c_im=ssm_c_im, ssm_d=ssm_d, w_glu=w_glu, w_out=w_out,
             w_router=w_router, b_router=b_router, w_e1=w_e1, w_e3=w_e3, w_e2=w_e2,
             w_s1=w_s1, w_s3=w_s3, w_s2=w_s2)
    depth = w_in.shape[0]
    bsz, seq, d = x_prompt.shape
    dbsz, dseq, _ = x_sample.shape
    past = cache_a_k.shape[3]

    mod_rows = 16
    cvec = jnp.concatenate([c.astype(F32), c_ctx.astype(F32)[None],
                            jnp.zeros((mod_rows - dbsz - 1, d), F32)], axis=0)
    mods = _modulation(cvec, w_ada.astype(F32), b_ada.astype(F32))

    lane = jnp.arange(LANES)
    hidden = N_EXPERTS * F_EXP
    consts = dict(
        rope=_rope_tables(dseq),
        seg64=(lane[:, None] // HD_A == lane[None, :] // HD_A).astype(BF16),
        seg32=(lane[:, None] // DC_B == lane[None, :] // DC_B).astype(BF16),
        expand=(jnp.arange(2 * N_EXPERTS)[:, None] % N_EXPERTS == jnp.arange(hidden)[None, :] // F_EXP).astype(BF16),
    )

    xp, xs = x_prompt, x_sample
    ak, av, bk, bv, sre, sim = [], [], [], [], [], []
    prepared = jax.vmap(_prep_layer)(p)
    for l in range(depth):
        lw = jax.tree.map(lambda v: v[l], prepared)
        lw["lam_init"] = 0.8 - 0.6 * math.exp(-0.3 * l)
        mod_lat = mods[l, :dbsz][:, None, :]
        mod_ctx = mods[l, dbsz:dbsz + 1][:, None, :]
        xp, (k_a, v_a, k_b, v_b, fin) = _trunk_layer(xp, mod_ctx, lw, consts, None)
        ak.append(k_a.reshape(bsz, seq, KV_A, HD_A).transpose(0, 2, 1, 3))
        av.append(v_a.reshape(bsz, seq, KV_A, HD_A).transpose(0, 2, 1, 3))
        bk.append(k_b.reshape(bsz, seq, H_B, 2, DC_B).transpose(0, 2, 3, 1, 4))
        bv.append(v_b.reshape(bsz, seq, H_B, HD_B).transpose(0, 2, 1, 3))
        f_re, f_im = _ssm_state_unrows(fin)
        sre.append(f_re)
        sim.append(f_im)
        ctx = dict(
            ak=cache_a_k[:, l].transpose(0, 2, 1, 3).reshape(dbsz, past, KV_A * HD_A).astype(BF16),
            av=cache_a_v[:, l].transpose(0, 2, 1, 3).reshape(dbsz, past, KV_A * HD_A).astype(BF16),
            bk=cache_b_k[:, l].transpose(0, 3, 1, 2, 4).reshape(dbsz, past, W_B).astype(BF16),
            bv=cache_b_v[:, l].transpose(0, 2, 1, 3).reshape(dbsz, past, W_B).astype(BF16),
            h0=_ssm_state_rows(state_ssm_re[:, l], state_ssm_im[:, l]),
        )
        xs, _ = _trunk_layer(xs, mod_lat, lw, consts, ctx)
    return (xp, xs, jnp.stack(ak, axis=1), jnp.stack(av, axis=1), jnp.stack(bk, axis=1),
            jnp.stack(bv, axis=1), jnp.stack(sre, axis=1), jnp.stack(sim, axis=1))
```

```python
import functools
import math

import jax
import jax.numpy as jnp
from jax import lax
from jax.experimental import pallas as pl
from jax.experimental.pallas import tpu as pltpu

F32 = jnp.float32
BF16 = jnp.bfloat16
F8 = jnp.float8_e4m3fn
F8_RANGE = 384.0
F8_TINY = 1e-30

D_MODEL = 1024
GRID_W = 64
BLOCK = 128
H_A, KV_A, HD_A = 6, 2, 64
GQ_A = H_A // KV_A
W_A = H_A * HD_A
H_B, HD_B = 4, 64
DC_B = HD_B // 2
W_B = H_B * HD_B
SSM_CH = 16
W_C = D_MODEL - W_A - W_B
G_C = W_C // SSM_CH
P_C = 64
N_EXPERTS, TOP_K, F_EXP, F_SHARED = 64, 6, 128, 256
N_EXP_GROUPS, TOPK_GROUPS = 8, 4
PER_GROUP = N_EXPERTS // N_EXP_GROUPS
ROUTED_SCALE = 2.5
ROPE_BASE = 10000.0
EPS = 1e-6
NEG = -1e30
LOG2E = 1.4426950408889634

LANES = 128
SSM_T = 16
N_PAIR = G_C // 2
SSM_ROW = 2 * SSM_T * SSM_CH
QA_COLS = H_A * LANES
IN_COLS_P = QA_COLS + 2 * KV_A * HD_A + 3 * W_B + W_C
VMEM_LIMIT = 56 << 20


def _cparams(*sem):
    return pltpu.CompilerParams(dimension_semantics=sem, vmem_limit_bytes=VMEM_LIMIT)


def _dot(a, b):
    return jnp.dot(a, b, preferred_element_type=F32)


def _dot_nt(a, b):
    return lax.dot_general(a, b, (((1,), (1,)), ((), ())), preferred_element_type=F32)


def _split_bf16(x):
    hi = x.astype(BF16)
    lo = (x - hi.astype(F32)).astype(BF16)
    return hi, lo


def _mod_body(c_ref, w_ref, b_ref, o_ref):
    c = c_ref[...]
    s = c * jax.nn.sigmoid(c)
    s_hi, s_lo = _split_bf16(s)
    w_hi, w_lo = _split_bf16(w_ref[0])
    o_ref[0] = _dot(s_hi, w_hi) + _dot(s_lo, w_hi) + _dot(s_hi, w_lo) + b_ref[0]


def _modulation(cvec, w_ada, b_ada):
    depth, d, n = w_ada.shape
    rows = cvec.shape[0]
    tn = 768
    return pl.pallas_call(
        _mod_body,
        grid=(depth, n // tn),
        in_specs=[pl.BlockSpec((rows, d), lambda l, j: (0, 0)),
                  pl.BlockSpec((1, d, tn), lambda l, j: (l, 0, j)),
                  pl.BlockSpec((1, 1, tn), lambda l, j: (l, 0, j))],
        out_specs=pl.BlockSpec((1, rows, tn), lambda l, j: (l, 0, j)),
        out_shape=jax.ShapeDtypeStruct((depth, rows, n), F32),
        compiler_params=_cparams("parallel", "parallel"),
        name="adaln_mod",
    )(cvec, w_ada, b_ada.reshape(depth, 1, n))


def _inproj_body(*refs, latent):
    if latent:
        (x_ref, mod_ref, g1_ref, w_ref, gains_ref, s64_ref, s32_ref, ca_ref, sa_ref, cb_ref, sb_ref,
         qa_ref, ka_ref, va_ref, qb_ref, kb_ref, vb_ref, u_ref, u_scr) = refs
    else:
        (x_ref, mod_ref, g1_ref, w_ref, gains_ref, s64_ref, s32_ref,
         qa_ref, ka_ref, va_ref, qb_ref, kb_ref, vb_ref, u_ref, u_scr) = refs
    d = D_MODEL
    x = x_ref[0]
    mod = mod_ref[0]
    xn = x * lax.rsqrt(jnp.mean(x * x, axis=-1, keepdims=True) + EPS) * g1_ref[...]
    h = xn * (1.0 + mod[:, d:2 * d]) + mod[:, 0:d]
    acc = _dot(h.astype(BF16), w_ref[...])

    tm = x.shape[0]
    lane = lax.broadcasted_iota(jnp.int32, (tm, LANES), 1)
    first_a = (lane % 32) < 16
    first_b = (lane % 16) < 8

    def normed(xb, seg_ref, inv_n, gain):
        ss = _dot((xb * xb).astype(BF16), seg_ref[...])
        return xb * lax.rsqrt(ss * inv_n + EPS) * gain

    def rope_a(y):
        if not latent:
            return y
        sw = jnp.where(first_a, pltpu.roll(y, LANES - 16, 1), pltpu.roll(y, 16, 1))
        return y * ca_ref[...] + sw * sa_ref[...]

    def rope_b(y):
        if not latent:
            return y
        sw = jnp.where(first_b, pltpu.roll(y, LANES - 8, 1), pltpu.roll(y, 8, 1))
        return y * cb_ref[...] + sw * sb_ref[...]

    gains = gains_ref[...]
    off = 0
    for b in range(H_A):
        y = normed(acc[:, off:off + LANES], s64_ref, 1.0 / HD_A, gains[0:1])
        qa_ref[0, :, b * LANES:(b + 1) * LANES] = rope_a(y).astype(qa_ref.dtype)
        off += LANES
    y = normed(acc[:, off:off + LANES], s64_ref, 1.0 / HD_A, gains[1:2])
    ka_ref[0] = rope_a(y).astype(ka_ref.dtype)
    off += LANES
    va_ref[0] = acc[:, off:off + LANES].astype(va_ref.dtype)
    off += LANES
    for b in range(W_B // LANES):
        y = normed(acc[:, off:off + LANES], s32_ref, 1.0 / DC_B, gains[2:3])
        qb_ref[0, :, b * LANES:(b + 1) * LANES] = rope_b(y).astype(qb_ref.dtype)
        off += LANES
    for b in range(W_B // LANES):
        y = normed(acc[:, off:off + LANES], s32_ref, 1.0 / DC_B, gains[3:4])
        kb_ref[0, :, b * LANES:(b + 1) * LANES] = rope_b(y).astype(kb_ref.dtype)
        off += LANES
    vb_ref[0] = acc[:, off:off + W_B].astype(vb_ref.dtype)
    off += W_B
    for blk in range(W_C // LANES):
        u_scr[blk] = acc[:, off + blk * LANES:off + (blk + 1) * LANES]
    pw = 2 * SSM_CH
    for t in range(SSM_T):
        for blk in range(W_C // LANES):
            xt = u_scr[blk, pl.ds(t, tm // SSM_T, stride=SSM_T), :]
            for pp in range(LANES // pw):
                u_ref[blk * (LANES // pw) + pp, :, t * pw:(t + 1) * pw] = xt[:, pp * pw:(pp + 1) * pw].astype(u_ref.dtype)


def _inproj(x, mod, g1, w_in_p, gains, seg64, seg32, rope, kv_dtype):
    bsz, seq, d = x.shape
    latent = rope is not None
    tm = next(t for t in (1024, 512, 256) if seq % t == 0)
    bm = mod.shape[0]
    mod_idx = (lambda b, i: (b, 0, 0)) if bm > 1 else (lambda b, i: (0, 0, 0))
    const2 = lambda b, i: (0, 0)
    tok = lambda w: pl.BlockSpec((1, tm, w), lambda b, i: (b, i, 0))
    in_specs = [tok(d),
                pl.BlockSpec((1, 1, 6 * d), mod_idx),
                pl.BlockSpec((1, d), const2),
                pl.BlockSpec((d, IN_COLS_P), const2),
                pl.BlockSpec((4, LANES), const2),
                pl.BlockSpec((LANES, LANES), const2),
                pl.BlockSpec((LANES, LANES), const2)]
    args = [x, mod, g1, w_in_p, gains, seg64, seg32]
    if latent:
        in_specs += [pl.BlockSpec((tm, LANES), lambda b, i: (i, 0))] * 4
        args += list(rope)
    widths = (QA_COLS, KV_A * HD_A, KV_A * HD_A, W_B, W_B, W_B)
    dtypes = (BF16, kv_dtype, kv_dtype, BF16, kv_dtype, kv_dtype)
    nt = seq // tm
    rows = tm // SSM_T
    u_spec = pl.BlockSpec((N_PAIR, rows, SSM_ROW), lambda b, i: (0, b * nt + i, 0))
    u_shape = jax.ShapeDtypeStruct((N_PAIR, bsz * seq // SSM_T, SSM_ROW), BF16)
    return pl.pallas_call(
        functools.partial(_inproj_body, latent=latent),
        grid=(bsz, nt),
        in_specs=in_specs,
        out_specs=[tok(w) for w in widths] + [u_spec],
        out_shape=[jax.ShapeDtypeStruct((bsz, seq, w), dt) for w, dt in zip(widths, dtypes)] + [u_shape],
        scratch_shapes=[pltpu.VMEM((W_C // LANES, tm, LANES), F32)],
        compiler_params=_cparams("parallel", "parallel"),
        name="inproj_latent" if latent else "inproj_ctx",
    )(*args)


def _attn_a_body(sink_ref, q_ref, *refs, latent, nblk):
    o_ref = refs[-1]
    nk = (len(refs) - 1) // 2
    kcat = jnp.concatenate([r[0].astype(BF16) for r in refs[:nk]], axis=0)
    vcat = jnp.concatenate([r[0].astype(BF16) for r in refs[nk:2 * nk]], axis=0)
    rows = GQ_A * BLOCK
    cols = kcat.shape[0]
    rowi = lax.broadcasted_iota(jnp.int32, (rows, 1), 0)
    if latent:
        i = pl.program_id(1)
        r = lax.broadcasted_iota(jnp.int32, (rows, cols), 0) & (BLOCK - 1)
        c = lax.broadcasted_iota(jnp.int32, (rows, cols), 1)
        p_off = jnp.where(i > 0, 0, 2 * BLOCK)
        n_off = jnp.where(i < nblk - 1, 0, 2 * BLOCK)
        prev_ok = (c >= r + p_off) | (c >= BLOCK)
        next_ok = ((c - 2 * BLOCK + n_off) <= r) | (c < 2 * BLOCK) | (c >= 3 * BLOCK)
        valid = prev_ok & next_ok
    lane = lax.broadcasted_iota(jnp.int32, (BLOCK, LANES), 1)
    heads = []
    for j in range(KV_A):
        q3 = jnp.concatenate([q_ref[0, :, (GQ_A * j + g) * LANES:(GQ_A * j + g + 1) * LANES]
                              for g in range(GQ_A)], axis=0)
        s = _dot_nt(q3, kcat)
        if latent:
            s = jnp.where(valid, s, NEG)
        sink = jnp.where(rowi < BLOCK, sink_ref[GQ_A * j],
                         jnp.where(rowi < 2 * BLOCK, sink_ref[GQ_A * j + 1], sink_ref[GQ_A * j + 2]))
        m = jnp.maximum(jnp.max(s, axis=-1, keepdims=True), sink)
        e = jnp.exp(s - m)
        den = jnp.sum(e, axis=-1, keepdims=True) + jnp.exp(sink - m)
        o = _dot(e.astype(BF16), vcat) / den
        for g in range(GQ_A):
            heads.append((j, o[g * BLOCK:(g + 1) * BLOCK]))
    for blk in range(H_A // 2):
        (j0, o0), (j1, o1) = heads[2 * blk], heads[2 * blk + 1]
        lo = o0 if j0 == 0 else pltpu.roll(o0, HD_A, 1)
        hi = o1 if j1 == 1 else pltpu.roll(o1, HD_A, 1)
        o_ref[0, :, blk * LANES:(blk + 1) * LANES] = jnp.where(lane < HD_A, lo, hi).astype(o_ref.dtype)


def _attn_a(qa, ka, va, sink, ctx_kv):
    bsz, seq, _ = qa.shape
    nblk = seq // BLOCK
    latent = ctx_kv is not None
    kvw = KV_A * HD_A
    if latent:
        past = ctx_kv[0].shape[1]
        band = [pl.BlockSpec((1, BLOCK, kvw), lambda b, i: (b, jnp.maximum(i - 1, 0), 0)),
                pl.BlockSpec((1, BLOCK, kvw), lambda b, i: (b, i, 0)),
                pl.BlockSpec((1, BLOCK, kvw), lambda b, i: (b, jnp.minimum(i + 1, nblk - 1), 0)),
                pl.BlockSpec((1, past, kvw), lambda b, i: (b, 0, 0))]
        kv_specs = band + band
        kv_args = [ka, ka, ka, ctx_kv[0], va, va, va, ctx_kv[1]]
    else:
        kv_specs = [pl.BlockSpec((1, seq, kvw), lambda b, i: (b, 0, 0))] * 2
        kv_args = [ka, va]
    return pl.pallas_call(
        functools.partial(_attn_a_body, latent=latent, nblk=nblk),
        grid=(bsz, nblk),
        in_specs=[pl.BlockSpec(memory_space=pltpu.SMEM),
                  pl.BlockSpec((1, BLOCK, QA_COLS), lambda b, i: (b, i, 0))] + kv_specs,
        out_specs=pl.BlockSpec((1, BLOCK, W_A), lambda b, i: (b, i, 0)),
        out_shape=jax.ShapeDtypeStruct((bsz, seq, W_A), BF16),
        compiler_params=_cparams("parallel", "parallel"),
        name="attn_a_latent" if latent else "attn_a_ctx",
    )(sink, qa, *kv_args)


def _attn_b_body(lam_ref, gain_ref, q_ref, *refs, part_lens, lam_init, kc):
    npart = len(part_lens)
    k_refs, v_refs = refs[:npart], refs[npart:2 * npart]
    o_ref, s_scr, vm_scr = refs[2 * npart:]
    tq = q_ref.shape[1]
    chunks = []
    col = 0
    for p, plen in enumerate(part_lens):
        step = min(kc, plen)
        for start in range(0, plen, step):
            chunks.append((p, start, col, step))
            col += step

    @pl.when(pl.program_id(2) == 0)
    def _():
        off = 0
        for p, plen in enumerate(part_lens):
            v = v_refs[p][0].astype(BF16)
            lane_v = lax.broadcasted_iota(jnp.int32, (plen, LANES), 1)
            for h in range(2):
                own = (lane_v >= h * HD_B) & (lane_v < (h + 1) * HD_B)
                ones = jnp.where(lane_v == (1 - h) * HD_B, 1.0, 0.0).astype(BF16)
                vm_scr[h, off:off + plen, :] = jnp.where(own, v, ones)
            off += plen

    lv = lam_ref[...]
    lam = (jnp.exp(jnp.sum(lv[0:1] * lv[1:2], axis=-1, keepdims=True))
           - jnp.exp(jnp.sum(lv[2:3] * lv[3:4], axis=-1, keepdims=True)) + lam_init)
    q = q_ref[0]
    lane_q = lax.broadcasted_iota(jnp.int32, (tq, LANES), 1)
    total = jnp.zeros((tq, LANES), F32)
    for h in range(2):
        qc = [jnp.where((lane_q >= h * HD_B + c * DC_B) & (lane_q < h * HD_B + (c + 1) * DC_B), q, jnp.zeros_like(q))
              for c in range(2)]
        rows = [slice(c * tq, (c + 1) * tq) for c in range(2)]
        macc = [None, None]
        for p, start, col, step in chunks:
            kch = k_refs[p][0, start:start + step, :].astype(BF16)
            for c in range(2):
                s = _dot_nt(qc[c], kch)
                s_scr[rows[c], col:col + step] = s
                for j in range(step // LANES):
                    t = s[:, j * LANES:(j + 1) * LANES]
                    macc[c] = t if macc[c] is None else jnp.maximum(macc[c], t)
        m = [jnp.max(macc[c], axis=-1, keepdims=True) for c in range(2)]
        acc = [jnp.zeros((tq, LANES), F32) for _ in range(2)]
        for p, start, col, step in chunks:
            vch = vm_scr[h, col:col + step, :]
            for c in range(2):
                e = jnp.exp2(s_scr[rows[c], col:col + step] - m[c]).astype(BF16)
                acc[c] = acc[c] + _dot(e, vch)
        o2 = [acc[c] / jnp.sum(jnp.where(lane_q == (1 - h) * HD_B, acc[c], 0.0), axis=-1, keepdims=True)
              for c in range(2)]
        own = (lane_q >= h * HD_B) & (lane_q < (h + 1) * HD_B)
        total = total + jnp.where(own, o2[0] - lam * o2[1], 0.0)
    sq = total * total
    ss_lo = jnp.sum(jnp.where(lane_q < HD_B, sq, 0.0), axis=-1, keepdims=True)
    ss_hi = jnp.sum(jnp.where(lane_q >= HD_B, sq, 0.0), axis=-1, keepdims=True)
    rinv = jnp.where(lane_q < HD_B, lax.rsqrt(ss_lo * (1.0 / HD_B) + EPS), lax.rsqrt(ss_hi * (1.0 / HD_B) + EPS))
    o_ref[0] = (total * rinv * gain_ref[...] * (1.0 - lam_init)).astype(o_ref.dtype)


def _attn_b(qb, k_parts, v_parts, lam_b, gain, lam_init):
    bsz, seq, _ = qb.shape
    tq = 512 if seq % 512 == 0 else 256
    part_lens = tuple(k.shape[1] for k in k_parts)
    lk = sum(part_lens)
    kv_specs = [pl.BlockSpec((1, n, LANES), lambda b, hp, i: (b, 0, hp)) for n in part_lens]
    return pl.pallas_call(
        functools.partial(_attn_b_body, part_lens=part_lens, lam_init=lam_init, kc=512),
        grid=(bsz, W_B // LANES, seq // tq),
        in_specs=[pl.BlockSpec((4, DC_B), lambda b, hp, i: (0, 0)),
                  pl.BlockSpec((1, LANES), lambda b, hp, i: (0, 0)),
                  pl.BlockSpec((1, tq, LANES), lambda b, hp, i: (b, i, hp))] + kv_specs + kv_specs,
        out_specs=pl.BlockSpec((1, tq, LANES), lambda b, hp, i: (b, i, hp)),
        out_shape=jax.ShapeDtypeStruct((bsz, seq, W_B), BF16),
        scratch_shapes=[pltpu.VMEM((2 * tq, lk), F32), pltpu.VMEM((2, lk, LANES), BF16)],
        compiler_params=_cparams("parallel", "parallel", "arbitrary"),
        name="attn_b_latent" if len(k_parts) > 1 else "attn_b_ctx",
    )(lam_b, gain, qb, *k_parts, *v_parts)


def _ssm_body(u_ref, m_ref, g_ref, cc_ref, a_ref, h0_ref, y_ref, fin_ref, s_scr, h_scr, *, nb, nc):
    u = u_ref[0]
    col = lambda k: slice(k * LANES, (k + 1) * LANES)
    s = _dot(u, g_ref[0])
    for k in range(4):
        s_scr[k] = s[:, col(k)]
    a = a_ref[0]
    afr, afi, abr, abi = (jnp.broadcast_to(a[k:k + 1], (nb, LANES)) for k in range(4))
    h0 = h0_ref[0]

    def step(c, carry):
        fr, fi, br, bi = carry
        rf = pl.ds(c, nb, stride=nc)
        rb = pl.ds(nc - 1 - c, nb, stride=nc)
        h_scr[0, rf, :] = fr
        h_scr[1, rf, :] = fi
        h_scr[2, rb, :] = br
        h_scr[3, rb, :] = bi
        nfr = afr * fr - afi * fi + s_scr[0, rf, :]
        nfi = afr * fi + afi * fr + s_scr[1, rf, :]
        nbr = abr * br - abi * bi + s_scr[2, rb, :]
        nbi = abr * bi + abi * br + s_scr[3, rb, :]
        return nfr, nfi, nbr, nbi

    fin = lax.fori_loop(0, nc, step, tuple(h0[:, col(k)] for k in range(4)))
    for k in range(4):
        fin_ref[0, :, col(k)] = fin[k]
    hin = jnp.concatenate([h_scr[k] for k in range(4)], axis=1).astype(BF16)
    y = _dot(u, m_ref[0]) + _dot(hin, cc_ref[0])
    y_ref[0] = y.astype(y_ref.dtype)


def _ssm(u_rows, mats, h0, nb):
    npair, rows, w = u_rows.shape
    nc = rows // nb
    mat_spec = pl.BlockSpec((1, w, w), lambda p: (p, 0, 0))
    return pl.pallas_call(
        functools.partial(_ssm_body, nb=nb, nc=nc),
        grid=(npair,),
        in_specs=[pl.BlockSpec((1, rows, w), lambda p: (p, 0, 0)), mat_spec, mat_spec, mat_spec,
                  pl.BlockSpec((1, 4, LANES), lambda p: (p, 0, 0)),
                  pl.BlockSpec((1, nb, w), lambda p: (p, 0, 0))],
        out_specs=[pl.BlockSpec((1, rows, w), lambda p: (p, 0, 0)),
                   pl.BlockSpec((1, nb, w), lambda p: (p, 0, 0))],
        out_shape=[jax.ShapeDtypeStruct((npair, rows, w), BF16),
                   jax.ShapeDtypeStruct((npair, nb, w), F32)],
        scratch_shapes=[pltpu.VMEM((4, rows, LANES), F32), pltpu.VMEM((4, rows, LANES), F32)],
        compiler_params=_cparams("parallel"),
        name="ssm_scan",
    )(u_rows, mats["m"], mats["g"], mats["cc"], mats["a16"], h0)


def _ssm_matrices(lp):
    t = SSM_T
    ks = jnp.arange(t + 1, dtype=F32)
    dirs = []
    for d in range(2):
        lam = lax.complex(lp["ssm_lam_re"][d].astype(F32), lp["ssm_lam_im"][d].astype(F32))
        dt = jnp.exp(lp["ssm_log_dt"][d].astype(F32))[:, None]
        a_bar = jnp.exp(lam * dt)
        b_bar = ((a_bar - 1.0) / lam)[..., None] * lax.complex(lp["ssm_b_re"][d].astype(F32),
                                                               lp["ssm_b_im"][d].astype(F32))
        c_mat = lax.complex(lp["ssm_c_re"][d].astype(F32), lp["ssm_c_im"][d].astype(F32))
        pw = jnp.exp((lam * dt)[None] * ks[:, None, None].astype(jnp.complex64))
        kern = jnp.real(jnp.einsum("gop,kgp,gpi->gkoi", c_mat, pw[:t], b_bar))
        dirs.append((pw, b_bar, c_mat, kern))
    (pw_f, bb_f, cm_f, k_f), (pw_b, bb_b, cm_b, k_b) = dirs
    eye2 = jnp.eye(2, dtype=F32)
    ch, pw2 = SSM_CH, 2 * SSM_CH
    hi = lax.Precision.HIGHEST

    def pair_bd(x):
        r, c = x.shape[1:]
        return jnp.einsum("pgrc,gh->pgrhc", x.reshape(N_PAIR, 2, r, c), eye2.astype(x.dtype)).reshape(N_PAIR, 2 * r, 2 * c)

    def pair_vec(x):
        return x.reshape(x.shape[0], N_PAIR, 2 * P_C).transpose(1, 0, 2)

    def lag_blocks(kern):
        x = kern.transpose(0, 1, 3, 2).reshape(N_PAIR, 2, t, ch, ch)
        return jnp.einsum("pglic,gh->plgihc", x, eye2).reshape(N_PAIR, t, pw2, pw2)
    kp_f, kp_b = lag_blocks(k_f), lag_blocks(k_b)
    d_blk = pair_bd(lp["ssm_d"].astype(F32)[:, :, None] * jnp.eye(ch, dtype=F32)[None])
    center = (kp_f[:, 0] + kp_b[:, 0] + d_blk)[:, None]
    band = jnp.concatenate([kp_b[:, :0:-1], center, kp_f[:, 1:]], axis=1)
    band = band.transpose(0, 2, 1, 3).reshape(N_PAIR, pw2, (2 * t - 1) * pw2)
    m_p = jnp.concatenate([band[:, :, (t - 1 - s) * pw2:(t - 1 - s) * pw2 + SSM_ROW] for s in range(t)], axis=1)

    def inject(pw_sel, b_bar):
        x1 = jnp.repeat(pair_vec(pw_sel), pw2, axis=1)
        x2 = jnp.tile(pair_bd(b_bar.transpose(0, 2, 1)), (1, t, 1))
        return x1 * x2
    g_f = inject(pw_f[t - 1 - jnp.arange(t)], bb_f)
    g_b = inject(pw_b[jnp.arange(t)], bb_b)
    g_p = jnp.concatenate([jnp.real(g_f), jnp.imag(g_f), jnp.real(g_b), jnp.imag(g_b)], axis=2)

    lane = jnp.arange(SSM_ROW)
    exp_t = (jnp.arange(t)[:, None] == lane[None, :] // pw2).astype(F32)
    exp_c = (jnp.arange(pw2)[:, None] == lane[None, :] % pw2).astype(F32)

    def widen(x, e):
        f = lambda v: jnp.einsum("pqk,kx->pqx", v, e, precision=hi)
        return lax.complex(f(jnp.real(x)), f(jnp.imag(x)))

    def readout(pw_sel, c_mat):
        y1 = widen(pair_vec(pw_sel).transpose(0, 2, 1), exp_t)
        y2 = widen(pair_bd(c_mat.transpose(0, 2, 1)), exp_c)
        return y1 * y2
    z_f = readout(pw_f[1 + jnp.arange(t)], cm_f)
    z_b = readout(pw_b[t - jnp.arange(t)], cm_b)
    cc_p = jnp.concatenate([jnp.real(z_f), -jnp.imag(z_f), jnp.real(z_b), -jnp.imag(z_b)], axis=1)
    a16 = jnp.stack([jnp.real(pw_f[t]), jnp.imag(pw_f[t]), jnp.real(pw_b[t]), jnp.imag(pw_b[t])], axis=0)
    a16 = a16.reshape(4, N_PAIR, 2 * P_C).transpose(1, 0, 2)
    return dict(m=m_p.astype(BF16), g=g_p.astype(BF16), cc=cc_p.astype(BF16), a16=a16)


def _ssm_state_rows(s_re, s_im):
    bsz = s_re.shape[0]
    parts = [s_re[:, 0], s_im[:, 0], s_re[:, 1], s_im[:, 1]]
    st = jnp.stack([p.reshape(bsz, N_PAIR, 2 * P_C) for p in parts], axis=2)
    return st.transpose(1, 0, 2, 3).reshape(N_PAIR, bsz, 8 * P_C).astype(F32)


def _ssm_state_unrows(fin):
    npair, bsz, _ = fin.shape
    st = fin.reshape(npair, bsz, 4, 2, P_C).transpose(1, 2, 0, 3, 4).reshape(bsz, 4, G_C, P_C)
    return jnp.stack([st[:, 0], st[:, 2]], axis=1), jnp.stack([st[:, 1], st[:, 3]], axis=1)


def _route(scores, bias):
    tm = scores.shape[1]
    biased = scores + bias
    iota8 = lax.broadcasted_iota(jnp.int32, (PER_GROUP, tm), 0)
    grp = [biased[PER_GROUP * g:PER_GROUP * (g + 1)] for g in range(N_EXP_GROUPS)]
    gscore = []
    for v in grp:
        m1 = jnp.max(v, axis=0, keepdims=True)
        first = jnp.min(jnp.where(v == m1, iota8, PER_GROUP), axis=0, keepdims=True)
        m2 = jnp.max(jnp.where(iota8 == first, -jnp.inf, v), axis=0, keepdims=True)
        gscore.append(m1 + m2)
    masked = []
    for g in range(N_EXP_GROUPS):
        rank = jnp.zeros((1, tm), jnp.int32)
        for o in range(N_EXP_GROUPS):
            if o == g:
                continue
            ahead = (gscore[o] >= gscore[g]) if o < g else (gscore[o] > gscore[g])
            rank = rank + jnp.where(ahead, 1, 0)
        masked.append(jnp.where(rank < TOPK_GROUPS, grp[g], -jnp.inf))
    chosen = [None] * N_EXP_GROUPS
    for _ in range(TOP_K):
        best = masked[0]
        for v in masked[1:]:
            best = jnp.maximum(best, v)
        best = jnp.max(best, axis=0, keepdims=True)
        first = jnp.full((1, tm), N_EXPERTS, jnp.int32)
        for g, v in enumerate(masked):
            cand = jnp.min(jnp.where(v == best, iota8 + PER_GROUP * g, N_EXPERTS), axis=0, keepdims=True)
            first = jnp.minimum(first, cand)
        for g in range(N_EXP_GROUPS):
            hit = (iota8 + PER_GROUP * g) == first
            chosen[g] = hit if chosen[g] is None else (chosen[g] | hit)
            masked[g] = jnp.where(hit, -jnp.inf, masked[g])
    w = [jnp.where(chosen[g], scores[PER_GROUP * g:PER_GROUP * (g + 1)], 0.0) for g in range(N_EXP_GROUPS)]
    wsum = w[0]
    for v in w[1:]:
        wsum = wsum + v
    wsum = jnp.sum(wsum, axis=0, keepdims=True)
    return jnp.concatenate([v / wsum * ROUTED_SCALE for v in w], axis=0)


def _post_body(x_ref, oa_ref, ob_ref, y_ref, mod_ref, wglu_ref, wout_ref, g2_ref, wrh_ref, wrl_ref, br_ref,
               x1_ref, h2_ref, gate_ref, y_scr):
    d = D_MODEL
    tm = x_ref.shape[1]
    pw = 2 * SSM_CH
    for t in range(SSM_T):
        for blk in range(W_C // LANES):
            piece = jnp.concatenate([y_ref[blk * (LANES // pw) + pp, :, t * pw:(t + 1) * pw].astype(F32)
                                     for pp in range(LANES // pw)], axis=1)
            y_scr[blk, pl.ds(t, tm // SSM_T, stride=SSM_T), :] = piece
    g = jax.nn.gelu(jnp.concatenate([y_scr[blk] for blk in range(W_C // LANES)], axis=1))
    oc = g * jax.nn.sigmoid(_dot(g.astype(BF16), wglu_ref[...]))
    mix = (_dot(oa_ref[0], wout_ref[0:W_A]) + _dot(ob_ref[0], wout_ref[W_A:W_A + W_B])
           + _dot(oc.astype(BF16), wout_ref[W_A + W_B:]))
    mod = mod_ref[0]
    x1 = x_ref[0] + mod[:, 2 * d:3 * d] * mix
    x1_ref[0] = x1
    xn = x1 * lax.rsqrt(jnp.mean(x1 * x1, axis=-1, keepdims=True) + EPS) * g2_ref[...]
    h2 = xn * (1.0 + mod[:, 4 * d:5 * d]) + mod[:, 3 * d:4 * d]
    h_hi, h_lo = _split_bf16(h2)
    h2_ref[0] = h_hi
    logits = _dot_nt(wrh_ref[...], h_hi) + _dot_nt(wrh_ref[...], h_lo) + _dot_nt(wrl_ref[...], h_hi)
    gate_ref[0] = _route(jax.nn.sigmoid(logits), br_ref[...]).T


def _post_mix(x, oa, ob, y, mod, w_glu, w_out, g2, wr_hi, wr_lo, b_r):
    bsz, seq, d = x.shape
    tm = 512 if seq % 512 == 0 else 256
    bm = mod.shape[0]
    mod_idx = (lambda b, i: (b, 0, 0)) if bm > 1 else (lambda b, i: (0, 0, 0))
    const2 = lambda b, i: (0, 0)
    tok = lambda w: pl.BlockSpec((1, tm, w), lambda b, i: (b, i, 0))
    nt = seq // tm
    return pl.pallas_call(
        _post_body,
        grid=(bsz, nt),
        in_specs=[tok(d), tok(W_A), tok(W_B),
                  pl.BlockSpec((N_PAIR, tm // SSM_T, SSM_ROW), lambda b, i: (0, b * nt + i, 0)),
                  pl.BlockSpec((1, 1, 6 * d), mod_idx),
                  pl.BlockSpec((W_C, W_C), const2),
                  pl.BlockSpec((d, d), const2),
                  pl.BlockSpec((1, d), const2),
                  pl.BlockSpec((N_EXPERTS, d), const2),
                  pl.BlockSpec((N_EXPERTS, d), const2),
                  pl.BlockSpec((N_EXPERTS, 1), const2)],
        out_specs=[tok(d), tok(d), tok(N_EXPERTS)],
        out_shape=[jax.ShapeDtypeStruct((bsz, seq, d), F32),
                   jax.ShapeDtypeStruct((bsz, seq, d), BF16),
                   jax.ShapeDtypeStruct((bsz, seq, N_EXPERTS), F32)],
        scratch_shapes=[pltpu.VMEM((W_C // LANES, tm, LANES), F32)],
        compiler_params=_cparams("parallel", "parallel"),
        name="post_mix",
    )(x, oa, ob, y, mod, w_glu, w_out, g2, wr_hi, wr_lo, b_r)


def _moe_body(x1_ref, h_ref, gate_ref, g2_ref, w1_ref, w3_ref, w2_ref, c1_ref, c32_ref, ex_ref,
              s1_ref, s3_ref, s2_ref, o_ref, acc_ref, h8_ref, hs_ref):
    j = pl.program_id(1)

    @pl.when(j == 0)
    def _():
        h = h_ref[...]
        a = _dot(h, s1_ref[...])
        acc_ref[...] = _dot((a * jax.nn.sigmoid(a) * _dot(h, s3_ref[...])).astype(BF16), s2_ref[...])
        hf = h.astype(F32)
        sc = jnp.maximum(jnp.max(jnp.abs(hf), axis=-1, keepdims=True), F8_TINY) * (1.0 / F8_RANGE)
        hs_ref[...] = sc
        h8_ref[...] = (hf * (1.0 / sc)).astype(F8)

    ne = w1_ref.shape[0]
    h8 = h8_ref[...]
    hs = hs_ref[...]
    a = _dot(h8, jnp.concatenate([w1_ref[e] for e in range(ne)], axis=1)) * c1_ref[...] * hs
    b = _dot(h8, jnp.concatenate([w3_ref[e] for e in range(ne)], axis=1))
    gexp = _dot(jnp.concatenate(_split_bf16(gate_ref[...]), axis=1), ex_ref[...])
    hid = a * jax.nn.sigmoid(a) * b * gexp * c32_ref[...]
    sc = jnp.maximum(jnp.max(jnp.abs(hid), axis=-1, keepdims=True), F8_TINY) * (1.0 / F8_RANGE)
    acc_ref[...] += _dot((hid * (1.0 / sc)).astype(F8), w2_ref[...]) * (sc * hs)

    @pl.when(j == pl.num_programs(1) - 1)
    def _():
        o_ref[...] = x1_ref[...] + g2_ref[0] * acc_ref[...]


def _moe(x1, h2, gates, mod, seq, ew, expand, ws1, ws3, ws2):
    tokens, d = x1.shape
    bm = mod.shape[0]
    span = seq if bm > 1 else tokens
    tm = next(t for t in (1024, 512, 256) if span % t == 0)
    per_b = seq // tm if bm > 1 else 1
    mod_idx = (lambda i, j: (i // per_b, 0, 5)) if bm > 1 else (lambda i, j: (0, 0, 5))
    ne = 8
    fc = ne * F_EXP
    hidden = ew["w2"].shape[0]
    const2 = lambda i, j: (0, 0)
    chunk_row = pl.BlockSpec((1, fc), lambda i, j: (0, j))
    return pl.pallas_call(
        _moe_body,
        grid=(tokens // tm, hidden // fc),
        in_specs=[pl.BlockSpec((tm, d), lambda i, j: (i, 0)),
                  pl.BlockSpec((tm, d), lambda i, j: (i, 0)),
                  pl.BlockSpec((tm, N_EXPERTS), lambda i, j: (i, 0)),
                  pl.BlockSpec((1, 1, d), mod_idx),
                  pl.BlockSpec((ne, d, F_EXP), lambda i, j: (j, 0, 0)),
                  pl.BlockSpec((ne, d, F_EXP), lambda i, j: (j, 0, 0)),
                  pl.BlockSpec((fc, d), lambda i, j: (j, 0)),
                  chunk_row, chunk_row,
                  pl.BlockSpec((2 * N_EXPERTS, fc), lambda i, j: (0, j)),
                  pl.BlockSpec((d, F_SHARED), const2),
                  pl.BlockSpec((d, F_SHARED), const2),
                  pl.BlockSpec((F_SHARED, d), const2)],
        out_specs=pl.BlockSpec((tm, d), lambda i, j: (i, 0)),
        out_shape=jax.ShapeDtypeStruct((tokens, d), F32),
        scratch_shapes=[pltpu.VMEM((tm, d), F32), pltpu.VMEM((tm, d), F8), pltpu.VMEM((tm, 1), F32)],
        compiler_params=_cparams("parallel", "arbitrary"),
        name="moe",
    )(x1, h2, gates, mod, ew["w1"], ew["w3"], ew["w2"], ew["c1"], ew["c32"], expand, ws1, ws3, ws2)


def _expert_fp8(w):
    sc = jnp.maximum(jnp.max(jnp.abs(w), axis=(1, 2)), F8_TINY) * (1.0 / F8_RANGE)
    return (w / sc[:, None, None]).astype(F8), jnp.repeat(sc, F_EXP)[None, :].astype(F32)


def _prep_experts(p):
    w1, c1 = _expert_fp8(p["w_e1"].astype(F32))
    w3, c3 = _expert_fp8(p["w_e3"].astype(F32))
    w2, c2 = _expert_fp8(p["w_e2"].astype(F32))
    return dict(w1=w1, w3=w3, w2=w2.reshape(N_EXPERTS * F_EXP, D_MODEL), c1=c1, c32=c3 * c2)


def _rope_tables(seq):
    pos = jnp.arange(seq)
    row = (pos // GRID_W).astype(F32)[:, None]
    colp = (pos % GRID_W).astype(F32)[:, None]
    lane = jnp.arange(LANES)

    def table(width):
        half, quarter = width // 2, width // 4
        i = lane % width
        freq = ROPE_BASE ** (-(2.0 * (i % quarter).astype(F32)) / half)
        ang = jnp.where((i // half) == 0, row, colp) * freq[None, :]
        sign = jnp.where((i % half) < quarter, -1.0, 1.0)
        return jnp.cos(ang), jnp.sin(ang) * sign[None, :]

    ca, sa = table(HD_A)
    cb, sb = table(DC_B)
    return ca, sa, cb, sb


def _prep_layer(p):
    d = D_MODEL
    w_in = p["w_in"]
    place = (jnp.arange(H_A)[:, None] // GQ_A == jnp.arange(KV_A)[None, :]).astype(w_in.dtype)
    qa_pad = w_in[:, :W_A].reshape(d, H_A, 1, HD_A) * place[None, :, :, None]
    w_in_p = jnp.concatenate([qa_pad.reshape(d, QA_COLS), w_in[:, W_A:]], axis=1).astype(BF16)
    gains = jnp.stack([jnp.tile(p["q_norm_a"], LANES // HD_A) * (HD_A ** -0.5),
                       jnp.tile(p["k_norm_a"], LANES // HD_A),
                       jnp.tile(p["q_norm_b"], LANES // DC_B) * (DC_B ** -0.5 * LOG2E),
                       jnp.tile(p["k_norm_b"], LANES // DC_B)], axis=0).astype(F32)
    lp = {k: p[k] for k in ("ssm_lam_re", "ssm_lam_im", "ssm_log_dt", "ssm_b_re", "ssm_b_im",
                            "ssm_c_re", "ssm_c_im", "ssm_d")}
    wr_hi, wr_lo = _split_bf16(p["w_router"].T.astype(F32))
    return dict(
        w_in_p=w_in_p, gains=gains,
        g1=p["norm1_g"].reshape(1, d).astype(F32), g2=p["norm2_g"].reshape(1, d).astype(F32),
        sink=p["sink_a"].astype(F32), lam_b=p["lam_b"].astype(F32),
        subln=jnp.tile(p["subln_b"], LANES // HD_B).reshape(1, LANES).astype(F32),
        ssm=_ssm_matrices(lp),
        w_glu=p["w_glu"].astype(BF16), w_out=p["w_out"].astype(BF16),
        wr_hi=wr_hi, wr_lo=wr_lo, b_r=p["b_router"].reshape(N_EXPERTS, 1).astype(F32),
        experts=_prep_experts(p),
        ws1=p["w_s1"].astype(BF16), ws3=p["w_s3"].astype(BF16), ws2=p["w_s2"].astype(BF16),
    )


def _trunk_layer(x, mod, lw, consts, ctx):
    bsz, seq, d = x.shape
    latent = ctx is not None
    rope = consts["rope"] if latent else None
    kv_dtype = BF16 if latent else F32
    qa, ka, va, qb, kb, vb, u = _inproj(x, mod, lw["g1"], lw["w_in_p"], lw["gains"],
                                        consts["seg64"], consts["seg32"], rope, kv_dtype)
    if latent:
        oa = _attn_a(qa, ka, va, lw["sink"], (ctx["ak"], ctx["av"]))
        ob = _attn_b(qb, [kb, ctx["bk"]], [vb, ctx["bv"]], lw["lam_b"], lw["subln"], lw["lam_init"])
        h0 = ctx["h0"]
    else:
        oa = _attn_a(qa, ka, va, lw["sink"], None)
        ob = _attn_b(qb, [kb], [vb], lw["lam_b"], lw["subln"], lw["lam_init"])
        h0 = jnp.zeros((N_PAIR, bsz, 8 * P_C), F32)
    y_rows, fin = _ssm(u, lw["ssm"], h0, bsz)
    x1, h2, gates = _post_mix(x, oa, ob, y_rows, mod, lw["w_glu"], lw["w_out"], lw["g2"],
                                lw["wr_hi"], lw["wr_lo"], lw["b_r"])
    out = _moe(x1.reshape(bsz * seq, d), h2.reshape(bsz * seq, d), gates.reshape(bsz * seq, N_EXPERTS), mod, seq,
               lw["experts"], consts["expand"], lw["ws1"], lw["ws3"], lw["ws2"])
    return out.reshape(bsz, seq, d), (ka, va, kb, vb, fin)


def kernel(x_prompt, x_sample, cache_a_k, cache_a_v, cache_b_k, cache_b_v, state_ssm_re, state_ssm_im, c, c_ctx, norm1_g, norm2_g, w_ada, b_ada, w_in, q_norm_a, k_norm_a, sink_a, q_norm_b, k_norm_b, lam_b, subln_b, ssm_lam_re, ssm_lam_im, ssm_log_dt, ssm_b_re, ssm_b_im, ssm_c_re, ssm_c_im, ssm_d, w_glu, w_out, w_router, b_router, w_e1, w_e3, w_e2, w_s1, w_s3, w_s2):
    p = dict(norm1_g=norm1_g, norm2_g=norm2_g, w_in=w_in, q_norm_a=q_norm_a, k_norm_a=k_norm_a, sink_a=sink_a,
             q_norm_b=q_norm_b, k_norm_b=k_norm_b, lam_b=lam_b, subln_b=subln_b,
             ssm_lam_re=ssm_lam_re, ssm_lam_im=ssm_lam_im, ssm_log_dt=ssm_log_dt, ssm_b_re=ssm_b_re,
             ssm_b_im=ssm_b_im, ssm_c_re=ssm_c_re, ssm_c_im=ssm_c_im, ssm_d=ssm_d, w_glu=w_glu, w_out=w_out,
             w_router=w_router, b_router=b_router, w_e1=w_e1, w_e3=w_e3, w_e2=w_e2,
             w_s1=w_s1, w_s3=w_s3, w_s2=w_s2)
    depth = w_in.shape[0]
    bsz, seq, d = x_prompt.shape
    dbsz, dseq, _ = x_sample.shape
    past = cache_a_k.shape[3]

    mod_rows = 16
    cvec = jnp.concatenate([c.astype(F32), c_ctx.astype(F32)[None],
                            jnp.zeros((mod_rows - dbsz - 1, d), F32)], axis=0)
    mods = _modulation(cvec, w_ada.astype(F32), b_ada.astype(F32))

    lane = jnp.arange(LANES)
    hidden = N_EXPERTS * F_EXP
    consts = dict(
        rope=_rope_tables(dseq),
        seg64=(lane[:, None] // HD_A == lane[None, :] // HD_A).astype(BF16),
        seg32=(lane[:, None] // DC_B == lane[None, :] // DC_B).astype(BF16),
        expand=(jnp.arange(2 * N_EXPERTS)[:, None] % N_EXPERTS == jnp.arange(hidden)[None, :] // F_EXP).astype(BF16),
    )

    xp, xs = x_prompt, x_sample
    ak, av, bk, bv, sre, sim = [], [], [], [], [], []
    prepared = jax.vmap(_prep_layer)(p)
    for l in range(depth):
        lw = jax.tree.map(lambda v: v[l], prepared)
        lw["lam_init"] = 0.8 - 0.6 * math.exp(-0.3 * l)
        mod_lat = mods[l, :dbsz][:, None, :]
        mod_ctx = mods[l, dbsz:dbsz + 1][:, None, :]
        xp, (k_a, v_a, k_b, v_b, fin) = _trunk_layer(xp, mod_ctx, lw, consts, None)
        ak.append(k_a.reshape(bsz, seq, KV_A, HD_A).transpose(0, 2, 1, 3))
        av.append(v_a.reshape(bsz, seq, KV_A, HD_A).transpose(0, 2, 1, 3))
        bk.append(k_b.reshape(bsz, seq, H_B, 2, DC_B).transpose(0, 2, 3, 1, 4))
        bv.append(v_b.reshape(bsz, seq, H_B, HD_B).transpose(0, 2, 1, 3))
        f_re, f_im = _ssm_state_unrows(fin)
        sre.append(f_re)
        sim.append(f_im)
        ctx = dict(
            ak=cache_a_k[:, l].transpose(0, 2, 1, 3).reshape(dbsz, past, KV_A * HD_A).astype(BF16),
            av=cache_a_v[:, l].transpose(0, 2, 1, 3).reshape(dbsz, past, KV_A * HD_A).astype(BF16),
            bk=cache_b_k[:, l].transpose(0, 3, 1, 2, 4).reshape(dbsz, past, W_B).astype(BF16),
            bv=cache_b_v[:, l].transpose(0, 2, 1, 3).reshape(dbsz, past, W_B).astype(BF16),
            h0=_ssm_state_rows(state_ssm_re[:, l], state_ssm_im[:, l]),
        )
        xs, _ = _trunk_layer(xs, mod_lat, lw, consts, ctx)
    return (xp, xs, jnp.stack(ak, axis=1), jnp.stack(av, axis=1), jnp.stack(bk, axis=1),
            jnp.stack(bv, axis=1), jnp.stack(sre, axis=1), jnp.stack(sim, axis=1))
```

```python
import functools
import math

import jax
import jax.numpy as jnp
from jax import lax
from jax.experimental import pallas as pl
from jax.experimental.pallas import tpu as pltpu

F32 = jnp.float32
BF16 = jnp.bfloat16
F8 = jnp.float8_e4m3fn
F8_RANGE = 384.0
F8_TINY = 1e-30

D_MODEL = 1024
GRID_W = 64
BLOCK = 128
H_A, KV_A, HD_A = 6, 2, 64
GQ_A = H_A // KV_A
W_A = H_A * HD_A
H_B, HD_B = 4, 64
DC_B = HD_B // 2
W_B = H_B * HD_B
SSM_CH = 16
W_C = D_MODEL - W_A - W_B
G_C = W_C // SSM_CH
P_C = 64
N_EXPERTS, TOP_K, F_EXP, F_SHARED = 64, 6, 128, 256
N_EXP_GROUPS, TOPK_GROUPS = 8, 4
PER_GROUP = N_EXPERTS // N_EXP_GROUPS
ROUTED_SCALE = 2.5
ROPE_BASE = 10000.0
EPS = 1e-6
NEG = -1e30
LOG2E = 1.4426950408889634

LANES = 128
SSM_T = 16
N_PAIR = G_C // 2
SSM_ROW = 2 * SSM_T * SSM_CH
QA_COLS = H_A * LANES
IN_COLS_P = QA_COLS + 2 * KV_A * HD_A + 3 * W_B + W_C
VMEM_LIMIT = 56 << 20
A_STEP_BLOCKS = 8


def _cparams(*sem):
    return pltpu.CompilerParams(dimension_semantics=sem, vmem_limit_bytes=VMEM_LIMIT)


def _dot(a, b):
    return jnp.dot(a, b, preferred_element_type=F32)


def _dot_nt(a, b):
    return lax.dot_general(a, b, (((1,), (1,)), ((), ())), preferred_element_type=F32)


def _split_bf16(x):
    hi = x.astype(BF16)
    lo = (x - hi.astype(F32)).astype(BF16)
    return hi, lo


def _mod_body(c_ref, w_ref, b_ref, o_ref):
    c = c_ref[...]
    s = c * jax.nn.sigmoid(c)
    s_hi, s_lo = _split_bf16(s)
    w_hi, w_lo = _split_bf16(w_ref[0])
    o_ref[0] = _dot(s_hi, w_hi) + _dot(s_lo, w_hi) + _dot(s_hi, w_lo) + b_ref[0]


def _modulation(cvec, w_ada, b_ada):
    depth, d, n = w_ada.shape
    rows = cvec.shape[0]
    tn = 768
    return pl.pallas_call(
        _mod_body,
        grid=(depth, n // tn),
        in_specs=[pl.BlockSpec((rows, d), lambda l, j: (0, 0)),
                  pl.BlockSpec((1, d, tn), lambda l, j: (l, 0, j)),
                  pl.BlockSpec((1, 1, tn), lambda l, j: (l, 0, j))],
        out_specs=pl.BlockSpec((1, rows, tn), lambda l, j: (l, 0, j)),
        out_shape=jax.ShapeDtypeStruct((depth, rows, n), F32),
        compiler_params=_cparams("parallel", "parallel"),
        name="adaln_mod",
    )(cvec, w_ada, b_ada.reshape(depth, 1, n))


def _inproj_body(*refs, latent):
    if latent:
        (x_ref, mod_ref, g1_ref, w_ref, gains_ref, s64_ref, s32_ref, ca_ref, sa_ref, cb_ref, sb_ref,
         qa_ref, ka_ref, va_ref, qb_ref, kb_ref, vb_ref, u_ref, u_scr) = refs
    else:
        (x_ref, mod_ref, g1_ref, w_ref, gains_ref, s64_ref, s32_ref,
         qa_ref, ka_ref, va_ref, qb_ref, kb_ref, vb_ref, u_ref, u_scr) = refs
    d = D_MODEL
    x = x_ref[0]
    mod = mod_ref[0]
    xn = x * lax.rsqrt(jnp.mean(x * x, axis=-1, keepdims=True) + EPS) * g1_ref[...]
    h = xn * (1.0 + mod[:, d:2 * d]) + mod[:, 0:d]
    acc = _dot(h.astype(BF16), w_ref[...])

    tm = x.shape[0]
    lane = lax.broadcasted_iota(jnp.int32, (tm, LANES), 1)
    first_a = (lane % 32) < 16
    first_b = (lane % 16) < 8

    def normed(xb, seg_ref, inv_n, gain):
        ss = _dot((xb * xb).astype(BF16), seg_ref[...])
        return xb * lax.rsqrt(ss * inv_n + EPS) * gain

    def rope_a(y):
        if not latent:
            return y
        sw = jnp.where(first_a, pltpu.roll(y, LANES - 16, 1), pltpu.roll(y, 16, 1))
        return y * ca_ref[...] + sw * sa_ref[...]

    def rope_b(y):
        if not latent:
            return y
        sw = jnp.where(first_b, pltpu.roll(y, LANES - 8, 1), pltpu.roll(y, 8, 1))
        return y * cb_ref[...] + sw * sb_ref[...]

    gains = gains_ref[...]
    off = 0
    for b in range(H_A):
        y = normed(acc[:, off:off + LANES], s64_ref, 1.0 / HD_A, gains[0:1])
        qa_ref[0, :, b * LANES:(b + 1) * LANES] = rope_a(y).astype(qa_ref.dtype)
        off += LANES
    y = normed(acc[:, off:off + LANES], s64_ref, 1.0 / HD_A, gains[1:2])
    ka_ref[0] = rope_a(y).astype(ka_ref.dtype)
    off += LANES
    va_ref[0] = acc[:, off:off + LANES].astype(va_ref.dtype)
    off += LANES
    for b in range(W_B // LANES):
        y = normed(acc[:, off:off + LANES], s32_ref, 1.0 / DC_B, gains[2:3])
        qb_ref[0, :, b * LANES:(b + 1) * LANES] = rope_b(y).astype(qb_ref.dtype)
        off += LANES
    for b in range(W_B // LANES):
        y = normed(acc[:, off:off + LANES], s32_ref, 1.0 / DC_B, gains[3:4])
        kb_ref[0, :, b * LANES:(b + 1) * LANES] = rope_b(y).astype(kb_ref.dtype)
        off += LANES
    vb_ref[0] = acc[:, off:off + W_B].astype(vb_ref.dtype)
    off += W_B
    for blk in range(W_C // LANES):
        u_scr[blk] = acc[:, off + blk * LANES:off + (blk + 1) * LANES]
    pw = 2 * SSM_CH
    for t in range(SSM_T):
        for blk in range(W_C // LANES):
            xt = u_scr[blk, pl.ds(t, tm // SSM_T, stride=SSM_T), :]
            for pp in range(LANES // pw):
                u_ref[blk * (LANES // pw) + pp, :, t * pw:(t + 1) * pw] = xt[:, pp * pw:(pp + 1) * pw].astype(u_ref.dtype)


def _inproj(x, mod, g1, w_in_p, gains, seg64, seg32, rope, kv_dtype):
    bsz, seq, d = x.shape
    latent = rope is not None
    tm = next(t for t in (1024, 512, 256) if seq % t == 0)
    bm = mod.shape[0]
    mod_idx = (lambda b, i: (b, 0, 0)) if bm > 1 else (lambda b, i: (0, 0, 0))
    const2 = lambda b, i: (0, 0)
    tok = lambda w: pl.BlockSpec((1, tm, w), lambda b, i: (b, i, 0))
    in_specs = [tok(d),
                pl.BlockSpec((1, 1, 6 * d), mod_idx),
                pl.BlockSpec((1, d), const2),
                pl.BlockSpec((d, IN_COLS_P), const2),
                pl.BlockSpec((4, LANES), const2),
                pl.BlockSpec((LANES, LANES), const2),
                pl.BlockSpec((LANES, LANES), const2)]
    args = [x, mod, g1, w_in_p, gains, seg64, seg32]
    if latent:
        in_specs += [pl.BlockSpec((tm, LANES), lambda b, i: (i, 0))] * 4
        args += list(rope)
    widths = (QA_COLS, KV_A * HD_A, KV_A * HD_A, W_B, W_B, W_B)
    dtypes = (BF16, kv_dtype, kv_dtype, BF16, kv_dtype, kv_dtype)
    nt = seq // tm
    rows = tm // SSM_T
    u_spec = pl.BlockSpec((N_PAIR, rows, SSM_ROW), lambda b, i: (0, b * nt + i, 0))
    u_shape = jax.ShapeDtypeStruct((N_PAIR, bsz * seq // SSM_T, SSM_ROW), BF16)
    return pl.pallas_call(
        functools.partial(_inproj_body, latent=latent),
        grid=(bsz, nt),
        in_specs=in_specs,
        out_specs=[tok(w) for w in widths] + [u_spec],
        out_shape=[jax.ShapeDtypeStruct((bsz, seq, w), dt) for w, dt in zip(widths, dtypes)] + [u_shape],
        scratch_shapes=[pltpu.VMEM((W_C // LANES, tm, LANES), F32)],
        compiler_params=_cparams("parallel", "parallel"),
        name="inproj_latent" if latent else "inproj_ctx",
    )(*args)


def _attn_a_body(sink_ref, q_ref, *refs, latent, nblk, nstep):
    o_ref = refs[-1]
    nk = (len(refs) - 1) // 2
    ks = [r[0].astype(BF16) for r in refs[:nk]]
    vs = [r[0].astype(BF16) for r in refs[nk:2 * nk]]
    rows = GQ_A * BLOCK
    rowi = lax.broadcasted_iota(jnp.int32, (rows, 1), 0)
    lane = lax.broadcasted_iota(jnp.int32, (BLOCK, LANES), 1)
    if latent:
        cols = 3 * BLOCK + ks[3].shape[0]
        r = lax.broadcasted_iota(jnp.int32, (rows, cols), 0) & (BLOCK - 1)
        c = lax.broadcasted_iota(jnp.int32, (rows, cols), 1)
        own_k = [ks[1][t * BLOCK:(t + 1) * BLOCK] for t in range(nstep)]
        own_v = [vs[1][t * BLOCK:(t + 1) * BLOCK] for t in range(nstep)]
        band_k = [ks[0]] + own_k + [ks[2]]
        band_v = [vs[0]] + own_v + [vs[2]]
    for t in range(nstep):
        qrows = slice(t * BLOCK, (t + 1) * BLOCK)
        if latent:
            kcat = jnp.concatenate(band_k[t:t + 3] + [ks[3]], axis=0)
            vcat = jnp.concatenate(band_v[t:t + 3] + [vs[3]], axis=0)
            qblk = pl.program_id(1) * nstep + t
            p_off = jnp.where(qblk > 0, 0, 2 * BLOCK)
            n_off = jnp.where(qblk < nblk - 1, 0, 2 * BLOCK)
            prev_ok = (c >= r + p_off) | (c >= BLOCK)
            next_ok = ((c - 2 * BLOCK + n_off) <= r) | (c < 2 * BLOCK) | (c >= 3 * BLOCK)
            valid = prev_ok & next_ok
        else:
            kcat, vcat = ks[0], vs[0]
        heads = []
        for j in range(KV_A):
            q3 = jnp.concatenate([q_ref[0, qrows, (GQ_A * j + g) * LANES:(GQ_A * j + g + 1) * LANES]
                                  for g in range(GQ_A)], axis=0)
            s = _dot_nt(q3, kcat)
            if latent:
                s = jnp.where(valid, s, NEG)
            sink = jnp.where(rowi < BLOCK, sink_ref[GQ_A * j],
                             jnp.where(rowi < 2 * BLOCK, sink_ref[GQ_A * j + 1], sink_ref[GQ_A * j + 2]))
            m = jnp.maximum(jnp.max(s, axis=-1, keepdims=True), sink)
            e = jnp.exp(s - m)
            den = jnp.sum(e, axis=-1, keepdims=True) + jnp.exp(sink - m)
            o = _dot(e.astype(BF16), vcat) / den
            for g in range(GQ_A):
                heads.append((j, o[g * BLOCK:(g + 1) * BLOCK]))
        for blk in range(H_A // 2):
            (j0, o0), (j1, o1) = heads[2 * blk], heads[2 * blk + 1]
            lo = o0 if j0 == 0 else pltpu.roll(o0, HD_A, 1)
            hi = o1 if j1 == 1 else pltpu.roll(o1, HD_A, 1)
            o_ref[0, qrows, blk * LANES:(blk + 1) * LANES] = jnp.where(lane < HD_A, lo, hi).astype(o_ref.dtype)


def _attn_a(qa, ka, va, sink, ctx_kv):
    bsz, seq, _ = qa.shape
    nblk = seq // BLOCK
    latent = ctx_kv is not None
    kvw = KV_A * HD_A
    nb = A_STEP_BLOCKS if nblk % A_STEP_BLOCKS == 0 else nblk
    if latent:
        past = ctx_kv[0].shape[1]
        band = [pl.BlockSpec((1, BLOCK, kvw), lambda b, i: (b, jnp.maximum(nb * i - 1, 0), 0)),
                pl.BlockSpec((1, nb * BLOCK, kvw), lambda b, i: (b, i, 0)),
                pl.BlockSpec((1, BLOCK, kvw), lambda b, i: (b, jnp.minimum(nb * i + nb, nblk - 1), 0)),
                pl.BlockSpec((1, past, kvw), lambda b, i: (b, 0, 0))]
        kv_specs = band + band
        kv_args = [ka, ka, ka, ctx_kv[0], va, va, va, ctx_kv[1]]
    else:
        kv_specs = [pl.BlockSpec((1, seq, kvw), lambda b, i: (b, 0, 0))] * 2
        kv_args = [ka, va]
    return pl.pallas_call(
        functools.partial(_attn_a_body, latent=latent, nblk=nblk, nstep=nb),
        grid=(bsz, nblk // nb),
        in_specs=[pl.BlockSpec(memory_space=pltpu.SMEM),
                  pl.BlockSpec((1, nb * BLOCK, QA_COLS), lambda b, i: (b, i, 0))] + kv_specs,
        out_specs=pl.BlockSpec((1, nb * BLOCK, W_A), lambda b, i: (b, i, 0)),
        out_shape=jax.ShapeDtypeStruct((bsz, seq, W_A), BF16),
        compiler_params=_cparams("parallel", "parallel"),
        name="attn_a_latent" if latent else "attn_a_ctx",
    )(sink, qa, *kv_args)


def _attn_b_body(lam_ref, gain_ref, q_ref, *refs, part_lens, lam_init, kc):
    npart = len(part_lens)
    k_refs, v_refs = refs[:npart], refs[npart:2 * npart]
    o_ref, s_scr, vm_scr = refs[2 * npart:]
    tq = q_ref.shape[1]
    chunks = []
    col = 0
    for p, plen in enumerate(part_lens):
        step = min(kc, plen)
        for start in range(0, plen, step):
            chunks.append((p, start, col, step))
            col += step

    @pl.when(pl.program_id(2) == 0)
    def _():
        off = 0
        for p, plen in enumerate(part_lens):
            v = v_refs[p][0].astype(BF16)
            lane_v = lax.broadcasted_iota(jnp.int32, (plen, LANES), 1)
            for h in range(2):
                own = (lane_v >= h * HD_B) & (lane_v < (h + 1) * HD_B)
                ones = jnp.where(lane_v == (1 - h) * HD_B, 1.0, 0.0).astype(BF16)
                vm_scr[h, off:off + plen, :] = jnp.where(own, v, ones)
            off += plen

    lv = lam_ref[...]
    lam = (jnp.exp(jnp.sum(lv[0:1] * lv[1:2], axis=-1, keepdims=True))
           - jnp.exp(jnp.sum(lv[2:3] * lv[3:4], axis=-1, keepdims=True)) + lam_init)
    q = q_ref[0]
    lane_q = lax.broadcasted_iota(jnp.int32, (tq, LANES), 1)
    total = jnp.zeros((tq, LANES), F32)
    for h in range(2):
        qc = [jnp.where((lane_q >= h * HD_B + c * DC_B) & (lane_q < h * HD_B + (c + 1) * DC_B), q, jnp.zeros_like(q))
              for c in range(2)]
        rows = [slice(c * tq, (c + 1) * tq) for c in range(2)]
        macc = [None, None]
        for p, start, col, step in chunks:
            kch = k_refs[p][0, start:start + step, :].astype(BF16)
            for c in range(2):
                s = _dot_nt(qc[c], kch)
                s_scr[rows[c], col:col + step] = s
                for j in range(step // LANES):
                    t = s[:, j * LANES:(j + 1) * LANES]
                    macc[c] = t if macc[c] is None else jnp.maximum(macc[c], t)
        m = [jnp.max(macc[c], axis=-1, keepdims=True) for c in range(2)]
        acc = [jnp.zeros((tq, LANES), F32) for _ in range(2)]
        for p, start, col, step in chunks:
            vch = vm_scr[h, col:col + step, :]
            for c in range(2):
                e = jnp.exp2(s_scr[rows[c], col:col + step] - m[c]).astype(BF16)
                acc[c] = acc[c] + _dot(e, vch)
        o2 = [acc[c] / jnp.sum(jnp.where(lane_q == (1 - h) * HD_B, acc[c], 0.0), axis=-1, keepdims=True)
              for c in range(2)]
        own = (lane_q >= h * HD_B) & (lane_q < (h + 1) * HD_B)
        total = total + jnp.where(own, o2[0] - lam * o2[1], 0.0)
    sq = total * total
    ss_lo = jnp.sum(jnp.where(lane_q < HD_B, sq, 0.0), axis=-1, keepdims=True)
    ss_hi = jnp.sum(jnp.where(lane_q >= HD_B, sq, 0.0), axis=-1, keepdims=True)
    rinv = jnp.where(lane_q < HD_B, lax.rsqrt(ss_lo * (1.0 / HD_B) + EPS), lax.rsqrt(ss_hi * (1.0 / HD_B) + EPS))
    o_ref[0] = (total * rinv * gain_ref[...] * (1.0 - lam_init)).astype(o_ref.dtype)


def _attn_b(qb, k_parts, v_parts, lam_b, gain, lam_init):
    bsz, seq, _ = qb.shape
    tq = 512 if seq % 512 == 0 else 256
    part_lens = tuple(k.shape[1] for k in k_parts)
    lk = sum(part_lens)
    kv_specs = [pl.BlockSpec((1, n, LANES), lambda b, hp, i: (b, 0, hp)) for n in part_lens]
    return pl.pallas_call(
        functools.partial(_attn_b_body, part_lens=part_lens, lam_init=lam_init, kc=512),
        grid=(bsz, W_B // LANES, seq // tq),
        in_specs=[pl.BlockSpec((4, DC_B), lambda b, hp, i: (0, 0)),
                  pl.BlockSpec((1, LANES), lambda b, hp, i: (0, 0)),
                  pl.BlockSpec((1, tq, LANES), lambda b, hp, i: (b, i, hp))] + kv_specs + kv_specs,
        out_specs=pl.BlockSpec((1, tq, LANES), lambda b, hp, i: (b, i, hp)),
        out_shape=jax.ShapeDtypeStruct((bsz, seq, W_B), BF16),
        scratch_shapes=[pltpu.VMEM((2 * tq, lk), F32), pltpu.VMEM((2, lk, LANES), BF16)],
        compiler_params=_cparams("parallel", "parallel", "arbitrary"),
        name="attn_b_latent" if len(k_parts) > 1 else "attn_b_ctx",
    )(lam_b, gain, qb, *k_parts, *v_parts)


def _ssm_body(u_ref, m_ref, g_ref, cc_ref, a_ref, h0_ref, y_ref, fin_ref, s_scr, h_scr, *, nb, nc):
    u = u_ref[0]
    col = lambda k: slice(k * LANES, (k + 1) * LANES)
    s = _dot(u, g_ref[0])
    for k in range(4):
        s_scr[k] = s[:, col(k)]
    a = a_ref[0]
    afr, afi, abr, abi = (jnp.broadcast_to(a[k:k + 1], (nb, LANES)) for k in range(4))
    h0 = h0_ref[0]

    def step(c, carry):
        fr, fi, br, bi = carry
        rf = pl.ds(c, nb, stride=nc)
        rb = pl.ds(nc - 1 - c, nb, stride=nc)
        h_scr[0, rf, :] = fr
        h_scr[1, rf, :] = fi
        h_scr[2, rb, :] = br
        h_scr[3, rb, :] = bi
        nfr = afr * fr - afi * fi + s_scr[0, rf, :]
        nfi = afr * fi + afi * fr + s_scr[1, rf, :]
        nbr = abr * br - abi * bi + s_scr[2, rb, :]
        nbi = abr * bi + abi * br + s_scr[3, rb, :]
        return nfr, nfi, nbr, nbi

    fin = lax.fori_loop(0, nc, step, tuple(h0[:, col(k)] for k in range(4)))
    for k in range(4):
        fin_ref[0, :, col(k)] = fin[k]
    hin = jnp.concatenate([h_scr[k] for k in range(4)], axis=1).astype(BF16)
    y = _dot(u, m_ref[0]) + _dot(hin, cc_ref[0])
    y_ref[0] = y.astype(y_ref.dtype)


def _ssm(u_rows, mats, h0, nb):
    npair, rows, w = u_rows.shape
    nc = rows // nb
    mat_spec = pl.BlockSpec((1, w, w), lambda p: (p, 0, 0))
    return pl.pallas_call(
        functools.partial(_ssm_body, nb=nb, nc=nc),
        grid=(npair,),
        in_specs=[pl.BlockSpec((1, rows, w), lambda p: (p, 0, 0)), mat_spec, mat_spec, mat_spec,
                  pl.BlockSpec((1, 4, LANES), lambda p: (p, 0, 0)),
                  pl.BlockSpec((1, nb, w), lambda p: (p, 0, 0))],
        out_specs=[pl.BlockSpec((1, rows, w), lambda p: (p, 0, 0)),
                   pl.BlockSpec((1, nb, w), lambda p: (p, 0, 0))],
        out_shape=[jax.ShapeDtypeStruct((npair, rows, w), BF16),
                   jax.ShapeDtypeStruct((npair, nb, w), F32)],
        scratch_shapes=[pltpu.VMEM((4, rows, LANES), F32), pltpu.VMEM((4, rows, LANES), F32)],
        compiler_params=_cparams("parallel"),
        name="ssm_scan",
    )(u_rows, mats["m"], mats["g"], mats["cc"], mats["a16"], h0)


def _ssm_matrices(lp):
    t = SSM_T
    ks = jnp.arange(t + 1, dtype=F32)
    dirs = []
    for d in range(2):
        lam = lax.complex(lp["ssm_lam_re"][d].astype(F32), lp["ssm_lam_im"][d].astype(F32))
        dt = jnp.exp(lp["ssm_log_dt"][d].astype(F32))[:, None]
        a_bar = jnp.exp(lam * dt)
        b_bar = ((a_bar - 1.0) / lam)[..., None] * lax.complex(lp["ssm_b_re"][d].astype(F32),
                                                               lp["ssm_b_im"][d].astype(F32))
        c_mat = lax.complex(lp["ssm_c_re"][d].astype(F32), lp["ssm_c_im"][d].astype(F32))
        pw = jnp.exp((lam * dt)[None] * ks[:, None, None].astype(jnp.complex64))
        kern = jnp.real(jnp.einsum("gop,kgp,gpi->gkoi", c_mat, pw[:t], b_bar))
        dirs.append((pw, b_bar, c_mat, kern))
    (pw_f, bb_f, cm_f, k_f), (pw_b, bb_b, cm_b, k_b) = dirs
    eye2 = jnp.eye(2, dtype=F32)
    ch, pw2 = SSM_CH, 2 * SSM_CH
    hi = lax.Precision.HIGHEST

    def pair_bd(x):
        r, c = x.shape[1:]
        return jnp.einsum("pgrc,gh->pgrhc", x.reshape(N_PAIR, 2, r, c), eye2.astype(x.dtype)).reshape(N_PAIR, 2 * r, 2 * c)

    def pair_vec(x):
        return x.reshape(x.shape[0], N_PAIR, 2 * P_C).transpose(1, 0, 2)

    def lag_blocks(kern):
        x = kern.transpose(0, 1, 3, 2).reshape(N_PAIR, 2, t, ch, ch)
        return jnp.einsum("pglic,gh->plgihc", x, eye2).reshape(N_PAIR, t, pw2, pw2)
    kp_f, kp_b = lag_blocks(k_f), lag_blocks(k_b)
    d_blk = pair_bd(lp["ssm_d"].astype(F32)[:, :, None] * jnp.eye(ch, dtype=F32)[None])
    center = (kp_f[:, 0] + kp_b[:, 0] + d_blk)[:, None]
    band = jnp.concatenate([kp_b[:, :0:-1], center, kp_f[:, 1:]], axis=1)
    band = band.transpose(0, 2, 1, 3).reshape(N_PAIR, pw2, (2 * t - 1) * pw2)
    m_p = jnp.concatenate([band[:, :, (t - 1 - s) * pw2:(t - 1 - s) * pw2 + SSM_ROW] for s in range(t)], axis=1)

    def inject(pw_sel, b_bar):
        x1 = jnp.repeat(pair_vec(pw_sel), pw2, axis=1)
        x2 = jnp.tile(pair_bd(b_bar.transpose(0, 2, 1)), (1, t, 1))
        return x1 * x2
    g_f = inject(pw_f[t - 1 - jnp.arange(t)], bb_f)
    g_b = inject(pw_b[jnp.arange(t)], bb_b)
    g_p = jnp.concatenate([jnp.real(g_f), jnp.imag(g_f), jnp.real(g_b), jnp.imag(g_b)], axis=2)

    lane = jnp.arange(SSM_ROW)
    exp_t = (jnp.arange(t)[:, None] == lane[None, :] // pw2).astype(F32)
    exp_c = (jnp.arange(pw2)[:, None] == lane[None, :] % pw2).astype(F32)

    def widen(x, e):
        f = lambda v: jnp.einsum("pqk,kx->pqx", v, e, precision=hi)
        return lax.complex(f(jnp.real(x)), f(jnp.imag(x)))

    def readout(pw_sel, c_mat):
        y1 = widen(pair_vec(pw_sel).transpose(0, 2, 1), exp_t)
        y2 = widen(pair_bd(c_mat.transpose(0, 2, 1)), exp_c)
        return y1 * y2
    z_f = readout(pw_f[1 + jnp.arange(t)], cm_f)
    z_b = readout(pw_b[t - jnp.arange(t)], cm_b)
    cc_p = jnp.concatenate([jnp.real(z_f), -jnp.imag(z_f), jnp.real(z_b), -jnp.imag(z_b)], axis=1)
    a16 = jnp.stack([jnp.real(pw_f[t]), jnp.imag(pw_f[t]), jnp.real(pw_b[t]), jnp.imag(pw_b[t])], axis=0)
    a16 = a16.reshape(4, N_PAIR, 2 * P_C).transpose(1, 0, 2)
    return dict(m=m_p.astype(BF16), g=g_p.astype(BF16), cc=cc_p.astype(BF16), a16=a16)


def _ssm_state_rows(s_re, s_im):
    bsz = s_re.shape[0]
    parts = [s_re[:, 0], s_im[:, 0], s_re[:, 1], s_im[:, 1]]
    st = jnp.stack([p.reshape(bsz, N_PAIR, 2 * P_C) for p in parts], axis=2)
    return st.transpose(1, 0, 2, 3).reshape(N_PAIR, bsz, 8 * P_C).astype(F32)


def _ssm_state_unrows(fin):
    npair, bsz, _ = fin.shape
    st = fin.reshape(npair, bsz, 4, 2, P_C).transpose(1, 2, 0, 3, 4).reshape(bsz, 4, G_C, P_C)
    return jnp.stack([st[:, 0], st[:, 2]], axis=1), jnp.stack([st[:, 1], st[:, 3]], axis=1)


def _route(scores, bias):
    tm = scores.shape[1]
    biased = scores + bias
    iota8 = lax.broadcasted_iota(jnp.int32, (PER_GROUP, tm), 0)
    grp = [biased[PER_GROUP * g:PER_GROUP * (g + 1)] for g in range(N_EXP_GROUPS)]
    gscore = []
    for v in grp:
        m1 = jnp.max(v, axis=0, keepdims=True)
        first = jnp.min(jnp.where(v == m1, iota8, PER_GROUP), axis=0, keepdims=True)
        m2 = jnp.max(jnp.where(iota8 == first, -jnp.inf, v), axis=0, keepdims=True)
        gscore.append(m1 + m2)
    masked = []
    for g in range(N_EXP_GROUPS):
        rank = jnp.zeros((1, tm), jnp.int32)
        for o in range(N_EXP_GROUPS):
            if o == g:
                continue
            ahead = (gscore[o] >= gscore[g]) if o < g else (gscore[o] > gscore[g])
            rank = rank + jnp.where(ahead, 1, 0)
        masked.append(jnp.where(rank < TOPK_GROUPS, grp[g], -jnp.inf))
    chosen = [None] * N_EXP_GROUPS
    for _ in range(TOP_K):
        best = masked[0]
        for v in masked[1:]:
            best = jnp.maximum(best, v)
        best = jnp.max(best, axis=0, keepdims=True)
        first = jnp.full((1, tm), N_EXPERTS, jnp.int32)
        for g, v in enumerate(masked):
            cand = jnp.min(jnp.where(v == best, iota8 + PER_GROUP * g, N_EXPERTS), axis=0, keepdims=True)
            first = jnp.minimum(first, cand)
        for g in range(N_EXP_GROUPS):
            hit = (iota8 + PER_GROUP * g) == first
            chosen[g] = hit if chosen[g] is None else (chosen[g] | hit)
            masked[g] = jnp.where(hit, -jnp.inf, masked[g])
    w = [jnp.where(chosen[g], scores[PER_GROUP * g:PER_GROUP * (g + 1)], 0.0) for g in range(N_EXP_GROUPS)]
    wsum = w[0]
    for v in w[1:]:
        wsum = wsum + v
    wsum = jnp.sum(wsum, axis=0, keepdims=True)
    return jnp.concatenate([v / wsum * ROUTED_SCALE for v in w], axis=0)


def _post_body(x_ref, oa_ref, ob_ref, y_ref, mod_ref, wglu_ref, wout_ref, g2_ref, wrh_ref, wrl_ref, br_ref,
               x1_ref, h2_ref, gate_ref, y_scr):
    d = D_MODEL
    tm = x_ref.shape[1]
    pw = 2 * SSM_CH
    for t in range(SSM_T):
        for blk in range(W_C // LANES):
            piece = jnp.concatenate([y_ref[blk * (LANES // pw) + pp, :, t * pw:(t + 1) * pw].astype(F32)
                                     for pp in range(LANES // pw)], axis=1)
            y_scr[blk, pl.ds(t, tm // SSM_T, stride=SSM_T), :] = piece
    g = jax.nn.gelu(jnp.concatenate([y_scr[blk] for blk in range(W_C // LANES)], axis=1))
    oc = g * jax.nn.sigmoid(_dot(g.astype(BF16), wglu_ref[...]))
    mix = (_dot(oa_ref[0], wout_ref[0:W_A]) + _dot(ob_ref[0], wout_ref[W_A:W_A + W_B])
           + _dot(oc.astype(BF16), wout_ref[W_A + W_B:]))
    mod = mod_ref[0]
    x1 = x_ref[0] + mod[:, 2 * d:3 * d] * mix
    x1_ref[0] = x1
    xn = x1 * lax.rsqrt(jnp.mean(x1 * x1, axis=-1, keepdims=True) + EPS) * g2_ref[...]
    h2 = xn * (1.0 + mod[:, 4 * d:5 * d]) + mod[:, 3 * d:4 * d]
    h_hi, h_lo = _split_bf16(h2)
    h2_ref[0] = h_hi
    logits = _dot_nt(wrh_ref[...], h_hi) + _dot_nt(wrh_ref[...], h_lo) + _dot_nt(wrl_ref[...], h_hi)
    gate_ref[0] = _route(jax.nn.sigmoid(logits), br_ref[...]).T


def _post_mix(x, oa, ob, y, mod, w_glu, w_out, g2, wr_hi, wr_lo, b_r):
    bsz, seq, d = x.shape
    tm = 512 if seq % 512 == 0 else 256
    bm = mod.shape[0]
    mod_idx = (lambda b, i: (b, 0, 0)) if bm > 1 else (lambda b, i: (0, 0, 0))
    const2 = lambda b, i: (0, 0)
    tok = lambda w: pl.BlockSpec((1, tm, w), lambda b, i: (b, i, 0))
    nt = seq // tm
    return pl.pallas_call(
        _post_body,
        grid=(bsz, nt),
        in_specs=[tok(d), tok(W_A), tok(W_B),
                  pl.BlockSpec((N_PAIR, tm // SSM_T, SSM_ROW), lambda b, i: (0, b * nt + i, 0)),
                  pl.BlockSpec((1, 1, 6 * d), mod_idx),
                  pl.BlockSpec((W_C, W_C), const2),
                  pl.BlockSpec((d, d), const2),
                  pl.BlockSpec((1, d), const2),
                  pl.BlockSpec((N_EXPERTS, d), const2),
                  pl.BlockSpec((N_EXPERTS, d), const2),
                  pl.BlockSpec((N_EXPERTS, 1), const2)],
        out_specs=[tok(d), tok(d), tok(N_EXPERTS)],
        out_shape=[jax.ShapeDtypeStruct((bsz, seq, d), F32),
                   jax.ShapeDtypeStruct((bsz, seq, d), BF16),
                   jax.ShapeDtypeStruct((bsz, seq, N_EXPERTS), F32)],
        scratch_shapes=[pltpu.VMEM((W_C // LANES, tm, LANES), F32)],
        compiler_params=_cparams("parallel", "parallel"),
        name="post_mix",
    )(x, oa, ob, y, mod, w_glu, w_out, g2, wr_hi, wr_lo, b_r)


def _moe_body(x1_ref, h_ref, gate_ref, g2_ref, w1_ref, w3_ref, w2_ref, c1_ref, c32_ref, ex_ref,
              s1_ref, s3_ref, s2_ref, o_ref, acc_ref, h8_ref, hs_ref):
    j = pl.program_id(1)

    @pl.when(j == 0)
    def _():
        h = h_ref[...]
        a = _dot(h, s1_ref[...])
        acc_ref[...] = _dot((a * jax.nn.sigmoid(a) * _dot(h, s3_ref[...])).astype(BF16), s2_ref[...])
        hf = h.astype(F32)
        sc = jnp.maximum(jnp.max(jnp.abs(hf), axis=-1, keepdims=True), F8_TINY) * (1.0 / F8_RANGE)
        hs_ref[...] = sc
        h8_ref[...] = (hf * (1.0 / sc)).astype(F8)

    ne = w1_ref.shape[0]
    h8 = h8_ref[...]
    hs = hs_ref[...]
    a = _dot(h8, jnp.concatenate([w1_ref[e] for e in range(ne)], axis=1)) * c1_ref[...] * hs
    b = _dot(h8, jnp.concatenate([w3_ref[e] for e in range(ne)], axis=1))
    gexp = _dot(jnp.concatenate(_split_bf16(gate_ref[...]), axis=1), ex_ref[...])
    hid = a * jax.nn.sigmoid(a) * b * gexp * c32_ref[...]
    sc = jnp.maximum(jnp.max(jnp.abs(hid), axis=-1, keepdims=True), F8_TINY) * (1.0 / F8_RANGE)
    acc_ref[...] += _dot((hid * (1.0 / sc)).astype(F8), w2_ref[...]) * (sc * hs)

    @pl.when(j == pl.num_programs(1) - 1)
    def _():
        o_ref[...] = x1_ref[...] + g2_ref[0] * acc_ref[...]


def _moe(x1, h2, gates, mod, seq, ew, expand, ws1, ws3, ws2):
    tokens, d = x1.shape
    bm = mod.shape[0]
    span = seq if bm > 1 else tokens
    tm = next(t for t in (1024, 512, 256) if span % t == 0)
    per_b = seq // tm if bm > 1 else 1
    mod_idx = (lambda i, j: (i // per_b, 0, 5)) if bm > 1 else (lambda i, j: (0, 0, 5))
    ne = 8
    fc = ne * F_EXP
    hidden = ew["w2"].shape[0]
    const2 = lambda i, j: (0, 0)
    chunk_row = pl.BlockSpec((1, fc), lambda i, j: (0, j))
    return pl.pallas_call(
        _moe_body,
        grid=(tokens // tm, hidden // fc),
        in_specs=[pl.BlockSpec((tm, d), lambda i, j: (i, 0)),
                  pl.BlockSpec((tm, d), lambda i, j: (i, 0)),
                  pl.BlockSpec((tm, N_EXPERTS), lambda i, j: (i, 0)),
                  pl.BlockSpec((1, 1, d), mod_idx),
                  pl.BlockSpec((ne, d, F_EXP), lambda i, j: (j, 0, 0)),
                  pl.BlockSpec((ne, d, F_EXP), lambda i, j: (j, 0, 0)),
                  pl.BlockSpec((fc, d), lambda i, j: (j, 0)),
                  chunk_row, chunk_row,
                  pl.BlockSpec((2 * N_EXPERTS, fc), lambda i, j: (0, j)),
                  pl.BlockSpec((d, F_SHARED), const2),
                  pl.BlockSpec((d, F_SHARED), const2),
                  pl.BlockSpec((F_SHARED, d), const2)],
        out_specs=pl.BlockSpec((tm, d), lambda i, j: (i, 0)),
        out_shape=jax.ShapeDtypeStruct((tokens, d), F32),
        scratch_shapes=[pltpu.VMEM((tm, d), F32), pltpu.VMEM((tm, d), F8), pltpu.VMEM((tm, 1), F32)],
        compiler_params=_cparams("parallel", "arbitrary"),
        name="moe",
    )(x1, h2, gates, mod, ew["w1"], ew["w3"], ew["w2"], ew["c1"], ew["c32"], expand, ws1, ws3, ws2)


def _expert_fp8(w):
    sc = jnp.maximum(jnp.max(jnp.abs(w), axis=(1, 2)), F8_TINY) * (1.0 / F8_RANGE)
    return (w / sc[:, None, None]).astype(F8), jnp.repeat(sc, F_EXP)[None, :].astype(F32)


def _prep_experts(p):
    w1, c1 = _expert_fp8(p["w_e1"].astype(F32))
    w3, c3 = _expert_fp8(p["w_e3"].astype(F32))
    w2, c2 = _expert_fp8(p["w_e2"].astype(F32))
    return dict(w1=w1, w3=w3, w2=w2.reshape(N_EXPERTS * F_EXP, D_MODEL), c1=c1, c32=c3 * c2)


def _rope_tables(seq):
    pos = jnp.arange(seq)
    row = (pos // GRID_W).astype(F32)[:, None]
    colp = (pos % GRID_W).astype(F32)[:, None]
    lane = jnp.arange(LANES)

    def table(width):
        half, quarter = width // 2, width // 4
        i = lane % width
        freq = ROPE_BASE ** (-(2.0 * (i % quarter).astype(F32)) / half)
        ang = jnp.where((i // half) == 0, row, colp) * freq[None, :]
        sign = jnp.where((i % half) < quarter, -1.0, 1.0)
        return jnp.cos(ang), jnp.sin(ang) * sign[None, :]

    ca, sa = table(HD_A)
    cb, sb = table(DC_B)
    return ca, sa, cb, sb


def _prep_layer(p):
    d = D_MODEL
    w_in = p["w_in"]
    place = (jnp.arange(H_A)[:, None] // GQ_A == jnp.arange(KV_A)[None, :]).astype(w_in.dtype)
    qa_pad = w_in[:, :W_A].reshape(d, H_A, 1, HD_A) * place[None, :, :, None]
    w_in_p = jnp.concatenate([qa_pad.reshape(d, QA_COLS), w_in[:, W_A:]], axis=1).astype(BF16)
    gains = jnp.stack([jnp.tile(p["q_norm_a"], LANES // HD_A) * (HD_A ** -0.5),
                       jnp.tile(p["k_norm_a"], LANES // HD_A),
                       jnp.tile(p["q_norm_b"], LANES // DC_B) * (DC_B ** -0.5 * LOG2E),
                       jnp.tile(p["k_norm_b"], LANES // DC_B)], axis=0).astype(F32)
    lp = {k: p[k] for k in ("ssm_lam_re", "ssm_lam_im", "ssm_log_dt", "ssm_b_re", "ssm_b_im",
                            "ssm_c_re", "ssm_c_im", "ssm_d")}
    wr_hi, wr_lo = _split_bf16(p["w_router"].T.astype(F32))
    return dict(
        w_in_p=w_in_p, gains=gains,
        g1=p["norm1_g"].reshape(1, d).astype(F32), g2=p["norm2_g"].reshape(1, d).astype(F32),
        sink=p["sink_a"].astype(F32), lam_b=p["lam_b"].astype(F32),
        subln=jnp.tile(p["subln_b"], LANES // HD_B).reshape(1, LANES).astype(F32),
        ssm=_ssm_matrices(lp),
        w_glu=p["w_glu"].astype(BF16), w_out=p["w_out"].astype(BF16),
        wr_hi=wr_hi, wr_lo=wr_lo, b_r=p["b_router"].reshape(N_EXPERTS, 1).astype(F32),
        experts=_prep_experts(p),
        ws1=p["w_s1"].astype(BF16), ws3=p["w_s3"].astype(BF16), ws2=p["w_s2"].astype(BF16),
    )


def _trunk_layer(x, mod, lw, consts, ctx):
    bsz, seq, d = x.shape
    latent = ctx is not None
    rope = consts["rope"] if latent else None
    kv_dtype = BF16 if latent else F32
    qa, ka, va, qb, kb, vb, u = _inproj(x, mod, lw["g1"], lw["w_in_p"], lw["gains"],
                                        consts["seg64"], consts["seg32"], rope, kv_dtype)
    if latent:
        oa = _attn_a(qa, ka, va, lw["sink"], (ctx["ak"], ctx["av"]))
        ob = _attn_b(qb, [kb, ctx["bk"]], [vb, ctx["bv"]], lw["lam_b"], lw["subln"], lw["lam_init"])
        h0 = ctx["h0"]
    else:
        oa = _attn_a(qa, ka, va, lw["sink"], None)
        ob = _attn_b(qb, [kb], [vb], lw["lam_b"], lw["subln"], lw["lam_init"])
        h0 = jnp.zeros((N_PAIR, bsz, 8 * P_C), F32)
    y_rows, fin = _ssm(u, lw["ssm"], h0, bsz)
    x1, h2, gates = _post_mix(x, oa, ob, y_rows, mod, lw["w_glu"], lw["w_out"], lw["g2"],
                                lw["wr_hi"], lw["wr_lo"], lw["b_r"])
    out = _moe(x1.reshape(bsz * seq, d), h2.reshape(bsz * seq, d), gates.reshape(bsz * seq, N_EXPERTS), mod, seq,
               lw["experts"], consts["expand"], lw["ws1"], lw["ws3"], lw["ws2"])
    return out.reshape(bsz, seq, d), (ka, va, kb, vb, fin)


def kernel(x_prompt, x_sample, cache_a_k, cache_a_v, cache_b_k, cache_b_v, state_ssm_re, state_ssm_im, c, c_ctx, norm1_g, norm2_g, w_ada, b_ada, w_in, q_norm_a, k_norm_a, sink_a, q_norm_b, k_norm_b, lam_b, subln_b, ssm_lam_re, ssm_lam_im, ssm_log_dt, ssm_b_re, ssm_b_im, ssm_c_re, ssm_c_im, ssm_d, w_glu, w_out, w_router, b_router, w_e1, w_e3, w_e2, w_s1, w_s3, w_s2):
    p = dict(norm1_g=norm1_g, norm2_g=norm2_g, w_in=w_in, q_norm_a=q_norm_a, k_norm_a=k_norm_a, sink_a=sink_a,
             q_norm_b=q_norm_b, k_norm_b=k_norm_b, lam_b=lam_b, subln_b=subln_b,
             ssm_lam_re=ssm_lam_re, ssm_lam_im=ssm_lam_im, ssm_log_dt=ssm_log_dt, ssm_b_re=ssm_b_re,
             ssm_b_im=ssm_b_im, ssm_c_re=ssm_c_re, ssm_c_im=ssm_c_im, ssm_d=ssm_d, w_glu=w_glu, w_out=w_out,
             w_router=w_router, b_router=b_router, w_e1=w_e1, w_e3=w_e3, w_e2=w_e2,
             w_s1=w_s1, w_s3=w_s3, w_s2=w_s2)
    depth = w_in.shape[0]
    bsz, seq, d = x_prompt.shape
    dbsz, dseq, _ = x_sample.shape
    past = cache_a_k.shape[3]

    mod_rows = 16
    cvec = jnp.concatenate([c.astype(F32), c_ctx.astype(F32)[None],
                            jnp.zeros((mod_rows - dbsz - 1, d), F32)], axis=0)
    mods = _modulation(cvec, w_ada.astype(F32), b_ada.astype(F32))

    lane = jnp.arange(LANES)
    hidden = N_EXPERTS * F_EXP
    consts = dict(
        rope=_rope_tables(dseq),
        seg64=(lane[:, None] // HD_A == lane[None, :] // HD_A).astype(BF16),
        seg32=(lane[:, None] // DC_B == lane[None, :] // DC_B).astype(BF16),
        expand=(jnp.arange(2 * N_EXPERTS)[:, None] % N_EXPERTS == jnp.arange(hidden)[None, :] // F_EXP).astype(BF16),
    )

    xp, xs = x_prompt, x_sample
    ak, av, bk, bv, sre, sim = [], [], [], [], [], []
    prepared = jax.vmap(_prep_layer)(p)
    for l in range(depth):
        lw = jax.tree.map(lambda v: v[l], prepared)
        lw["lam_init"] = 0.8 - 0.6 * math.exp(-0.3 * l)
        mod_lat = mods[l, :dbsz][:, None, :]
        mod_ctx = mods[l, dbsz:dbsz + 1][:, None, :]
        xp, (k_a, v_a, k_b, v_b, fin) = _trunk_layer(xp, mod_ctx, lw, consts, None)
        ak.append(k_a.reshape(bsz, seq, KV_A, HD_A).transpose(0, 2, 1, 3))
        av.append(v_a.reshape(bsz, seq, KV_A, HD_A).transpose(0, 2, 1, 3))
        bk.append(k_b.reshape(bsz, seq, H_B, 2, DC_B).transpose(0, 2, 3, 1, 4))
        bv.append(v_b.reshape(bsz, seq, H_B, HD_B).transpose(0, 2, 1, 3))
        f_re, f_im = _ssm_state_unrows(fin)
        sre.append(f_re)
        sim.append(f_im)
        ctx = dict(
            ak=cache_a_k[:, l].transpose(0, 2, 1, 3).reshape(dbsz, past, KV_A * HD_A).astype(BF16),
            av=cache_a_v[:, l].transpose(0, 2, 1, 3).reshape(dbsz, past, KV_A * HD_A).astype(BF16),
            bk=cache_b_k[:, l].transpose(0, 3, 1, 2, 4).reshape(dbsz, past, W_B).astype(BF16),
            bv=cache_b_v[:, l].transpose(0, 2, 1, 3).reshape(dbsz, past, W_B).astype(BF16),
            h0=_ssm_state_rows(state_ssm_re[:, l], state_ssm_im[:, l]),
        )
        xs, _ = _trunk_layer(xs, mod_lat, lw, consts, ctx)
    return (xp, xs, jnp.stack(ak, axis=1), jnp.stack(av, axis=1), jnp.stack(bk, axis=1),
            jnp.stack(bv, axis=1), jnp.stack(sre, axis=1), jnp.stack(sim, axis=1))
```

```python
import functools
import math

import jax
import jax.numpy as jnp
from jax import lax
from jax.experimental import pallas as pl
from jax.experimental.pallas import tpu as pltpu

F32 = jnp.float32
BF16 = jnp.bfloat16
F8 = jnp.float8_e4m3fn
F8_RANGE = 384.0
F8_TINY = 1e-30

D_MODEL = 1024
GRID_W = 64
BLOCK = 128
H_A, KV_A, HD_A = 6, 2, 64
GQ_A = H_A // KV_A
W_A = H_A * HD_A
H_B, HD_B = 4, 64
DC_B = HD_B // 2
W_B = H_B * HD_B
SSM_CH = 16
W_C = D_MODEL - W_A - W_B
G_C = W_C // SSM_CH
P_C = 64
N_EXPERTS, TOP_K, F_EXP, F_SHARED = 64, 6, 128, 256
N_EXP_GROUPS, TOPK_GROUPS = 8, 4
PER_GROUP = N_EXPERTS // N_EXP_GROUPS
ROUTED_SCALE = 2.5
ROPE_BASE = 10000.0
EPS = 1e-6
NEG = -1e30
LOG2E = 1.4426950408889634

LANES = 128
SSM_T = 16
N_PAIR = G_C // 2
SSM_ROW = 2 * SSM_T * SSM_CH
QA_COLS = H_A * LANES
IN_COLS_P = QA_COLS + 2 * KV_A * HD_A + 3 * W_B + W_C
VMEM_LIMIT = 56 << 20
A_STEP_BLOCKS = 8


def _cparams(*sem):
    return pltpu.CompilerParams(dimension_semantics=sem, vmem_limit_bytes=VMEM_LIMIT)


def _dot(a, b):
    return jnp.dot(a, b, preferred_element_type=F32)


def _dot_nt(a, b):
    return lax.dot_general(a, b, (((1,), (1,)), ((), ())), preferred_element_type=F32)


def _split_bf16(x):
    hi = x.astype(BF16)
    lo = (x - hi.astype(F32)).astype(BF16)
    return hi, lo


def _mod_body(c_ref, w_ref, b_ref, o_ref):
    c = c_ref[...]
    s = c * jax.nn.sigmoid(c)
    s_hi, s_lo = _split_bf16(s)
    w_hi, w_lo = _split_bf16(w_ref[0])
    o_ref[0] = _dot(s_hi, w_hi) + _dot(s_lo, w_hi) + _dot(s_hi, w_lo) + b_ref[0]


def _modulation(cvec, w_ada, b_ada):
    depth, d, n = w_ada.shape
    rows = cvec.shape[0]
    tn = 768
    return pl.pallas_call(
        _mod_body,
        grid=(depth, n // tn),
        in_specs=[pl.BlockSpec((rows, d), lambda l, j: (0, 0)),
                  pl.BlockSpec((1, d, tn), lambda l, j: (l, 0, j)),
                  pl.BlockSpec((1, 1, tn), lambda l, j: (l, 0, j))],
        out_specs=pl.BlockSpec((1, rows, tn), lambda l, j: (l, 0, j)),
        out_shape=jax.ShapeDtypeStruct((depth, rows, n), F32),
        compiler_params=_cparams("parallel", "parallel"),
        name="adaln_mod",
    )(cvec, w_ada, b_ada.reshape(depth, 1, n))


def _inproj_body(*refs, latent):
    if latent:
        (x_ref, mod_ref, g1_ref, w_ref, gains_ref, s64_ref, s32_ref, ca_ref, sa_ref, cb_ref, sb_ref,
         qa_ref, ka_ref, va_ref, qb_ref, kb_ref, vb_ref, u_ref, u_scr) = refs
    else:
        (x_ref, mod_ref, g1_ref, w_ref, gains_ref, s64_ref, s32_ref,
         qa_ref, ka_ref, va_ref, qb_ref, kb_ref, vb_ref, u_ref, u_scr) = refs
    d = D_MODEL
    x = x_ref[0]
    mod = mod_ref[0]
    xn = x * lax.rsqrt(jnp.mean(x * x, axis=-1, keepdims=True) + EPS) * g1_ref[...]
    h = xn * (1.0 + mod[:, d:2 * d]) + mod[:, 0:d]
    acc = _dot(h.astype(BF16), w_ref[...])

    tm = x.shape[0]
    lane = lax.broadcasted_iota(jnp.int32, (tm, LANES), 1)
    first_a = (lane % 32) < 16
    first_b = (lane % 16) < 8

    def normed(xb, seg_ref, inv_n, gain):
        ss = _dot((xb * xb).astype(BF16), seg_ref[...])
        return xb * lax.rsqrt(ss * inv_n + EPS) * gain

    def rope_a(y):
        if not latent:
            return y
        sw = jnp.where(first_a, pltpu.roll(y, LANES - 16, 1), pltpu.roll(y, 16, 1))
        return y * ca_ref[...] + sw * sa_ref[...]

    def rope_b(y):
        if not latent:
            return y
        sw = jnp.where(first_b, pltpu.roll(y, LANES - 8, 1), pltpu.roll(y, 8, 1))
        return y * cb_ref[...] + sw * sb_ref[...]

    gains = gains_ref[...]
    off = 0
    for b in range(H_A):
        y = normed(acc[:, off:off + LANES], s64_ref, 1.0 / HD_A, gains[0:1])
        qa_ref[0, :, b * LANES:(b + 1) * LANES] = rope_a(y).astype(qa_ref.dtype)
        off += LANES
    y = normed(acc[:, off:off + LANES], s64_ref, 1.0 / HD_A, gains[1:2])
    ka_ref[0] = rope_a(y).astype(ka_ref.dtype)
    off += LANES
    va_ref[0] = acc[:, off:off + LANES].astype(va_ref.dtype)
    off += LANES
    for b in range(W_B // LANES):
        y = normed(acc[:, off:off + LANES], s32_ref, 1.0 / DC_B, gains[2:3])
        qb_ref[0, :, b * LANES:(b + 1) * LANES] = rope_b(y).astype(qb_ref.dtype)
        off += LANES
    for b in range(W_B // LANES):
        y = normed(acc[:, off:off + LANES], s32_ref, 1.0 / DC_B, gains[3:4])
        kb_ref[0, :, b * LANES:(b + 1) * LANES] = rope_b(y).astype(kb_ref.dtype)
        off += LANES
    vb_ref[0] = acc[:, off:off + W_B].astype(vb_ref.dtype)
    off += W_B
    for blk in range(W_C // LANES):
        u_scr[blk] = acc[:, off + blk * LANES:off + (blk + 1) * LANES]
    pw = 2 * SSM_CH
    for t in range(SSM_T):
        for blk in range(W_C // LANES):
            xt = u_scr[blk, pl.ds(t, tm // SSM_T, stride=SSM_T), :]
            for pp in range(LANES // pw):
                u_ref[blk * (LANES // pw) + pp, :, t * pw:(t + 1) * pw] = xt[:, pp * pw:(pp + 1) * pw].astype(u_ref.dtype)


def _inproj(x, mod, g1, w_in_p, gains, seg64, seg32, rope, kv_dtype):
    bsz, seq, d = x.shape
    latent = rope is not None
    tm = next(t for t in (1024, 512, 256) if seq % t == 0)
    bm = mod.shape[0]
    mod_idx = (lambda b, i: (b, 0, 0)) if bm > 1 else (lambda b, i: (0, 0, 0))
    const2 = lambda b, i: (0, 0)
    tok = lambda w: pl.BlockSpec((1, tm, w), lambda b, i: (b, i, 0))
    in_specs = [tok(d),
                pl.BlockSpec((1, 1, 6 * d), mod_idx),
                pl.BlockSpec((1, d), const2),
                pl.BlockSpec((d, IN_COLS_P), const2),
                pl.BlockSpec((4, LANES), const2),
                pl.BlockSpec((LANES, LANES), const2),
                pl.BlockSpec((LANES, LANES), const2)]
    args = [x, mod, g1, w_in_p, gains, seg64, seg32]
    if latent:
        in_specs += [pl.BlockSpec((tm, LANES), lambda b, i: (i, 0))] * 4
        args += list(rope)
    widths = (QA_COLS, KV_A * HD_A, KV_A * HD_A, W_B, W_B, W_B)
    dtypes = (BF16, kv_dtype, kv_dtype, BF16, kv_dtype, kv_dtype)
    nt = seq // tm
    rows = tm // SSM_T
    u_spec = pl.BlockSpec((N_PAIR, rows, SSM_ROW), lambda b, i: (0, b * nt + i, 0))
    u_shape = jax.ShapeDtypeStruct((N_PAIR, bsz * seq // SSM_T, SSM_ROW), BF16)
    return pl.pallas_call(
        functools.partial(_inproj_body, latent=latent),
        grid=(bsz, nt),
        in_specs=in_specs,
        out_specs=[tok(w) for w in widths] + [u_spec],
        out_shape=[jax.ShapeDtypeStruct((bsz, seq, w), dt) for w, dt in zip(widths, dtypes)] + [u_shape],
        scratch_shapes=[pltpu.VMEM((W_C // LANES, tm, LANES), F32)],
        compiler_params=_cparams("parallel", "parallel"),
        name="inproj_latent" if latent else "inproj_ctx",
    )(*args)


def _attn_a_body(sink_ref, q_ref, *refs, latent, nblk, nstep):
    o_ref = refs[-1]
    nk = (len(refs) - 1) // 2
    ks = [r[0].astype(BF16) for r in refs[:nk]]
    vs = [r[0].astype(BF16) for r in refs[nk:2 * nk]]
    rows = GQ_A * BLOCK
    rowi = lax.broadcasted_iota(jnp.int32, (rows, 1), 0)
    lane = lax.broadcasted_iota(jnp.int32, (BLOCK, LANES), 1)
    if latent:
        cols = 3 * BLOCK + ks[3].shape[0]
        r = lax.broadcasted_iota(jnp.int32, (rows, cols), 0) & (BLOCK - 1)
        c = lax.broadcasted_iota(jnp.int32, (rows, cols), 1)
        own_k = [ks[1][t * BLOCK:(t + 1) * BLOCK] for t in range(nstep)]
        own_v = [vs[1][t * BLOCK:(t + 1) * BLOCK] for t in range(nstep)]
        band_k = [ks[0]] + own_k + [ks[2]]
        band_v = [vs[0]] + own_v + [vs[2]]
    for t in range(nstep):
        qrows = slice(t * BLOCK, (t + 1) * BLOCK)
        if latent:
            kcat = jnp.concatenate(band_k[t:t + 3] + [ks[3]], axis=0)
            vcat = jnp.concatenate(band_v[t:t + 3] + [vs[3]], axis=0)
            qblk = pl.program_id(1) * nstep + t
            p_off = jnp.where(qblk > 0, 0, 2 * BLOCK)
            n_off = jnp.where(qblk < nblk - 1, 0, 2 * BLOCK)
            prev_ok = (c >= r + p_off) | (c >= BLOCK)
            next_ok = ((c - 2 * BLOCK + n_off) <= r) | (c < 2 * BLOCK) | (c >= 3 * BLOCK)
            valid = prev_ok & next_ok
        else:
            kcat, vcat = ks[0], vs[0]
        heads = []
        for j in range(KV_A):
            q3 = jnp.concatenate([q_ref[0, qrows, (GQ_A * j + g) * LANES:(GQ_A * j + g + 1) * LANES]
                                  for g in range(GQ_A)], axis=0)
            s = _dot_nt(q3, kcat)
            if latent:
                s = jnp.where(valid, s, NEG)
            sink = jnp.where(rowi < BLOCK, sink_ref[GQ_A * j],
                             jnp.where(rowi < 2 * BLOCK, sink_ref[GQ_A * j + 1], sink_ref[GQ_A * j + 2]))
            m = jnp.maximum(jnp.max(s, axis=-1, keepdims=True), sink)
            e = jnp.exp(s - m)
            den = jnp.sum(e, axis=-1, keepdims=True) + jnp.exp(sink - m)
            o = _dot(e.astype(BF16), vcat) / den
            for g in range(GQ_A):
                heads.append((j, o[g * BLOCK:(g + 1) * BLOCK]))
        for blk in range(H_A // 2):
            (j0, o0), (j1, o1) = heads[2 * blk], heads[2 * blk + 1]
            lo = o0 if j0 == 0 else pltpu.roll(o0, HD_A, 1)
            hi = o1 if j1 == 1 else pltpu.roll(o1, HD_A, 1)
            o_ref[0, qrows, blk * LANES:(blk + 1) * LANES] = jnp.where(lane < HD_A, lo, hi).astype(o_ref.dtype)


def _attn_a(qa, ka, va, sink, ctx_kv):
    bsz, seq, _ = qa.shape
    nblk = seq // BLOCK
    latent = ctx_kv is not None
    kvw = KV_A * HD_A
    nb = A_STEP_BLOCKS if nblk % A_STEP_BLOCKS == 0 else nblk
    if latent:
        past = ctx_kv[0].shape[1]
        band = [pl.BlockSpec((1, BLOCK, kvw), lambda b, i: (b, jnp.maximum(nb * i - 1, 0), 0)),
                pl.BlockSpec((1, nb * BLOCK, kvw), lambda b, i: (b, i, 0)),
                pl.BlockSpec((1, BLOCK, kvw), lambda b, i: (b, jnp.minimum(nb * i + nb, nblk - 1), 0)),
                pl.BlockSpec((1, past, kvw), lambda b, i: (b, 0, 0))]
        kv_specs = band + band
        kv_args = [ka, ka, ka, ctx_kv[0], va, va, va, ctx_kv[1]]
    else:
        kv_specs = [pl.BlockSpec((1, seq, kvw), lambda b, i: (b, 0, 0))] * 2
        kv_args = [ka, va]
    return pl.pallas_call(
        functools.partial(_attn_a_body, latent=latent, nblk=nblk, nstep=nb),
        grid=(bsz, nblk // nb),
        in_specs=[pl.BlockSpec(memory_space=pltpu.SMEM),
                  pl.BlockSpec((1, nb * BLOCK, QA_COLS), lambda b, i: (b, i, 0))] + kv_specs,
        out_specs=pl.BlockSpec((1, nb * BLOCK, W_A), lambda b, i: (b, i, 0)),
        out_shape=jax.ShapeDtypeStruct((bsz, seq, W_A), BF16),
        compiler_params=_cparams("parallel", "parallel"),
        name="attn_a_latent" if latent else "attn_a_ctx",
    )(sink, qa, *kv_args)


def _attn_b_body(lam_ref, gain_ref, q_ref, *refs, part_lens, lam_init, kc):
    npart = len(part_lens)
    k_refs, v_refs = refs[:npart], refs[npart:2 * npart]
    o_ref, s_scr, vm_scr = refs[2 * npart:]
    tq = q_ref.shape[1]
    chunks = []
    col = 0
    for p, plen in enumerate(part_lens):
        step = min(kc, plen)
        for start in range(0, plen, step):
            chunks.append((p, start, col, step))
            col += step

    @pl.when(pl.program_id(2) == 0)
    def _():
        off = 0
        for p, plen in enumerate(part_lens):
            v = v_refs[p][0].astype(BF16)
            lane_v = lax.broadcasted_iota(jnp.int32, (plen, LANES), 1)
            for h in range(2):
                own = (lane_v >= h * HD_B) & (lane_v < (h + 1) * HD_B)
                ones = jnp.where(lane_v == (1 - h) * HD_B, 1.0, 0.0).astype(BF16)
                vm_scr[h, off:off + plen, :] = jnp.where(own, v, ones)
            off += plen

    lv = lam_ref[...]
    lam = (jnp.exp(jnp.sum(lv[0:1] * lv[1:2], axis=-1, keepdims=True))
           - jnp.exp(jnp.sum(lv[2:3] * lv[3:4], axis=-1, keepdims=True)) + lam_init)
    q = q_ref[0]
    lane_q = lax.broadcasted_iota(jnp.int32, (tq, LANES), 1)
    total = jnp.zeros((tq, LANES), F32)
    for h in range(2):
        qc = [jnp.where((lane_q >= h * HD_B + c * DC_B) & (lane_q < h * HD_B + (c + 1) * DC_B), q, jnp.zeros_like(q))
              for c in range(2)]
        rows = [slice(c * tq, (c + 1) * tq) for c in range(2)]
        macc = [None, None]
        for p, start, col, step in chunks:
            kch = k_refs[p][0, start:start + step, :].astype(BF16)
            for c in range(2):
                s = _dot_nt(qc[c], kch)
                s_scr[rows[c], col:col + step] = s
                for j in range(step // LANES):
                    t = s[:, j * LANES:(j + 1) * LANES]
                    macc[c] = t if macc[c] is None else jnp.maximum(macc[c], t)
        m = [jnp.max(macc[c], axis=-1, keepdims=True) for c in range(2)]
        acc = [jnp.zeros((tq, LANES), F32) for _ in range(2)]
        for p, start, col, step in chunks:
            vch = vm_scr[h, col:col + step, :]
            for c in range(2):
                e = jnp.exp2(s_scr[rows[c], col:col + step] - m[c]).astype(BF16)
                acc[c] = acc[c] + _dot(e, vch)
        o2 = [acc[c] / jnp.sum(jnp.where(lane_q == (1 - h) * HD_B, acc[c], 0.0), axis=-1, keepdims=True)
              for c in range(2)]
        own = (lane_q >= h * HD_B) & (lane_q < (h + 1) * HD_B)
        total = total + jnp.where(own, o2[0] - lam * o2[1], 0.0)
    sq = total * total
    ss_lo = jnp.sum(jnp.where(lane_q < HD_B, sq, 0.0), axis=-1, keepdims=True)
    ss_hi = jnp.sum(jnp.where(lane_q >= HD_B, sq, 0.0), axis=-1, keepdims=True)
    rinv = jnp.where(lane_q < HD_B, lax.rsqrt(ss_lo * (1.0 / HD_B) + EPS), lax.rsqrt(ss_hi * (1.0 / HD_B) + EPS))
    o_ref[0] = (total * rinv * gain_ref[...] * (1.0 - lam_init)).astype(o_ref.dtype)


def _attn_b(qb, k_parts, v_parts, lam_b, gain, lam_init):
    bsz, seq, _ = qb.shape
    tq = 512 if seq % 512 == 0 else 256
    part_lens = tuple(k.shape[1] for k in k_parts)
    lk = sum(part_lens)
    kv_specs = [pl.BlockSpec((1, n, LANES), lambda b, hp, i: (b, 0, hp)) for n in part_lens]
    return pl.pallas_call(
        functools.partial(_attn_b_body, part_lens=part_lens, lam_init=lam_init, kc=512),
        grid=(bsz, W_B // LANES, seq // tq),
        in_specs=[pl.BlockSpec((4, DC_B), lambda b, hp, i: (0, 0)),
                  pl.BlockSpec((1, LANES), lambda b, hp, i: (0, 0)),
                  pl.BlockSpec((1, tq, LANES), lambda b, hp, i: (b, i, hp))] + kv_specs + kv_specs,
        out_specs=pl.BlockSpec((1, tq, LANES), lambda b, hp, i: (b, i, hp)),
        out_shape=jax.ShapeDtypeStruct((bsz, seq, W_B), BF16),
        scratch_shapes=[pltpu.VMEM((2 * tq, lk), F32), pltpu.VMEM((2, lk, LANES), BF16)],
        compiler_params=_cparams("parallel", "parallel", "arbitrary"),
        name="attn_b_latent" if len(k_parts) > 1 else "attn_b_ctx",
    )(lam_b, gain, qb, *k_parts, *v_parts)


def _ssm_body(u_ref, m_ref, g_ref, cc_ref, a_ref, h0_ref, y_ref, fin_ref, s_scr, h_scr, *, nb, nc):
    u = u_ref[0]
    col = lambda k: slice(k * LANES, (k + 1) * LANES)
    s = _dot(u, g_ref[0])
    for k in range(4):
        s_scr[k] = s[:, col(k)]
    a = a_ref[0]
    afr, afi, abr, abi = (jnp.broadcast_to(a[k:k + 1], (nb, LANES)) for k in range(4))
    h0 = h0_ref[0]

    def step(c, carry):
        fr, fi, br, bi = carry
        rf = pl.ds(c, nb, stride=nc)
        rb = pl.ds(nc - 1 - c, nb, stride=nc)
        h_scr[0, rf, :] = fr
        h_scr[1, rf, :] = fi
        h_scr[2, rb, :] = br
        h_scr[3, rb, :] = bi
        nfr = afr * fr - afi * fi + s_scr[0, rf, :]
        nfi = afr * fi + afi * fr + s_scr[1, rf, :]
        nbr = abr * br - abi * bi + s_scr[2, rb, :]
        nbi = abr * bi + abi * br + s_scr[3, rb, :]
        return nfr, nfi, nbr, nbi

    fin = lax.fori_loop(0, nc, step, tuple(h0[:, col(k)] for k in range(4)), unroll=8)
    for k in range(4):
        fin_ref[0, :, col(k)] = fin[k]
    hin = jnp.concatenate([h_scr[k] for k in range(4)], axis=1).astype(BF16)
    y = _dot(u, m_ref[0]) + _dot(hin, cc_ref[0])
    y_ref[0] = y.astype(y_ref.dtype)


def _ssm(u_rows, mats, h0, nb):
    npair, rows, w = u_rows.shape
    nc = rows // nb
    mat_spec = pl.BlockSpec((1, w, w), lambda p: (p, 0, 0))
    return pl.pallas_call(
        functools.partial(_ssm_body, nb=nb, nc=nc),
        grid=(npair,),
        in_specs=[pl.BlockSpec((1, rows, w), lambda p: (p, 0, 0)), mat_spec, mat_spec, mat_spec,
                  pl.BlockSpec((1, 4, LANES), lambda p: (p, 0, 0)),
                  pl.BlockSpec((1, nb, w), lambda p: (p, 0, 0))],
        out_specs=[pl.BlockSpec((1, rows, w), lambda p: (p, 0, 0)),
                   pl.BlockSpec((1, nb, w), lambda p: (p, 0, 0))],
        out_shape=[jax.ShapeDtypeStruct((npair, rows, w), BF16),
                   jax.ShapeDtypeStruct((npair, nb, w), F32)],
        scratch_shapes=[pltpu.VMEM((4, rows, LANES), F32), pltpu.VMEM((4, rows, LANES), F32)],
        compiler_params=_cparams("parallel"),
        name="ssm_scan",
    )(u_rows, mats["m"], mats["g"], mats["cc"], mats["a16"], h0)


def _ssm_matrices(lp):
    t = SSM_T
    ks = jnp.arange(t + 1, dtype=F32)
    dirs = []
    for d in range(2):
        lam = lax.complex(lp["ssm_lam_re"][d].astype(F32), lp["ssm_lam_im"][d].astype(F32))
        dt = jnp.exp(lp["ssm_log_dt"][d].astype(F32))[:, None]
        a_bar = jnp.exp(lam * dt)
        b_bar = ((a_bar - 1.0) / lam)[..., None] * lax.complex(lp["ssm_b_re"][d].astype(F32),
                                                               lp["ssm_b_im"][d].astype(F32))
        c_mat = lax.complex(lp["ssm_c_re"][d].astype(F32), lp["ssm_c_im"][d].astype(F32))
        pw = jnp.exp((lam * dt)[None] * ks[:, None, None].astype(jnp.complex64))
        kern = jnp.real(jnp.einsum("gop,kgp,gpi->gkoi", c_mat, pw[:t], b_bar))
        dirs.append((pw, b_bar, c_mat, kern))
    (pw_f, bb_f, cm_f, k_f), (pw_b, bb_b, cm_b, k_b) = dirs
    eye2 = jnp.eye(2, dtype=F32)
    ch, pw2 = SSM_CH, 2 * SSM_CH
    hi = lax.Precision.HIGHEST

    def pair_bd(x):
        r, c = x.shape[1:]
        return jnp.einsum("pgrc,gh->pgrhc", x.reshape(N_PAIR, 2, r, c), eye2.astype(x.dtype)).reshape(N_PAIR, 2 * r, 2 * c)

    def pair_vec(x):
        return x.reshape(x.shape[0], N_PAIR, 2 * P_C).transpose(1, 0, 2)

    def lag_blocks(kern):
        x = kern.transpose(0, 1, 3, 2).reshape(N_PAIR, 2, t, ch, ch)
        return jnp.einsum("pglic,gh->plgihc", x, eye2).reshape(N_PAIR, t, pw2, pw2)
    kp_f, kp_b = lag_blocks(k_f), lag_blocks(k_b)
    d_blk = pair_bd(lp["ssm_d"].astype(F32)[:, :, None] * jnp.eye(ch, dtype=F32)[None])
    center = (kp_f[:, 0] + kp_b[:, 0] + d_blk)[:, None]
    band = jnp.concatenate([kp_b[:, :0:-1], center, kp_f[:, 1:]], axis=1)
    band = band.transpose(0, 2, 1, 3).reshape(N_PAIR, pw2, (2 * t - 1) * pw2)
    m_p = jnp.concatenate([band[:, :, (t - 1 - s) * pw2:(t - 1 - s) * pw2 + SSM_ROW] for s in range(t)], axis=1)

    def inject(pw_sel, b_bar):
        x1 = jnp.repeat(pair_vec(pw_sel), pw2, axis=1)
        x2 = jnp.tile(pair_bd(b_bar.transpose(0, 2, 1)), (1, t, 1))
        return x1 * x2
    g_f = inject(pw_f[t - 1 - jnp.arange(t)], bb_f)
    g_b = inject(pw_b[jnp.arange(t)], bb_b)
    g_p = jnp.concatenate([jnp.real(g_f), jnp.imag(g_f), jnp.real(g_b), jnp.imag(g_b)], axis=2)

    lane = jnp.arange(SSM_ROW)
    exp_t = (jnp.arange(t)[:, None] == lane[None, :] // pw2).astype(F32)
    exp_c = (jnp.arange(pw2)[:, None] == lane[None, :] % pw2).astype(F32)

    def widen(x, e):
        f = lambda v: jnp.einsum("pqk,kx->pqx", v, e, precision=hi)
        return lax.complex(f(jnp.real(x)), f(jnp.imag(x)))

    def readout(pw_sel, c_mat):
        y1 = widen(pair_vec(pw_sel).transpose(0, 2, 1), exp_t)
        y2 = widen(pair_bd(c_mat.transpose(0, 2, 1)), exp_c)
        return y1 * y2
    z_f = readout(pw_f[1 + jnp.arange(t)], cm_f)
    z_b = readout(pw_b[t - jnp.arange(t)], cm_b)
    cc_p = jnp.concatenate([jnp.real(z_f), -jnp.imag(z_f), jnp.real(z_b), -jnp.imag(z_b)], axis=1)
    a16 = jnp.stack([jnp.real(pw_f[t]), jnp.imag(pw_f[t]), jnp.real(pw_b[t]), jnp.imag(pw_b[t])], axis=0)
    a16 = a16.reshape(4, N_PAIR, 2 * P_C).transpose(1, 0, 2)
    return dict(m=m_p.astype(BF16), g=g_p.astype(BF16), cc=cc_p.astype(BF16), a16=a16)


def _ssm_state_rows(s_re, s_im):
    bsz = s_re.shape[0]
    parts = [s_re[:, 0], s_im[:, 0], s_re[:, 1], s_im[:, 1]]
    st = jnp.stack([p.reshape(bsz, N_PAIR, 2 * P_C) for p in parts], axis=2)
    return st.transpose(1, 0, 2, 3).reshape(N_PAIR, bsz, 8 * P_C).astype(F32)


def _ssm_state_unrows(fin):
    npair, bsz, _ = fin.shape
    st = fin.reshape(npair, bsz, 4, 2, P_C).transpose(1, 2, 0, 3, 4).reshape(bsz, 4, G_C, P_C)
    return jnp.stack([st[:, 0], st[:, 2]], axis=1), jnp.stack([st[:, 1], st[:, 3]], axis=1)


def _route(scores, bias):
    tm = scores.shape[1]
    biased = scores + bias
    iota8 = lax.broadcasted_iota(jnp.int32, (PER_GROUP, tm), 0)
    grp = [biased[PER_GROUP * g:PER_GROUP * (g + 1)] for g in range(N_EXP_GROUPS)]
    gscore = []
    for v in grp:
        m1 = jnp.max(v, axis=0, keepdims=True)
        first = jnp.min(jnp.where(v == m1, iota8, PER_GROUP), axis=0, keepdims=True)
        m2 = jnp.max(jnp.where(iota8 == first, -jnp.inf, v), axis=0, keepdims=True)
        gscore.append(m1 + m2)
    masked = []
    for g in range(N_EXP_GROUPS):
        rank = jnp.zeros((1, tm), jnp.int32)
        for o in range(N_EXP_GROUPS):
            if o == g:
                continue
            ahead = (gscore[o] >= gscore[g]) if o < g else (gscore[o] > gscore[g])
            rank = rank + jnp.where(ahead, 1, 0)
        masked.append(jnp.where(rank < TOPK_GROUPS, grp[g], -jnp.inf))
    chosen = [None] * N_EXP_GROUPS
    for _ in range(TOP_K):
        best = masked[0]
        for v in masked[1:]:
            best = jnp.maximum(best, v)
        best = jnp.max(best, axis=0, keepdims=True)
        first = jnp.full((1, tm), N_EXPERTS, jnp.int32)
        for g, v in enumerate(masked):
            cand = jnp.min(jnp.where(v == best, iota8 + PER_GROUP * g, N_EXPERTS), axis=0, keepdims=True)
            first = jnp.minimum(first, cand)
        for g in range(N_EXP_GROUPS):
            hit = (iota8 + PER_GROUP * g) == first
            chosen[g] = hit if chosen[g] is None else (chosen[g] | hit)
            masked[g] = jnp.where(hit, -jnp.inf, masked[g])
    w = [jnp.where(chosen[g], scores[PER_GROUP * g:PER_GROUP * (g + 1)], 0.0) for g in range(N_EXP_GROUPS)]
    wsum = w[0]
    for v in w[1:]:
        wsum = wsum + v
    wsum = jnp.sum(wsum, axis=0, keepdims=True)
    return jnp.concatenate([v / wsum * ROUTED_SCALE for v in w], axis=0)


def _post_body(x_ref, oa_ref, ob_ref, y_ref, mod_ref, wglu_ref, wout_ref, g2_ref, wrh_ref, wrl_ref, br_ref,
               x1_ref, h2_ref, gate_ref, y_scr):
    d = D_MODEL
    tm = x_ref.shape[1]
    pw = 2 * SSM_CH
    for t in range(SSM_T):
        for blk in range(W_C // LANES):
            piece = jnp.concatenate([y_ref[blk * (LANES // pw) + pp, :, t * pw:(t + 1) * pw].astype(F32)
                                     for pp in range(LANES // pw)], axis=1)
            y_scr[blk, pl.ds(t, tm // SSM_T, stride=SSM_T), :] = piece
    g = jax.nn.gelu(jnp.concatenate([y_scr[blk] for blk in range(W_C // LANES)], axis=1))
    oc = g * jax.nn.sigmoid(_dot(g.astype(BF16), wglu_ref[...]))
    mix = (_dot(oa_ref[0], wout_ref[0:W_A]) + _dot(ob_ref[0], wout_ref[W_A:W_A + W_B])
           + _dot(oc.astype(BF16), wout_ref[W_A + W_B:]))
    mod = mod_ref[0]
    x1 = x_ref[0] + mod[:, 2 * d:3 * d] * mix
    x1_ref[0] = x1
    xn = x1 * lax.rsqrt(jnp.mean(x1 * x1, axis=-1, keepdims=True) + EPS) * g2_ref[...]
    h2 = xn * (1.0 + mod[:, 4 * d:5 * d]) + mod[:, 3 * d:4 * d]
    h_hi, h_lo = _split_bf16(h2)
    h2_ref[0] = h_hi
    logits = _dot_nt(wrh_ref[...], h_hi) + _dot_nt(wrh_ref[...], h_lo) + _dot_nt(wrl_ref[...], h_hi)
    gate_ref[0] = _route(jax.nn.sigmoid(logits), br_ref[...]).T


def _post_mix(x, oa, ob, y, mod, w_glu, w_out, g2, wr_hi, wr_lo, b_r):
    bsz, seq, d = x.shape
    tm = 512 if seq % 512 == 0 else 256
    bm = mod.shape[0]
    mod_idx = (lambda b, i: (b, 0, 0)) if bm > 1 else (lambda b, i: (0, 0, 0))
    const2 = lambda b, i: (0, 0)
    tok = lambda w: pl.BlockSpec((1, tm, w), lambda b, i: (b, i, 0))
    nt = seq // tm
    return pl.pallas_call(
        _post_body,
        grid=(bsz, nt),
        in_specs=[tok(d), tok(W_A), tok(W_B),
                  pl.BlockSpec((N_PAIR, tm // SSM_T, SSM_ROW), lambda b, i: (0, b * nt + i, 0)),
                  pl.BlockSpec((1, 1, 6 * d), mod_idx),
                  pl.BlockSpec((W_C, W_C), const2),
                  pl.BlockSpec((d, d), const2),
                  pl.BlockSpec((1, d), const2),
                  pl.BlockSpec((N_EXPERTS, d), const2),
                  pl.BlockSpec((N_EXPERTS, d), const2),
                  pl.BlockSpec((N_EXPERTS, 1), const2)],
        out_specs=[tok(d), tok(d), tok(N_EXPERTS)],
        out_shape=[jax.ShapeDtypeStruct((bsz, seq, d), F32),
                   jax.ShapeDtypeStruct((bsz, seq, d), BF16),
                   jax.ShapeDtypeStruct((bsz, seq, N_EXPERTS), F32)],
        scratch_shapes=[pltpu.VMEM((W_C // LANES, tm, LANES), F32)],
        compiler_params=_cparams("parallel", "parallel"),
        name="post_mix",
    )(x, oa, ob, y, mod, w_glu, w_out, g2, wr_hi, wr_lo, b_r)


def _moe_body(x1_ref, h_ref, gate_ref, g2_ref, w1_ref, w3_ref, w2_ref, c1_ref, c32_ref, ex_ref,
              s1_ref, s3_ref, s2_ref, o_ref, acc_ref, h8_ref, hs_ref):
    j = pl.program_id(1)

    @pl.when(j == 0)
    def _():
        h = h_ref[...]
        a = _dot(h, s1_ref[...])
        acc_ref[...] = _dot((a * jax.nn.sigmoid(a) * _dot(h, s3_ref[...])).astype(BF16), s2_ref[...])
        hf = h.astype(F32)
        sc = jnp.maximum(jnp.max(jnp.abs(hf), axis=-1, keepdims=True), F8_TINY) * (1.0 / F8_RANGE)
        hs_ref[...] = sc
        h8_ref[...] = (hf * (1.0 / sc)).astype(F8)

    ne = w1_ref.shape[0]
    h8 = h8_ref[...]
    hs = hs_ref[...]
    a = _dot(h8, jnp.concatenate([w1_ref[e] for e in range(ne)], axis=1)) * c1_ref[...] * hs
    b = _dot(h8, jnp.concatenate([w3_ref[e] for e in range(ne)], axis=1))
    gexp = _dot(jnp.concatenate(_split_bf16(gate_ref[...]), axis=1), ex_ref[...])
    hid = a * jax.nn.sigmoid(a) * b * gexp * c32_ref[...]
    sc = jnp.maximum(jnp.max(jnp.abs(hid), axis=-1, keepdims=True), F8_TINY) * (1.0 / F8_RANGE)
    acc_ref[...] += _dot((hid * (1.0 / sc)).astype(F8), w2_ref[...]) * (sc * hs)

    @pl.when(j == pl.num_programs(1) - 1)
    def _():
        o_ref[...] = x1_ref[...] + g2_ref[0] * acc_ref[...]


def _moe(x1, h2, gates, mod, seq, ew, expand, ws1, ws3, ws2):
    tokens, d = x1.shape
    bm = mod.shape[0]
    span = seq if bm > 1 else tokens
    tm = next(t for t in (1024, 512, 256) if span % t == 0)
    per_b = seq // tm if bm > 1 else 1
    mod_idx = (lambda i, j: (i // per_b, 0, 5)) if bm > 1 else (lambda i, j: (0, 0, 5))
    ne = 8
    fc = ne * F_EXP
    hidden = ew["w2"].shape[0]
    const2 = lambda i, j: (0, 0)
    chunk_row = pl.BlockSpec((1, fc), lambda i, j: (0, j))
    return pl.pallas_call(
        _moe_body,
        grid=(tokens // tm, hidden // fc),
        in_specs=[pl.BlockSpec((tm, d), lambda i, j: (i, 0)),
                  pl.BlockSpec((tm, d), lambda i, j: (i, 0)),
                  pl.BlockSpec((tm, N_EXPERTS), lambda i, j: (i, 0)),
                  pl.BlockSpec((1, 1, d), mod_idx),
                  pl.BlockSpec((ne, d, F_EXP), lambda i, j: (j, 0, 0)),
                  pl.BlockSpec((ne, d, F_EXP), lambda i, j: (j, 0, 0)),
                  pl.BlockSpec((fc, d), lambda i, j: (j, 0)),
                  chunk_row, chunk_row,
                  pl.BlockSpec((2 * N_EXPERTS, fc), lambda i, j: (0, j)),
                  pl.BlockSpec((d, F_SHARED), const2),
                  pl.BlockSpec((d, F_SHARED), const2),
                  pl.BlockSpec((F_SHARED, d), const2)],
        out_specs=pl.BlockSpec((tm, d), lambda i, j: (i, 0)),
        out_shape=jax.ShapeDtypeStruct((tokens, d), F32),
        scratch_shapes=[pltpu.VMEM((tm, d), F32), pltpu.VMEM((tm, d), F8), pltpu.VMEM((tm, 1), F32)],
        compiler_params=_cparams("parallel", "arbitrary"),
        name="moe",
    )(x1, h2, gates, mod, ew["w1"], ew["w3"], ew["w2"], ew["c1"], ew["c32"], expand, ws1, ws3, ws2)


def _expert_fp8(w):
    sc = jnp.maximum(jnp.max(jnp.abs(w), axis=(1, 2)), F8_TINY) * (1.0 / F8_RANGE)
    return (w / sc[:, None, None]).astype(F8), jnp.repeat(sc, F_EXP)[None, :].astype(F32)


def _prep_experts(p):
    w1, c1 = _expert_fp8(p["w_e1"].astype(F32))
    w3, c3 = _expert_fp8(p["w_e3"].astype(F32))
    w2, c2 = _expert_fp8(p["w_e2"].astype(F32))
    return dict(w1=w1, w3=w3, w2=w2.reshape(N_EXPERTS * F_EXP, D_MODEL), c1=c1, c32=c3 * c2)


def _rope_tables(seq):
    pos = jnp.arange(seq)
    row = (pos // GRID_W).astype(F32)[:, None]
    colp = (pos % GRID_W).astype(F32)[:, None]
    lane = jnp.arange(LANES)

    def table(width):
        half, quarter = width // 2, width // 4
        i = lane % width
        freq = ROPE_BASE ** (-(2.0 * (i % quarter).astype(F32)) / half)
        ang = jnp.where((i // half) == 0, row, colp) * freq[None, :]
        sign = jnp.where((i % half) < quarter, -1.0, 1.0)
        return jnp.cos(ang), jnp.sin(ang) * sign[None, :]

    ca, sa = table(HD_A)
    cb, sb = table(DC_B)
    return ca, sa, cb, sb


def _prep_layer(p):
    d = D_MODEL
    w_in = p["w_in"]
    place = (jnp.arange(H_A)[:, None] // GQ_A == jnp.arange(KV_A)[None, :]).astype(w_in.dtype)
    qa_pad = w_in[:, :W_A].reshape(d, H_A, 1, HD_A) * place[None, :, :, None]
    w_in_p = jnp.concatenate([qa_pad.reshape(d, QA_COLS), w_in[:, W_A:]], axis=1).astype(BF16)
    gains = jnp.stack([jnp.tile(p["q_norm_a"], LANES // HD_A) * (HD_A ** -0.5),
                       jnp.tile(p["k_norm_a"], LANES // HD_A),
                       jnp.tile(p["q_norm_b"], LANES // DC_B) * (DC_B ** -0.5 * LOG2E),
                       jnp.tile(p["k_norm_b"], LANES // DC_B)], axis=0).astype(F32)
    lp = {k: p[k] for k in ("ssm_lam_re", "ssm_lam_im", "ssm_log_dt", "ssm_b_re", "ssm_b_im",
                            "ssm_c_re", "ssm_c_im", "ssm_d")}
    wr_hi, wr_lo = _split_bf16(p["w_router"].T.astype(F32))
    return dict(
        w_in_p=w_in_p, gains=gains,
        g1=p["norm1_g"].reshape(1, d).astype(F32), g2=p["norm2_g"].reshape(1, d).astype(F32),
        sink=p["sink_a"].astype(F32), lam_b=p["lam_b"].astype(F32),
        subln=jnp.tile(p["subln_b"], LANES // HD_B).reshape(1, LANES).astype(F32),
        ssm=_ssm_matrices(lp),
        w_glu=p["w_glu"].astype(BF16), w_out=p["w_out"].astype(BF16),
        wr_hi=wr_hi, wr_lo=wr_lo, b_r=p["b_router"].reshape(N_EXPERTS, 1).astype(F32),
        experts=_prep_experts(p),
        ws1=p["w_s1"].astype(BF16), ws3=p["w_s3"].astype(BF16), ws2=p["w_s2"].astype(BF16),
    )


def _trunk_layer(x, mod, lw, consts, ctx):
    bsz, seq, d = x.shape
    latent = ctx is not None
    rope = consts["rope"] if latent else None
    kv_dtype = BF16 if latent else F32
    grp = 1 if latent else next(g for g in (4, 2, 1) if bsz % g == 0)
    tok = lambda a: a.reshape(bsz // grp, grp * seq, a.shape[-1])
    per_seq = lambda a: a.reshape(bsz, seq, a.shape[-1])
    qa, ka, va, qb, kb, vb, u = _inproj(tok(x), mod, lw["g1"], lw["w_in_p"], lw["gains"],
                                        consts["seg64"], consts["seg32"], rope, kv_dtype)
    qa, ka, va, qb, kb, vb = (per_seq(a) for a in (qa, ka, va, qb, kb, vb))
    if latent:
        oa = _attn_a(qa, ka, va, lw["sink"], (ctx["ak"], ctx["av"]))
        ob = _attn_b(qb, [kb, ctx["bk"]], [vb, ctx["bv"]], lw["lam_b"], lw["subln"], lw["lam_init"])
        h0 = ctx["h0"]
    else:
        oa = _attn_a(qa, ka, va, lw["sink"], None)
        ob = _attn_b(qb, [kb], [vb], lw["lam_b"], lw["subln"], lw["lam_init"])
        h0 = jnp.zeros((N_PAIR, bsz, 8 * P_C), F32)
    y_rows, fin = _ssm(u, lw["ssm"], h0, bsz)
    x1, h2, gates = _post_mix(tok(x), tok(oa), tok(ob), y_rows, mod, lw["w_glu"], lw["w_out"], lw["g2"],
                                lw["wr_hi"], lw["wr_lo"], lw["b_r"])
    out = _moe(x1.reshape(bsz * seq, d), h2.reshape(bsz * seq, d), gates.reshape(bsz * seq, N_EXPERTS), mod, seq,
               lw["experts"], consts["expand"], lw["ws1"], lw["ws3"], lw["ws2"])
    return out.reshape(bsz, seq, d), (ka, va, kb, vb, fin)


def kernel(x_prompt, x_sample, cache_a_k, cache_a_v, cache_b_k, cache_b_v, state_ssm_re, state_ssm_im, c, c_ctx, norm1_g, norm2_g, w_ada, b_ada, w_in, q_norm_a, k_norm_a, sink_a, q_norm_b, k_norm_b, lam_b, subln_b, ssm_lam_re, ssm_lam_im, ssm_log_dt, ssm_b_re, ssm_b_im, ssm_c_re, ssm_c_im, ssm_d, w_glu, w_out, w_router, b_router, w_e1, w_e3, w_e2, w_s1, w_s3, w_s2):
    p = dict(norm1_g=norm1_g, norm2_g=norm2_g, w_in=w_in, q_norm_a=q_norm_a, k_norm_a=k_norm_a, sink_a=sink_a,
             q_norm_b=q_norm_b, k_norm_b=k_norm_b, lam_b=lam_b, subln_b=subln_b,
             ssm_lam_re=ssm_lam_re, ssm_lam_im=ssm_lam_im, ssm_log_dt=ssm_log_dt, ssm_b_re=ssm_b_re,
             ssm_b_im=ssm_b_im, ssm_c_re=ssm_c_re, ssm_c_im=ssm_c_im, ssm_d=ssm_d, w_glu=w_glu, w_out=w_out,
             w_router=w_router, b_router=b_router, w_e1=w_e1, w_e3=w_e3, w_e2=w_e2,
             w_s1=w_s1, w_s3=w_s3, w_s2=w_s2)
    depth = w_in.shape[0]
    bsz, seq, d = x_prompt.shape
    dbsz, dseq, _ = x_sample.shape
    past = cache_a_k.shape[3]

    mod_rows = 16
    cvec = jnp.concatenate([c.astype(F32), c_ctx.astype(F32)[None],
                            jnp.zeros((mod_rows - dbsz - 1, d), F32)], axis=0)
    mods = _modulation(cvec, w_ada.astype(F32), b_ada.astype(F32))

    lane = jnp.arange(LANES)
    hidden = N_EXPERTS * F_EXP
    consts = dict(
        rope=_rope_tables(dseq),
        seg64=(lane[:, None] // HD_A == lane[None, :] // HD_A).astype(BF16),
        seg32=(lane[:, None] // DC_B == lane[None, :] // DC_B).astype(BF16),
        expand=(jnp.arange(2 * N_EXPERTS)[:, None] % N_EXPERTS == jnp.arange(hidden)[None, :] // F_EXP).astype(BF16),
    )

    xp, xs = x_prompt, x_sample
    ak, av, bk, bv, sre, sim = [], [], [], [], [], []
    prepared = jax.vmap(_prep_layer)(p)
    for l in range(depth):
        lw = jax.tree.map(lambda v: v[l], prepared)
        lw["lam_init"] = 0.8 - 0.6 * math.exp(-0.3 * l)
        mod_lat = mods[l, :dbsz][:, None, :]
        mod_ctx = mods[l, dbsz:dbsz + 1][:, None, :]
        xp, (k_a, v_a, k_b, v_b, fin) = _trunk_layer(xp, mod_ctx, lw, consts, None)
        ak.append(k_a.reshape(bsz, seq, KV_A, HD_A).transpose(0, 2, 1, 3))
        av.append(v_a.reshape(bsz, seq, KV_A, HD_A).transpose(0, 2, 1, 3))
        bk.append(k_b.reshape(bsz, seq, H_B, 2, DC_B).transpose(0, 2, 3, 1, 4))
        bv.append(v_b.reshape(bsz, seq, H_B, HD_B).transpose(0, 2, 1, 3))
        f_re, f_im = _ssm_state_unrows(fin)
        sre.append(f_re)
        sim.append(f_im)
        ctx = dict(
            ak=cache_a_k[:, l].transpose(0, 2, 1, 3).reshape(dbsz, past, KV_A * HD_A).astype(BF16),
            av=cache_a_v[:, l].transpose(0, 2, 1, 3).reshape(dbsz, past, KV_A * HD_A).astype(BF16),
            bk=cache_b_k[:, l].transpose(0, 3, 1, 2, 4).reshape(dbsz, past, W_B).astype(BF16),
            bv=cache_b_v[:, l].transpose(0, 2, 1, 3).reshape(dbsz, past, W_B).astype(BF16),
            h0=_ssm_state_rows(state_ssm_re[:, l], state_ssm_im[:, l]),
        )
        xs, _ = _trunk_layer(xs, mod_lat, lw, consts, ctx)
    return (xp, xs, jnp.stack(ak, axis=1), jnp.stack(av, axis=1), jnp.stack(bk, axis=1),
            jnp.stack(bv, axis=1), jnp.stack(sre, axis=1), jnp.stack(sim, axis=1))
```

```python
import functools
import math

import jax
import jax.numpy as jnp
from jax import lax
from jax.experimental import pallas as pl
from jax.experimental.pallas import tpu as pltpu

F32 = jnp.float32
BF16 = jnp.bfloat16
F8 = jnp.float8_e4m3fn
F8_RANGE = 384.0
F8_TINY = 1e-30

D_MODEL = 1024
GRID_W = 64
BLOCK = 128
H_A, KV_A, HD_A = 6, 2, 64
GQ_A = H_A // KV_A
W_A = H_A * HD_A
H_B, HD_B = 4, 64
DC_B = HD_B // 2
W_B = H_B * HD_B
SSM_CH = 16
W_C = D_MODEL - W_A - W_B
G_C = W_C // SSM_CH
P_C = 64
N_EXPERTS, TOP_K, F_EXP, F_SHARED = 64, 6, 128, 256
N_EXP_GROUPS, TOPK_GROUPS = 8, 4
PER_GROUP = N_EXPERTS // N_EXP_GROUPS
ROUTED_SCALE = 2.5
ROPE_BASE = 10000.0
EPS = 1e-6
NEG = -1e30
LOG2E = 1.4426950408889634

LANES = 128
SSM_T = 16
N_PAIR = G_C // 2
SSM_ROW = 2 * SSM_T * SSM_CH
QA_COLS = H_A * LANES
IN_COLS_P = QA_COLS + 2 * KV_A * HD_A + 3 * W_B + W_C
VMEM_LIMIT = 56 << 20
A_STEP_BLOCKS = 8


def _cparams(*sem):
    return pltpu.CompilerParams(dimension_semantics=sem, vmem_limit_bytes=VMEM_LIMIT)


def _dot(a, b):
    return jnp.dot(a, b, preferred_element_type=F32)


def _dot_nt(a, b):
    return lax.dot_general(a, b, (((1,), (1,)), ((), ())), preferred_element_type=F32)


def _split_bf16(x):
    hi = x.astype(BF16)
    lo = (x - hi.astype(F32)).astype(BF16)
    return hi, lo


def _mod_body(c_ref, w_ref, b_ref, o_ref):
    c = c_ref[...]
    s = c * jax.nn.sigmoid(c)
    s_hi, s_lo = _split_bf16(s)
    w_hi, w_lo = _split_bf16(w_ref[0])
    o_ref[0] = _dot(s_hi, w_hi) + _dot(s_lo, w_hi) + _dot(s_hi, w_lo) + b_ref[0]


def _modulation(cvec, w_ada, b_ada):
    depth, d, n = w_ada.shape
    rows = cvec.shape[0]
    tn = 768
    return pl.pallas_call(
        _mod_body,
        grid=(depth, n // tn),
        in_specs=[pl.BlockSpec((rows, d), lambda l, j: (0, 0)),
                  pl.BlockSpec((1, d, tn), lambda l, j: (l, 0, j)),
                  pl.BlockSpec((1, 1, tn), lambda l, j: (l, 0, j))],
        out_specs=pl.BlockSpec((1, rows, tn), lambda l, j: (l, 0, j)),
        out_shape=jax.ShapeDtypeStruct((depth, rows, n), F32),
        compiler_params=_cparams("parallel", "parallel"),
        name="adaln_mod",
    )(cvec, w_ada, b_ada.reshape(depth, 1, n))


def _inproj_body(*refs, latent):
    if latent:
        (x_ref, mod_ref, g1_ref, w_ref, gains_ref, s64_ref, s32_ref, ca_ref, sa_ref, cb_ref, sb_ref,
         qa_ref, ka_ref, va_ref, qb_ref, kb_ref, vb_ref, u_ref, u_scr) = refs
    else:
        (x_ref, mod_ref, g1_ref, w_ref, gains_ref, s64_ref, s32_ref,
         qa_ref, ka_ref, va_ref, qb_ref, kb_ref, vb_ref, u_ref, u_scr) = refs
    d = D_MODEL
    x = x_ref[0]
    mod = mod_ref[0]
    xn = x * lax.rsqrt(jnp.mean(x * x, axis=-1, keepdims=True) + EPS) * g1_ref[...]
    h = xn * (1.0 + mod[:, d:2 * d]) + mod[:, 0:d]
    acc = _dot(h.astype(BF16), w_ref[...])

    tm = x.shape[0]
    lane = lax.broadcasted_iota(jnp.int32, (tm, LANES), 1)
    first_a = (lane % 32) < 16
    first_b = (lane % 16) < 8

    def normed(xb, seg_ref, inv_n, gain):
        ss = _dot((xb * xb).astype(BF16), seg_ref[...])
        return xb * lax.rsqrt(ss * inv_n + EPS) * gain

    def rope_a(y):
        if not latent:
            return y
        sw = jnp.where(first_a, pltpu.roll(y, LANES - 16, 1), pltpu.roll(y, 16, 1))
        return y * ca_ref[...] + sw * sa_ref[...]

    def rope_b(y):
        if not latent:
            return y
        sw = jnp.where(first_b, pltpu.roll(y, LANES - 8, 1), pltpu.roll(y, 8, 1))
        return y * cb_ref[...] + sw * sb_ref[...]

    gains = gains_ref[...]
    off = 0
    for b in range(H_A):
        y = normed(acc[:, off:off + LANES], s64_ref, 1.0 / HD_A, gains[0:1])
        qa_ref[0, :, b * LANES:(b + 1) * LANES] = rope_a(y).astype(qa_ref.dtype)
        off += LANES
    y = normed(acc[:, off:off + LANES], s64_ref, 1.0 / HD_A, gains[1:2])
    ka_ref[0] = rope_a(y).astype(ka_ref.dtype)
    off += LANES
    va_ref[0] = acc[:, off:off + LANES].astype(va_ref.dtype)
    off += LANES
    for b in range(W_B // LANES):
        y = normed(acc[:, off:off + LANES], s32_ref, 1.0 / DC_B, gains[2:3])
        qb_ref[0, :, b * LANES:(b + 1) * LANES] = rope_b(y).astype(qb_ref.dtype)
        off += LANES
    for b in range(W_B // LANES):
        y = normed(acc[:, off:off + LANES], s32_ref, 1.0 / DC_B, gains[3:4])
        kb_ref[0, :, b * LANES:(b + 1) * LANES] = rope_b(y).astype(kb_ref.dtype)
        off += LANES
    vb_ref[0] = acc[:, off:off + W_B].astype(vb_ref.dtype)
    off += W_B
    for blk in range(W_C // LANES):
        u_scr[blk] = acc[:, off + blk * LANES:off + (blk + 1) * LANES]
    pw = 2 * SSM_CH
    for t in range(SSM_T):
        for blk in range(W_C // LANES):
            xt = u_scr[blk, pl.ds(t, tm // SSM_T, stride=SSM_T), :]
            for pp in range(LANES // pw):
                u_ref[blk * (LANES // pw) + pp, :, t * pw:(t + 1) * pw] = xt[:, pp * pw:(pp + 1) * pw].astype(u_ref.dtype)


def _inproj(x, mod, g1, w_in_p, gains, seg64, seg32, rope, kv_dtype):
    bsz, seq, d = x.shape
    latent = rope is not None
    tm = next(t for t in (1024, 512, 256) if seq % t == 0)
    bm = mod.shape[0]
    mod_idx = (lambda b, i: (b, 0, 0)) if bm > 1 else (lambda b, i: (0, 0, 0))
    const2 = lambda b, i: (0, 0)
    tok = lambda w: pl.BlockSpec((1, tm, w), lambda b, i: (b, i, 0))
    in_specs = [tok(d),
                pl.BlockSpec((1, 1, 6 * d), mod_idx),
                pl.BlockSpec((1, d), const2),
                pl.BlockSpec((d, IN_COLS_P), const2),
                pl.BlockSpec((4, LANES), const2),
                pl.BlockSpec((LANES, LANES), const2),
                pl.BlockSpec((LANES, LANES), const2)]
    args = [x, mod, g1, w_in_p, gains, seg64, seg32]
    if latent:
        in_specs += [pl.BlockSpec((tm, LANES), lambda b, i: (i, 0))] * 4
        args += list(rope)
    widths = (QA_COLS, KV_A * HD_A, KV_A * HD_A, W_B, W_B, W_B)
    dtypes = (BF16, kv_dtype, kv_dtype, BF16, kv_dtype, kv_dtype)
    nt = seq // tm
    rows = tm // SSM_T
    u_spec = pl.BlockSpec((N_PAIR, rows, SSM_ROW), lambda b, i: (0, b * nt + i, 0))
    u_shape = jax.ShapeDtypeStruct((N_PAIR, bsz * seq // SSM_T, SSM_ROW), BF16)
    return pl.pallas_call(
        functools.partial(_inproj_body, latent=latent),
        grid=(bsz, nt),
        in_specs=in_specs,
        out_specs=[tok(w) for w in widths] + [u_spec],
        out_shape=[jax.ShapeDtypeStruct((bsz, seq, w), dt) for w, dt in zip(widths, dtypes)] + [u_shape],
        scratch_shapes=[pltpu.VMEM((W_C // LANES, tm, LANES), F32)],
        compiler_params=_cparams("parallel", "parallel"),
        name="inproj_latent" if latent else "inproj_ctx",
    )(*args)


def _attn_a_body(sink_ref, q_ref, *refs, latent, nblk, nstep):
    o_ref = refs[-1]
    nk = (len(refs) - 1) // 2
    ks = [r[0].astype(BF16) for r in refs[:nk]]
    vs = [r[0].astype(BF16) for r in refs[nk:2 * nk]]
    rows = GQ_A * BLOCK
    rowi = lax.broadcasted_iota(jnp.int32, (rows, 1), 0)
    lane = lax.broadcasted_iota(jnp.int32, (BLOCK, LANES), 1)
    if latent:
        cols = 3 * BLOCK + ks[3].shape[0]
        r = lax.broadcasted_iota(jnp.int32, (rows, cols), 0) & (BLOCK - 1)
        c = lax.broadcasted_iota(jnp.int32, (rows, cols), 1)
        own_k = [ks[1][t * BLOCK:(t + 1) * BLOCK] for t in range(nstep)]
        own_v = [vs[1][t * BLOCK:(t + 1) * BLOCK] for t in range(nstep)]
        band_k = [ks[0]] + own_k + [ks[2]]
        band_v = [vs[0]] + own_v + [vs[2]]
    for t in range(nstep):
        qrows = slice(t * BLOCK, (t + 1) * BLOCK)
        if latent:
            kcat = jnp.concatenate(band_k[t:t + 3] + [ks[3]], axis=0)
            vcat = jnp.concatenate(band_v[t:t + 3] + [vs[3]], axis=0)
            qblk = pl.program_id(1) * nstep + t
            p_off = jnp.where(qblk > 0, 0, 2 * BLOCK)
            n_off = jnp.where(qblk < nblk - 1, 0, 2 * BLOCK)
            prev_ok = (c >= r + p_off) | (c >= BLOCK)
            next_ok = ((c - 2 * BLOCK + n_off) <= r) | (c < 2 * BLOCK) | (c >= 3 * BLOCK)
            valid = prev_ok & next_ok
        else:
            kcat, vcat = ks[0], vs[0]
        heads = []
        for j in range(KV_A):
            q3 = jnp.concatenate([q_ref[0, qrows, (GQ_A * j + g) * LANES:(GQ_A * j + g + 1) * LANES]
                                  for g in range(GQ_A)], axis=0)
            s = _dot_nt(q3, kcat)
            if latent:
                s = jnp.where(valid, s, NEG)
            sink = jnp.where(rowi < BLOCK, sink_ref[GQ_A * j],
                             jnp.where(rowi < 2 * BLOCK, sink_ref[GQ_A * j + 1], sink_ref[GQ_A * j + 2]))
            m = jnp.maximum(jnp.max(s, axis=-1, keepdims=True), sink)
            e = jnp.exp(s - m)
            den = jnp.sum(e, axis=-1, keepdims=True) + jnp.exp(sink - m)
            o = _dot(e.astype(BF16), vcat) / den
            for g in range(GQ_A):
                heads.append((j, o[g * BLOCK:(g + 1) * BLOCK]))
        for blk in range(H_A // 2):
            (j0, o0), (j1, o1) = heads[2 * blk], heads[2 * blk + 1]
            lo = o0 if j0 == 0 else pltpu.roll(o0, HD_A, 1)
            hi = o1 if j1 == 1 else pltpu.roll(o1, HD_A, 1)
            o_ref[0, qrows, blk * LANES:(blk + 1) * LANES] = jnp.where(lane < HD_A, lo, hi).astype(o_ref.dtype)


def _attn_a(qa, ka, va, sink, ctx_kv):
    bsz, seq, _ = qa.shape
    nblk = seq // BLOCK
    latent = ctx_kv is not None
    kvw = KV_A * HD_A
    nb = A_STEP_BLOCKS if nblk % A_STEP_BLOCKS == 0 else nblk
    if latent:
        past = ctx_kv[0].shape[1]
        band = [pl.BlockSpec((1, BLOCK, kvw), lambda b, i: (b, jnp.maximum(nb * i - 1, 0), 0)),
                pl.BlockSpec((1, nb * BLOCK, kvw), lambda b, i: (b, i, 0)),
                pl.BlockSpec((1, BLOCK, kvw), lambda b, i: (b, jnp.minimum(nb * i + nb, nblk - 1), 0)),
                pl.BlockSpec((1, past, kvw), lambda b, i: (b, 0, 0))]
        kv_specs = band + band
        kv_args = [ka, ka, ka, ctx_kv[0], va, va, va, ctx_kv[1]]
    else:
        kv_specs = [pl.BlockSpec((1, seq, kvw), lambda b, i: (b, 0, 0))] * 2
        kv_args = [ka, va]
    return pl.pallas_call(
        functools.partial(_attn_a_body, latent=latent, nblk=nblk, nstep=nb),
        grid=(bsz, nblk // nb),
        in_specs=[pl.BlockSpec(memory_space=pltpu.SMEM),
                  pl.BlockSpec((1, nb * BLOCK, QA_COLS), lambda b, i: (b, i, 0))] + kv_specs,
        out_specs=pl.BlockSpec((1, nb * BLOCK, W_A), lambda b, i: (b, i, 0)),
        out_shape=jax.ShapeDtypeStruct((bsz, seq, W_A), BF16),
        compiler_params=_cparams("parallel", "parallel"),
        name="attn_a_latent" if latent else "attn_a_ctx",
    )(sink, qa, *kv_args)


def _attn_b_body(lam_ref, gain_ref, q_ref, *refs, part_lens, lam_init, kc):
    npart = len(part_lens)
    k_refs, v_refs = refs[:npart], refs[npart:2 * npart]
    o_ref, s_scr, vm_scr = refs[2 * npart:]
    tq = q_ref.shape[1]
    chunks = []
    col = 0
    for p, plen in enumerate(part_lens):
        step = min(kc, plen)
        for start in range(0, plen, step):
            chunks.append((p, start, col, step))
            col += step

    @pl.when(pl.program_id(2) == 0)
    def _():
        off = 0
        for p, plen in enumerate(part_lens):
            v = v_refs[p][0].astype(BF16)
            lane_v = lax.broadcasted_iota(jnp.int32, (plen, LANES), 1)
            for h in range(2):
                own = (lane_v >= h * HD_B) & (lane_v < (h + 1) * HD_B)
                ones = jnp.where(lane_v == (1 - h) * HD_B, 1.0, 0.0).astype(BF16)
                vm_scr[h, off:off + plen, :] = jnp.where(own, v, ones)
            off += plen

    lv = lam_ref[...]
    lam = (jnp.exp(jnp.sum(lv[0:1] * lv[1:2], axis=-1, keepdims=True))
           - jnp.exp(jnp.sum(lv[2:3] * lv[3:4], axis=-1, keepdims=True)) + lam_init)
    q = q_ref[0]
    lane_q = lax.broadcasted_iota(jnp.int32, (tq, LANES), 1)
    total = jnp.zeros((tq, LANES), F32)
    for h in range(2):
        qc = [jnp.where((lane_q >= h * HD_B + c * DC_B) & (lane_q < h * HD_B + (c + 1) * DC_B), q, jnp.zeros_like(q))
              for c in range(2)]
        rows = [slice(c * tq, (c + 1) * tq) for c in range(2)]
        macc = [None, None]
        for p, start, col, step in chunks:
            kch = k_refs[p][0, start:start + step, :].astype(BF16)
            for c in range(2):
                s = _dot_nt(qc[c], kch)
                s_scr[rows[c], col:col + step] = s
                for j in range(step // LANES):
                    t = s[:, j * LANES:(j + 1) * LANES]
                    macc[c] = t if macc[c] is None else jnp.maximum(macc[c], t)
        m = [jnp.max(macc[c], axis=-1, keepdims=True) for c in range(2)]
        acc = [jnp.zeros((tq, LANES), F32) for _ in range(2)]
        for p, start, col, step in chunks:
            vch = vm_scr[h, col:col + step, :]
            for c in range(2):
                e = jnp.exp2(s_scr[rows[c], col:col + step] - m[c]).astype(BF16)
                acc[c] = acc[c] + _dot(e, vch)
        o2 = [acc[c] / jnp.sum(jnp.where(lane_q == (1 - h) * HD_B, acc[c], 0.0), axis=-1, keepdims=True)
              for c in range(2)]
        own = (lane_q >= h * HD_B) & (lane_q < (h + 1) * HD_B)
        total = total + jnp.where(own, o2[0] - lam * o2[1], 0.0)
    sq = total * total
    ss_lo = jnp.sum(jnp.where(lane_q < HD_B, sq, 0.0), axis=-1, keepdims=True)
    ss_hi = jnp.sum(jnp.where(lane_q >= HD_B, sq, 0.0), axis=-1, keepdims=True)
    rinv = jnp.where(lane_q < HD_B, lax.rsqrt(ss_lo * (1.0 / HD_B) + EPS), lax.rsqrt(ss_hi * (1.0 / HD_B) + EPS))
    o_ref[0] = (total * rinv * gain_ref[...] * (1.0 - lam_init)).astype(o_ref.dtype)


def _attn_b(qb, k_parts, v_parts, lam_b, gain, lam_init):
    bsz, seq, _ = qb.shape
    tq = next(t for t in (1024, 512, 256) if seq % t == 0)
    part_lens = tuple(k.shape[1] for k in k_parts)
    lk = sum(part_lens)
    kv_specs = [pl.BlockSpec((1, n, LANES), lambda b, hp, i: (b, 0, hp)) for n in part_lens]
    return pl.pallas_call(
        functools.partial(_attn_b_body, part_lens=part_lens, lam_init=lam_init, kc=512),
        grid=(bsz, W_B // LANES, seq // tq),
        in_specs=[pl.BlockSpec((4, DC_B), lambda b, hp, i: (0, 0)),
                  pl.BlockSpec((1, LANES), lambda b, hp, i: (0, 0)),
                  pl.BlockSpec((1, tq, LANES), lambda b, hp, i: (b, i, hp))] + kv_specs + kv_specs,
        out_specs=pl.BlockSpec((1, tq, LANES), lambda b, hp, i: (b, i, hp)),
        out_shape=jax.ShapeDtypeStruct((bsz, seq, W_B), BF16),
        scratch_shapes=[pltpu.VMEM((2 * tq, lk), F32), pltpu.VMEM((2, lk, LANES), BF16)],
        compiler_params=_cparams("parallel", "parallel", "arbitrary"),
        name="attn_b_latent" if len(k_parts) > 1 else "attn_b_ctx",
    )(lam_b, gain, qb, *k_parts, *v_parts)


def _ssm_body(u_ref, m_ref, g_ref, cc_ref, a_ref, h0_ref, y_ref, fin_ref, s_scr, h_scr, *, nb, nc):
    u = u_ref[0]
    col = lambda k: slice(k * LANES, (k + 1) * LANES)
    s = _dot(u, g_ref[0])
    for k in range(4):
        s_scr[k] = s[:, col(k)]
    a = a_ref[0]
    afr, afi, abr, abi = (jnp.broadcast_to(a[k:k + 1], (nb, LANES)) for k in range(4))
    h0 = h0_ref[0]

    def step(c, carry):
        fr, fi, br, bi = carry
        rf = pl.ds(c, nb, stride=nc)
        rb = pl.ds(nc - 1 - c, nb, stride=nc)
        h_scr[0, rf, :] = fr
        h_scr[1, rf, :] = fi
        h_scr[2, rb, :] = br
        h_scr[3, rb, :] = bi
        nfr = afr * fr - afi * fi + s_scr[0, rf, :]
        nfi = afr * fi + afi * fr + s_scr[1, rf, :]
        nbr = abr * br - abi * bi + s_scr[2, rb, :]
        nbi = abr * bi + abi * br + s_scr[3, rb, :]
        return nfr, nfi, nbr, nbi

    fin = lax.fori_loop(0, nc, step, tuple(h0[:, col(k)] for k in range(4)), unroll=8)
    for k in range(4):
        fin_ref[0, :, col(k)] = fin[k]
    hin = jnp.concatenate([h_scr[k] for k in range(4)], axis=1).astype(BF16)
    y = _dot(u, m_ref[0]) + _dot(hin, cc_ref[0])
    y_ref[0] = y.astype(y_ref.dtype)


def _ssm(u_rows, mats, h0, nb):
    npair, rows, w = u_rows.shape
    nc = rows // nb
    mat_spec = pl.BlockSpec((1, w, w), lambda p: (p, 0, 0))
    return pl.pallas_call(
        functools.partial(_ssm_body, nb=nb, nc=nc),
        grid=(npair,),
        in_specs=[pl.BlockSpec((1, rows, w), lambda p: (p, 0, 0)), mat_spec, mat_spec, mat_spec,
                  pl.BlockSpec((1, 4, LANES), lambda p: (p, 0, 0)),
                  pl.BlockSpec((1, nb, w), lambda p: (p, 0, 0))],
        out_specs=[pl.BlockSpec((1, rows, w), lambda p: (p, 0, 0)),
                   pl.BlockSpec((1, nb, w), lambda p: (p, 0, 0))],
        out_shape=[jax.ShapeDtypeStruct((npair, rows, w), BF16),
                   jax.ShapeDtypeStruct((npair, nb, w), F32)],
        scratch_shapes=[pltpu.VMEM((4, rows, LANES), F32), pltpu.VMEM((4, rows, LANES), F32)],
        compiler_params=_cparams("parallel"),
        name="ssm_scan",
    )(u_rows, mats["m"], mats["g"], mats["cc"], mats["a16"], h0)


def _ssm_matrices(lp):
    t = SSM_T
    ks = jnp.arange(t + 1, dtype=F32)
    dirs = []
    for d in range(2):
        lam = lax.complex(lp["ssm_lam_re"][d].astype(F32), lp["ssm_lam_im"][d].astype(F32))
        dt = jnp.exp(lp["ssm_log_dt"][d].astype(F32))[:, None]
        a_bar = jnp.exp(lam * dt)
        b_bar = ((a_bar - 1.0) / lam)[..., None] * lax.complex(lp["ssm_b_re"][d].astype(F32),
                                                               lp["ssm_b_im"][d].astype(F32))
        c_mat = lax.complex(lp["ssm_c_re"][d].astype(F32), lp["ssm_c_im"][d].astype(F32))
        pw = jnp.exp((lam * dt)[None] * ks[:, None, None].astype(jnp.complex64))
        kern = jnp.real(jnp.einsum("gop,kgp,gpi->gkoi", c_mat, pw[:t], b_bar))
        dirs.append((pw, b_bar, c_mat, kern))
    (pw_f, bb_f, cm_f, k_f), (pw_b, bb_b, cm_b, k_b) = dirs
    eye2 = jnp.eye(2, dtype=F32)
    ch, pw2 = SSM_CH, 2 * SSM_CH
    hi = lax.Precision.HIGHEST

    def pair_bd(x):
        r, c = x.shape[1:]
        return jnp.einsum("pgrc,gh->pgrhc", x.reshape(N_PAIR, 2, r, c), eye2.astype(x.dtype)).reshape(N_PAIR, 2 * r, 2 * c)

    def pair_vec(x):
        return x.reshape(x.shape[0], N_PAIR, 2 * P_C).transpose(1, 0, 2)

    def lag_blocks(kern):
        x = kern.transpose(0, 1, 3, 2).reshape(N_PAIR, 2, t, ch, ch)
        return jnp.einsum("pglic,gh->plgihc", x, eye2).reshape(N_PAIR, t, pw2, pw2)
    kp_f, kp_b = lag_blocks(k_f), lag_blocks(k_b)
    d_blk = pair_bd(lp["ssm_d"].astype(F32)[:, :, None] * jnp.eye(ch, dtype=F32)[None])
    center = (kp_f[:, 0] + kp_b[:, 0] + d_blk)[:, None]
    band = jnp.concatenate([kp_b[:, :0:-1], center, kp_f[:, 1:]], axis=1)
    band = band.transpose(0, 2, 1, 3).reshape(N_PAIR, pw2, (2 * t - 1) * pw2)
    m_p = jnp.concatenate([band[:, :, (t - 1 - s) * pw2:(t - 1 - s) * pw2 + SSM_ROW] for s in range(t)], axis=1)

    def inject(pw_sel, b_bar):
        x1 = jnp.repeat(pair_vec(pw_sel), pw2, axis=1)
        x2 = jnp.tile(pair_bd(b_bar.transpose(0, 2, 1)), (1, t, 1))
        return x1 * x2
    g_f = inject(pw_f[t - 1 - jnp.arange(t)], bb_f)
    g_b = inject(pw_b[jnp.arange(t)], bb_b)
    g_p = jnp.concatenate([jnp.real(g_f), jnp.imag(g_f), jnp.real(g_b), jnp.imag(g_b)], axis=2)

    lane = jnp.arange(SSM_ROW)
    exp_t = (jnp.arange(t)[:, None] == lane[None, :] // pw2).astype(F32)
    exp_c = (jnp.arange(pw2)[:, None] == lane[None, :] % pw2).astype(F32)

    def widen(x, e):
        f = lambda v: jnp.einsum("pqk,kx->pqx", v, e, precision=hi)
        return lax.complex(f(jnp.real(x)), f(jnp.imag(x)))

    def readout(pw_sel, c_mat):
        y1 = widen(pair_vec(pw_sel).transpose(0, 2, 1), exp_t)
        y2 = widen(pair_bd(c_mat.transpose(0, 2, 1)), exp_c)
        return y1 * y2
    z_f = readout(pw_f[1 + jnp.arange(t)], cm_f)
    z_b = readout(pw_b[t - jnp.arange(t)], cm_b)
    cc_p = jnp.concatenate([jnp.real(z_f), -jnp.imag(z_f), jnp.real(z_b), -jnp.imag(z_b)], axis=1)
    a16 = jnp.stack([jnp.real(pw_f[t]), jnp.imag(pw_f[t]), jnp.real(pw_b[t]), jnp.imag(pw_b[t])], axis=0)
    a16 = a16.reshape(4, N_PAIR, 2 * P_C).transpose(1, 0, 2)
    return dict(m=m_p.astype(BF16), g=g_p.astype(BF16), cc=cc_p.astype(BF16), a16=a16)


def _ssm_state_rows(s_re, s_im):
    bsz = s_re.shape[0]
    parts = [s_re[:, 0], s_im[:, 0], s_re[:, 1], s_im[:, 1]]
    st = jnp.stack([p.reshape(bsz, N_PAIR, 2 * P_C) for p in parts], axis=2)
    return st.transpose(1, 0, 2, 3).reshape(N_PAIR, bsz, 8 * P_C).astype(F32)


def _ssm_state_unrows(fin):
    npair, bsz, _ = fin.shape
    st = fin.reshape(npair, bsz, 4, 2, P_C).transpose(1, 2, 0, 3, 4).reshape(bsz, 4, G_C, P_C)
    return jnp.stack([st[:, 0], st[:, 2]], axis=1), jnp.stack([st[:, 1], st[:, 3]], axis=1)


def _route(scores, bias):
    tm = scores.shape[1]
    biased = scores + bias
    iota8 = lax.broadcasted_iota(jnp.int32, (PER_GROUP, tm), 0)
    grp = [biased[PER_GROUP * g:PER_GROUP * (g + 1)] for g in range(N_EXP_GROUPS)]
    gscore = []
    for v in grp:
        m1 = jnp.max(v, axis=0, keepdims=True)
        first = jnp.min(jnp.where(v == m1, iota8, PER_GROUP), axis=0, keepdims=True)
        m2 = jnp.max(jnp.where(iota8 == first, -jnp.inf, v), axis=0, keepdims=True)
        gscore.append(m1 + m2)
    masked = []
    for g in range(N_EXP_GROUPS):
        rank = jnp.zeros((1, tm), jnp.int32)
        for o in range(N_EXP_GROUPS):
            if o == g:
                continue
            ahead = (gscore[o] >= gscore[g]) if o < g else (gscore[o] > gscore[g])
            rank = rank + jnp.where(ahead, 1, 0)
        masked.append(jnp.where(rank < TOPK_GROUPS, grp[g], -jnp.inf))
    chosen = [None] * N_EXP_GROUPS
    for _ in range(TOP_K):
        best = masked[0]
        for v in masked[1:]:
            best = jnp.maximum(best, v)
        best = jnp.max(best, axis=0, keepdims=True)
        first = jnp.full((1, tm), N_EXPERTS, jnp.int32)
        for g, v in enumerate(masked):
            cand = jnp.min(jnp.where(v == best, iota8 + PER_GROUP * g, N_EXPERTS), axis=0, keepdims=True)
            first = jnp.minimum(first, cand)
        for g in range(N_EXP_GROUPS):
            hit = (iota8 + PER_GROUP * g) == first
            chosen[g] = hit if chosen[g] is None else (chosen[g] | hit)
            masked[g] = jnp.where(hit, -jnp.inf, masked[g])
    w = [jnp.where(chosen[g], scores[PER_GROUP * g:PER_GROUP * (g + 1)], 0.0) for g in range(N_EXP_GROUPS)]
    wsum = w[0]
    for v in w[1:]:
        wsum = wsum + v
    wsum = jnp.sum(wsum, axis=0, keepdims=True)
    return jnp.concatenate([v / wsum * ROUTED_SCALE for v in w], axis=0)


def _post_body(x_ref, oa_ref, ob_ref, y_ref, mod_ref, wglu_ref, wout_ref, g2_ref, wrh_ref, wrl_ref, br_ref,
               x1_ref, h2_ref, gate_ref, y_scr):
    d = D_MODEL
    tm = x_ref.shape[1]
    pw = 2 * SSM_CH
    for t in range(SSM_T):
        for blk in range(W_C // LANES):
            piece = jnp.concatenate([y_ref[blk * (LANES // pw) + pp, :, t * pw:(t + 1) * pw].astype(F32)
                                     for pp in range(LANES // pw)], axis=1)
            y_scr[blk, pl.ds(t, tm // SSM_T, stride=SSM_T), :] = piece
    g = jax.nn.gelu(jnp.concatenate([y_scr[blk] for blk in range(W_C // LANES)], axis=1))
    oc = g * jax.nn.sigmoid(_dot(g.astype(BF16), wglu_ref[...]))
    mix = (_dot(oa_ref[0], wout_ref[0:W_A]) + _dot(ob_ref[0], wout_ref[W_A:W_A + W_B])
           + _dot(oc.astype(BF16), wout_ref[W_A + W_B:]))
    mod = mod_ref[0]
    x1 = x_ref[0] + mod[:, 2 * d:3 * d] * mix
    x1_ref[0] = x1
    xn = x1 * lax.rsqrt(jnp.mean(x1 * x1, axis=-1, keepdims=True) + EPS) * g2_ref[...]
    h2 = xn * (1.0 + mod[:, 4 * d:5 * d]) + mod[:, 3 * d:4 * d]
    h_hi, h_lo = _split_bf16(h2)
    h2_ref[0] = h_hi
    logits = _dot_nt(wrh_ref[...], h_hi) + _dot_nt(wrh_ref[...], h_lo) + _dot_nt(wrl_ref[...], h_hi)
    gate_ref[0] = _route(jax.nn.sigmoid(logits), br_ref[...]).T


def _post_mix(x, oa, ob, y, mod, w_glu, w_out, g2, wr_hi, wr_lo, b_r):
    bsz, seq, d = x.shape
    tm = next(t for t in (1024, 512, 256) if seq % t == 0)
    bm = mod.shape[0]
    mod_idx = (lambda b, i: (b, 0, 0)) if bm > 1 else (lambda b, i: (0, 0, 0))
    const2 = lambda b, i: (0, 0)
    tok = lambda w: pl.BlockSpec((1, tm, w), lambda b, i: (b, i, 0))
    nt = seq // tm
    return pl.pallas_call(
        _post_body,
        grid=(bsz, nt),
        in_specs=[tok(d), tok(W_A), tok(W_B),
                  pl.BlockSpec((N_PAIR, tm // SSM_T, SSM_ROW), lambda b, i: (0, b * nt + i, 0)),
                  pl.BlockSpec((1, 1, 6 * d), mod_idx),
                  pl.BlockSpec((W_C, W_C), const2),
                  pl.BlockSpec((d, d), const2),
                  pl.BlockSpec((1, d), const2),
                  pl.BlockSpec((N_EXPERTS, d), const2),
                  pl.BlockSpec((N_EXPERTS, d), const2),
                  pl.BlockSpec((N_EXPERTS, 1), const2)],
        out_specs=[tok(d), tok(d), tok(N_EXPERTS)],
        out_shape=[jax.ShapeDtypeStruct((bsz, seq, d), F32),
                   jax.ShapeDtypeStruct((bsz, seq, d), BF16),
                   jax.ShapeDtypeStruct((bsz, seq, N_EXPERTS), F32)],
        scratch_shapes=[pltpu.VMEM((W_C // LANES, tm, LANES), F32)],
        compiler_params=_cparams("parallel", "parallel"),
        name="post_mix",
    )(x, oa, ob, y, mod, w_glu, w_out, g2, wr_hi, wr_lo, b_r)


def _moe_body(x1_ref, h_ref, gate_ref, g2_ref, w1_ref, w3_ref, w2_ref, c1_ref, c32_ref, ex_ref,
              s1_ref, s3_ref, s2_ref, o_ref, acc_ref, h8_ref, hs_ref):
    j = pl.program_id(1)

    @pl.when(j == 0)
    def _():
        h = h_ref[...]
        a = _dot(h, s1_ref[...])
        acc_ref[...] = _dot((a * jax.nn.sigmoid(a) * _dot(h, s3_ref[...])).astype(BF16), s2_ref[...])
        hf = h.astype(F32)
        sc = jnp.maximum(jnp.max(jnp.abs(hf), axis=-1, keepdims=True), F8_TINY) * (1.0 / F8_RANGE)
        hs_ref[...] = sc
        h8_ref[...] = (hf * (1.0 / sc)).astype(F8)

    ne = w1_ref.shape[0]
    h8 = h8_ref[...]
    hs = hs_ref[...]
    a = _dot(h8, jnp.concatenate([w1_ref[e] for e in range(ne)], axis=1)) * c1_ref[...] * hs
    b = _dot(h8, jnp.concatenate([w3_ref[e] for e in range(ne)], axis=1))
    gexp = _dot(jnp.concatenate(_split_bf16(gate_ref[...]), axis=1), ex_ref[...])
    hid = a * jax.nn.sigmoid(a) * b * gexp * c32_ref[...]
    sc = jnp.maximum(jnp.max(jnp.abs(hid), axis=-1, keepdims=True), F8_TINY) * (1.0 / F8_RANGE)
    acc_ref[...] += _dot((hid * (1.0 / sc)).astype(F8), w2_ref[...]) * (sc * hs)

    @pl.when(j == pl.num_programs(1) - 1)
    def _():
        o_ref[...] = x1_ref[...] + g2_ref[0] * acc_ref[...]


def _moe(x1, h2, gates, mod, seq, ew, expand, ws1, ws3, ws2):
    tokens, d = x1.shape
    bm = mod.shape[0]
    span = seq if bm > 1 else tokens
    tm = next(t for t in (1024, 512, 256) if span % t == 0)
    per_b = seq // tm if bm > 1 else 1
    mod_idx = (lambda i, j: (i // per_b, 0, 5)) if bm > 1 else (lambda i, j: (0, 0, 5))
    ne = 8
    fc = ne * F_EXP
    hidden = ew["w2"].shape[0]
    const2 = lambda i, j: (0, 0)
    chunk_row = pl.BlockSpec((1, fc), lambda i, j: (0, j))
    return pl.pallas_call(
        _moe_body,
        grid=(tokens // tm, hidden // fc),
        in_specs=[pl.BlockSpec((tm, d), lambda i, j: (i, 0)),
                  pl.BlockSpec((tm, d), lambda i, j: (i, 0)),
                  pl.BlockSpec((tm, N_EXPERTS), lambda i, j: (i, 0)),
                  pl.BlockSpec((1, 1, d), mod_idx),
                  pl.BlockSpec((ne, d, F_EXP), lambda i, j: (j, 0, 0)),
                  pl.BlockSpec((ne, d, F_EXP), lambda i, j: (j, 0, 0)),
                  pl.BlockSpec((fc, d), lambda i, j: (j, 0)),
                  chunk_row, chunk_row,
                  pl.BlockSpec((2 * N_EXPERTS, fc), lambda i, j: (0, j)),
                  pl.BlockSpec((d, F_SHARED), const2),
                  pl.BlockSpec((d, F_SHARED), const2),
                  pl.BlockSpec((F_SHARED, d), const2)],
        out_specs=pl.BlockSpec((tm, d), lambda i, j: (i, 0)),
        out_shape=jax.ShapeDtypeStruct((tokens, d), F32),
        scratch_shapes=[pltpu.VMEM((tm, d), F32), pltpu.VMEM((tm, d), F8), pltpu.VMEM((tm, 1), F32)],
        compiler_params=_cparams("parallel", "arbitrary"),
        name="moe",
    )(x1, h2, gates, mod, ew["w1"], ew["w3"], ew["w2"], ew["c1"], ew["c32"], expand, ws1, ws3, ws2)


def _expert_fp8(w):
    sc = jnp.maximum(jnp.max(jnp.abs(w), axis=(1, 2)), F8_TINY) * (1.0 / F8_RANGE)
    return (w / sc[:, None, None]).astype(F8), jnp.repeat(sc, F_EXP)[None, :].astype(F32)


def _prep_experts(p):
    w1, c1 = _expert_fp8(p["w_e1"].astype(F32))
    w3, c3 = _expert_fp8(p["w_e3"].astype(F32))
    w2, c2 = _expert_fp8(p["w_e2"].astype(F32))
    return dict(w1=w1, w3=w3, w2=w2.reshape(N_EXPERTS * F_EXP, D_MODEL), c1=c1, c32=c3 * c2)


def _rope_tables(seq):
    pos = jnp.arange(seq)
    row = (pos // GRID_W).astype(F32)[:, None]
    colp = (pos % GRID_W).astype(F32)[:, None]
    lane = jnp.arange(LANES)

    def table(width):
        half, quarter = width // 2, width // 4
        i = lane % width
        freq = ROPE_BASE ** (-(2.0 * (i % quarter).astype(F32)) / half)
        ang = jnp.where((i // half) == 0, row, colp) * freq[None, :]
        sign = jnp.where((i % half) < quarter, -1.0, 1.0)
        return jnp.cos(ang), jnp.sin(ang) * sign[None, :]

    ca, sa = table(HD_A)
    cb, sb = table(DC_B)
    return ca, sa, cb, sb


def _prep_layer(p):
    d = D_MODEL
    w_in = p["w_in"]
    place = (jnp.arange(H_A)[:, None] // GQ_A == jnp.arange(KV_A)[None, :]).astype(w_in.dtype)
    qa_pad = w_in[:, :W_A].reshape(d, H_A, 1, HD_A) * place[None, :, :, None]
    w_in_p = jnp.concatenate([qa_pad.reshape(d, QA_COLS), w_in[:, W_A:]], axis=1).astype(BF16)
    gains = jnp.stack([jnp.tile(p["q_norm_a"], LANES // HD_A) * (HD_A ** -0.5),
                       jnp.tile(p["k_norm_a"], LANES // HD_A),
                       jnp.tile(p["q_norm_b"], LANES // DC_B) * (DC_B ** -0.5 * LOG2E),
                       jnp.tile(p["k_norm_b"], LANES // DC_B)], axis=0).astype(F32)
    lp = {k: p[k] for k in ("ssm_lam_re", "ssm_lam_im", "ssm_log_dt", "ssm_b_re", "ssm_b_im",
                            "ssm_c_re", "ssm_c_im", "ssm_d")}
    wr_hi, wr_lo = _split_bf16(p["w_router"].T.astype(F32))
    return dict(
        w_in_p=w_in_p, gains=gains,
        g1=p["norm1_g"].reshape(1, d).astype(F32), g2=p["norm2_g"].reshape(1, d).astype(F32),
        sink=p["sink_a"].astype(F32), lam_b=p["lam_b"].astype(F32),
        subln=jnp.tile(p["subln_b"], LANES // HD_B).reshape(1, LANES).astype(F32),
        ssm=_ssm_matrices(lp),
        w_glu=p["w_glu"].astype(BF16), w_out=p["w_out"].astype(BF16),
        wr_hi=wr_hi, wr_lo=wr_lo, b_r=p["b_router"].reshape(N_EXPERTS, 1).astype(F32),
        experts=_prep_experts(p),
        ws1=p["w_s1"].astype(BF16), ws3=p["w_s3"].astype(BF16), ws2=p["w_s2"].astype(BF16),
    )


def _trunk_layer(x, mod, lw, consts, ctx):
    bsz, seq, d = x.shape
    latent = ctx is not None
    rope = consts["rope"] if latent else None
    kv_dtype = BF16 if latent else F32
    grp = 1 if latent else next(g for g in (4, 2, 1) if bsz % g == 0)
    tok = lambda a: a.reshape(bsz // grp, grp * seq, a.shape[-1])
    per_seq = lambda a: a.reshape(bsz, seq, a.shape[-1])
    qa, ka, va, qb, kb, vb, u = _inproj(tok(x), mod, lw["g1"], lw["w_in_p"], lw["gains"],
                                        consts["seg64"], consts["seg32"], rope, kv_dtype)
    qa, ka, va, qb, kb, vb = (per_seq(a) for a in (qa, ka, va, qb, kb, vb))
    if latent:
        oa = _attn_a(qa, ka, va, lw["sink"], (ctx["ak"], ctx["av"]))
        ob = _attn_b(qb, [kb, ctx["bk"]], [vb, ctx["bv"]], lw["lam_b"], lw["subln"], lw["lam_init"])
        h0 = ctx["h0"]
    else:
        oa = _attn_a(qa, ka, va, lw["sink"], None)
        ob = _attn_b(qb, [kb], [vb], lw["lam_b"], lw["subln"], lw["lam_init"])
        h0 = jnp.zeros((N_PAIR, bsz, 8 * P_C), F32)
    y_rows, fin = _ssm(u, lw["ssm"], h0, bsz)
    x1, h2, gates = _post_mix(tok(x), tok(oa), tok(ob), y_rows, mod, lw["w_glu"], lw["w_out"], lw["g2"],
                                lw["wr_hi"], lw["wr_lo"], lw["b_r"])
    out = _moe(x1.reshape(bsz * seq, d), h2.reshape(bsz * seq, d), gates.reshape(bsz * seq, N_EXPERTS), mod, seq,
               lw["experts"], consts["expand"], lw["ws1"], lw["ws3"], lw["ws2"])
    return out.reshape(bsz, seq, d), (ka, va, kb, vb, fin)


def kernel(x_prompt, x_sample, cache_a_k, cache_a_v, cache_b_k, cache_b_v, state_ssm_re, state_ssm_im, c, c_ctx, norm1_g, norm2_g, w_ada, b_ada, w_in, q_norm_a, k_norm_a, sink_a, q_norm_b, k_norm_b, lam_b, subln_b, ssm_lam_re, ssm_lam_im, ssm_log_dt, ssm_b_re, ssm_b_im, ssm_c_re, ssm_c_im, ssm_d, w_glu, w_out, w_router, b_router, w_e1, w_e3, w_e2, w_s1, w_s3, w_s2):
    p = dict(norm1_g=norm1_g, norm2_g=norm2_g, w_in=w_in, q_norm_a=q_norm_a, k_norm_a=k_norm_a, sink_a=sink_a,
             q_norm_b=q_norm_b, k_norm_b=k_norm_b, lam_b=lam_b, subln_b=subln_b,
             ssm_lam_re=ssm_lam_re, ssm_lam_im=ssm_lam_im, ssm_log_dt=ssm_log_dt, ssm_b_re=ssm_b_re,
             ssm_b_im=ssm_b_im, ssm_c_re=ssm_c_re, ssm_c_im=ssm_c_im, ssm_d=ssm_d, w_glu=w_glu, w_out=w_out,
             w_router=w_router, b_router=b_router, w_e1=w_e1, w_e3=w_e3, w_e2=w_e2,
             w_s1=w_s1, w_s3=w_s3, w_s2=w_s2)
    depth = w_in.shape[0]
    bsz, seq, d = x_prompt.shape
    dbsz, dseq, _ = x_sample.shape
    past = cache_a_k.shape[3]

    mod_rows = 16
    cvec = jnp.concatenate([c.astype(F32), c_ctx.astype(F32)[None],
                            jnp.zeros((mod_rows - dbsz - 1, d), F32)], axis=0)
    mods = _modulation(cvec, w_ada.astype(F32), b_ada.astype(F32))

    lane = jnp.arange(LANES)
    hidden = N_EXPERTS * F_EXP
    consts = dict(
        rope=_rope_tables(dseq),
        seg64=(lane[:, None] // HD_A == lane[None, :] // HD_A).astype(BF16),
        seg32=(lane[:, None] // DC_B == lane[None, :] // DC_B).astype(BF16),
        expand=(jnp.arange(2 * N_EXPERTS)[:, None] % N_EXPERTS == jnp.arange(hidden)[None, :] // F_EXP).astype(BF16),
    )

    xp, xs = x_prompt, x_sample
    ak, av, bk, bv, sre, sim = [], [], [], [], [], []
    prepared = jax.vmap(_prep_layer)(p)
    for l in range(depth):
        lw = jax.tree.map(lambda v: v[l], prepared)
        lw["lam_init"] = 0.8 - 0.6 * math.exp(-0.3 * l)
        mod_lat = mods[l, :dbsz][:, None, :]
        mod_ctx = mods[l, dbsz:dbsz + 1][:, None, :]
        xp, (k_a, v_a, k_b, v_b, fin) = _trunk_layer(xp, mod_ctx, lw, consts, None)
        ak.append(k_a.reshape(bsz, seq, KV_A, HD_A).transpose(0, 2, 1, 3))
        av.append(v_a.reshape(bsz, seq, KV_A, HD_A).transpose(0, 2, 1, 3))
        bk.append(k_b.reshape(bsz, seq, H_B, 2, DC_B).transpose(0, 2, 3, 1, 4))
        bv.append(v_b.reshape(bsz, seq, H_B, HD_B).transpose(0, 2, 1, 3))
        f_re, f_im = _ssm_state_unrows(fin)
        sre.append(f_re)
        sim.append(f_im)
        ctx = dict(
            ak=cache_a_k[:, l].transpose(0, 2, 1, 3).reshape(dbsz, past, KV_A * HD_A).astype(BF16),
            av=cache_a_v[:, l].transpose(0, 2, 1, 3).reshape(dbsz, past, KV_A * HD_A).astype(BF16),
            bk=cache_b_k[:, l].transpose(0, 3, 1, 2, 4).reshape(dbsz, past, W_B).astype(BF16),
            bv=cache_b_v[:, l].transpose(0, 2, 1, 3).reshape(dbsz, past, W_B).astype(BF16),
            h0=_ssm_state_rows(state_ssm_re[:, l], state_ssm_im[:, l]),
        )
        xs, _ = _trunk_layer(xs, mod_lat, lw, consts, ctx)
    return (xp, xs, jnp.stack(ak, axis=1), jnp.stack(av, axis=1), jnp.stack(bk, axis=1),
            jnp.stack(bv, axis=1), jnp.stack(sre, axis=1), jnp.stack(sim, axis=1))
```

```python
import functools
import math

import jax
import jax.numpy as jnp
from jax import lax
from jax.experimental import pallas as pl
from jax.experimental.pallas import tpu as pltpu

F32 = jnp.float32
BF16 = jnp.bfloat16
F8 = jnp.float8_e4m3fn
F8_RANGE = 384.0
F8_TINY = 1e-30

D_MODEL = 1024
GRID_W = 64
BLOCK = 128
H_A, KV_A, HD_A = 6, 2, 64
GQ_A = H_A // KV_A
W_A = H_A * HD_A
H_B, HD_B = 4, 64
DC_B = HD_B // 2
W_B = H_B * HD_B
SSM_CH = 16
W_C = D_MODEL - W_A - W_B
G_C = W_C // SSM_CH
P_C = 64
N_EXPERTS, TOP_K, F_EXP, F_SHARED = 64, 6, 128, 256
N_EXP_GROUPS, TOPK_GROUPS = 8, 4
PER_GROUP = N_EXPERTS // N_EXP_GROUPS
ROUTED_SCALE = 2.5
ROPE_BASE = 10000.0
EPS = 1e-6
NEG = -1e30
LOG2E = 1.4426950408889634

LANES = 128
SSM_T = 16
N_PAIR = G_C // 2
SSM_ROW = 2 * SSM_T * SSM_CH
QA_COLS = H_A * LANES
IN_COLS_P = QA_COLS + 2 * KV_A * HD_A + 3 * W_B + W_C
VMEM_LIMIT = 56 << 20
A_STEP_BLOCKS = 8


def _cparams(*sem):
    return pltpu.CompilerParams(dimension_semantics=sem, vmem_limit_bytes=VMEM_LIMIT)


def _dot(a, b):
    return jnp.dot(a, b, preferred_element_type=F32)


def _dot_nt(a, b):
    return lax.dot_general(a, b, (((1,), (1,)), ((), ())), preferred_element_type=F32)


def _split_bf16(x):
    hi = x.astype(BF16)
    lo = (x - hi.astype(F32)).astype(BF16)
    return hi, lo


def _mod_body(c_ref, w_ref, b_ref, o_ref):
    c = c_ref[...]
    s = c * jax.nn.sigmoid(c)
    s_hi, s_lo = _split_bf16(s)
    w_hi, w_lo = _split_bf16(w_ref[0])
    o_ref[0] = _dot(s_hi, w_hi) + _dot(s_lo, w_hi) + _dot(s_hi, w_lo) + b_ref[0]


def _modulation(cvec, w_ada, b_ada):
    depth, d, n = w_ada.shape
    rows = cvec.shape[0]
    tn = 768
    return pl.pallas_call(
        _mod_body,
        grid=(depth, n // tn),
        in_specs=[pl.BlockSpec((rows, d), lambda l, j: (0, 0)),
                  pl.BlockSpec((1, d, tn), lambda l, j: (l, 0, j)),
                  pl.BlockSpec((1, 1, tn), lambda l, j: (l, 0, j))],
        out_specs=pl.BlockSpec((1, rows, tn), lambda l, j: (l, 0, j)),
        out_shape=jax.ShapeDtypeStruct((depth, rows, n), F32),
        compiler_params=_cparams("parallel", "parallel"),
        name="adaln_mod",
    )(cvec, w_ada, b_ada.reshape(depth, 1, n))


def _inproj_body(*refs, latent):
    if latent:
        (x_ref, mod_ref, g1_ref, w_ref, gains_ref, s64_ref, s32_ref, ca_ref, sa_ref, cb_ref, sb_ref,
         qa_ref, ka_ref, va_ref, qb_ref, kb_ref, vb_ref, u_ref, u_scr) = refs
    else:
        (x_ref, mod_ref, g1_ref, w_ref, gains_ref, s64_ref, s32_ref,
         qa_ref, ka_ref, va_ref, qb_ref, kb_ref, vb_ref, u_ref, u_scr) = refs
    d = D_MODEL
    x = x_ref[0]
    mod = mod_ref[0]
    xn = x * lax.rsqrt(jnp.mean(x * x, axis=-1, keepdims=True) + EPS) * g1_ref[...]
    h = xn * (1.0 + mod[:, d:2 * d]) + mod[:, 0:d]
    acc = _dot(h.astype(BF16), w_ref[...])

    tm = x.shape[0]
    lane = lax.broadcasted_iota(jnp.int32, (tm, LANES), 1)
    first_a = (lane % 32) < 16
    first_b = (lane % 16) < 8

    def normed(xb, seg_ref, inv_n, gain):
        ss = _dot((xb * xb).astype(BF16), seg_ref[...])
        return xb * lax.rsqrt(ss * inv_n + EPS) * gain

    def rope_a(y):
        if not latent:
            return y
        sw = jnp.where(first_a, pltpu.roll(y, LANES - 16, 1), pltpu.roll(y, 16, 1))
        return y * ca_ref[...] + sw * sa_ref[...]

    def rope_b(y):
        if not latent:
            return y
        sw = jnp.where(first_b, pltpu.roll(y, LANES - 8, 1), pltpu.roll(y, 8, 1))
        return y * cb_ref[...] + sw * sb_ref[...]

    gains = gains_ref[...]
    off = 0
    for b in range(H_A):
        y = normed(acc[:, off:off + LANES], s64_ref, 1.0 / HD_A, gains[0:1])
        qa_ref[0, :, b * LANES:(b + 1) * LANES] = rope_a(y).astype(qa_ref.dtype)
        off += LANES
    y = normed(acc[:, off:off + LANES], s64_ref, 1.0 / HD_A, gains[1:2])
    ka_ref[0] = rope_a(y).astype(ka_ref.dtype)
    off += LANES
    va_ref[0] = acc[:, off:off + LANES].astype(va_ref.dtype)
    off += LANES
    for b in range(W_B // LANES):
        y = normed(acc[:, off:off + LANES], s32_ref, 1.0 / DC_B, gains[2:3])
        qb_ref[0, :, b * LANES:(b + 1) * LANES] = rope_b(y).astype(qb_ref.dtype)
        off += LANES
    for b in range(W_B // LANES):
        y = normed(acc[:, off:off + LANES], s32_ref, 1.0 / DC_B, gains[3:4])
        kb_ref[0, :, b * LANES:(b + 1) * LANES] = rope_b(y).astype(kb_ref.dtype)
        off += LANES
    vb_ref[0] = acc[:, off:off + W_B].astype(vb_ref.dtype)
    off += W_B
    for blk in range(W_C // LANES):
        u_scr[blk] = acc[:, off + blk * LANES:off + (blk + 1) * LANES]
    pw = 2 * SSM_CH
    for t in range(SSM_T):
        for blk in range(W_C // LANES):
            xt = u_scr[blk, pl.ds(t, tm // SSM_T, stride=SSM_T), :]
            for pp in range(LANES // pw):
                u_ref[blk * (LANES // pw) + pp, :, t * pw:(t + 1) * pw] = xt[:, pp * pw:(pp + 1) * pw].astype(u_ref.dtype)


def _inproj(x, mod, g1, w_in_p, gains, seg64, seg32, rope, kv_dtype):
    bsz, seq, d = x.shape
    latent = rope is not None
    tm = next(t for t in (1024, 512, 256) if seq % t == 0)
    bm = mod.shape[0]
    mod_idx = (lambda b, i: (b, 0, 0)) if bm > 1 else (lambda b, i: (0, 0, 0))
    const2 = lambda b, i: (0, 0)
    tok = lambda w: pl.BlockSpec((1, tm, w), lambda b, i: (b, i, 0))
    in_specs = [tok(d),
                pl.BlockSpec((1, 1, 6 * d), mod_idx),
                pl.BlockSpec((1, d), const2),
                pl.BlockSpec((d, IN_COLS_P), const2),
                pl.BlockSpec((4, LANES), const2),
                pl.BlockSpec((LANES, LANES), const2),
                pl.BlockSpec((LANES, LANES), const2)]
    args = [x, mod, g1, w_in_p, gains, seg64, seg32]
    if latent:
        in_specs += [pl.BlockSpec((tm, LANES), lambda b, i: (i, 0))] * 4
        args += list(rope)
    widths = (QA_COLS, KV_A * HD_A, KV_A * HD_A, W_B, W_B, W_B)
    dtypes = (BF16, kv_dtype, kv_dtype, BF16, kv_dtype, kv_dtype)
    nt = seq // tm
    rows = tm // SSM_T
    u_spec = pl.BlockSpec((N_PAIR, rows, SSM_ROW), lambda b, i: (0, b * nt + i, 0))
    u_shape = jax.ShapeDtypeStruct((N_PAIR, bsz * seq // SSM_T, SSM_ROW), BF16)
    return pl.pallas_call(
        functools.partial(_inproj_body, latent=latent),
        grid=(bsz, nt),
        in_specs=in_specs,
        out_specs=[tok(w) for w in widths] + [u_spec],
        out_shape=[jax.ShapeDtypeStruct((bsz, seq, w), dt) for w, dt in zip(widths, dtypes)] + [u_shape],
        scratch_shapes=[pltpu.VMEM((W_C // LANES, tm, LANES), F32)],
        compiler_params=_cparams("parallel", "parallel"),
        name="inproj_latent" if latent else "inproj_ctx",
    )(*args)


def _attn_a_body(sink_ref, q_ref, *refs, latent, nblk, nstep):
    o_ref = refs[-1]
    nk = (len(refs) - 1) // 2
    ks = [r[0].astype(BF16) for r in refs[:nk]]
    vs = [r[0].astype(BF16) for r in refs[nk:2 * nk]]
    rows = GQ_A * BLOCK
    rowi = lax.broadcasted_iota(jnp.int32, (rows, 1), 0)
    lane = lax.broadcasted_iota(jnp.int32, (BLOCK, LANES), 1)
    if latent:
        cols = 3 * BLOCK + ks[3].shape[0]
        r = lax.broadcasted_iota(jnp.int32, (rows, cols), 0) & (BLOCK - 1)
        c = lax.broadcasted_iota(jnp.int32, (rows, cols), 1)
        own_k = [ks[1][t * BLOCK:(t + 1) * BLOCK] for t in range(nstep)]
        own_v = [vs[1][t * BLOCK:(t + 1) * BLOCK] for t in range(nstep)]
        band_k = [ks[0]] + own_k + [ks[2]]
        band_v = [vs[0]] + own_v + [vs[2]]
    for t in range(nstep):
        qrows = slice(t * BLOCK, (t + 1) * BLOCK)
        if latent:
            kcat = jnp.concatenate(band_k[t:t + 3] + [ks[3]], axis=0)
            vcat = jnp.concatenate(band_v[t:t + 3] + [vs[3]], axis=0)
            qblk = pl.program_id(1) * nstep + t
            p_off = jnp.where(qblk > 0, 0, 2 * BLOCK)
            n_off = jnp.where(qblk < nblk - 1, 0, 2 * BLOCK)
            prev_ok = (c >= r + p_off) | (c >= BLOCK)
            next_ok = ((c - 2 * BLOCK + n_off) <= r) | (c < 2 * BLOCK) | (c >= 3 * BLOCK)
            valid = prev_ok & next_ok
        else:
            kcat, vcat = ks[0], vs[0]
        heads = []
        for j in range(KV_A):
            q3 = jnp.concatenate([q_ref[0, qrows, (GQ_A * j + g) * LANES:(GQ_A * j + g + 1) * LANES]
                                  for g in range(GQ_A)], axis=0)
            s = _dot_nt(q3, kcat)
            if latent:
                s = jnp.where(valid, s, NEG)
            sink = jnp.where(rowi < BLOCK, sink_ref[GQ_A * j],
                             jnp.where(rowi < 2 * BLOCK, sink_ref[GQ_A * j + 1], sink_ref[GQ_A * j + 2]))
            m = jnp.maximum(jnp.max(s, axis=-1, keepdims=True), sink)
            e = jnp.exp(s - m)
            den = jnp.sum(e, axis=-1, keepdims=True) + jnp.exp(sink - m)
            o = _dot(e.astype(BF16), vcat) / den
            for g in range(GQ_A):
                heads.append((j, o[g * BLOCK:(g + 1) * BLOCK]))
        for blk in range(H_A // 2):
            (j0, o0), (j1, o1) = heads[2 * blk], heads[2 * blk + 1]
            lo = o0 if j0 == 0 else pltpu.roll(o0, HD_A, 1)
            hi = o1 if j1 == 1 else pltpu.roll(o1, HD_A, 1)
            o_ref[0, qrows, blk * LANES:(blk + 1) * LANES] = jnp.where(lane < HD_A, lo, hi).astype(o_ref.dtype)


def _attn_a(qa, ka, va, sink, ctx_kv):
    bsz, seq, _ = qa.shape
    nblk = seq // BLOCK
    latent = ctx_kv is not None
    kvw = KV_A * HD_A
    nb = A_STEP_BLOCKS if nblk % A_STEP_BLOCKS == 0 else nblk
    if latent:
        past = ctx_kv[0].shape[1]
        band = [pl.BlockSpec((1, BLOCK, kvw), lambda b, i: (b, jnp.maximum(nb * i - 1, 0), 0)),
                pl.BlockSpec((1, nb * BLOCK, kvw), lambda b, i: (b, i, 0)),
                pl.BlockSpec((1, BLOCK, kvw), lambda b, i: (b, jnp.minimum(nb * i + nb, nblk - 1), 0)),
                pl.BlockSpec((1, past, kvw), lambda b, i: (b, 0, 0))]
        kv_specs = band + band
        kv_args = [ka, ka, ka, ctx_kv[0], va, va, va, ctx_kv[1]]
    else:
        kv_specs = [pl.BlockSpec((1, seq, kvw), lambda b, i: (b, 0, 0))] * 2
        kv_args = [ka, va]
    return pl.pallas_call(
        functools.partial(_attn_a_body, latent=latent, nblk=nblk, nstep=nb),
        grid=(bsz, nblk // nb),
        in_specs=[pl.BlockSpec(memory_space=pltpu.SMEM),
                  pl.BlockSpec((1, nb * BLOCK, QA_COLS), lambda b, i: (b, i, 0))] + kv_specs,
        out_specs=pl.BlockSpec((1, nb * BLOCK, W_A), lambda b, i: (b, i, 0)),
        out_shape=jax.ShapeDtypeStruct((bsz, seq, W_A), BF16),
        compiler_params=_cparams("parallel", "parallel"),
        name="attn_a_latent" if latent else "attn_a_ctx",
    )(sink, qa, *kv_args)


def _attn_b_body(lam_ref, gain_ref, q_ref, *refs, part_lens, lam_init, kc):
    npart = len(part_lens)
    k_refs, v_refs = refs[:npart], refs[npart:2 * npart]
    o_ref, s_scr, vm_scr = refs[2 * npart:]
    tq = q_ref.shape[1]
    chunks = []
    col = 0
    for p, plen in enumerate(part_lens):
        step = min(kc, plen)
        for start in range(0, plen, step):
            chunks.append((p, start, col, step))
            col += step

    @pl.when(pl.program_id(2) == 0)
    def _():
        off = 0
        for p, plen in enumerate(part_lens):
            v = v_refs[p][0].astype(BF16)
            lane_v = lax.broadcasted_iota(jnp.int32, (plen, LANES), 1)
            for h in range(2):
                own = (lane_v >= h * HD_B) & (lane_v < (h + 1) * HD_B)
                ones = jnp.where(lane_v == (1 - h) * HD_B, 1.0, 0.0).astype(BF16)
                vm_scr[h, off:off + plen, :] = jnp.where(own, v, ones)
            off += plen

    lv = lam_ref[...]
    lam = (jnp.exp(jnp.sum(lv[0:1] * lv[1:2], axis=-1, keepdims=True))
           - jnp.exp(jnp.sum(lv[2:3] * lv[3:4], axis=-1, keepdims=True)) + lam_init)
    q = q_ref[0]
    lane_q = lax.broadcasted_iota(jnp.int32, (tq, LANES), 1)
    total = jnp.zeros((tq, LANES), F32)
    for h in range(2):
        qc = [jnp.where((lane_q >= h * HD_B + c * DC_B) & (lane_q < h * HD_B + (c + 1) * DC_B), q, jnp.zeros_like(q))
              for c in range(2)]
        rows = [slice(c * tq, (c + 1) * tq) for c in range(2)]
        macc = [None, None]
        for p, start, col, step in chunks:
            kch = k_refs[p][0, start:start + step, :].astype(BF16)
            for c in range(2):
                s = _dot_nt(qc[c], kch)
                s_scr[rows[c], col:col + step] = s
                for j in range(step // LANES):
                    t = s[:, j * LANES:(j + 1) * LANES]
                    macc[c] = t if macc[c] is None else jnp.maximum(macc[c], t)
        m = [jnp.max(macc[c], axis=-1, keepdims=True) for c in range(2)]
        acc = [jnp.zeros((tq, LANES), F32) for _ in range(2)]
        for p, start, col, step in chunks:
            vch = vm_scr[h, col:col + step, :]
            for c in range(2):
                e = jnp.exp2(s_scr[rows[c], col:col + step] - m[c]).astype(BF16)
                acc[c] = acc[c] + _dot(e, vch)
        o2 = [acc[c] / jnp.sum(jnp.where(lane_q == (1 - h) * HD_B, acc[c], 0.0), axis=-1, keepdims=True)
              for c in range(2)]
        own = (lane_q >= h * HD_B) & (lane_q < (h + 1) * HD_B)
        total = total + jnp.where(own, o2[0] - lam * o2[1], 0.0)
    sq = total * total
    ss_lo = jnp.sum(jnp.where(lane_q < HD_B, sq, 0.0), axis=-1, keepdims=True)
    ss_hi = jnp.sum(jnp.where(lane_q >= HD_B, sq, 0.0), axis=-1, keepdims=True)
    rinv = jnp.where(lane_q < HD_B, lax.rsqrt(ss_lo * (1.0 / HD_B) + EPS), lax.rsqrt(ss_hi * (1.0 / HD_B) + EPS))
    o_ref[0] = (total * rinv * gain_ref[...] * (1.0 - lam_init)).astype(o_ref.dtype)


def _attn_b(qb, k_parts, v_parts, lam_b, gain, lam_init):
    bsz, seq, _ = qb.shape
    tq = next(t for t in (1024, 512, 256) if seq % t == 0)
    part_lens = tuple(k.shape[1] for k in k_parts)
    lk = sum(part_lens)
    kv_specs = [pl.BlockSpec((1, n, LANES), lambda b, hp, i: (b, 0, hp)) for n in part_lens]
    return pl.pallas_call(
        functools.partial(_attn_b_body, part_lens=part_lens, lam_init=lam_init, kc=512),
        grid=(bsz, W_B // LANES, seq // tq),
        in_specs=[pl.BlockSpec((4, DC_B), lambda b, hp, i: (0, 0)),
                  pl.BlockSpec((1, LANES), lambda b, hp, i: (0, 0)),
                  pl.BlockSpec((1, tq, LANES), lambda b, hp, i: (b, i, hp))] + kv_specs + kv_specs,
        out_specs=pl.BlockSpec((1, tq, LANES), lambda b, hp, i: (b, i, hp)),
        out_shape=jax.ShapeDtypeStruct((bsz, seq, W_B), BF16),
        scratch_shapes=[pltpu.VMEM((2 * tq, lk), F32), pltpu.VMEM((2, lk, LANES), BF16)],
        compiler_params=_cparams("parallel", "parallel", "arbitrary"),
        name="attn_b_latent" if len(k_parts) > 1 else "attn_b_ctx",
    )(lam_b, gain, qb, *k_parts, *v_parts)


def _ssm_body(u_ref, m_ref, g_ref, cc_ref, a_ref, h0_ref, y_ref, fin_ref, s_scr, h_scr, *, nb, nc):
    u = u_ref[0]
    col = lambda k: slice(k * LANES, (k + 1) * LANES)
    s = _dot(u, g_ref[0])
    for k in range(4):
        s_scr[k] = s[:, col(k)]
    a = a_ref[0]
    afr, afi, abr, abi = (jnp.broadcast_to(a[k:k + 1], (nb, LANES)) for k in range(4))
    h0 = h0_ref[0]

    def step(c, carry):
        fr, fi, br, bi = carry
        rf = pl.ds(c, nb, stride=nc)
        rb = pl.ds(nc - 1 - c, nb, stride=nc)
        h_scr[0, rf, :] = fr
        h_scr[1, rf, :] = fi
        h_scr[2, rb, :] = br
        h_scr[3, rb, :] = bi
        nfr = afr * fr - afi * fi + s_scr[0, rf, :]
        nfi = afr * fi + afi * fr + s_scr[1, rf, :]
        nbr = abr * br - abi * bi + s_scr[2, rb, :]
        nbi = abr * bi + abi * br + s_scr[3, rb, :]
        return nfr, nfi, nbr, nbi

    fin = lax.fori_loop(0, nc, step, tuple(h0[:, col(k)] for k in range(4)), unroll=8)
    for k in range(4):
        fin_ref[0, :, col(k)] = fin[k]
    hin = jnp.concatenate([h_scr[k] for k in range(4)], axis=1).astype(BF16)
    y = _dot(u, m_ref[0]) + _dot(hin, cc_ref[0])
    y_ref[0] = y.astype(y_ref.dtype)


def _ssm(u_rows, mats, h0, nb):
    npair, rows, w = u_rows.shape
    nc = rows // nb
    mat_spec = pl.BlockSpec((1, w, w), lambda p: (p, 0, 0))
    return pl.pallas_call(
        functools.partial(_ssm_body, nb=nb, nc=nc),
        grid=(npair,),
        in_specs=[pl.BlockSpec((1, rows, w), lambda p: (p, 0, 0)), mat_spec, mat_spec, mat_spec,
                  pl.BlockSpec((1, 4, LANES), lambda p: (p, 0, 0)),
                  pl.BlockSpec((1, nb, w), lambda p: (p, 0, 0))],
        out_specs=[pl.BlockSpec((1, rows, w), lambda p: (p, 0, 0)),
                   pl.BlockSpec((1, nb, w), lambda p: (p, 0, 0))],
        out_shape=[jax.ShapeDtypeStruct((npair, rows, w), BF16),
                   jax.ShapeDtypeStruct((npair, nb, w), F32)],
        scratch_shapes=[pltpu.VMEM((4, rows, LANES), F32), pltpu.VMEM((4, rows, LANES), F32)],
        compiler_params=_cparams("parallel"),
        name="ssm_scan",
    )(u_rows, mats["m"], mats["g"], mats["cc"], mats["a16"], h0)


def _ssm_matrices(lp):
    t = SSM_T
    ks = jnp.arange(t + 1, dtype=F32)
    dirs = []
    for d in range(2):
        lam = lax.complex(lp["ssm_lam_re"][d].astype(F32), lp["ssm_lam_im"][d].astype(F32))
        dt = jnp.exp(lp["ssm_log_dt"][d].astype(F32))[:, None]
        a_bar = jnp.exp(lam * dt)
        b_bar = ((a_bar - 1.0) / lam)[..., None] * lax.complex(lp["ssm_b_re"][d].astype(F32),
                                                               lp["ssm_b_im"][d].astype(F32))
        c_mat = lax.complex(lp["ssm_c_re"][d].astype(F32), lp["ssm_c_im"][d].astype(F32))
        pw = jnp.exp((lam * dt)[None] * ks[:, None, None].astype(jnp.complex64))
        kern = jnp.real(jnp.einsum("gop,kgp,gpi->gkoi", c_mat, pw[:t], b_bar))
        dirs.append((pw, b_bar, c_mat, kern))
    (pw_f, bb_f, cm_f, k_f), (pw_b, bb_b, cm_b, k_b) = dirs
    eye2 = jnp.eye(2, dtype=F32)
    ch, pw2 = SSM_CH, 2 * SSM_CH
    hi = lax.Precision.HIGHEST

    def pair_bd(x):
        r, c = x.shape[1:]
        return jnp.einsum("pgrc,gh->pgrhc", x.reshape(N_PAIR, 2, r, c), eye2.astype(x.dtype)).reshape(N_PAIR, 2 * r, 2 * c)

    def pair_vec(x):
        return x.reshape(x.shape[0], N_PAIR, 2 * P_C).transpose(1, 0, 2)

    def lag_blocks(kern):
        x = kern.transpose(0, 1, 3, 2).reshape(N_PAIR, 2, t, ch, ch)
        return jnp.einsum("pglic,gh->plgihc", x, eye2).reshape(N_PAIR, t, pw2, pw2)
    kp_f, kp_b = lag_blocks(k_f), lag_blocks(k_b)
    d_blk = pair_bd(lp["ssm_d"].astype(F32)[:, :, None] * jnp.eye(ch, dtype=F32)[None])
    center = (kp_f[:, 0] + kp_b[:, 0] + d_blk)[:, None]
    band = jnp.concatenate([kp_b[:, :0:-1], center, kp_f[:, 1:]], axis=1)
    band = band.transpose(0, 2, 1, 3).reshape(N_PAIR, pw2, (2 * t - 1) * pw2)
    m_p = jnp.concatenate([band[:, :, (t - 1 - s) * pw2:(t - 1 - s) * pw2 + SSM_ROW] for s in range(t)], axis=1)

    def inject(pw_sel, b_bar):
        x1 = jnp.repeat(pair_vec(pw_sel), pw2, axis=1)
        x2 = jnp.tile(pair_bd(b_bar.transpose(0, 2, 1)), (1, t, 1))
        return x1 * x2
    g_f = inject(pw_f[t - 1 - jnp.arange(t)], bb_f)
    g_b = inject(pw_b[jnp.arange(t)], bb_b)
    g_p = jnp.concatenate([jnp.real(g_f), jnp.imag(g_f), jnp.real(g_b), jnp.imag(g_b)], axis=2)

    lane = jnp.arange(SSM_ROW)
    exp_t = (jnp.arange(t)[:, None] == lane[None, :] // pw2).astype(F32)
    exp_c = (jnp.arange(pw2)[:, None] == lane[None, :] % pw2).astype(F32)

    def widen(x, e):
        f = lambda v: jnp.einsum("pqk,kx->pqx", v, e, precision=hi)
        return lax.complex(f(jnp.real(x)), f(jnp.imag(x)))

    def readout(pw_sel, c_mat):
        y1 = widen(pair_vec(pw_sel).transpose(0, 2, 1), exp_t)
        y2 = widen(pair_bd(c_mat.transpose(0, 2, 1)), exp_c)
        return y1 * y2
    z_f = readout(pw_f[1 + jnp.arange(t)], cm_f)
    z_b = readout(pw_b[t - jnp.arange(t)], cm_b)
    cc_p = jnp.concatenate([jnp.real(z_f), -jnp.imag(z_f), jnp.real(z_b), -jnp.imag(z_b)], axis=1)
    a16 = jnp.stack([jnp.real(pw_f[t]), jnp.imag(pw_f[t]), jnp.real(pw_b[t]), jnp.imag(pw_b[t])], axis=0)
    a16 = a16.reshape(4, N_PAIR, 2 * P_C).transpose(1, 0, 2)
    return dict(m=m_p.astype(BF16), g=g_p.astype(BF16), cc=cc_p.astype(BF16), a16=a16)


def _ssm_state_rows(s_re, s_im):
    bsz = s_re.shape[0]
    parts = [s_re[:, 0], s_im[:, 0], s_re[:, 1], s_im[:, 1]]
    st = jnp.stack([p.reshape(bsz, N_PAIR, 2 * P_C) for p in parts], axis=2)
    return st.transpose(1, 0, 2, 3).reshape(N_PAIR, bsz, 8 * P_C).astype(F32)


def _ssm_state_unrows(fin):
    npair, bsz, _ = fin.shape
    st = fin.reshape(npair, bsz, 4, 2, P_C).transpose(1, 2, 0, 3, 4).reshape(bsz, 4, G_C, P_C)
    return jnp.stack([st[:, 0], st[:, 2]], axis=1), jnp.stack([st[:, 1], st[:, 3]], axis=1)


def _route(scores, bias):
    tm = scores.shape[1]
    biased = scores + bias
    iota8 = lax.broadcasted_iota(jnp.int32, (PER_GROUP, tm), 0)
    grp = [biased[PER_GROUP * g:PER_GROUP * (g + 1)] for g in range(N_EXP_GROUPS)]
    gscore = []
    for v in grp:
        m1 = jnp.max(v, axis=0, keepdims=True)
        first = jnp.min(jnp.where(v == m1, iota8, PER_GROUP), axis=0, keepdims=True)
        m2 = jnp.max(jnp.where(iota8 == first, -jnp.inf, v), axis=0, keepdims=True)
        gscore.append(m1 + m2)
    masked = []
    for g in range(N_EXP_GROUPS):
        rank = jnp.zeros((1, tm), jnp.int32)
        for o in range(N_EXP_GROUPS):
            if o == g:
                continue
            ahead = (gscore[o] >= gscore[g]) if o < g else (gscore[o] > gscore[g])
            rank = rank + jnp.where(ahead, 1, 0)
        masked.append(jnp.where(rank < TOPK_GROUPS, grp[g], -jnp.inf))
    chosen = [None] * N_EXP_GROUPS
    for _ in range(TOP_K):
        best = masked[0]
        for v in masked[1:]:
            best = jnp.maximum(best, v)
        best = jnp.max(best, axis=0, keepdims=True)
        first = jnp.full((1, tm), N_EXPERTS, jnp.int32)
        for g, v in enumerate(masked):
            cand = jnp.min(jnp.where(v == best, iota8 + PER_GROUP * g, N_EXPERTS), axis=0, keepdims=True)
            first = jnp.minimum(first, cand)
        for g in range(N_EXP_GROUPS):
            hit = (iota8 + PER_GROUP * g) == first
            chosen[g] = hit if chosen[g] is None else (chosen[g] | hit)
            masked[g] = jnp.where(hit, -jnp.inf, masked[g])
    w = [jnp.where(chosen[g], scores[PER_GROUP * g:PER_GROUP * (g + 1)], 0.0) for g in range(N_EXP_GROUPS)]
    wsum = w[0]
    for v in w[1:]:
        wsum = wsum + v
    wsum = jnp.sum(wsum, axis=0, keepdims=True)
    return jnp.concatenate([v / wsum * ROUTED_SCALE for v in w], axis=0)


def _post_body(x_ref, oa_ref, ob_ref, y_ref, mod_ref, wglu_ref, wout_ref, g2_ref, wrh_ref, wrl_ref, br_ref,
               x1_ref, h2_ref, gate_ref, y_scr):
    d = D_MODEL
    tm = x_ref.shape[1]
    pw = 2 * SSM_CH
    for t in range(SSM_T):
        for blk in range(W_C // LANES):
            piece = jnp.concatenate([y_ref[blk * (LANES // pw) + pp, :, t * pw:(t + 1) * pw].astype(F32)
                                     for pp in range(LANES // pw)], axis=1)
            y_scr[blk, pl.ds(t, tm // SSM_T, stride=SSM_T), :] = piece
    g = jax.nn.gelu(jnp.concatenate([y_scr[blk] for blk in range(W_C // LANES)], axis=1))
    oc = g * jax.nn.sigmoid(_dot(g.astype(BF16), wglu_ref[...]))
    mix = (_dot(oa_ref[0], wout_ref[0:W_A]) + _dot(ob_ref[0], wout_ref[W_A:W_A + W_B])
           + _dot(oc.astype(BF16), wout_ref[W_A + W_B:]))
    mod = mod_ref[0]
    x1 = x_ref[0] + mod[:, 2 * d:3 * d] * mix
    x1_ref[0] = x1
    xn = x1 * lax.rsqrt(jnp.mean(x1 * x1, axis=-1, keepdims=True) + EPS) * g2_ref[...]
    h2 = xn * (1.0 + mod[:, 4 * d:5 * d]) + mod[:, 3 * d:4 * d]
    h_hi, h_lo = _split_bf16(h2)
    h2_ref[0] = h_hi
    logits = _dot_nt(wrh_ref[...], h_hi) + _dot_nt(wrh_ref[...], h_lo) + _dot_nt(wrl_ref[...], h_hi)
    gate_ref[0] = _route(jax.nn.sigmoid(logits), br_ref[...]).T


def _post_mix(x, oa, ob, y, mod, w_glu, w_out, g2, wr_hi, wr_lo, b_r):
    bsz, seq, d = x.shape
    tm = next(t for t in (1024, 512, 256) if seq % t == 0)
    bm = mod.shape[0]
    mod_idx = (lambda b, i: (b, 0, 0)) if bm > 1 else (lambda b, i: (0, 0, 0))
    const2 = lambda b, i: (0, 0)
    tok = lambda w: pl.BlockSpec((1, tm, w), lambda b, i: (b, i, 0))
    nt = seq // tm
    return pl.pallas_call(
        _post_body,
        grid=(bsz, nt),
        in_specs=[tok(d), tok(W_A), tok(W_B),
                  pl.BlockSpec((N_PAIR, tm // SSM_T, SSM_ROW), lambda b, i: (0, b * nt + i, 0)),
                  pl.BlockSpec((1, 1, 6 * d), mod_idx),
                  pl.BlockSpec((W_C, W_C), const2),
                  pl.BlockSpec((d, d), const2),
                  pl.BlockSpec((1, d), const2),
                  pl.BlockSpec((N_EXPERTS, d), const2),
                  pl.BlockSpec((N_EXPERTS, d), const2),
                  pl.BlockSpec((N_EXPERTS, 1), const2)],
        out_specs=[tok(d), tok(d), tok(N_EXPERTS)],
        out_shape=[jax.ShapeDtypeStruct((bsz, seq, d), F32),
                   jax.ShapeDtypeStruct((bsz, seq, d), BF16),
                   jax.ShapeDtypeStruct((bsz, seq, N_EXPERTS), F32)],
        scratch_shapes=[pltpu.VMEM((W_C // LANES, tm, LANES), F32)],
        compiler_params=_cparams("parallel", "parallel"),
        name="post_mix",
    )(x, oa, ob, y, mod, w_glu, w_out, g2, wr_hi, wr_lo, b_r)


def _moe_body(x1_ref, h_ref, gate_ref, g2_ref, w1_ref, w3_ref, w2_ref, c1_ref, c32_ref, ex_ref,
              s1_ref, s3_ref, s2_ref, o_ref, acc_ref, h8_ref, hs_ref):
    j = pl.program_id(1)

    @pl.when(j == 0)
    def _():
        h = h_ref[...]
        a = _dot(h, s1_ref[...])
        acc_ref[...] = _dot((a * jax.nn.sigmoid(a) * _dot(h, s3_ref[...])).astype(BF16), s2_ref[...])
        hf = h.astype(F32)
        sc = jnp.maximum(jnp.max(jnp.abs(hf), axis=-1, keepdims=True), F8_TINY) * (1.0 / F8_RANGE)
        hs_ref[...] = sc
        h8_ref[...] = (hf * (1.0 / sc)).astype(F8)

    ne = w1_ref.shape[0]
    h8 = h8_ref[...]
    hs = hs_ref[...]
    a = _dot(h8, jnp.concatenate([w1_ref[e] for e in range(ne)], axis=1)) * c1_ref[...] * hs
    b = _dot(h8, jnp.concatenate([w3_ref[e] for e in range(ne)], axis=1))
    gexp = _dot(jnp.concatenate(_split_bf16(gate_ref[...]), axis=1), ex_ref[...])
    hid = a * jax.nn.sigmoid(a) * b * gexp * c32_ref[...]
    sc = jnp.maximum(jnp.max(jnp.abs(hid), axis=-1, keepdims=True), F8_TINY) * (1.0 / F8_RANGE)
    acc_ref[...] += _dot((hid * (1.0 / sc)).astype(F8), w2_ref[...]) * (sc * hs)

    @pl.when(j == pl.num_programs(1) - 1)
    def _():
        o_ref[...] = x1_ref[...] + g2_ref[0] * acc_ref[...]


def _moe(x1, h2, gates, mod, seq, ew, expand, ws1, ws3, ws2):
    tokens, d = x1.shape
    bm = mod.shape[0]
    span = seq if bm > 1 else tokens
    tm = next(t for t in (1024, 512, 256) if span % t == 0)
    per_b = seq // tm if bm > 1 else 1
    mod_idx = (lambda i, j: (i // per_b, 0, 5)) if bm > 1 else (lambda i, j: (0, 0, 5))
    ne = 8
    fc = ne * F_EXP
    hidden = ew["w2"].shape[0]
    const2 = lambda i, j: (0, 0)
    chunk_row = pl.BlockSpec((1, fc), lambda i, j: (0, j))
    return pl.pallas_call(
        _moe_body,
        grid=(tokens // tm, hidden // fc),
        in_specs=[pl.BlockSpec((tm, d), lambda i, j: (i, 0)),
                  pl.BlockSpec((tm, d), lambda i, j: (i, 0)),
                  pl.BlockSpec((tm, N_EXPERTS), lambda i, j: (i, 0)),
                  pl.BlockSpec((1, 1, d), mod_idx),
                  pl.BlockSpec((ne, d, F_EXP), lambda i, j: (j, 0, 0)),
                  pl.BlockSpec((ne, d, F_EXP), lambda i, j: (j, 0, 0)),
                  pl.BlockSpec((fc, d), lambda i, j: (j, 0)),
                  chunk_row, chunk_row,
                  pl.BlockSpec((2 * N_EXPERTS, fc), lambda i, j: (0, j)),
                  pl.BlockSpec((d, F_SHARED), const2),
                  pl.BlockSpec((d, F_SHARED), const2),
                  pl.BlockSpec((F_SHARED, d), const2)],
        out_specs=pl.BlockSpec((tm, d), lambda i, j: (i, 0)),
        out_shape=jax.ShapeDtypeStruct((tokens, d), F32),
        scratch_shapes=[pltpu.VMEM((tm, d), F32), pltpu.VMEM((tm, d), F8), pltpu.VMEM((tm, 1), F32)],
        compiler_params=_cparams("parallel", "arbitrary"),
        name="moe",
    )(x1, h2, gates, mod, ew["w1"], ew["w3"], ew["w2"], ew["c1"], ew["c32"], expand, ws1, ws3, ws2)


def _expert_fp8_body(w_ref, q_ref, c_ref):
    for e in range(w_ref.shape[1]):
        w = w_ref[0, e]
        top = jnp.max(jnp.max(jnp.abs(w), axis=0, keepdims=True), axis=1, keepdims=True)
        sc = jnp.maximum(top, F8_TINY) * (1.0 / F8_RANGE)
        q_ref[0, e] = (w * (1.0 / sc)).astype(F8)
        c_ref[0, :, e * F_EXP:(e + 1) * F_EXP] = jnp.broadcast_to(sc, (1, F_EXP))


def _expert_fp8(w):
    depth, ne, r, c = w.shape
    blk = 8
    return pl.pallas_call(
        _expert_fp8_body,
        grid=(depth, ne // blk),
        in_specs=[pl.BlockSpec((1, blk, r, c), lambda l, i: (l, i, 0, 0))],
        out_specs=[pl.BlockSpec((1, blk, r, c), lambda l, i: (l, i, 0, 0)),
                   pl.BlockSpec((1, 1, blk * F_EXP), lambda l, i: (l, 0, i))],
        out_shape=[jax.ShapeDtypeStruct((depth, ne, r, c), F8), jax.ShapeDtypeStruct((depth, 1, ne * F_EXP), F32)],
        compiler_params=_cparams("parallel", "parallel"),
        name="expert_fp8",
    )(w)


def _prep_experts(p):
    w1, c1 = _expert_fp8(p["w_e1"].astype(F32))
    w3, c3 = _expert_fp8(p["w_e3"].astype(F32))
    w2, c2 = _expert_fp8(p["w_e2"].astype(F32))
    return dict(w1=w1, w3=w3, w2=w2.reshape(w2.shape[0], N_EXPERTS * F_EXP, D_MODEL), c1=c1, c32=c3 * c2)


def _rope_tables(seq):
    pos = jnp.arange(seq)
    row = (pos // GRID_W).astype(F32)[:, None]
    colp = (pos % GRID_W).astype(F32)[:, None]
    lane = jnp.arange(LANES)

    def table(width):
        half, quarter = width // 2, width // 4
        i = lane % width
        freq = ROPE_BASE ** (-(2.0 * (i % quarter).astype(F32)) / half)
        ang = jnp.where((i // half) == 0, row, colp) * freq[None, :]
        sign = jnp.where((i % half) < quarter, -1.0, 1.0)
        return jnp.cos(ang), jnp.sin(ang) * sign[None, :]

    ca, sa = table(HD_A)
    cb, sb = table(DC_B)
    return ca, sa, cb, sb


def _prep_layer(p):
    d = D_MODEL
    w_in = p["w_in"]
    place = (jnp.arange(H_A)[:, None] // GQ_A == jnp.arange(KV_A)[None, :]).astype(w_in.dtype)
    qa_pad = w_in[:, :W_A].reshape(d, H_A, 1, HD_A) * place[None, :, :, None]
    w_in_p = jnp.concatenate([qa_pad.reshape(d, QA_COLS), w_in[:, W_A:]], axis=1).astype(BF16)
    gains = jnp.stack([jnp.tile(p["q_norm_a"], LANES // HD_A) * (HD_A ** -0.5),
                       jnp.tile(p["k_norm_a"], LANES // HD_A),
                       jnp.tile(p["q_norm_b"], LANES // DC_B) * (DC_B ** -0.5 * LOG2E),
                       jnp.tile(p["k_norm_b"], LANES // DC_B)], axis=0).astype(F32)
    lp = {k: p[k] for k in ("ssm_lam_re", "ssm_lam_im", "ssm_log_dt", "ssm_b_re", "ssm_b_im",
                            "ssm_c_re", "ssm_c_im", "ssm_d")}
    wr_hi, wr_lo = _split_bf16(p["w_router"].T.astype(F32))
    return dict(
        w_in_p=w_in_p, gains=gains,
        g1=p["norm1_g"].reshape(1, d).astype(F32), g2=p["norm2_g"].reshape(1, d).astype(F32),
        sink=p["sink_a"].astype(F32), lam_b=p["lam_b"].astype(F32),
        subln=jnp.tile(p["subln_b"], LANES // HD_B).reshape(1, LANES).astype(F32),
        ssm=_ssm_matrices(lp),
        w_glu=p["w_glu"].astype(BF16), w_out=p["w_out"].astype(BF16),
        wr_hi=wr_hi, wr_lo=wr_lo, b_r=p["b_router"].reshape(N_EXPERTS, 1).astype(F32),
        ws1=p["w_s1"].astype(BF16), ws3=p["w_s3"].astype(BF16), ws2=p["w_s2"].astype(BF16),
    )


def _trunk_layer(x, mod, lw, consts, ctx):
    bsz, seq, d = x.shape
    latent = ctx is not None
    rope = consts["rope"] if latent else None
    kv_dtype = BF16 if latent else F32
    grp = 1 if latent else next(g for g in (4, 2, 1) if bsz % g == 0)
    tok = lambda a: a.reshape(bsz // grp, grp * seq, a.shape[-1])
    per_seq = lambda a: a.reshape(bsz, seq, a.shape[-1])
    qa, ka, va, qb, kb, vb, u = _inproj(tok(x), mod, lw["g1"], lw["w_in_p"], lw["gains"],
                                        consts["seg64"], consts["seg32"], rope, kv_dtype)
    qa, ka, va, qb, kb, vb = (per_seq(a) for a in (qa, ka, va, qb, kb, vb))
    if latent:
        oa = _attn_a(qa, ka, va, lw["sink"], (ctx["ak"], ctx["av"]))
        ob = _attn_b(qb, [kb, ctx["bk"]], [vb, ctx["bv"]], lw["lam_b"], lw["subln"], lw["lam_init"])
        h0 = ctx["h0"]
    else:
        oa = _attn_a(qa, ka, va, lw["sink"], None)
        ob = _attn_b(qb, [kb], [vb], lw["lam_b"], lw["subln"], lw["lam_init"])
        h0 = jnp.zeros((N_PAIR, bsz, 8 * P_C), F32)
    y_rows, fin = _ssm(u, lw["ssm"], h0, bsz)
    x1, h2, gates = _post_mix(tok(x), tok(oa), tok(ob), y_rows, mod, lw["w_glu"], lw["w_out"], lw["g2"],
                                lw["wr_hi"], lw["wr_lo"], lw["b_r"])
    out = _moe(x1.reshape(bsz * seq, d), h2.reshape(bsz * seq, d), gates.reshape(bsz * seq, N_EXPERTS), mod, seq,
               lw["experts"], consts["expand"], lw["ws1"], lw["ws3"], lw["ws2"])
    return out.reshape(bsz, seq, d), (ka, va, kb, vb, fin)


def kernel(x_prompt, x_sample, cache_a_k, cache_a_v, cache_b_k, cache_b_v, state_ssm_re, state_ssm_im, c, c_ctx, norm1_g, norm2_g, w_ada, b_ada, w_in, q_norm_a, k_norm_a, sink_a, q_norm_b, k_norm_b, lam_b, subln_b, ssm_lam_re, ssm_lam_im, ssm_log_dt, ssm_b_re, ssm_b_im, ssm_c_re, ssm_c_im, ssm_d, w_glu, w_out, w_router, b_router, w_e1, w_e3, w_e2, w_s1, w_s3, w_s2):
    p = dict(norm1_g=norm1_g, norm2_g=norm2_g, w_in=w_in, q_norm_a=q_norm_a, k_norm_a=k_norm_a, sink_a=sink_a,
             q_norm_b=q_norm_b, k_norm_b=k_norm_b, lam_b=lam_b, subln_b=subln_b,
             ssm_lam_re=ssm_lam_re, ssm_lam_im=ssm_lam_im, ssm_log_dt=ssm_log_dt, ssm_b_re=ssm_b_re,
             ssm_b_im=ssm_b_im, ssm_c_re=ssm_c_re, ssm_c_im=ssm_c_im, ssm_d=ssm_d, w_glu=w_glu, w_out=w_out,
             w_router=w_router, b_router=b_router, w_e1=w_e1, w_e3=w_e3, w_e2=w_e2,
             w_s1=w_s1, w_s3=w_s3, w_s2=w_s2)
    depth = w_in.shape[0]
    bsz, seq, d = x_prompt.shape
    dbsz, dseq, _ = x_sample.shape
    past = cache_a_k.shape[3]

    mod_rows = 16
    cvec = jnp.concatenate([c.astype(F32), c_ctx.astype(F32)[None],
                            jnp.zeros((mod_rows - dbsz - 1, d), F32)], axis=0)
    mods = _modulation(cvec, w_ada.astype(F32), b_ada.astype(F32))

    lane = jnp.arange(LANES)
    hidden = N_EXPERTS * F_EXP
    consts = dict(
        rope=_rope_tables(dseq),
        seg64=(lane[:, None] // HD_A == lane[None, :] // HD_A).astype(BF16),
        seg32=(lane[:, None] // DC_B == lane[None, :] // DC_B).astype(BF16),
        expand=(jnp.arange(2 * N_EXPERTS)[:, None] % N_EXPERTS == jnp.arange(hidden)[None, :] // F_EXP).astype(BF16),
    )

    xp, xs = x_prompt, x_sample
    ak, av, bk, bv, sre, sim = [], [], [], [], [], []
    prepared = jax.vmap(_prep_layer)(p)
    prepared["experts"] = _prep_experts(p)
    for l in range(depth):
        lw = jax.tree.map(lambda v: v[l], prepared)
        lw["lam_init"] = 0.8 - 0.6 * math.exp(-0.3 * l)
        mod_lat = mods[l, :dbsz][:, None, :]
        mod_ctx = mods[l, dbsz:dbsz + 1][:, None, :]
        xp, (k_a, v_a, k_b, v_b, fin) = _trunk_layer(xp, mod_ctx, lw, consts, None)
        ak.append(k_a.reshape(bsz, seq, KV_A, HD_A).transpose(0, 2, 1, 3))
        av.append(v_a.reshape(bsz, seq, KV_A, HD_A).transpose(0, 2, 1, 3))
        bk.append(k_b.reshape(bsz, seq, H_B, 2, DC_B).transpose(0, 2, 3, 1, 4))
        bv.append(v_b.reshape(bsz, seq, H_B, HD_B).transpose(0, 2, 1, 3))
        f_re, f_im = _ssm_state_unrows(fin)
        sre.append(f_re)
        sim.append(f_im)
        ctx = dict(
            ak=cache_a_k[:, l].transpose(0, 2, 1, 3).reshape(dbsz, past, KV_A * HD_A).astype(BF16),
            av=cache_a_v[:, l].transpose(0, 2, 1, 3).reshape(dbsz, past, KV_A * HD_A).astype(BF16),
            bk=cache_b_k[:, l].transpose(0, 3, 1, 2, 4).reshape(dbsz, past, W_B).astype(BF16),
            bv=cache_b_v[:, l].transpose(0, 2, 1, 3).reshape(dbsz, past, W_B).astype(BF16),
            h0=_ssm_state_rows(state_ssm_re[:, l], state_ssm_im[:, l]),
        )
        xs, _ = _trunk_layer(xs, mod_lat, lw, consts, ctx)
    return (xp, xs, jnp.stack(ak, axis=1), jnp.stack(av, axis=1), jnp.stack(bk, axis=1),
            jnp.stack(bv, axis=1), jnp.stack(sre, axis=1), jnp.stack(sim, axis=1))
```

```python
import functools
import math

import jax
import jax.numpy as jnp
from jax import lax
from jax.experimental import pallas as pl
from jax.experimental.pallas import tpu as pltpu

F32 = jnp.float32
BF16 = jnp.bfloat16
F8 = jnp.float8_e4m3fn
F8_RANGE = 384.0
F8_TINY = 1e-30

D_MODEL = 1024
GRID_W = 64
BLOCK = 128
H_A, KV_A, HD_A = 6, 2, 64
GQ_A = H_A // KV_A
W_A = H_A * HD_A
H_B, HD_B = 4, 64
DC_B = HD_B // 2
W_B = H_B * HD_B
SSM_CH = 16
W_C = D_MODEL - W_A - W_B
G_C = W_C // SSM_CH
P_C = 64
N_EXPERTS, TOP_K, F_EXP, F_SHARED = 64, 6, 128, 256
N_EXP_GROUPS, TOPK_GROUPS = 8, 4
PER_GROUP = N_EXPERTS // N_EXP_GROUPS
ROUTED_SCALE = 2.5
ROPE_BASE = 10000.0
EPS = 1e-6
NEG = -1e30
LOG2E = 1.4426950408889634

LANES = 128
SSM_T = 16
N_PAIR = G_C // 2
SSM_ROW = 2 * SSM_T * SSM_CH
QA_COLS = H_A * LANES
IN_COLS_P = QA_COLS + 2 * KV_A * HD_A + 3 * W_B + W_C
VMEM_LIMIT = 56 << 20
A_STEP_BLOCKS = 8


def _cparams(*sem):
    return pltpu.CompilerParams(dimension_semantics=sem, vmem_limit_bytes=VMEM_LIMIT)


def _dot(a, b):
    return jnp.dot(a, b, preferred_element_type=F32)


def _dot_nt(a, b):
    return lax.dot_general(a, b, (((1,), (1,)), ((), ())), preferred_element_type=F32)


def _split_bf16(x):
    hi = x.astype(BF16)
    lo = (x - hi.astype(F32)).astype(BF16)
    return hi, lo


def _mod_body(c_ref, w_ref, b_ref, o_ref):
    c = c_ref[...]
    s = c * jax.nn.sigmoid(c)
    s_hi, s_lo = _split_bf16(s)
    w_hi, w_lo = _split_bf16(w_ref[0])
    o_ref[0] = _dot(s_hi, w_hi) + _dot(s_lo, w_hi) + _dot(s_hi, w_lo) + b_ref[0]


def _modulation(cvec, w_ada, b_ada):
    depth, d, n = w_ada.shape
    rows = cvec.shape[0]
    tn = 768
    return pl.pallas_call(
        _mod_body,
        grid=(depth, n // tn),
        in_specs=[pl.BlockSpec((rows, d), lambda l, j: (0, 0)),
                  pl.BlockSpec((1, d, tn), lambda l, j: (l, 0, j)),
                  pl.BlockSpec((1, 1, tn), lambda l, j: (l, 0, j))],
        out_specs=pl.BlockSpec((1, rows, tn), lambda l, j: (l, 0, j)),
        out_shape=jax.ShapeDtypeStruct((depth, rows, n), F32),
        compiler_params=_cparams("parallel", "parallel"),
        name="adaln_mod",
    )(cvec, w_ada, b_ada.reshape(depth, 1, n))


def _inproj_body(*refs, latent):
    if latent:
        (x_ref, mod_ref, g1_ref, w_ref, gains_ref, s64_ref, s32_ref, ca_ref, sa_ref, cb_ref, sb_ref,
         qa_ref, ka_ref, va_ref, qb_ref, kb_ref, vb_ref, u_ref, u_scr) = refs
    else:
        (x_ref, mod_ref, g1_ref, w_ref, gains_ref, s64_ref, s32_ref,
         qa_ref, ka_ref, va_ref, qb_ref, kb_ref, vb_ref, u_ref, u_scr) = refs
    d = D_MODEL
    x = x_ref[0]
    mod = mod_ref[0]
    xn = x * lax.rsqrt(jnp.mean(x * x, axis=-1, keepdims=True) + EPS) * g1_ref[...]
    h = xn * (1.0 + mod[:, d:2 * d]) + mod[:, 0:d]
    acc = _dot(h.astype(BF16), w_ref[...])

    tm = x.shape[0]
    lane = lax.broadcasted_iota(jnp.int32, (tm, LANES), 1)
    first_a = (lane % 32) < 16
    first_b = (lane % 16) < 8

    def normed(xb, seg_ref, inv_n, gain):
        ss = _dot((xb * xb).astype(BF16), seg_ref[...])
        return xb * lax.rsqrt(ss * inv_n + EPS) * gain

    def rope_a(y):
        if not latent:
            return y
        sw = jnp.where(first_a, pltpu.roll(y, LANES - 16, 1), pltpu.roll(y, 16, 1))
        return y * ca_ref[...] + sw * sa_ref[...]

    def rope_b(y):
        if not latent:
            return y
        sw = jnp.where(first_b, pltpu.roll(y, LANES - 8, 1), pltpu.roll(y, 8, 1))
        return y * cb_ref[...] + sw * sb_ref[...]

    gains = gains_ref[...]
    off = 0
    for b in range(H_A):
        y = normed(acc[:, off:off + LANES], s64_ref, 1.0 / HD_A, gains[0:1])
        qa_ref[0, :, b * LANES:(b + 1) * LANES] = rope_a(y).astype(qa_ref.dtype)
        off += LANES
    y = normed(acc[:, off:off + LANES], s64_ref, 1.0 / HD_A, gains[1:2])
    ka_ref[0] = rope_a(y).astype(ka_ref.dtype)
    off += LANES
    va_ref[0] = acc[:, off:off + LANES].astype(va_ref.dtype)
    off += LANES
    for b in range(W_B // LANES):
        y = normed(acc[:, off:off + LANES], s32_ref, 1.0 / DC_B, gains[2:3])
        qb_ref[0, :, b * LANES:(b + 1) * LANES] = rope_b(y).astype(qb_ref.dtype)
        off += LANES
    for b in range(W_B // LANES):
        y = normed(acc[:, off:off + LANES], s32_ref, 1.0 / DC_B, gains[3:4])
        kb_ref[0, :, b * LANES:(b + 1) * LANES] = rope_b(y).astype(kb_ref.dtype)
        off += LANES
    vb_ref[0] = acc[:, off:off + W_B].astype(vb_ref.dtype)
    off += W_B
    for blk in range(W_C // LANES):
        u_scr[blk] = acc[:, off + blk * LANES:off + (blk + 1) * LANES]
    pw = 2 * SSM_CH
    for t in range(SSM_T):
        for blk in range(W_C // LANES):
            xt = u_scr[blk, pl.ds(t, tm // SSM_T, stride=SSM_T), :]
            for pp in range(LANES // pw):
                u_ref[blk * (LANES // pw) + pp, :, t * pw:(t + 1) * pw] = xt[:, pp * pw:(pp + 1) * pw].astype(u_ref.dtype)


def _inproj(x, mod, g1, w_in_p, gains, seg64, seg32, rope, kv_dtype):
    bsz, seq, d = x.shape
    latent = rope is not None
    tm = next(t for t in (1024, 512, 256) if seq % t == 0)
    bm = mod.shape[0]
    mod_idx = (lambda b, i: (b, 0, 0)) if bm > 1 else (lambda b, i: (0, 0, 0))
    const2 = lambda b, i: (0, 0)
    tok = lambda w: pl.BlockSpec((1, tm, w), lambda b, i: (b, i, 0))
    in_specs = [tok(d),
                pl.BlockSpec((1, 1, 6 * d), mod_idx),
                pl.BlockSpec((1, d), const2),
                pl.BlockSpec((d, IN_COLS_P), const2),
                pl.BlockSpec((4, LANES), const2),
                pl.BlockSpec((LANES, LANES), const2),
                pl.BlockSpec((LANES, LANES), const2)]
    args = [x, mod, g1, w_in_p, gains, seg64, seg32]
    if latent:
        in_specs += [pl.BlockSpec((tm, LANES), lambda b, i: (i, 0))] * 4
        args += list(rope)
    widths = (QA_COLS, KV_A * HD_A, KV_A * HD_A, W_B, W_B, W_B)
    dtypes = (BF16, kv_dtype, kv_dtype, BF16, kv_dtype, kv_dtype)
    nt = seq // tm
    rows = tm // SSM_T
    u_spec = pl.BlockSpec((N_PAIR, rows, SSM_ROW), lambda b, i: (0, b * nt + i, 0))
    u_shape = jax.ShapeDtypeStruct((N_PAIR, bsz * seq // SSM_T, SSM_ROW), BF16)
    return pl.pallas_call(
        functools.partial(_inproj_body, latent=latent),
        grid=(bsz, nt),
        in_specs=in_specs,
        out_specs=[tok(w) for w in widths] + [u_spec],
        out_shape=[jax.ShapeDtypeStruct((bsz, seq, w), dt) for w, dt in zip(widths, dtypes)] + [u_shape],
        scratch_shapes=[pltpu.VMEM((W_C // LANES, tm, LANES), F32)],
        compiler_params=_cparams("parallel", "parallel"),
        name="inproj_latent" if latent else "inproj_ctx",
    )(*args)


def _attn_a_body(sink_ref, q_ref, *refs, latent, nblk, nstep):
    o_ref = refs[-1]
    nk = (len(refs) - 1) // 2
    ks = [r[0].astype(BF16) for r in refs[:nk]]
    vs = [r[0].astype(BF16) for r in refs[nk:2 * nk]]
    rows = GQ_A * BLOCK
    rowi = lax.broadcasted_iota(jnp.int32, (rows, 1), 0)
    lane = lax.broadcasted_iota(jnp.int32, (BLOCK, LANES), 1)
    if latent:
        cols = 3 * BLOCK + ks[3].shape[0]
        r = lax.broadcasted_iota(jnp.int32, (rows, cols), 0) & (BLOCK - 1)
        c = lax.broadcasted_iota(jnp.int32, (rows, cols), 1)
        own_k = [ks[1][t * BLOCK:(t + 1) * BLOCK] for t in range(nstep)]
        own_v = [vs[1][t * BLOCK:(t + 1) * BLOCK] for t in range(nstep)]
        band_k = [ks[0]] + own_k + [ks[2]]
        band_v = [vs[0]] + own_v + [vs[2]]
    for t in range(nstep):
        qrows = slice(t * BLOCK, (t + 1) * BLOCK)
        if latent:
            kcat = jnp.concatenate(band_k[t:t + 3] + [ks[3]], axis=0)
            vcat = jnp.concatenate(band_v[t:t + 3] + [vs[3]], axis=0)
            qblk = pl.program_id(1) * nstep + t
            p_off = jnp.where(qblk > 0, 0, 2 * BLOCK)
            n_off = jnp.where(qblk < nblk - 1, 0, 2 * BLOCK)
            prev_ok = (c >= r + p_off) | (c >= BLOCK)
            next_ok = ((c - 2 * BLOCK + n_off) <= r) | (c < 2 * BLOCK) | (c >= 3 * BLOCK)
            valid = prev_ok & next_ok
        else:
            kcat, vcat = ks[0], vs[0]
        heads = []
        for j in range(KV_A):
            q3 = jnp.concatenate([q_ref[0, qrows, (GQ_A * j + g) * LANES:(GQ_A * j + g + 1) * LANES]
                                  for g in range(GQ_A)], axis=0)
            s = _dot_nt(q3, kcat)
            if latent:
                s = jnp.where(valid, s, NEG)
            sink = jnp.where(rowi < BLOCK, sink_ref[GQ_A * j],
                             jnp.where(rowi < 2 * BLOCK, sink_ref[GQ_A * j + 1], sink_ref[GQ_A * j + 2]))
            m = jnp.maximum(jnp.max(s, axis=-1, keepdims=True), sink)
            e = jnp.exp(s - m)
            den = jnp.sum(e, axis=-1, keepdims=True) + jnp.exp(sink - m)
            o = _dot(e.astype(BF16), vcat) / den
            for g in range(GQ_A):
                heads.append((j, o[g * BLOCK:(g + 1) * BLOCK]))
        for blk in range(H_A // 2):
            (j0, o0), (j1, o1) = heads[2 * blk], heads[2 * blk + 1]
            lo = o0 if j0 == 0 else pltpu.roll(o0, HD_A, 1)
            hi = o1 if j1 == 1 else pltpu.roll(o1, HD_A, 1)
            o_ref[0, qrows, blk * LANES:(blk + 1) * LANES] = jnp.where(lane < HD_A, lo, hi).astype(o_ref.dtype)


def _attn_a(qa, ka, va, sink, ctx_kv):
    bsz, seq, _ = qa.shape
    nblk = seq // BLOCK
    latent = ctx_kv is not None
    kvw = KV_A * HD_A
    nb = A_STEP_BLOCKS if nblk % A_STEP_BLOCKS == 0 else nblk
    if latent:
        past = ctx_kv[0].shape[1]
        band = [pl.BlockSpec((1, BLOCK, kvw), lambda b, i: (b, jnp.maximum(nb * i - 1, 0), 0)),
                pl.BlockSpec((1, nb * BLOCK, kvw), lambda b, i: (b, i, 0)),
                pl.BlockSpec((1, BLOCK, kvw), lambda b, i: (b, jnp.minimum(nb * i + nb, nblk - 1), 0)),
                pl.BlockSpec((1, past, kvw), lambda b, i: (b, 0, 0))]
        kv_specs = band + band
        kv_args = [ka, ka, ka, ctx_kv[0], va, va, va, ctx_kv[1]]
    else:
        kv_specs = [pl.BlockSpec((1, seq, kvw), lambda b, i: (b, 0, 0))] * 2
        kv_args = [ka, va]
    return pl.pallas_call(
        functools.partial(_attn_a_body, latent=latent, nblk=nblk, nstep=nb),
        grid=(bsz, nblk // nb),
        in_specs=[pl.BlockSpec(memory_space=pltpu.SMEM),
                  pl.BlockSpec((1, nb * BLOCK, QA_COLS), lambda b, i: (b, i, 0))] + kv_specs,
        out_specs=pl.BlockSpec((1, nb * BLOCK, W_A), lambda b, i: (b, i, 0)),
        out_shape=jax.ShapeDtypeStruct((bsz, seq, W_A), BF16),
        compiler_params=_cparams("parallel", "parallel"),
        name="attn_a_latent" if latent else "attn_a_ctx",
    )(sink, qa, *kv_args)


def _attn_b_body(lam_ref, gain_ref, q_ref, *refs, part_lens, lam_init, kc):
    npart = len(part_lens)
    k_refs, v_refs = refs[:npart], refs[npart:2 * npart]
    o_ref, s_scr, vm_scr = refs[2 * npart:]
    tq = q_ref.shape[1]
    chunks = []
    col = 0
    for p, plen in enumerate(part_lens):
        step = min(kc, plen)
        for start in range(0, plen, step):
            chunks.append((p, start, col, step))
            col += step

    @pl.when(pl.program_id(2) == 0)
    def _():
        off = 0
        for p, plen in enumerate(part_lens):
            v = v_refs[p][0].astype(BF16)
            lane_v = lax.broadcasted_iota(jnp.int32, (plen, LANES), 1)
            for h in range(2):
                own = (lane_v >= h * HD_B) & (lane_v < (h + 1) * HD_B)
                ones = jnp.where(lane_v == (1 - h) * HD_B, 1.0, 0.0).astype(BF16)
                vm_scr[h, off:off + plen, :] = jnp.where(own, v, ones)
            off += plen

    lv = lam_ref[...]
    lam = (jnp.exp(jnp.sum(lv[0:1] * lv[1:2], axis=-1, keepdims=True))
           - jnp.exp(jnp.sum(lv[2:3] * lv[3:4], axis=-1, keepdims=True)) + lam_init)
    q = q_ref[0]
    lane_q = lax.broadcasted_iota(jnp.int32, (tq, LANES), 1)
    total = jnp.zeros((tq, LANES), F32)
    for h in range(2):
        qc = [jnp.where((lane_q >= h * HD_B + c * DC_B) & (lane_q < h * HD_B + (c + 1) * DC_B), q, jnp.zeros_like(q))
              for c in range(2)]
        rows = [slice(c * tq, (c + 1) * tq) for c in range(2)]
        macc = [None, None]
        for p, start, col, step in chunks:
            kch = k_refs[p][0, start:start + step, :].astype(BF16)
            for c in range(2):
                s = _dot_nt(qc[c], kch)
                s_scr[rows[c], col:col + step] = s
                for j in range(step // LANES):
                    t = s[:, j * LANES:(j + 1) * LANES]
                    macc[c] = t if macc[c] is None else jnp.maximum(macc[c], t)
        m = [jnp.max(macc[c], axis=-1, keepdims=True) for c in range(2)]
        acc = [jnp.zeros((tq, LANES), F32) for _ in range(2)]
        for p, start, col, step in chunks:
            vch = vm_scr[h, col:col + step, :]
            for c in range(2):
                e = jnp.exp2(s_scr[rows[c], col:col + step] - m[c]).astype(BF16)
                acc[c] = acc[c] + _dot(e, vch)
        o2 = [acc[c] / jnp.sum(jnp.where(lane_q == (1 - h) * HD_B, acc[c], 0.0), axis=-1, keepdims=True)
              for c in range(2)]
        own = (lane_q >= h * HD_B) & (lane_q < (h + 1) * HD_B)
        total = total + jnp.where(own, o2[0] - lam * o2[1], 0.0)
    sq = total * total
    ss_lo = jnp.sum(jnp.where(lane_q < HD_B, sq, 0.0), axis=-1, keepdims=True)
    ss_hi = jnp.sum(jnp.where(lane_q >= HD_B, sq, 0.0), axis=-1, keepdims=True)
    rinv = jnp.where(lane_q < HD_B, lax.rsqrt(ss_lo * (1.0 / HD_B) + EPS), lax.rsqrt(ss_hi * (1.0 / HD_B) + EPS))
    o_ref[0] = (total * rinv * gain_ref[...] * (1.0 - lam_init)).astype(o_ref.dtype)


def _attn_b(qb, k_parts, v_parts, lam_b, gain, lam_init):
    bsz, seq, _ = qb.shape
    tq = next(t for t in (1024, 512, 256) if seq % t == 0)
    part_lens = tuple(k.shape[1] for k in k_parts)
    lk = sum(part_lens)
    kv_specs = [pl.BlockSpec((1, n, LANES), lambda b, hp, i: (b, 0, hp)) for n in part_lens]
    return pl.pallas_call(
        functools.partial(_attn_b_body, part_lens=part_lens, lam_init=lam_init, kc=512),
        grid=(bsz, W_B // LANES, seq // tq),
        in_specs=[pl.BlockSpec((4, DC_B), lambda b, hp, i: (0, 0)),
                  pl.BlockSpec((1, LANES), lambda b, hp, i: (0, 0)),
                  pl.BlockSpec((1, tq, LANES), lambda b, hp, i: (b, i, hp))] + kv_specs + kv_specs,
        out_specs=pl.BlockSpec((1, tq, LANES), lambda b, hp, i: (b, i, hp)),
        out_shape=jax.ShapeDtypeStruct((bsz, seq, W_B), BF16),
        scratch_shapes=[pltpu.VMEM((2 * tq, lk), F32), pltpu.VMEM((2, lk, LANES), BF16)],
        compiler_params=_cparams("parallel", "parallel", "arbitrary"),
        name="attn_b_latent" if len(k_parts) > 1 else "attn_b_ctx",
    )(lam_b, gain, qb, *k_parts, *v_parts)


def _ssm_body(u_ref, m_ref, g_ref, cc_ref, a_ref, h0_ref, y_ref, fin_ref, s_scr, h_scr, *, nb, nc):
    u = u_ref[0]
    col = lambda k: slice(k * LANES, (k + 1) * LANES)
    s = _dot(u, g_ref[0])
    for k in range(4):
        s_scr[k] = s[:, col(k)]
    a = a_ref[0]
    afr, afi, abr, abi = (jnp.broadcast_to(a[k:k + 1], (nb, LANES)) for k in range(4))
    h0 = h0_ref[0]

    def step(c, carry):
        fr, fi, br, bi = carry
        rf = pl.ds(c, nb, stride=nc)
        rb = pl.ds(nc - 1 - c, nb, stride=nc)
        h_scr[0, rf, :] = fr
        h_scr[1, rf, :] = fi
        h_scr[2, rb, :] = br
        h_scr[3, rb, :] = bi
        nfr = afr * fr - afi * fi + s_scr[0, rf, :]
        nfi = afr * fi + afi * fr + s_scr[1, rf, :]
        nbr = abr * br - abi * bi + s_scr[2, rb, :]
        nbi = abr * bi + abi * br + s_scr[3, rb, :]
        return nfr, nfi, nbr, nbi

    fin = lax.fori_loop(0, nc, step, tuple(h0[:, col(k)] for k in range(4)), unroll=8)
    for k in range(4):
        fin_ref[0, :, col(k)] = fin[k]
    hin = jnp.concatenate([h_scr[k] for k in range(4)], axis=1).astype(BF16)
    y = _dot(u, m_ref[0]) + _dot(hin, cc_ref[0])
    y_ref[0] = y.astype(y_ref.dtype)


def _ssm(u_rows, mats, h0, nb):
    npair, rows, w = u_rows.shape
    nc = rows // nb
    mat_spec = pl.BlockSpec((1, w, w), lambda p: (p, 0, 0))
    return pl.pallas_call(
        functools.partial(_ssm_body, nb=nb, nc=nc),
        grid=(npair,),
        in_specs=[pl.BlockSpec((1, rows, w), lambda p: (p, 0, 0)), mat_spec, mat_spec, mat_spec,
                  pl.BlockSpec((1, 4, LANES), lambda p: (p, 0, 0)),
                  pl.BlockSpec((1, nb, w), lambda p: (p, 0, 0))],
        out_specs=[pl.BlockSpec((1, rows, w), lambda p: (p, 0, 0)),
                   pl.BlockSpec((1, nb, w), lambda p: (p, 0, 0))],
        out_shape=[jax.ShapeDtypeStruct((npair, rows, w), BF16),
                   jax.ShapeDtypeStruct((npair, nb, w), F32)],
        scratch_shapes=[pltpu.VMEM((4, rows, LANES), F32), pltpu.VMEM((4, rows, LANES), F32)],
        compiler_params=_cparams("parallel"),
        name="ssm_scan",
    )(u_rows, mats["m"], mats["g"], mats["cc"], mats["a16"], h0)


def _ssm_matrices(lp):
    t = SSM_T
    ks = jnp.arange(t + 1, dtype=F32)
    dirs = []
    for d in range(2):
        lam = lax.complex(lp["ssm_lam_re"][d].astype(F32), lp["ssm_lam_im"][d].astype(F32))
        dt = jnp.exp(lp["ssm_log_dt"][d].astype(F32))[:, None]
        a_bar = jnp.exp(lam * dt)
        b_bar = ((a_bar - 1.0) / lam)[..., None] * lax.complex(lp["ssm_b_re"][d].astype(F32),
                                                               lp["ssm_b_im"][d].astype(F32))
        c_mat = lax.complex(lp["ssm_c_re"][d].astype(F32), lp["ssm_c_im"][d].astype(F32))
        pw = jnp.exp((lam * dt)[None] * ks[:, None, None].astype(jnp.complex64))
        kern = jnp.real(jnp.einsum("gop,kgp,gpi->gkoi", c_mat, pw[:t], b_bar))
        dirs.append((pw, b_bar, c_mat, kern))
    (pw_f, bb_f, cm_f, k_f), (pw_b, bb_b, cm_b, k_b) = dirs
    eye2 = jnp.eye(2, dtype=F32)
    ch, pw2 = SSM_CH, 2 * SSM_CH
    hi = lax.Precision.HIGHEST

    def pair_bd(x):
        r, c = x.shape[1:]
        return jnp.einsum("pgrc,gh->pgrhc", x.reshape(N_PAIR, 2, r, c), eye2.astype(x.dtype)).reshape(N_PAIR, 2 * r, 2 * c)

    def pair_vec(x):
        return x.reshape(x.shape[0], N_PAIR, 2 * P_C).transpose(1, 0, 2)

    def lag_blocks(kern):
        x = kern.transpose(0, 1, 3, 2).reshape(N_PAIR, 2, t, ch, ch)
        return jnp.einsum("pglic,gh->plgihc", x, eye2).reshape(N_PAIR, t, pw2, pw2)
    kp_f, kp_b = lag_blocks(k_f), lag_blocks(k_b)
    d_blk = pair_bd(lp["ssm_d"].astype(F32)[:, :, None] * jnp.eye(ch, dtype=F32)[None])
    center = (kp_f[:, 0] + kp_b[:, 0] + d_blk)[:, None]
    band = jnp.concatenate([kp_b[:, :0:-1], center, kp_f[:, 1:]], axis=1)
    band = band.transpose(0, 2, 1, 3).reshape(N_PAIR, pw2, (2 * t - 1) * pw2)
    m_p = jnp.concatenate([band[:, :, (t - 1 - s) * pw2:(t - 1 - s) * pw2 + SSM_ROW] for s in range(t)], axis=1)

    def inject(pw_sel, b_bar):
        x1 = jnp.repeat(pair_vec(pw_sel), pw2, axis=1)
        x2 = jnp.tile(pair_bd(b_bar.transpose(0, 2, 1)), (1, t, 1))
        return x1 * x2
    g_f = inject(pw_f[t - 1 - jnp.arange(t)], bb_f)
    g_b = inject(pw_b[jnp.arange(t)], bb_b)
    g_p = jnp.concatenate([jnp.real(g_f), jnp.imag(g_f), jnp.real(g_b), jnp.imag(g_b)], axis=2)

    lane = jnp.arange(SSM_ROW)
    exp_t = (jnp.arange(t)[:, None] == lane[None, :] // pw2).astype(F32)
    exp_c = (jnp.arange(pw2)[:, None] == lane[None, :] % pw2).astype(F32)

    def widen(x, e):
        f = lambda v: jnp.einsum("pqk,kx->pqx", v, e, precision=hi)
        return lax.complex(f(jnp.real(x)), f(jnp.imag(x)))

    def readout(pw_sel, c_mat):
        y1 = widen(pair_vec(pw_sel).transpose(0, 2, 1), exp_t)
        y2 = widen(pair_bd(c_mat.transpose(0, 2, 1)), exp_c)
        return y1 * y2
    z_f = readout(pw_f[1 + jnp.arange(t)], cm_f)
    z_b = readout(pw_b[t - jnp.arange(t)], cm_b)
    cc_p = jnp.concatenate([jnp.real(z_f), -jnp.imag(z_f), jnp.real(z_b), -jnp.imag(z_b)], axis=1)
    a16 = jnp.stack([jnp.real(pw_f[t]), jnp.imag(pw_f[t]), jnp.real(pw_b[t]), jnp.imag(pw_b[t])], axis=0)
    a16 = a16.reshape(4, N_PAIR, 2 * P_C).transpose(1, 0, 2)
    return dict(m=m_p.astype(BF16), g=g_p.astype(BF16), cc=cc_p.astype(BF16), a16=a16)


def _ssm_state_rows(s_re, s_im):
    bsz = s_re.shape[0]
    parts = [s_re[:, 0], s_im[:, 0], s_re[:, 1], s_im[:, 1]]
    st = jnp.stack([p.reshape(bsz, N_PAIR, 2 * P_C) for p in parts], axis=2)
    return st.transpose(1, 0, 2, 3).reshape(N_PAIR, bsz, 8 * P_C).astype(F32)


def _ssm_state_unrows(fin):
    npair, bsz, _ = fin.shape
    st = fin.reshape(npair, bsz, 4, 2, P_C).transpose(1, 2, 0, 3, 4).reshape(bsz, 4, G_C, P_C)
    return jnp.stack([st[:, 0], st[:, 2]], axis=1), jnp.stack([st[:, 1], st[:, 3]], axis=1)


def _route(scores, bias):
    tm = scores.shape[1]
    biased = scores + bias
    iota8 = lax.broadcasted_iota(jnp.int32, (PER_GROUP, tm), 0)
    grp = [biased[PER_GROUP * g:PER_GROUP * (g + 1)] for g in range(N_EXP_GROUPS)]
    gscore = []
    for v in grp:
        m1 = jnp.max(v, axis=0, keepdims=True)
        first = jnp.min(jnp.where(v == m1, iota8, PER_GROUP), axis=0, keepdims=True)
        m2 = jnp.max(jnp.where(iota8 == first, -jnp.inf, v), axis=0, keepdims=True)
        gscore.append(m1 + m2)
    masked = []
    for g in range(N_EXP_GROUPS):
        rank = jnp.zeros((1, tm), jnp.int32)
        for o in range(N_EXP_GROUPS):
            if o == g:
                continue
            ahead = (gscore[o] >= gscore[g]) if o < g else (gscore[o] > gscore[g])
            rank = rank + jnp.where(ahead, 1, 0)
        masked.append(jnp.where(rank < TOPK_GROUPS, grp[g], -jnp.inf))
    chosen = [None] * N_EXP_GROUPS
    for _ in range(TOP_K):
        best = masked[0]
        for v in masked[1:]:
            best = jnp.maximum(best, v)
        best = jnp.max(best, axis=0, keepdims=True)
        first = jnp.full((1, tm), N_EXPERTS, jnp.int32)
        for g, v in enumerate(masked):
            cand = jnp.min(jnp.where(v == best, iota8 + PER_GROUP * g, N_EXPERTS), axis=0, keepdims=True)
            first = jnp.minimum(first, cand)
        for g in range(N_EXP_GROUPS):
            hit = (iota8 + PER_GROUP * g) == first
            chosen[g] = hit if chosen[g] is None else (chosen[g] | hit)
            masked[g] = jnp.where(hit, -jnp.inf, masked[g])
    w = [jnp.where(chosen[g], scores[PER_GROUP * g:PER_GROUP * (g + 1)], 0.0) for g in range(N_EXP_GROUPS)]
    wsum = w[0]
    for v in w[1:]:
        wsum = wsum + v
    wsum = jnp.sum(wsum, axis=0, keepdims=True)
    return jnp.concatenate([v / wsum * ROUTED_SCALE for v in w], axis=0)


def _post_body(x_ref, oa_ref, ob_ref, y_ref, mod_ref, wglu_ref, wout_ref, g2_ref, wrh_ref, wrl_ref, br_ref,
               x1_ref, h2_ref, gate_ref, y_scr):
    d = D_MODEL
    tm = x_ref.shape[1]
    pw = 2 * SSM_CH
    for t in range(SSM_T):
        for blk in range(W_C // LANES):
            piece = jnp.concatenate([y_ref[blk * (LANES // pw) + pp, :, t * pw:(t + 1) * pw].astype(F32)
                                     for pp in range(LANES // pw)], axis=1)
            y_scr[blk, pl.ds(t, tm // SSM_T, stride=SSM_T), :] = piece
    g = jax.nn.gelu(jnp.concatenate([y_scr[blk] for blk in range(W_C // LANES)], axis=1))
    oc = g * jax.nn.sigmoid(_dot(g.astype(BF16), wglu_ref[...]))
    mix = (_dot(oa_ref[0], wout_ref[0:W_A]) + _dot(ob_ref[0], wout_ref[W_A:W_A + W_B])
           + _dot(oc.astype(BF16), wout_ref[W_A + W_B:]))
    mod = mod_ref[0]
    x1 = x_ref[0] + mod[:, 2 * d:3 * d] * mix
    x1_ref[0] = x1
    xn = x1 * lax.rsqrt(jnp.mean(x1 * x1, axis=-1, keepdims=True) + EPS) * g2_ref[...]
    h2 = xn * (1.0 + mod[:, 4 * d:5 * d]) + mod[:, 3 * d:4 * d]
    h_hi, h_lo = _split_bf16(h2)
    h2_ref[0] = h_hi
    logits = _dot_nt(wrh_ref[...], h_hi) + _dot_nt(wrh_ref[...], h_lo) + _dot_nt(wrl_ref[...], h_hi)
    gate_ref[0] = _route(jax.nn.sigmoid(logits), br_ref[...]).T


def _post_mix(x, oa, ob, y, mod, w_glu, w_out, g2, wr_hi, wr_lo, b_r):
    bsz, seq, d = x.shape
    tm = next(t for t in (1024, 512, 256) if seq % t == 0)
    bm = mod.shape[0]
    mod_idx = (lambda b, i: (b, 0, 0)) if bm > 1 else (lambda b, i: (0, 0, 0))
    const2 = lambda b, i: (0, 0)
    tok = lambda w: pl.BlockSpec((1, tm, w), lambda b, i: (b, i, 0))
    nt = seq // tm
    return pl.pallas_call(
        _post_body,
        grid=(bsz, nt),
        in_specs=[tok(d), tok(W_A), tok(W_B),
                  pl.BlockSpec((N_PAIR, tm // SSM_T, SSM_ROW), lambda b, i: (0, b * nt + i, 0)),
                  pl.BlockSpec((1, 1, 6 * d), mod_idx),
                  pl.BlockSpec((W_C, W_C), const2),
                  pl.BlockSpec((d, d), const2),
                  pl.BlockSpec((1, d), const2),
                  pl.BlockSpec((N_EXPERTS, d), const2),
                  pl.BlockSpec((N_EXPERTS, d), const2),
                  pl.BlockSpec((N_EXPERTS, 1), const2)],
        out_specs=[tok(d), tok(d), tok(N_EXPERTS)],
        out_shape=[jax.ShapeDtypeStruct((bsz, seq, d), F32),
                   jax.ShapeDtypeStruct((bsz, seq, d), BF16),
                   jax.ShapeDtypeStruct((bsz, seq, N_EXPERTS), F32)],
        scratch_shapes=[pltpu.VMEM((W_C // LANES, tm, LANES), F32)],
        compiler_params=_cparams("parallel", "parallel"),
        name="post_mix",
    )(x, oa, ob, y, mod, w_glu, w_out, g2, wr_hi, wr_lo, b_r)


def _moe_body(x1_ref, h_ref, gate_ref, g2_ref, w1_ref, w3_ref, w2_ref, c1_ref, c32_ref, ex_ref,
              s1_ref, s3_ref, s2_ref, o_ref, acc_ref, h8_ref, hs_ref):
    j = pl.program_id(1)

    @pl.when(j == 0)
    def _():
        h = h_ref[...]
        a = _dot(h, s1_ref[...])
        acc_ref[...] = _dot((a * jax.nn.sigmoid(a) * _dot(h, s3_ref[...])).astype(BF16), s2_ref[...])
        hf = h.astype(F32)
        sc = jnp.maximum(jnp.max(jnp.abs(hf), axis=-1, keepdims=True), F8_TINY) * (1.0 / F8_RANGE)
        hs_ref[...] = sc
        h8_ref[...] = (hf * (1.0 / sc)).astype(F8)

    ne = w1_ref.shape[0]
    h8 = h8_ref[...]
    hs = hs_ref[...]
    a = _dot(h8, jnp.concatenate([w1_ref[e] for e in range(ne)], axis=1)) * c1_ref[...] * hs
    b = _dot(h8, jnp.concatenate([w3_ref[e] for e in range(ne)], axis=1))
    gexp = _dot(jnp.concatenate(_split_bf16(gate_ref[...]), axis=1), ex_ref[...])
    hid = a * jax.nn.sigmoid(a) * b * gexp * c32_ref[...]
    sc = jnp.maximum(jnp.max(jnp.abs(hid), axis=-1, keepdims=True), F8_TINY) * (1.0 / F8_RANGE)
    w2 = w2_ref[...].reshape(ne * F_EXP, w2_ref.shape[-1])
    acc_ref[...] += _dot((hid * (1.0 / sc)).astype(F8), w2) * (sc * hs)

    @pl.when(j == pl.num_programs(1) - 1)
    def _():
        o_ref[...] = x1_ref[...] + g2_ref[0] * acc_ref[...]


def _moe(x1, h2, gates, mod, seq, ew, layer, expand, ws1, ws3, ws2):
    tokens, d = x1.shape
    bm = mod.shape[0]
    span = seq if bm > 1 else tokens
    tm = next(t for t in (1024, 512, 256) if span % t == 0)
    per_b = seq // tm if bm > 1 else 1
    mod_idx = (lambda i, j: (i // per_b, 0, 5)) if bm > 1 else (lambda i, j: (0, 0, 5))
    ne = 8
    fc = ne * F_EXP
    hidden = ew["w2"].shape[1] * F_EXP
    const2 = lambda i, j: (0, 0)
    chunk_row = pl.BlockSpec((None, 1, fc), lambda i, j: (layer, 0, j))
    return pl.pallas_call(
        _moe_body,
        grid=(tokens // tm, hidden // fc),
        in_specs=[pl.BlockSpec((tm, d), lambda i, j: (i, 0)),
                  pl.BlockSpec((tm, d), lambda i, j: (i, 0)),
                  pl.BlockSpec((tm, N_EXPERTS), lambda i, j: (i, 0)),
                  pl.BlockSpec((1, 1, d), mod_idx),
                  pl.BlockSpec((None, ne, d, F_EXP), lambda i, j: (layer, j, 0, 0)),
                  pl.BlockSpec((None, ne, d, F_EXP), lambda i, j: (layer, j, 0, 0)),
                  pl.BlockSpec((None, ne, F_EXP, d), lambda i, j: (layer, j, 0, 0)),
                  chunk_row, chunk_row,
                  pl.BlockSpec((2 * N_EXPERTS, fc), lambda i, j: (0, j)),
                  pl.BlockSpec((d, F_SHARED), const2),
                  pl.BlockSpec((d, F_SHARED), const2),
                  pl.BlockSpec((F_SHARED, d), const2)],
        out_specs=pl.BlockSpec((tm, d), lambda i, j: (i, 0)),
        out_shape=jax.ShapeDtypeStruct((tokens, d), F32),
        scratch_shapes=[pltpu.VMEM((tm, d), F32), pltpu.VMEM((tm, d), F8), pltpu.VMEM((tm, 1), F32)],
        compiler_params=_cparams("parallel", "arbitrary"),
        name="moe",
    )(x1, h2, gates, mod, ew["w1"], ew["w3"], ew["w2"], ew["c1"], ew["c32"], expand, ws1, ws3, ws2)


def _expert_fp8_body(w_ref, q_ref, c_ref):
    for e in range(w_ref.shape[1]):
        w = w_ref[0, e]
        top = jnp.max(jnp.max(jnp.abs(w), axis=0, keepdims=True), axis=1, keepdims=True)
        sc = jnp.maximum(top, F8_TINY) * (1.0 / F8_RANGE)
        q_ref[0, e] = (w * (1.0 / sc)).astype(F8)
        c_ref[0, :, e * F_EXP:(e + 1) * F_EXP] = jnp.broadcast_to(sc, (1, F_EXP))


def _expert_fp8(w):
    depth, ne, r, c = w.shape
    blk = 8
    return pl.pallas_call(
        _expert_fp8_body,
        grid=(depth, ne // blk),
        in_specs=[pl.BlockSpec((1, blk, r, c), lambda l, i: (l, i, 0, 0))],
        out_specs=[pl.BlockSpec((1, blk, r, c), lambda l, i: (l, i, 0, 0)),
                   pl.BlockSpec((1, 1, blk * F_EXP), lambda l, i: (l, 0, i))],
        out_shape=[jax.ShapeDtypeStruct((depth, ne, r, c), F8), jax.ShapeDtypeStruct((depth, 1, ne * F_EXP), F32)],
        compiler_params=_cparams("parallel", "parallel"),
        name="expert_fp8",
    )(w)


def _prep_experts(p):
    w1, c1 = _expert_fp8(p["w_e1"].astype(F32))
    w3, c3 = _expert_fp8(p["w_e3"].astype(F32))
    w2, c2 = _expert_fp8(p["w_e2"].astype(F32))
    return dict(w1=w1, w3=w3, w2=w2, c1=c1, c32=c3 * c2)


def _rope_tables(seq):
    pos = jnp.arange(seq)
    row = (pos // GRID_W).astype(F32)[:, None]
    colp = (pos % GRID_W).astype(F32)[:, None]
    lane = jnp.arange(LANES)

    def table(width):
        half, quarter = width // 2, width // 4
        i = lane % width
        freq = ROPE_BASE ** (-(2.0 * (i % quarter).astype(F32)) / half)
        ang = jnp.where((i // half) == 0, row, colp) * freq[None, :]
        sign = jnp.where((i % half) < quarter, -1.0, 1.0)
        return jnp.cos(ang), jnp.sin(ang) * sign[None, :]

    ca, sa = table(HD_A)
    cb, sb = table(DC_B)
    return ca, sa, cb, sb


def _prep_layer(p):
    d = D_MODEL
    w_in = p["w_in"]
    place = (jnp.arange(H_A)[:, None] // GQ_A == jnp.arange(KV_A)[None, :]).astype(w_in.dtype)
    qa_pad = w_in[:, :W_A].reshape(d, H_A, 1, HD_A) * place[None, :, :, None]
    w_in_p = jnp.concatenate([qa_pad.reshape(d, QA_COLS), w_in[:, W_A:]], axis=1).astype(BF16)
    gains = jnp.stack([jnp.tile(p["q_norm_a"], LANES // HD_A) * (HD_A ** -0.5),
                       jnp.tile(p["k_norm_a"], LANES // HD_A),
                       jnp.tile(p["q_norm_b"], LANES // DC_B) * (DC_B ** -0.5 * LOG2E),
                       jnp.tile(p["k_norm_b"], LANES // DC_B)], axis=0).astype(F32)
    lp = {k: p[k] for k in ("ssm_lam_re", "ssm_lam_im", "ssm_log_dt", "ssm_b_re", "ssm_b_im",
                            "ssm_c_re", "ssm_c_im", "ssm_d")}
    wr_hi, wr_lo = _split_bf16(p["w_router"].T.astype(F32))
    return dict(
        w_in_p=w_in_p, gains=gains,
        g1=p["norm1_g"].reshape(1, d).astype(F32), g2=p["norm2_g"].reshape(1, d).astype(F32),
        sink=p["sink_a"].astype(F32), lam_b=p["lam_b"].astype(F32),
        subln=jnp.tile(p["subln_b"], LANES // HD_B).reshape(1, LANES).astype(F32),
        ssm=_ssm_matrices(lp),
        w_glu=p["w_glu"].astype(BF16), w_out=p["w_out"].astype(BF16),
        wr_hi=wr_hi, wr_lo=wr_lo, b_r=p["b_router"].reshape(N_EXPERTS, 1).astype(F32),
        ws1=p["w_s1"].astype(BF16), ws3=p["w_s3"].astype(BF16), ws2=p["w_s2"].astype(BF16),
    )


def _trunk_layer(x, mod, lw, consts, ctx):
    bsz, seq, d = x.shape
    latent = ctx is not None
    rope = consts["rope"] if latent else None
    kv_dtype = BF16 if latent else F32
    grp = 1 if latent else next(g for g in (4, 2, 1) if bsz % g == 0)
    tok = lambda a: a.reshape(bsz // grp, grp * seq, a.shape[-1])
    per_seq = lambda a: a.reshape(bsz, seq, a.shape[-1])
    qa, ka, va, qb, kb, vb, u = _inproj(tok(x), mod, lw["g1"], lw["w_in_p"], lw["gains"],
                                        consts["seg64"], consts["seg32"], rope, kv_dtype)
    qa, ka, va, qb, kb, vb = (per_seq(a) for a in (qa, ka, va, qb, kb, vb))
    if latent:
        oa = _attn_a(qa, ka, va, lw["sink"], (ctx["ak"], ctx["av"]))
        ob = _attn_b(qb, [kb, ctx["bk"]], [vb, ctx["bv"]], lw["lam_b"], lw["subln"], lw["lam_init"])
        h0 = ctx["h0"]
    else:
        oa = _attn_a(qa, ka, va, lw["sink"], None)
        ob = _attn_b(qb, [kb], [vb], lw["lam_b"], lw["subln"], lw["lam_init"])
        h0 = jnp.zeros((N_PAIR, bsz, 8 * P_C), F32)
    y_rows, fin = _ssm(u, lw["ssm"], h0, bsz)
    x1, h2, gates = _post_mix(tok(x), tok(oa), tok(ob), y_rows, mod, lw["w_glu"], lw["w_out"], lw["g2"],
                                lw["wr_hi"], lw["wr_lo"], lw["b_r"])
    out = _moe(x1.reshape(bsz * seq, d), h2.reshape(bsz * seq, d), gates.reshape(bsz * seq, N_EXPERTS), mod, seq,
               consts["experts"], lw["layer"], consts["expand"], lw["ws1"], lw["ws3"], lw["ws2"])
    return out.reshape(bsz, seq, d), (ka, va, kb, vb, fin)


def kernel(x_prompt, x_sample, cache_a_k, cache_a_v, cache_b_k, cache_b_v, state_ssm_re, state_ssm_im, c, c_ctx, norm1_g, norm2_g, w_ada, b_ada, w_in, q_norm_a, k_norm_a, sink_a, q_norm_b, k_norm_b, lam_b, subln_b, ssm_lam_re, ssm_lam_im, ssm_log_dt, ssm_b_re, ssm_b_im, ssm_c_re, ssm_c_im, ssm_d, w_glu, w_out, w_router, b_router, w_e1, w_e3, w_e2, w_s1, w_s3, w_s2):
    p = dict(norm1_g=norm1_g, norm2_g=norm2_g, w_in=w_in, q_norm_a=q_norm_a, k_norm_a=k_norm_a, sink_a=sink_a,
             q_norm_b=q_norm_b, k_norm_b=k_norm_b, lam_b=lam_b, subln_b=subln_b,
             ssm_lam_re=ssm_lam_re, ssm_lam_im=ssm_lam_im, ssm_log_dt=ssm_log_dt, ssm_b_re=ssm_b_re,
             ssm_b_im=ssm_b_im, ssm_c_re=ssm_c_re, ssm_c_im=ssm_c_im, ssm_d=ssm_d, w_glu=w_glu, w_out=w_out,
             w_router=w_router, b_router=b_router, w_e1=w_e1, w_e3=w_e3, w_e2=w_e2,
             w_s1=w_s1, w_s3=w_s3, w_s2=w_s2)
    depth = w_in.shape[0]
    bsz, seq, d = x_prompt.shape
    dbsz, dseq, _ = x_sample.shape
    past = cache_a_k.shape[3]

    mod_rows = 16
    cvec = jnp.concatenate([c.astype(F32), c_ctx.astype(F32)[None],
                            jnp.zeros((mod_rows - dbsz - 1, d), F32)], axis=0)
    mods = _modulation(cvec, w_ada.astype(F32), b_ada.astype(F32))

    lane = jnp.arange(LANES)
    hidden = N_EXPERTS * F_EXP
    consts = dict(
        rope=_rope_tables(dseq),
        seg64=(lane[:, None] // HD_A == lane[None, :] // HD_A).astype(BF16),
        seg32=(lane[:, None] // DC_B == lane[None, :] // DC_B).astype(BF16),
        expand=(jnp.arange(2 * N_EXPERTS)[:, None] % N_EXPERTS == jnp.arange(hidden)[None, :] // F_EXP).astype(BF16),
    )

    xp, xs = x_prompt, x_sample
    ak, av, bk, bv, sre, sim = [], [], [], [], [], []
    prepared = jax.vmap(_prep_layer)(p)
    consts["experts"] = _prep_experts(p)
    for l in range(depth):
        lw = jax.tree.map(lambda v: v[l], prepared)
        lw["layer"] = l
        lw["lam_init"] = 0.8 - 0.6 * math.exp(-0.3 * l)
        mod_lat = mods[l, :dbsz][:, None, :]
        mod_ctx = mods[l, dbsz:dbsz + 1][:, None, :]
        xp, (k_a, v_a, k_b, v_b, fin) = _trunk_layer(xp, mod_ctx, lw, consts, None)
        ak.append(k_a.reshape(bsz, seq, KV_A, HD_A).transpose(0, 2, 1, 3))
        av.append(v_a.reshape(bsz, seq, KV_A, HD_A).transpose(0, 2, 1, 3))
        bk.append(k_b.reshape(bsz, seq, H_B, 2, DC_B).transpose(0, 2, 3, 1, 4))
        bv.append(v_b.reshape(bsz, seq, H_B, HD_B).transpose(0, 2, 1, 3))
        f_re, f_im = _ssm_state_unrows(fin)
        sre.append(f_re)
        sim.append(f_im)
        ctx = dict(
            ak=cache_a_k[:, l].transpose(0, 2, 1, 3).reshape(dbsz, past, KV_A * HD_A).astype(BF16),
            av=cache_a_v[:, l].transpose(0, 2, 1, 3).reshape(dbsz, past, KV_A * HD_A).astype(BF16),
            bk=cache_b_k[:, l].transpose(0, 3, 1, 2, 4).reshape(dbsz, past, W_B).astype(BF16),
            bv=cache_b_v[:, l].transpose(0, 2, 1, 3).reshape(dbsz, past, W_B).astype(BF16),
            h0=_ssm_state_rows(state_ssm_re[:, l], state_ssm_im[:, l]),
        )
        xs, _ = _trunk_layer(xs, mod_lat, lw, consts, ctx)
    return (xp, xs, jnp.stack(ak, axis=1), jnp.stack(av, axis=1), jnp.stack(bk, axis=1),
            jnp.stack(bv, axis=1), jnp.stack(sre, axis=1), jnp.stack(sim, axis=1))
```

```python
import functools
import math

import jax
import jax.numpy as jnp
from jax import lax
from jax.experimental import pallas as pl
from jax.experimental.pallas import tpu as pltpu

F32 = jnp.float32
BF16 = jnp.bfloat16
F8 = jnp.float8_e4m3fn
F8_RANGE = 384.0
F8_TINY = 1e-30

D_MODEL = 1024
GRID_W = 64
BLOCK = 128
H_A, KV_A, HD_A = 6, 2, 64
GQ_A = H_A // KV_A
W_A = H_A * HD_A
H_B, HD_B = 4, 64
DC_B = HD_B // 2
W_B = H_B * HD_B
SSM_CH = 16
W_C = D_MODEL - W_A - W_B
G_C = W_C // SSM_CH
P_C = 64
N_EXPERTS, TOP_K, F_EXP, F_SHARED = 64, 6, 128, 256
N_EXP_GROUPS, TOPK_GROUPS = 8, 4
PER_GROUP = N_EXPERTS // N_EXP_GROUPS
ROUTED_SCALE = 2.5
ROPE_BASE = 10000.0
EPS = 1e-6
NEG = -1e30
LOG2E = 1.4426950408889634

LANES = 128
SSM_T = 16
N_PAIR = G_C // 2
SSM_ROW = 2 * SSM_T * SSM_CH
QA_COLS = H_A * LANES
IN_COLS_P = QA_COLS + 2 * KV_A * HD_A + 3 * W_B + W_C
VMEM_LIMIT = 56 << 20
A_STEP_BLOCKS = 8


def _cparams(*sem):
    return pltpu.CompilerParams(dimension_semantics=sem, vmem_limit_bytes=VMEM_LIMIT)


def _dot(a, b):
    return jnp.dot(a, b, preferred_element_type=F32)


def _dot_nt(a, b):
    return lax.dot_general(a, b, (((1,), (1,)), ((), ())), preferred_element_type=F32)


def _split_bf16(x):
    hi = x.astype(BF16)
    lo = (x - hi.astype(F32)).astype(BF16)
    return hi, lo


def _mod_body(c_ref, w_ref, b_ref, o_ref):
    c = c_ref[...]
    s = c * jax.nn.sigmoid(c)
    s_hi, s_lo = _split_bf16(s)
    w_hi, w_lo = _split_bf16(w_ref[0])
    o_ref[0] = _dot(s_hi, w_hi) + _dot(s_lo, w_hi) + _dot(s_hi, w_lo) + b_ref[0]


def _modulation(cvec, w_ada, b_ada):
    depth, d, n = w_ada.shape
    rows = cvec.shape[0]
    tn = 768
    return pl.pallas_call(
        _mod_body,
        grid=(depth, n // tn),
        in_specs=[pl.BlockSpec((rows, d), lambda l, j: (0, 0)),
                  pl.BlockSpec((1, d, tn), lambda l, j: (l, 0, j)),
                  pl.BlockSpec((1, 1, tn), lambda l, j: (l, 0, j))],
        out_specs=pl.BlockSpec((1, rows, tn), lambda l, j: (l, 0, j)),
        out_shape=jax.ShapeDtypeStruct((depth, rows, n), F32),
        compiler_params=_cparams("parallel", "parallel"),
        name="adaln_mod",
    )(cvec, w_ada, b_ada.reshape(depth, 1, n))


def _inproj_body(*refs, latent):
    if latent:
        (x_ref, mod_ref, g1_ref, w_ref, gains_ref, s64_ref, s32_ref, ca_ref, sa_ref, cb_ref, sb_ref,
         qa_ref, ka_ref, va_ref, qb_ref, kb_ref, vb_ref, u_ref, u_scr) = refs
    else:
        (x_ref, mod_ref, g1_ref, w_ref, gains_ref, s64_ref, s32_ref,
         qa_ref, ka_ref, va_ref, qb_ref, kb_ref, vb_ref, u_ref, u_scr) = refs
    d = D_MODEL
    x = x_ref[0]
    mod = mod_ref[0]
    xn = x * lax.rsqrt(jnp.mean(x * x, axis=-1, keepdims=True) + EPS) * g1_ref[...]
    h = xn * (1.0 + mod[:, d:2 * d]) + mod[:, 0:d]
    acc = _dot(h.astype(BF16), w_ref[...])

    tm = x.shape[0]
    lane = lax.broadcasted_iota(jnp.int32, (tm, LANES), 1)
    first_a = (lane % 32) < 16
    first_b = (lane % 16) < 8

    def normed(xb, seg_ref, inv_n, gain):
        ss = _dot((xb * xb).astype(BF16), seg_ref[...])
        return xb * lax.rsqrt(ss * inv_n + EPS) * gain

    def rope_a(y):
        if not latent:
            return y
        sw = jnp.where(first_a, pltpu.roll(y, LANES - 16, 1), pltpu.roll(y, 16, 1))
        return y * ca_ref[...] + sw * sa_ref[...]

    def rope_b(y):
        if not latent:
            return y
        sw = jnp.where(first_b, pltpu.roll(y, LANES - 8, 1), pltpu.roll(y, 8, 1))
        return y * cb_ref[...] + sw * sb_ref[...]

    gains = gains_ref[...]
    off = 0
    for b in range(H_A):
        y = normed(acc[:, off:off + LANES], s64_ref, 1.0 / HD_A, gains[0:1])
        qa_ref[0, :, b * LANES:(b + 1) * LANES] = rope_a(y).astype(qa_ref.dtype)
        off += LANES
    y = normed(acc[:, off:off + LANES], s64_ref, 1.0 / HD_A, gains[1:2])
    ka_ref[0] = rope_a(y).astype(ka_ref.dtype)
    off += LANES
    va_ref[0] = acc[:, off:off + LANES].astype(va_ref.dtype)
    off += LANES
    for b in range(W_B // LANES):
        y = normed(acc[:, off:off + LANES], s32_ref, 1.0 / DC_B, gains[2:3])
        qb_ref[0, :, b * LANES:(b + 1) * LANES] = rope_b(y).astype(qb_ref.dtype)
        off += LANES
    for b in range(W_B // LANES):
        y = normed(acc[:, off:off + LANES], s32_ref, 1.0 / DC_B, gains[3:4])
        kb_ref[0, :, b * LANES:(b + 1) * LANES] = rope_b(y).astype(kb_ref.dtype)
        off += LANES
    vb_ref[0] = acc[:, off:off + W_B].astype(vb_ref.dtype)
    off += W_B
    for blk in range(W_C // LANES):
        u_scr[blk] = acc[:, off + blk * LANES:off + (blk + 1) * LANES]
    pw = 2 * SSM_CH
    for t in range(SSM_T):
        for blk in range(W_C // LANES):
            xt = u_scr[blk, pl.ds(t, tm // SSM_T, stride=SSM_T), :]
            for pp in range(LANES // pw):
                u_ref[blk * (LANES // pw) + pp, :, t * pw:(t + 1) * pw] = xt[:, pp * pw:(pp + 1) * pw].astype(u_ref.dtype)


def _inproj(x, mod, g1, w_in_p, gains, seg64, seg32, rope, kv_dtype):
    bsz, seq, d = x.shape
    latent = rope is not None
    tm = next(t for t in (1024, 512, 256) if seq % t == 0)
    bm = mod.shape[0]
    mod_idx = (lambda b, i: (b, 0, 0)) if bm > 1 else (lambda b, i: (0, 0, 0))
    const2 = lambda b, i: (0, 0)
    tok = lambda w: pl.BlockSpec((1, tm, w), lambda b, i: (b, i, 0))
    in_specs = [tok(d),
                pl.BlockSpec((1, 1, 6 * d), mod_idx),
                pl.BlockSpec((1, d), const2),
                pl.BlockSpec((d, IN_COLS_P), const2),
                pl.BlockSpec((4, LANES), const2),
                pl.BlockSpec((LANES, LANES), const2),
                pl.BlockSpec((LANES, LANES), const2)]
    args = [x, mod, g1, w_in_p, gains, seg64, seg32]
    if latent:
        in_specs += [pl.BlockSpec((tm, LANES), lambda b, i: (i, 0))] * 4
        args += list(rope)
    widths = (QA_COLS, KV_A * HD_A, KV_A * HD_A, W_B, W_B, W_B)
    dtypes = (BF16, kv_dtype, kv_dtype, BF16, kv_dtype, kv_dtype)
    nt = seq // tm
    rows = tm // SSM_T
    u_spec = pl.BlockSpec((N_PAIR, rows, SSM_ROW), lambda b, i: (0, b * nt + i, 0))
    u_shape = jax.ShapeDtypeStruct((N_PAIR, bsz * seq // SSM_T, SSM_ROW), BF16)
    return pl.pallas_call(
        functools.partial(_inproj_body, latent=latent),
        grid=(bsz, nt),
        in_specs=in_specs,
        out_specs=[tok(w) for w in widths] + [u_spec],
        out_shape=[jax.ShapeDtypeStruct((bsz, seq, w), dt) for w, dt in zip(widths, dtypes)] + [u_shape],
        scratch_shapes=[pltpu.VMEM((W_C // LANES, tm, LANES), F32)],
        compiler_params=_cparams("parallel", "parallel"),
        name="inproj_latent" if latent else "inproj_ctx",
    )(*args)


def _attn_a_body(sink_ref, q_ref, *refs, latent, nblk, nstep):
    o_ref = refs[-1]
    nk = (len(refs) - 1) // 2
    ks = [r[0].astype(BF16) for r in refs[:nk]]
    vs = [r[0].astype(BF16) for r in refs[nk:2 * nk]]
    rows = GQ_A * BLOCK
    rowi = lax.broadcasted_iota(jnp.int32, (rows, 1), 0)
    lane = lax.broadcasted_iota(jnp.int32, (BLOCK, LANES), 1)
    if latent:
        cols = 3 * BLOCK + ks[3].shape[0]
        r = lax.broadcasted_iota(jnp.int32, (rows, cols), 0) & (BLOCK - 1)
        c = lax.broadcasted_iota(jnp.int32, (rows, cols), 1)
        own_k = [ks[1][t * BLOCK:(t + 1) * BLOCK] for t in range(nstep)]
        own_v = [vs[1][t * BLOCK:(t + 1) * BLOCK] for t in range(nstep)]
        band_k = [ks[0]] + own_k + [ks[2]]
        band_v = [vs[0]] + own_v + [vs[2]]
    for t in range(nstep):
        qrows = slice(t * BLOCK, (t + 1) * BLOCK)
        if latent:
            kcat = jnp.concatenate(band_k[t:t + 3] + [ks[3]], axis=0)
            vcat = jnp.concatenate(band_v[t:t + 3] + [vs[3]], axis=0)
            qblk = pl.program_id(1) * nstep + t
            p_off = jnp.where(qblk > 0, 0, 2 * BLOCK)
            n_off = jnp.where(qblk < nblk - 1, 0, 2 * BLOCK)
            prev_ok = (c >= r + p_off) | (c >= BLOCK)
            next_ok = ((c - 2 * BLOCK + n_off) <= r) | (c < 2 * BLOCK) | (c >= 3 * BLOCK)
            valid = prev_ok & next_ok
        else:
            kcat, vcat = ks[0], vs[0]
        heads = []
        for j in range(KV_A):
            q3 = jnp.concatenate([q_ref[0, qrows, (GQ_A * j + g) * LANES:(GQ_A * j + g + 1) * LANES]
                                  for g in range(GQ_A)], axis=0)
            s = _dot_nt(q3, kcat)
            if latent:
                s = jnp.where(valid, s, NEG)
            sink = jnp.where(rowi < BLOCK, sink_ref[GQ_A * j],
                             jnp.where(rowi < 2 * BLOCK, sink_ref[GQ_A * j + 1], sink_ref[GQ_A * j + 2]))
            m = jnp.maximum(jnp.max(s, axis=-1, keepdims=True), sink)
            e = jnp.exp(s - m)
            den = jnp.sum(e, axis=-1, keepdims=True) + jnp.exp(sink - m)
            o = _dot(e.astype(BF16), vcat) / den
            for g in range(GQ_A):
                heads.append((j, o[g * BLOCK:(g + 1) * BLOCK]))
        for blk in range(H_A // 2):
            (j0, o0), (j1, o1) = heads[2 * blk], heads[2 * blk + 1]
            lo = o0 if j0 == 0 else pltpu.roll(o0, HD_A, 1)
            hi = o1 if j1 == 1 else pltpu.roll(o1, HD_A, 1)
            o_ref[0, qrows, blk * LANES:(blk + 1) * LANES] = jnp.where(lane < HD_A, lo, hi).astype(o_ref.dtype)


def _attn_a(qa, ka, va, sink, ctx_kv):
    bsz, seq, _ = qa.shape
    nblk = seq // BLOCK
    latent = ctx_kv is not None
    kvw = KV_A * HD_A
    nb = A_STEP_BLOCKS if nblk % A_STEP_BLOCKS == 0 else nblk
    if latent:
        past = ctx_kv[0].shape[1]
        band = [pl.BlockSpec((1, BLOCK, kvw), lambda b, i: (b, jnp.maximum(nb * i - 1, 0), 0)),
                pl.BlockSpec((1, nb * BLOCK, kvw), lambda b, i: (b, i, 0)),
                pl.BlockSpec((1, BLOCK, kvw), lambda b, i: (b, jnp.minimum(nb * i + nb, nblk - 1), 0)),
                pl.BlockSpec((1, past, kvw), lambda b, i: (b, 0, 0))]
        kv_specs = band + band
        kv_args = [ka, ka, ka, ctx_kv[0], va, va, va, ctx_kv[1]]
    else:
        kv_specs = [pl.BlockSpec((1, seq, kvw), lambda b, i: (b, 0, 0))] * 2
        kv_args = [ka, va]
    return pl.pallas_call(
        functools.partial(_attn_a_body, latent=latent, nblk=nblk, nstep=nb),
        grid=(bsz, nblk // nb),
        in_specs=[pl.BlockSpec(memory_space=pltpu.SMEM),
                  pl.BlockSpec((1, nb * BLOCK, QA_COLS), lambda b, i: (b, i, 0))] + kv_specs,
        out_specs=pl.BlockSpec((1, nb * BLOCK, W_A), lambda b, i: (b, i, 0)),
        out_shape=jax.ShapeDtypeStruct((bsz, seq, W_A), BF16),
        compiler_params=_cparams("parallel", "parallel"),
        name="attn_a_latent" if latent else "attn_a_ctx",
    )(sink, qa, *kv_args)


def _attn_b_body(lam_ref, gain_ref, q_ref, *refs, part_lens, lam_init, kc):
    npart = len(part_lens)
    k_refs, v_refs = refs[:npart], refs[npart:2 * npart]
    o_ref, s_scr, vm_scr = refs[2 * npart:]
    tq = q_ref.shape[1]
    chunks = []
    col = 0
    for p, plen in enumerate(part_lens):
        step = min(kc, plen)
        for start in range(0, plen, step):
            chunks.append((p, start, col, step))
            col += step

    @pl.when(pl.program_id(2) == 0)
    def _():
        off = 0
        for p, plen in enumerate(part_lens):
            v = v_refs[p][0].astype(BF16)
            lane_v = lax.broadcasted_iota(jnp.int32, (plen, LANES), 1)
            for h in range(2):
                own = (lane_v >= h * HD_B) & (lane_v < (h + 1) * HD_B)
                ones = jnp.where(lane_v == (1 - h) * HD_B, 1.0, 0.0).astype(BF16)
                vm_scr[h, off:off + plen, :] = jnp.where(own, v, ones)
            off += plen

    lv = lam_ref[...]
    lam = (jnp.exp(jnp.sum(lv[0:1] * lv[1:2], axis=-1, keepdims=True))
           - jnp.exp(jnp.sum(lv[2:3] * lv[3:4], axis=-1, keepdims=True)) + lam_init)
    q = q_ref[0]
    lane_q = lax.broadcasted_iota(jnp.int32, (tq, LANES), 1)
    total = jnp.zeros((tq, LANES), F32)
    for h in range(2):
        qc = [jnp.where((lane_q >= h * HD_B + c * DC_B) & (lane_q < h * HD_B + (c + 1) * DC_B), q, jnp.zeros_like(q))
              for c in range(2)]
        rows = [slice(c * tq, (c + 1) * tq) for c in range(2)]
        macc = [None, None]
        for p, start, col, step in chunks:
            kch = k_refs[p][0, start:start + step, :].astype(BF16)
            for c in range(2):
                s = _dot_nt(qc[c], kch)
                s_scr[rows[c], col:col + step] = s
                for j in range(step // LANES):
                    t = s[:, j * LANES:(j + 1) * LANES]
                    macc[c] = t if macc[c] is None else jnp.maximum(macc[c], t)
        m = [jnp.max(macc[c], axis=-1, keepdims=True) for c in range(2)]
        acc = [jnp.zeros((tq, LANES), F32) for _ in range(2)]
        for p, start, col, step in chunks:
            vch = vm_scr[h, col:col + step, :]
            for c in range(2):
                e = jnp.exp2(s_scr[rows[c], col:col + step] - m[c]).astype(BF16)
                acc[c] = acc[c] + _dot(e, vch)
        o2 = [acc[c] / jnp.sum(jnp.where(lane_q == (1 - h) * HD_B, acc[c], 0.0), axis=-1, keepdims=True)
              for c in range(2)]
        own = (lane_q >= h * HD_B) & (lane_q < (h + 1) * HD_B)
        total = total + jnp.where(own, o2[0] - lam * o2[1], 0.0)
    sq = total * total
    ss_lo = jnp.sum(jnp.where(lane_q < HD_B, sq, 0.0), axis=-1, keepdims=True)
    ss_hi = jnp.sum(jnp.where(lane_q >= HD_B, sq, 0.0), axis=-1, keepdims=True)
    rinv = jnp.where(lane_q < HD_B, lax.rsqrt(ss_lo * (1.0 / HD_B) + EPS), lax.rsqrt(ss_hi * (1.0 / HD_B) + EPS))
    o_ref[0] = (total * rinv * gain_ref[...] * (1.0 - lam_init)).astype(o_ref.dtype)


def _attn_b(qb, k_parts, v_parts, lam_b, gain, lam_init):
    bsz, seq, _ = qb.shape
    tq = next(t for t in (1024, 512, 256) if seq % t == 0)
    part_lens = tuple(k.shape[1] for k in k_parts)
    lk = sum(part_lens)
    kv_specs = [pl.BlockSpec((1, n, LANES), lambda b, hp, i: (b, 0, hp)) for n in part_lens]
    return pl.pallas_call(
        functools.partial(_attn_b_body, part_lens=part_lens, lam_init=lam_init, kc=512),
        grid=(bsz, W_B // LANES, seq // tq),
        in_specs=[pl.BlockSpec((4, DC_B), lambda b, hp, i: (0, 0)),
                  pl.BlockSpec((1, LANES), lambda b, hp, i: (0, 0)),
                  pl.BlockSpec((1, tq, LANES), lambda b, hp, i: (b, i, hp))] + kv_specs + kv_specs,
        out_specs=pl.BlockSpec((1, tq, LANES), lambda b, hp, i: (b, i, hp)),
        out_shape=jax.ShapeDtypeStruct((bsz, seq, W_B), BF16),
        scratch_shapes=[pltpu.VMEM((2 * tq, lk), F32), pltpu.VMEM((2, lk, LANES), BF16)],
        compiler_params=_cparams("parallel", "parallel", "arbitrary"),
        name="attn_b_latent" if len(k_parts) > 1 else "attn_b_ctx",
    )(lam_b, gain, qb, *k_parts, *v_parts)


def _ssm_body(u_ref, m_ref, g_ref, cc_ref, a_ref, h0_ref, y_ref, fin_ref, s_scr, h_scr, *, nb, nc):
    u = u_ref[0]
    col = lambda k: slice(k * LANES, (k + 1) * LANES)
    s = _dot(u, g_ref[0])
    for k in range(4):
        s_scr[k] = s[:, col(k)]
    a = a_ref[0]
    afr, afi, abr, abi = (jnp.broadcast_to(a[k:k + 1], (nb, LANES)) for k in range(4))
    h0 = h0_ref[0]

    def step(c, carry):
        fr, fi, br, bi = carry
        rf = pl.ds(c, nb, stride=nc)
        rb = pl.ds(nc - 1 - c, nb, stride=nc)
        h_scr[0, rf, :] = fr
        h_scr[1, rf, :] = fi
        h_scr[2, rb, :] = br
        h_scr[3, rb, :] = bi
        nfr = afr * fr - afi * fi + s_scr[0, rf, :]
        nfi = afr * fi + afi * fr + s_scr[1, rf, :]
        nbr = abr * br - abi * bi + s_scr[2, rb, :]
        nbi = abr * bi + abi * br + s_scr[3, rb, :]
        return nfr, nfi, nbr, nbi

    fin = lax.fori_loop(0, nc, step, tuple(h0[:, col(k)] for k in range(4)), unroll=16)
    for k in range(4):
        fin_ref[0, :, col(k)] = fin[k]
    hin = jnp.concatenate([h_scr[k] for k in range(4)], axis=1).astype(BF16)
    y = _dot(u, m_ref[0]) + _dot(hin, cc_ref[0])
    y_ref[0] = y.astype(y_ref.dtype)


def _ssm(u_rows, mats, h0, nb):
    npair, rows, w = u_rows.shape
    nc = rows // nb
    mat_spec = pl.BlockSpec((1, w, w), lambda p: (p, 0, 0))
    return pl.pallas_call(
        functools.partial(_ssm_body, nb=nb, nc=nc),
        grid=(npair,),
        in_specs=[pl.BlockSpec((1, rows, w), lambda p: (p, 0, 0)), mat_spec, mat_spec, mat_spec,
                  pl.BlockSpec((1, 4, LANES), lambda p: (p, 0, 0)),
                  pl.BlockSpec((1, nb, w), lambda p: (p, 0, 0))],
        out_specs=[pl.BlockSpec((1, rows, w), lambda p: (p, 0, 0)),
                   pl.BlockSpec((1, nb, w), lambda p: (p, 0, 0))],
        out_shape=[jax.ShapeDtypeStruct((npair, rows, w), BF16),
                   jax.ShapeDtypeStruct((npair, nb, w), F32)],
        scratch_shapes=[pltpu.VMEM((4, rows, LANES), F32), pltpu.VMEM((4, rows, LANES), F32)],
        compiler_params=_cparams("parallel"),
        name="ssm_scan",
    )(u_rows, mats["m"], mats["g"], mats["cc"], mats["a16"], h0)


def _ssm_matrices(lp):
    t = SSM_T
    ks = jnp.arange(t + 1, dtype=F32)
    dirs = []
    for d in range(2):
        lam = lax.complex(lp["ssm_lam_re"][d].astype(F32), lp["ssm_lam_im"][d].astype(F32))
        dt = jnp.exp(lp["ssm_log_dt"][d].astype(F32))[:, None]
        a_bar = jnp.exp(lam * dt)
        b_bar = ((a_bar - 1.0) / lam)[..., None] * lax.complex(lp["ssm_b_re"][d].astype(F32),
                                                               lp["ssm_b_im"][d].astype(F32))
        c_mat = lax.complex(lp["ssm_c_re"][d].astype(F32), lp["ssm_c_im"][d].astype(F32))
        pw = jnp.exp((lam * dt)[None] * ks[:, None, None].astype(jnp.complex64))
        kern = jnp.real(jnp.einsum("gop,kgp,gpi->gkoi", c_mat, pw[:t], b_bar))
        dirs.append((pw, b_bar, c_mat, kern))
    (pw_f, bb_f, cm_f, k_f), (pw_b, bb_b, cm_b, k_b) = dirs
    eye2 = jnp.eye(2, dtype=F32)
    ch, pw2 = SSM_CH, 2 * SSM_CH
    hi = lax.Precision.HIGHEST

    def pair_bd(x):
        r, c = x.shape[1:]
        return jnp.einsum("pgrc,gh->pgrhc", x.reshape(N_PAIR, 2, r, c), eye2.astype(x.dtype)).reshape(N_PAIR, 2 * r, 2 * c)

    def pair_vec(x):
        return x.reshape(x.shape[0], N_PAIR, 2 * P_C).transpose(1, 0, 2)

    def lag_blocks(kern):
        x = kern.transpose(0, 1, 3, 2).reshape(N_PAIR, 2, t, ch, ch)
        return jnp.einsum("pglic,gh->plgihc", x, eye2).reshape(N_PAIR, t, pw2, pw2)
    kp_f, kp_b = lag_blocks(k_f), lag_blocks(k_b)
    d_blk = pair_bd(lp["ssm_d"].astype(F32)[:, :, None] * jnp.eye(ch, dtype=F32)[None])
    center = (kp_f[:, 0] + kp_b[:, 0] + d_blk)[:, None]
    band = jnp.concatenate([kp_b[:, :0:-1], center, kp_f[:, 1:]], axis=1)
    band = band.transpose(0, 2, 1, 3).reshape(N_PAIR, pw2, (2 * t - 1) * pw2)
    m_p = jnp.concatenate([band[:, :, (t - 1 - s) * pw2:(t - 1 - s) * pw2 + SSM_ROW] for s in range(t)], axis=1)

    def inject(pw_sel, b_bar):
        x1 = jnp.repeat(pair_vec(pw_sel), pw2, axis=1)
        x2 = jnp.tile(pair_bd(b_bar.transpose(0, 2, 1)), (1, t, 1))
        return x1 * x2
    g_f = inject(pw_f[t - 1 - jnp.arange(t)], bb_f)
    g_b = inject(pw_b[jnp.arange(t)], bb_b)
    g_p = jnp.concatenate([jnp.real(g_f), jnp.imag(g_f), jnp.real(g_b), jnp.imag(g_b)], axis=2)

    lane = jnp.arange(SSM_ROW)
    exp_t = (jnp.arange(t)[:, None] == lane[None, :] // pw2).astype(F32)
    exp_c = (jnp.arange(pw2)[:, None] == lane[None, :] % pw2).astype(F32)

    def widen(x, e):
        f = lambda v: jnp.einsum("pqk,kx->pqx", v, e, precision=hi)
        return lax.complex(f(jnp.real(x)), f(jnp.imag(x)))

    def readout(pw_sel, c_mat):
        y1 = widen(pair_vec(pw_sel).transpose(0, 2, 1), exp_t)
        y2 = widen(pair_bd(c_mat.transpose(0, 2, 1)), exp_c)
        return y1 * y2
    z_f = readout(pw_f[1 + jnp.arange(t)], cm_f)
    z_b = readout(pw_b[t - jnp.arange(t)], cm_b)
    cc_p = jnp.concatenate([jnp.real(z_f), -jnp.imag(z_f), jnp.real(z_b), -jnp.imag(z_b)], axis=1)
    a16 = jnp.stack([jnp.real(pw_f[t]), jnp.imag(pw_f[t]), jnp.real(pw_b[t]), jnp.imag(pw_b[t])], axis=0)
    a16 = a16.reshape(4, N_PAIR, 2 * P_C).transpose(1, 0, 2)
    return dict(m=m_p.astype(BF16), g=g_p.astype(BF16), cc=cc_p.astype(BF16), a16=a16)


def _ssm_state_rows(s_re, s_im):
    bsz = s_re.shape[0]
    parts = [s_re[:, 0], s_im[:, 0], s_re[:, 1], s_im[:, 1]]
    st = jnp.stack([p.reshape(bsz, N_PAIR, 2 * P_C) for p in parts], axis=2)
    return st.transpose(1, 0, 2, 3).reshape(N_PAIR, bsz, 8 * P_C).astype(F32)


def _ssm_state_unrows(fin):
    npair, bsz, _ = fin.shape
    st = fin.reshape(npair, bsz, 4, 2, P_C).transpose(1, 2, 0, 3, 4).reshape(bsz, 4, G_C, P_C)
    return jnp.stack([st[:, 0], st[:, 2]], axis=1), jnp.stack([st[:, 1], st[:, 3]], axis=1)


def _route(scores, bias):
    tm = scores.shape[1]
    biased = scores + bias
    iota8 = lax.broadcasted_iota(jnp.int32, (PER_GROUP, tm), 0)
    grp = [biased[PER_GROUP * g:PER_GROUP * (g + 1)] for g in range(N_EXP_GROUPS)]
    gscore = []
    for v in grp:
        m1 = jnp.max(v, axis=0, keepdims=True)
        first = jnp.min(jnp.where(v == m1, iota8, PER_GROUP), axis=0, keepdims=True)
        m2 = jnp.max(jnp.where(iota8 == first, -jnp.inf, v), axis=0, keepdims=True)
        gscore.append(m1 + m2)
    masked = []
    for g in range(N_EXP_GROUPS):
        rank = jnp.zeros((1, tm), jnp.int32)
        for o in range(N_EXP_GROUPS):
            if o == g:
                continue
            ahead = (gscore[o] >= gscore[g]) if o < g else (gscore[o] > gscore[g])
            rank = rank + jnp.where(ahead, 1, 0)
        masked.append(jnp.where(rank < TOPK_GROUPS, grp[g], -jnp.inf))
    chosen = [None] * N_EXP_GROUPS
    for _ in range(TOP_K):
        best = masked[0]
        for v in masked[1:]:
            best = jnp.maximum(best, v)
        best = jnp.max(best, axis=0, keepdims=True)
        first = jnp.full((1, tm), N_EXPERTS, jnp.int32)
        for g, v in enumerate(masked):
            cand = jnp.min(jnp.where(v == best, iota8 + PER_GROUP * g, N_EXPERTS), axis=0, keepdims=True)
            first = jnp.minimum(first, cand)
        for g in range(N_EXP_GROUPS):
            hit = (iota8 + PER_GROUP * g) == first
            chosen[g] = hit if chosen[g] is None else (chosen[g] | hit)
            masked[g] = jnp.where(hit, -jnp.inf, masked[g])
    w = [jnp.where(chosen[g], scores[PER_GROUP * g:PER_GROUP * (g + 1)], 0.0) for g in range(N_EXP_GROUPS)]
    wsum = w[0]
    for v in w[1:]:
        wsum = wsum + v
    wsum = jnp.sum(wsum, axis=0, keepdims=True)
    return jnp.concatenate([v / wsum * ROUTED_SCALE for v in w], axis=0)


def _post_body(x_ref, oa_ref, ob_ref, y_ref, mod_ref, wglu_ref, wout_ref, g2_ref, wrh_ref, wrl_ref, br_ref,
               x1_ref, h2_ref, gate_ref, y_scr):
    d = D_MODEL
    tm = x_ref.shape[1]
    pw = 2 * SSM_CH
    for t in range(SSM_T):
        for blk in range(W_C // LANES):
            piece = jnp.concatenate([y_ref[blk * (LANES // pw) + pp, :, t * pw:(t + 1) * pw].astype(F32)
                                     for pp in range(LANES // pw)], axis=1)
            y_scr[blk, pl.ds(t, tm // SSM_T, stride=SSM_T), :] = piece
    g = jax.nn.gelu(jnp.concatenate([y_scr[blk] for blk in range(W_C // LANES)], axis=1))
    oc = g * jax.nn.sigmoid(_dot(g.astype(BF16), wglu_ref[...]))
    mix = (_dot(oa_ref[0], wout_ref[0:W_A]) + _dot(ob_ref[0], wout_ref[W_A:W_A + W_B])
           + _dot(oc.astype(BF16), wout_ref[W_A + W_B:]))
    mod = mod_ref[0]
    x1 = x_ref[0] + mod[:, 2 * d:3 * d] * mix
    x1_ref[0] = x1
    xn = x1 * lax.rsqrt(jnp.mean(x1 * x1, axis=-1, keepdims=True) + EPS) * g2_ref[...]
    h2 = xn * (1.0 + mod[:, 4 * d:5 * d]) + mod[:, 3 * d:4 * d]
    h_hi, h_lo = _split_bf16(h2)
    h2_ref[0] = h_hi
    logits = _dot_nt(wrh_ref[...], h_hi) + _dot_nt(wrh_ref[...], h_lo) + _dot_nt(wrl_ref[...], h_hi)
    gate_ref[0] = _route(jax.nn.sigmoid(logits), br_ref[...]).T


def _post_mix(x, oa, ob, y, mod, w_glu, w_out, g2, wr_hi, wr_lo, b_r):
    bsz, seq, d = x.shape
    tm = next(t for t in (1024, 512, 256) if seq % t == 0)
    bm = mod.shape[0]
    mod_idx = (lambda b, i: (b, 0, 0)) if bm > 1 else (lambda b, i: (0, 0, 0))
    const2 = lambda b, i: (0, 0)
    tok = lambda w: pl.BlockSpec((1, tm, w), lambda b, i: (b, i, 0))
    nt = seq // tm
    return pl.pallas_call(
        _post_body,
        grid=(bsz, nt),
        in_specs=[tok(d), tok(W_A), tok(W_B),
                  pl.BlockSpec((N_PAIR, tm // SSM_T, SSM_ROW), lambda b, i: (0, b * nt + i, 0)),
                  pl.BlockSpec((1, 1, 6 * d), mod_idx),
                  pl.BlockSpec((W_C, W_C), const2),
                  pl.BlockSpec((d, d), const2),
                  pl.BlockSpec((1, d), const2),
                  pl.BlockSpec((N_EXPERTS, d), const2),
                  pl.BlockSpec((N_EXPERTS, d), const2),
                  pl.BlockSpec((N_EXPERTS, 1), const2)],
        out_specs=[tok(d), tok(d), tok(N_EXPERTS)],
        out_shape=[jax.ShapeDtypeStruct((bsz, seq, d), F32),
                   jax.ShapeDtypeStruct((bsz, seq, d), BF16),
                   jax.ShapeDtypeStruct((bsz, seq, N_EXPERTS), F32)],
        scratch_shapes=[pltpu.VMEM((W_C // LANES, tm, LANES), F32)],
        compiler_params=_cparams("parallel", "parallel"),
        name="post_mix",
    )(x, oa, ob, y, mod, w_glu, w_out, g2, wr_hi, wr_lo, b_r)


def _moe_body(x1_ref, h_ref, gate_ref, g2_ref, w1_ref, w3_ref, w2_ref, c1_ref, c32_ref, ex_ref,
              s1_ref, s3_ref, s2_ref, o_ref, acc_ref, h8_ref, hs_ref):
    j = pl.program_id(1)

    @pl.when(j == 0)
    def _():
        h = h_ref[...]
        a = _dot(h, s1_ref[...])
        acc_ref[...] = _dot((a * jax.nn.sigmoid(a) * _dot(h, s3_ref[...])).astype(BF16), s2_ref[...])
        hf = h.astype(F32)
        sc = jnp.maximum(jnp.max(jnp.abs(hf), axis=-1, keepdims=True), F8_TINY) * (1.0 / F8_RANGE)
        hs_ref[...] = sc
        h8_ref[...] = (hf * (1.0 / sc)).astype(F8)

    ne = w1_ref.shape[0]
    h8 = h8_ref[...]
    hs = hs_ref[...]
    a = _dot(h8, jnp.concatenate([w1_ref[e] for e in range(ne)], axis=1)) * c1_ref[...] * hs
    b = _dot(h8, jnp.concatenate([w3_ref[e] for e in range(ne)], axis=1))
    gexp = _dot(jnp.concatenate(_split_bf16(gate_ref[...]), axis=1), ex_ref[...])
    hid = a * jax.nn.sigmoid(a) * b * gexp * c32_ref[...]
    sc = jnp.maximum(jnp.max(jnp.abs(hid), axis=-1, keepdims=True), F8_TINY) * (1.0 / F8_RANGE)
    w2 = w2_ref[...].reshape(ne * F_EXP, w2_ref.shape[-1])
    acc_ref[...] += _dot((hid * (1.0 / sc)).astype(F8), w2) * (sc * hs)

    @pl.when(j == pl.num_programs(1) - 1)
    def _():
        o_ref[...] = x1_ref[...] + g2_ref[0] * acc_ref[...]


def _moe(x1, h2, gates, mod, seq, ew, layer, expand, ws1, ws3, ws2):
    tokens, d = x1.shape
    bm = mod.shape[0]
    span = seq if bm > 1 else tokens
    tm = next(t for t in (1024, 512, 256) if span % t == 0)
    per_b = seq // tm if bm > 1 else 1
    mod_idx = (lambda i, j: (i // per_b, 0, 5)) if bm > 1 else (lambda i, j: (0, 0, 5))
    ne = 8
    fc = ne * F_EXP
    hidden = ew["w2"].shape[1] * F_EXP
    const2 = lambda i, j: (0, 0)
    chunk_row = pl.BlockSpec((None, 1, fc), lambda i, j: (layer, 0, j))
    return pl.pallas_call(
        _moe_body,
        grid=(tokens // tm, hidden // fc),
        in_specs=[pl.BlockSpec((tm, d), lambda i, j: (i, 0)),
                  pl.BlockSpec((tm, d), lambda i, j: (i, 0)),
                  pl.BlockSpec((tm, N_EXPERTS), lambda i, j: (i, 0)),
                  pl.BlockSpec((1, 1, d), mod_idx),
                  pl.BlockSpec((None, ne, d, F_EXP), lambda i, j: (layer, j, 0, 0)),
                  pl.BlockSpec((None, ne, d, F_EXP), lambda i, j: (layer, j, 0, 0)),
                  pl.BlockSpec((None, ne, F_EXP, d), lambda i, j: (layer, j, 0, 0)),
                  chunk_row, chunk_row,
                  pl.BlockSpec((2 * N_EXPERTS, fc), lambda i, j: (0, j)),
                  pl.BlockSpec((d, F_SHARED), const2),
                  pl.BlockSpec((d, F_SHARED), const2),
                  pl.BlockSpec((F_SHARED, d), const2)],
        out_specs=pl.BlockSpec((tm, d), lambda i, j: (i, 0)),
        out_shape=jax.ShapeDtypeStruct((tokens, d), F32),
        scratch_shapes=[pltpu.VMEM((tm, d), F32), pltpu.VMEM((tm, d), F8), pltpu.VMEM((tm, 1), F32)],
        compiler_params=_cparams("parallel", "arbitrary"),
        name="moe",
    )(x1, h2, gates, mod, ew["w1"], ew["w3"], ew["w2"], ew["c1"], ew["c32"], expand, ws1, ws3, ws2)


def _expert_fp8_body(w_ref, q_ref, c_ref):
    for e in range(w_ref.shape[1]):
        w = w_ref[0, e]
        top = jnp.max(jnp.max(jnp.abs(w), axis=0, keepdims=True), axis=1, keepdims=True)
        sc = jnp.maximum(top, F8_TINY) * (1.0 / F8_RANGE)
        q_ref[0, e] = (w * (1.0 / sc)).astype(F8)
        c_ref[0, :, e * F_EXP:(e + 1) * F_EXP] = jnp.broadcast_to(sc, (1, F_EXP))


def _expert_fp8(w):
    depth, ne, r, c = w.shape
    blk = 8
    return pl.pallas_call(
        _expert_fp8_body,
        grid=(depth, ne // blk),
        in_specs=[pl.BlockSpec((1, blk, r, c), lambda l, i: (l, i, 0, 0))],
        out_specs=[pl.BlockSpec((1, blk, r, c), lambda l, i: (l, i, 0, 0)),
                   pl.BlockSpec((1, 1, blk * F_EXP), lambda l, i: (l, 0, i))],
        out_shape=[jax.ShapeDtypeStruct((depth, ne, r, c), F8), jax.ShapeDtypeStruct((depth, 1, ne * F_EXP), F32)],
        compiler_params=_cparams("parallel", "parallel"),
        name="expert_fp8",
    )(w)


def _prep_experts(p):
    w1, c1 = _expert_fp8(p["w_e1"].astype(F32))
    w3, c3 = _expert_fp8(p["w_e3"].astype(F32))
    w2, c2 = _expert_fp8(p["w_e2"].astype(F32))
    return dict(w1=w1, w3=w3, w2=w2, c1=c1, c32=c3 * c2)


def _rope_tables(seq):
    pos = jnp.arange(seq)
    row = (pos // GRID_W).astype(F32)[:, None]
    colp = (pos % GRID_W).astype(F32)[:, None]
    lane = jnp.arange(LANES)

    def table(width):
        half, quarter = width // 2, width // 4
        i = lane % width
        freq = ROPE_BASE ** (-(2.0 * (i % quarter).astype(F32)) / half)
        ang = jnp.where((i // half) == 0, row, colp) * freq[None, :]
        sign = jnp.where((i % half) < quarter, -1.0, 1.0)
        return jnp.cos(ang), jnp.sin(ang) * sign[None, :]

    ca, sa = table(HD_A)
    cb, sb = table(DC_B)
    return ca, sa, cb, sb


def _prep_layer(p):
    d = D_MODEL
    w_in = p["w_in"]
    src = jnp.arange(W_A)
    dst = (src // HD_A) * LANES + (src // HD_A // GQ_A) * HD_A + src % HD_A
    place = (dst[:, None] == jnp.arange(QA_COLS)[None, :]).astype(F32)
    qa_pad = jnp.dot(w_in[:, :W_A].astype(F32), place, precision=lax.Precision.HIGHEST)
    w_in_p = jnp.concatenate([qa_pad, w_in[:, W_A:].astype(F32)], axis=1).astype(BF16)
    gains = jnp.stack([jnp.tile(p["q_norm_a"], LANES // HD_A) * (HD_A ** -0.5),
                       jnp.tile(p["k_norm_a"], LANES // HD_A),
                       jnp.tile(p["q_norm_b"], LANES // DC_B) * (DC_B ** -0.5 * LOG2E),
                       jnp.tile(p["k_norm_b"], LANES // DC_B)], axis=0).astype(F32)
    lp = {k: p[k] for k in ("ssm_lam_re", "ssm_lam_im", "ssm_log_dt", "ssm_b_re", "ssm_b_im",
                            "ssm_c_re", "ssm_c_im", "ssm_d")}
    wr_hi, wr_lo = _split_bf16(p["w_router"].T.astype(F32))
    return dict(
        w_in_p=w_in_p, gains=gains,
        g1=p["norm1_g"].reshape(1, d).astype(F32), g2=p["norm2_g"].reshape(1, d).astype(F32),
        sink=p["sink_a"].astype(F32), lam_b=p["lam_b"].astype(F32),
        subln=jnp.tile(p["subln_b"], LANES // HD_B).reshape(1, LANES).astype(F32),
        ssm=_ssm_matrices(lp),
        w_glu=p["w_glu"].astype(BF16), w_out=p["w_out"].astype(BF16),
        wr_hi=wr_hi, wr_lo=wr_lo, b_r=p["b_router"].reshape(N_EXPERTS, 1).astype(F32),
        ws1=p["w_s1"].astype(BF16), ws3=p["w_s3"].astype(BF16), ws2=p["w_s2"].astype(BF16),
    )


def _trunk_layer(x, mod, lw, consts, ctx):
    bsz, seq, d = x.shape
    latent = ctx is not None
    rope = consts["rope"] if latent else None
    kv_dtype = BF16 if latent else F32
    grp = 1 if latent else next(g for g in (4, 2, 1) if bsz % g == 0)
    tok = lambda a: a.reshape(bsz // grp, grp * seq, a.shape[-1])
    per_seq = lambda a: a.reshape(bsz, seq, a.shape[-1])
    qa, ka, va, qb, kb, vb, u = _inproj(tok(x), mod, lw["g1"], lw["w_in_p"], lw["gains"],
                                        consts["seg64"], consts["seg32"], rope, kv_dtype)
    qa, ka, va, qb, kb, vb = (per_seq(a) for a in (qa, ka, va, qb, kb, vb))
    if latent:
        oa = _attn_a(qa, ka, va, lw["sink"], (ctx["ak"], ctx["av"]))
        ob = _attn_b(qb, [kb, ctx["bk"]], [vb, ctx["bv"]], lw["lam_b"], lw["subln"], lw["lam_init"])
        h0 = ctx["h0"]
    else:
        oa = _attn_a(qa, ka, va, lw["sink"], None)
        ob = _attn_b(qb, [kb], [vb], lw["lam_b"], lw["subln"], lw["lam_init"])
        h0 = jnp.zeros((N_PAIR, bsz, 8 * P_C), F32)
    y_rows, fin = _ssm(u, lw["ssm"], h0, bsz)
    x1, h2, gates = _post_mix(tok(x), tok(oa), tok(ob), y_rows, mod, lw["w_glu"], lw["w_out"], lw["g2"],
                                lw["wr_hi"], lw["wr_lo"], lw["b_r"])
    out = _moe(x1.reshape(bsz * seq, d), h2.reshape(bsz * seq, d), gates.reshape(bsz * seq, N_EXPERTS), mod, seq,
               consts["experts"], lw["layer"], consts["expand"], lw["ws1"], lw["ws3"], lw["ws2"])
    return out.reshape(bsz, seq, d), (ka, va, kb, vb, fin)


def kernel(x_prompt, x_sample, cache_a_k, cache_a_v, cache_b_k, cache_b_v, state_ssm_re, state_ssm_im, c, c_ctx, norm1_g, norm2_g, w_ada, b_ada, w_in, q_norm_a, k_norm_a, sink_a, q_norm_b, k_norm_b, lam_b, subln_b, ssm_lam_re, ssm_lam_im, ssm_log_dt, ssm_b_re, ssm_b_im, ssm_c_re, ssm_c_im, ssm_d, w_glu, w_out, w_router, b_router, w_e1, w_e3, w_e2, w_s1, w_s3, w_s2):
    p = dict(norm1_g=norm1_g, norm2_g=norm2_g, w_in=w_in, q_norm_a=q_norm_a, k_norm_a=k_norm_a, sink_a=sink_a,
             q_norm_b=q_norm_b, k_norm_b=k_norm_b, lam_b=lam_b, subln_b=subln_b,
             ssm_lam_re=ssm_lam_re, ssm_lam_im=ssm_lam_im, ssm_log_dt=ssm_log_dt, ssm_b_re=ssm_b_re,
             ssm_b_im=ssm_b_im, ssm_c_re=ssm_c_re, ssm_c_im=ssm_c_im, ssm_d=ssm_d, w_glu=w_glu, w_out=w_out,
             w_router=w_router, b_router=b_router, w_e1=w_e1, w_e3=w_e3, w_e2=w_e2,
             w_s1=w_s1, w_s3=w_s3, w_s2=w_s2)
    depth = w_in.shape[0]
    bsz, seq, d = x_prompt.shape
    dbsz, dseq, _ = x_sample.shape
    past = cache_a_k.shape[3]

    mod_rows = 16
    cvec = jnp.concatenate([c.astype(F32), c_ctx.astype(F32)[None],
                            jnp.zeros((mod_rows - dbsz - 1, d), F32)], axis=0)
    mods = _modulation(cvec, w_ada.astype(F32), b_ada.astype(F32))

    lane = jnp.arange(LANES)
    hidden = N_EXPERTS * F_EXP
    consts = dict(
        rope=_rope_tables(dseq),
        seg64=(lane[:, None] // HD_A == lane[None, :] // HD_A).astype(BF16),
        seg32=(lane[:, None] // DC_B == lane[None, :] // DC_B).astype(BF16),
        expand=(jnp.arange(2 * N_EXPERTS)[:, None] % N_EXPERTS == jnp.arange(hidden)[None, :] // F_EXP).astype(BF16),
    )

    xp, xs = x_prompt, x_sample
    ak, av, bk, bv, sre, sim = [], [], [], [], [], []
    prepared = jax.vmap(_prep_layer)(p)
    consts["experts"] = _prep_experts(p)
    for l in range(depth):
        lw = jax.tree.map(lambda v: v[l], prepared)
        lw["layer"] = l
        lw["lam_init"] = 0.8 - 0.6 * math.exp(-0.3 * l)
        mod_lat = mods[l, :dbsz][:, None, :]
        mod_ctx = mods[l, dbsz:dbsz + 1][:, None, :]
        xp, (k_a, v_a, k_b, v_b, fin) = _trunk_layer(xp, mod_ctx, lw, consts, None)
        ak.append(k_a.reshape(bsz, seq, KV_A, HD_A).transpose(0, 2, 1, 3))
        av.append(v_a.reshape(bsz, seq, KV_A, HD_A).transpose(0, 2, 1, 3))
        bk.append(k_b.reshape(bsz, seq, H_B, 2, DC_B).transpose(0, 2, 3, 1, 4))
        bv.append(v_b.reshape(bsz, seq, H_B, HD_B).transpose(0, 2, 1, 3))
        f_re, f_im = _ssm_state_unrows(fin)
        sre.append(f_re)
        sim.append(f_im)
        ctx = dict(
            ak=cache_a_k[:, l].transpose(0, 2, 1, 3).reshape(dbsz, past, KV_A * HD_A).astype(BF16),
            av=cache_a_v[:, l].transpose(0, 2, 1, 3).reshape(dbsz, past, KV_A * HD_A).astype(BF16),
            bk=cache_b_k[:, l].transpose(0, 3, 1, 2, 4).reshape(dbsz, past, W_B).astype(BF16),
            bv=cache_b_v[:, l].transpose(0, 2, 1, 3).reshape(dbsz, past, W_B).astype(BF16),
            h0=_ssm_state_rows(state_ssm_re[:, l], state_ssm_im[:, l]),
        )
        xs, _ = _trunk_layer(xs, mod_lat, lw, consts, ctx)
    return (xp, xs, jnp.stack(ak, axis=1), jnp.stack(av, axis=1), jnp.stack(bk, axis=1),
            jnp.stack(bv, axis=1), jnp.stack(sre, axis=1), jnp.stack(sim, axis=1))
```

```python
import functools
import math

import jax
import jax.numpy as jnp
from jax import lax
from jax.experimental import pallas as pl
from jax.experimental.pallas import tpu as pltpu

F32 = jnp.float32
BF16 = jnp.bfloat16
F8 = jnp.float8_e4m3fn
F8_RANGE = 384.0
F8_TINY = 1e-30

D_MODEL = 1024
GRID_W = 64
BLOCK = 128
H_A, KV_A, HD_A = 6, 2, 64
GQ_A = H_A // KV_A
W_A = H_A * HD_A
H_B, HD_B = 4, 64
DC_B = HD_B // 2
W_B = H_B * HD_B
SSM_CH = 16
W_C = D_MODEL - W_A - W_B
G_C = W_C // SSM_CH
P_C = 64
N_EXPERTS, TOP_K, F_EXP, F_SHARED = 64, 6, 128, 256
N_EXP_GROUPS, TOPK_GROUPS = 8, 4
PER_GROUP = N_EXPERTS // N_EXP_GROUPS
ROUTED_SCALE = 2.5
ROPE_BASE = 10000.0
EPS = 1e-6
NEG = -1e30
LOG2E = 1.4426950408889634

LANES = 128
SSM_T = 16
N_PAIR = G_C // 2
SSM_ROW = 2 * SSM_T * SSM_CH
QA_COLS = H_A * LANES
IN_COLS_P = QA_COLS + 2 * KV_A * HD_A + 3 * W_B + W_C
VMEM_LIMIT = 56 << 20
A_STEP_BLOCKS = 8
INPROJ_ROW_BLOCK = 256


def _cparams(*sem):
    return pltpu.CompilerParams(dimension_semantics=sem, vmem_limit_bytes=VMEM_LIMIT)


def _dot(a, b):
    return jnp.dot(a, b, preferred_element_type=F32)


def _dot_nt(a, b):
    return lax.dot_general(a, b, (((1,), (1,)), ((), ())), preferred_element_type=F32)


def _split_bf16(x):
    hi = x.astype(BF16)
    lo = (x - hi.astype(F32)).astype(BF16)
    return hi, lo


def _mod_body(c_ref, w_ref, b_ref, o_ref):
    c = c_ref[...]
    s = c * jax.nn.sigmoid(c)
    s_hi, s_lo = _split_bf16(s)
    w_hi, w_lo = _split_bf16(w_ref[0])
    o_ref[0] = _dot(s_hi, w_hi) + _dot(s_lo, w_hi) + _dot(s_hi, w_lo) + b_ref[0]


def _modulation(cvec, w_ada, b_ada):
    depth, d, n = w_ada.shape
    rows = cvec.shape[0]
    tn = 768
    return pl.pallas_call(
        _mod_body,
        grid=(depth, n // tn),
        in_specs=[pl.BlockSpec((rows, d), lambda l, j: (0, 0)),
                  pl.BlockSpec((1, d, tn), lambda l, j: (l, 0, j)),
                  pl.BlockSpec((1, 1, tn), lambda l, j: (l, 0, j))],
        out_specs=pl.BlockSpec((1, rows, tn), lambda l, j: (l, 0, j)),
        out_shape=jax.ShapeDtypeStruct((depth, rows, n), F32),
        compiler_params=_cparams("parallel", "parallel"),
        name="adaln_mod",
    )(cvec, w_ada, b_ada.reshape(depth, 1, n))


def _inproj_body(*refs, latent):
    if latent:
        (x_ref, mod_ref, g1_ref, w_ref, gains_ref, s64_ref, s32_ref, ca_ref, sa_ref, cb_ref, sb_ref,
         qa_ref, ka_ref, va_ref, qb_ref, kb_ref, vb_ref, u_ref, u_scr) = refs
    else:
        (x_ref, mod_ref, g1_ref, w_ref, gains_ref, s64_ref, s32_ref,
         qa_ref, ka_ref, va_ref, qb_ref, kb_ref, vb_ref, u_ref, u_scr) = refs
    d = D_MODEL
    mod = mod_ref[0]
    gains = gains_ref[...]
    tm = x_ref.shape[1]
    rb = min(INPROJ_ROW_BLOCK, tm)
    lane = lax.broadcasted_iota(jnp.int32, (rb, LANES), 1)
    first_a = (lane % 32) < 16
    first_b = (lane % 16) < 8
    pw = 2 * SSM_CH

    def normed(xb, seg_ref, inv_n, gain):
        ss = _dot((xb * xb).astype(BF16), seg_ref[...])
        return xb * lax.rsqrt(ss * inv_n + EPS) * gain

    for r0 in range(0, tm, rb):
        rs = slice(r0, r0 + rb)
        x = x_ref[0, rs]
        xn = x * lax.rsqrt(jnp.mean(x * x, axis=-1, keepdims=True) + EPS) * g1_ref[...]
        h = xn * (1.0 + mod[:, d:2 * d]) + mod[:, 0:d]
        acc = _dot(h.astype(BF16), w_ref[...])

        def rope_a(y):
            if not latent:
                return y
            sw = jnp.where(first_a, pltpu.roll(y, LANES - 16, 1), pltpu.roll(y, 16, 1))
            return y * ca_ref[rs] + sw * sa_ref[rs]

        def rope_b(y):
            if not latent:
                return y
            sw = jnp.where(first_b, pltpu.roll(y, LANES - 8, 1), pltpu.roll(y, 8, 1))
            return y * cb_ref[rs] + sw * sb_ref[rs]

        off = 0
        for b in range(H_A):
            y = normed(acc[:, off:off + LANES], s64_ref, 1.0 / HD_A, gains[0:1])
            qa_ref[0, rs, b * LANES:(b + 1) * LANES] = rope_a(y).astype(qa_ref.dtype)
            off += LANES
        y = normed(acc[:, off:off + LANES], s64_ref, 1.0 / HD_A, gains[1:2])
        ka_ref[0, rs] = rope_a(y).astype(ka_ref.dtype)
        off += LANES
        va_ref[0, rs] = acc[:, off:off + LANES].astype(va_ref.dtype)
        off += LANES
        for b in range(W_B // LANES):
            y = normed(acc[:, off:off + LANES], s32_ref, 1.0 / DC_B, gains[2:3])
            qb_ref[0, rs, b * LANES:(b + 1) * LANES] = rope_b(y).astype(qb_ref.dtype)
            off += LANES
        for b in range(W_B // LANES):
            y = normed(acc[:, off:off + LANES], s32_ref, 1.0 / DC_B, gains[3:4])
            kb_ref[0, rs, b * LANES:(b + 1) * LANES] = rope_b(y).astype(kb_ref.dtype)
            off += LANES
        vb_ref[0, rs] = acc[:, off:off + W_B].astype(vb_ref.dtype)
        off += W_B
        for blk in range(W_C // LANES):
            u_scr[blk, rs] = acc[:, off + blk * LANES:off + (blk + 1) * LANES]
        crows = slice(r0 // SSM_T, (r0 + rb) // SSM_T)
        for t in range(SSM_T):
            for blk in range(W_C // LANES):
                xt = u_scr[blk, pl.ds(r0 + t, rb // SSM_T, stride=SSM_T), :]
                for pp in range(LANES // pw):
                    u_ref[blk * (LANES // pw) + pp, crows, t * pw:(t + 1) * pw] = (
                        xt[:, pp * pw:(pp + 1) * pw].astype(u_ref.dtype))


def _inproj(x, mod, g1, w_in_p, gains, seg64, seg32, rope, kv_dtype):
    bsz, seq, d = x.shape
    latent = rope is not None
    tm = next(t for t in (1024, 512, 256) if seq % t == 0)
    bm = mod.shape[0]
    mod_idx = (lambda b, i: (b, 0, 0)) if bm > 1 else (lambda b, i: (0, 0, 0))
    const2 = lambda b, i: (0, 0)
    tok = lambda w: pl.BlockSpec((1, tm, w), lambda b, i: (b, i, 0))
    in_specs = [tok(d),
                pl.BlockSpec((1, 1, 6 * d), mod_idx),
                pl.BlockSpec((1, d), const2),
                pl.BlockSpec((d, IN_COLS_P), const2),
                pl.BlockSpec((4, LANES), const2),
                pl.BlockSpec((LANES, LANES), const2),
                pl.BlockSpec((LANES, LANES), const2)]
    args = [x, mod, g1, w_in_p, gains, seg64, seg32]
    if latent:
        in_specs += [pl.BlockSpec((tm, LANES), lambda b, i: (i, 0))] * 4
        args += list(rope)
    widths = (QA_COLS, KV_A * HD_A, KV_A * HD_A, W_B, W_B, W_B)
    dtypes = (BF16, kv_dtype, kv_dtype, BF16, kv_dtype, kv_dtype)
    nt = seq // tm
    rows = tm // SSM_T
    u_spec = pl.BlockSpec((N_PAIR, rows, SSM_ROW), lambda b, i: (0, b * nt + i, 0))
    u_shape = jax.ShapeDtypeStruct((N_PAIR, bsz * seq // SSM_T, SSM_ROW), BF16)
    return pl.pallas_call(
        functools.partial(_inproj_body, latent=latent),
        grid=(bsz, nt),
        in_specs=in_specs,
        out_specs=[tok(w) for w in widths] + [u_spec],
        out_shape=[jax.ShapeDtypeStruct((bsz, seq, w), dt) for w, dt in zip(widths, dtypes)] + [u_shape],
        scratch_shapes=[pltpu.VMEM((W_C // LANES, tm, LANES), F32)],
        compiler_params=_cparams("parallel", "parallel"),
        name="inproj_latent" if latent else "inproj_ctx",
    )(*args)


def _attn_a_body(sink_ref, q_ref, *refs, latent, nblk, nstep):
    o_ref = refs[-1]
    nk = (len(refs) - 1) // 2
    ks = [r[0].astype(BF16) for r in refs[:nk]]
    vs = [r[0].astype(BF16) for r in refs[nk:2 * nk]]
    rows = GQ_A * BLOCK
    rowi = lax.broadcasted_iota(jnp.int32, (rows, 1), 0)
    lane = lax.broadcasted_iota(jnp.int32, (BLOCK, LANES), 1)
    if latent:
        cols = 3 * BLOCK + ks[3].shape[0]
        r = lax.broadcasted_iota(jnp.int32, (rows, cols), 0) & (BLOCK - 1)
        c = lax.broadcasted_iota(jnp.int32, (rows, cols), 1)
        own_k = [ks[1][t * BLOCK:(t + 1) * BLOCK] for t in range(nstep)]
        own_v = [vs[1][t * BLOCK:(t + 1) * BLOCK] for t in range(nstep)]
        band_k = [ks[0]] + own_k + [ks[2]]
        band_v = [vs[0]] + own_v + [vs[2]]
    for t in range(nstep):
        qrows = slice(t * BLOCK, (t + 1) * BLOCK)
        if latent:
            kcat = jnp.concatenate(band_k[t:t + 3] + [ks[3]], axis=0)
            vcat = jnp.concatenate(band_v[t:t + 3] + [vs[3]], axis=0)
            qblk = pl.program_id(1) * nstep + t
            p_off = jnp.where(qblk > 0, 0, 2 * BLOCK)
            n_off = jnp.where(qblk < nblk - 1, 0, 2 * BLOCK)
            prev_ok = (c >= r + p_off) | (c >= BLOCK)
            next_ok = ((c - 2 * BLOCK + n_off) <= r) | (c < 2 * BLOCK) | (c >= 3 * BLOCK)
            valid = prev_ok & next_ok
        else:
            kcat, vcat = ks[0], vs[0]
        heads = []
        for j in range(KV_A):
            q3 = jnp.concatenate([q_ref[0, qrows, (GQ_A * j + g) * LANES:(GQ_A * j + g + 1) * LANES]
                                  for g in range(GQ_A)], axis=0)
            s = _dot_nt(q3, kcat)
            if latent:
                s = jnp.where(valid, s, NEG)
            sink = jnp.where(rowi < BLOCK, sink_ref[GQ_A * j],
                             jnp.where(rowi < 2 * BLOCK, sink_ref[GQ_A * j + 1], sink_ref[GQ_A * j + 2]))
            m = jnp.maximum(jnp.max(s, axis=-1, keepdims=True), sink)
            e = jnp.exp(s - m)
            den = jnp.sum(e, axis=-1, keepdims=True) + jnp.exp(sink - m)
            o = _dot(e.astype(BF16), vcat) / den
            for g in range(GQ_A):
                heads.append((j, o[g * BLOCK:(g + 1) * BLOCK]))
        for blk in range(H_A // 2):
            (j0, o0), (j1, o1) = heads[2 * blk], heads[2 * blk + 1]
            lo = o0 if j0 == 0 else pltpu.roll(o0, HD_A, 1)
            hi = o1 if j1 == 1 else pltpu.roll(o1, HD_A, 1)
            o_ref[0, qrows, blk * LANES:(blk + 1) * LANES] = jnp.where(lane < HD_A, lo, hi).astype(o_ref.dtype)


def _attn_a(qa, ka, va, sink, ctx_kv):
    bsz, seq, _ = qa.shape
    nblk = seq // BLOCK
    latent = ctx_kv is not None
    kvw = KV_A * HD_A
    nb = A_STEP_BLOCKS if nblk % A_STEP_BLOCKS == 0 else nblk
    if latent:
        past = ctx_kv[0].shape[1]
        band = [pl.BlockSpec((1, BLOCK, kvw), lambda b, i: (b, jnp.maximum(nb * i - 1, 0), 0)),
                pl.BlockSpec((1, nb * BLOCK, kvw), lambda b, i: (b, i, 0)),
                pl.BlockSpec((1, BLOCK, kvw), lambda b, i: (b, jnp.minimum(nb * i + nb, nblk - 1), 0)),
                pl.BlockSpec((1, past, kvw), lambda b, i: (b, 0, 0))]
        kv_specs = band + band
        kv_args = [ka, ka, ka, ctx_kv[0], va, va, va, ctx_kv[1]]
    else:
        kv_specs = [pl.BlockSpec((1, seq, kvw), lambda b, i: (b, 0, 0))] * 2
        kv_args = [ka, va]
    return pl.pallas_call(
        functools.partial(_attn_a_body, latent=latent, nblk=nblk, nstep=nb),
        grid=(bsz, nblk // nb),
        in_specs=[pl.BlockSpec(memory_space=pltpu.SMEM),
                  pl.BlockSpec((1, nb * BLOCK, QA_COLS), lambda b, i: (b, i, 0))] + kv_specs,
        out_specs=pl.BlockSpec((1, nb * BLOCK, W_A), lambda b, i: (b, i, 0)),
        out_shape=jax.ShapeDtypeStruct((bsz, seq, W_A), BF16),
        compiler_params=_cparams("parallel", "parallel"),
        name="attn_a_latent" if latent else "attn_a_ctx",
    )(sink, qa, *kv_args)


def _attn_b_body(lam_ref, gain_ref, q_ref, *refs, part_lens, lam_init, kc):
    npart = len(part_lens)
    k_refs, v_refs = refs[:npart], refs[npart:2 * npart]
    o_ref, s_scr, vm_scr = refs[2 * npart:]
    tq = q_ref.shape[1]
    chunks = []
    col = 0
    for p, plen in enumerate(part_lens):
        step = min(kc, plen)
        for start in range(0, plen, step):
            chunks.append((p, start, col, step))
            col += step

    @pl.when(pl.program_id(2) == 0)
    def _():
        off = 0
        for p, plen in enumerate(part_lens):
            v = v_refs[p][0].astype(BF16)
            lane_v = lax.broadcasted_iota(jnp.int32, (plen, LANES), 1)
            for h in range(2):
                own = (lane_v >= h * HD_B) & (lane_v < (h + 1) * HD_B)
                ones = jnp.where(lane_v == (1 - h) * HD_B, 1.0, 0.0).astype(BF16)
                vm_scr[h, off:off + plen, :] = jnp.where(own, v, ones)
            off += plen

    lv = lam_ref[...]
    lam = (jnp.exp(jnp.sum(lv[0:1] * lv[1:2], axis=-1, keepdims=True))
           - jnp.exp(jnp.sum(lv[2:3] * lv[3:4], axis=-1, keepdims=True)) + lam_init)
    q = q_ref[0]
    lane_q = lax.broadcasted_iota(jnp.int32, (tq, LANES), 1)
    total = jnp.zeros((tq, LANES), F32)
    for h in range(2):
        qc = [jnp.where((lane_q >= h * HD_B + c * DC_B) & (lane_q < h * HD_B + (c + 1) * DC_B), q, jnp.zeros_like(q))
              for c in range(2)]
        rows = [slice(c * tq, (c + 1) * tq) for c in range(2)]
        macc = [None, None]
        for p, start, col, step in chunks:
            kch = k_refs[p][0, start:start + step, :].astype(BF16)
            for c in range(2):
                s = _dot_nt(qc[c], kch)
                s_scr[rows[c], col:col + step] = s
                for j in range(step // LANES):
                    t = s[:, j * LANES:(j + 1) * LANES]
                    macc[c] = t if macc[c] is None else jnp.maximum(macc[c], t)
        m = [jnp.max(macc[c], axis=-1, keepdims=True) for c in range(2)]
        acc = [jnp.zeros((tq, LANES), F32) for _ in range(2)]
        for p, start, col, step in chunks:
            vch = vm_scr[h, col:col + step, :]
            for c in range(2):
                e = jnp.exp2(s_scr[rows[c], col:col + step] - m[c]).astype(BF16)
                acc[c] = acc[c] + _dot(e, vch)
        o2 = [acc[c] / jnp.sum(jnp.where(lane_q == (1 - h) * HD_B, acc[c], 0.0), axis=-1, keepdims=True)
              for c in range(2)]
        own = (lane_q >= h * HD_B) & (lane_q < (h + 1) * HD_B)
        total = total + jnp.where(own, o2[0] - lam * o2[1], 0.0)
    sq = total * total
    ss_lo = jnp.sum(jnp.where(lane_q < HD_B, sq, 0.0), axis=-1, keepdims=True)
    ss_hi = jnp.sum(jnp.where(lane_q >= HD_B, sq, 0.0), axis=-1, keepdims=True)
    rinv = jnp.where(lane_q < HD_B, lax.rsqrt(ss_lo * (1.0 / HD_B) + EPS), lax.rsqrt(ss_hi * (1.0 / HD_B) + EPS))
    o_ref[0] = (total * rinv * gain_ref[...] * (1.0 - lam_init)).astype(o_ref.dtype)


def _attn_b(qb, k_parts, v_parts, lam_b, gain, lam_init):
    bsz, seq, _ = qb.shape
    tq = next(t for t in (1024, 512, 256) if seq % t == 0)
    part_lens = tuple(k.shape[1] for k in k_parts)
    lk = sum(part_lens)
    kv_specs = [pl.BlockSpec((1, n, LANES), lambda b, hp, i: (b, 0, hp)) for n in part_lens]
    return pl.pallas_call(
        functools.partial(_attn_b_body, part_lens=part_lens, lam_init=lam_init, kc=512),
        grid=(bsz, W_B // LANES, seq // tq),
        in_specs=[pl.BlockSpec((4, DC_B), lambda b, hp, i: (0, 0)),
                  pl.BlockSpec((1, LANES), lambda b, hp, i: (0, 0)),
                  pl.BlockSpec((1, tq, LANES), lambda b, hp, i: (b, i, hp))] + kv_specs + kv_specs,
        out_specs=pl.BlockSpec((1, tq, LANES), lambda b, hp, i: (b, i, hp)),
        out_shape=jax.ShapeDtypeStruct((bsz, seq, W_B), BF16),
        scratch_shapes=[pltpu.VMEM((2 * tq, lk), F32), pltpu.VMEM((2, lk, LANES), BF16)],
        compiler_params=_cparams("parallel", "parallel", "arbitrary"),
        name="attn_b_latent" if len(k_parts) > 1 else "attn_b_ctx",
    )(lam_b, gain, qb, *k_parts, *v_parts)


def _ssm_body(u_ref, m_ref, g_ref, cc_ref, a_ref, h0_ref, y_ref, fin_ref, s_scr, h_scr, *, nb, nc):
    u = u_ref[0]
    col = lambda k: slice(k * LANES, (k + 1) * LANES)
    s = _dot(u, g_ref[0])
    for k in range(4):
        s_scr[k] = s[:, col(k)]
    a = a_ref[0]
    afr, afi, abr, abi = (jnp.broadcast_to(a[k:k + 1], (nb, LANES)) for k in range(4))
    h0 = h0_ref[0]

    def step(c, carry):
        fr, fi, br, bi = carry
        rf = pl.ds(c, nb, stride=nc)
        rb = pl.ds(nc - 1 - c, nb, stride=nc)
        h_scr[0, rf, :] = fr
        h_scr[1, rf, :] = fi
        h_scr[2, rb, :] = br
        h_scr[3, rb, :] = bi
        nfr = afr * fr - afi * fi + s_scr[0, rf, :]
        nfi = afr * fi + afi * fr + s_scr[1, rf, :]
        nbr = abr * br - abi * bi + s_scr[2, rb, :]
        nbi = abr * bi + abi * br + s_scr[3, rb, :]
        return nfr, nfi, nbr, nbi

    fin = lax.fori_loop(0, nc, step, tuple(h0[:, col(k)] for k in range(4)), unroll=16)
    for k in range(4):
        fin_ref[0, :, col(k)] = fin[k]
    hin = jnp.concatenate([h_scr[k] for k in range(4)], axis=1).astype(BF16)
    y = _dot(u, m_ref[0]) + _dot(hin, cc_ref[0])
    y_ref[0] = y.astype(y_ref.dtype)


def _ssm(u_rows, mats, h0, nb):
    npair, rows, w = u_rows.shape
    nc = rows // nb
    mat_spec = pl.BlockSpec((1, w, w), lambda p: (p, 0, 0))
    return pl.pallas_call(
        functools.partial(_ssm_body, nb=nb, nc=nc),
        grid=(npair,),
        in_specs=[pl.BlockSpec((1, rows, w), lambda p: (p, 0, 0)), mat_spec, mat_spec, mat_spec,
                  pl.BlockSpec((1, 4, LANES), lambda p: (p, 0, 0)),
                  pl.BlockSpec((1, nb, w), lambda p: (p, 0, 0))],
        out_specs=[pl.BlockSpec((1, rows, w), lambda p: (p, 0, 0)),
                   pl.BlockSpec((1, nb, w), lambda p: (p, 0, 0))],
        out_shape=[jax.ShapeDtypeStruct((npair, rows, w), BF16),
                   jax.ShapeDtypeStruct((npair, nb, w), F32)],
        scratch_shapes=[pltpu.VMEM((4, rows, LANES), F32), pltpu.VMEM((4, rows, LANES), F32)],
        compiler_params=_cparams("parallel"),
        name="ssm_scan",
    )(u_rows, mats["m"], mats["g"], mats["cc"], mats["a16"], h0)


def _ssm_matrices(lp):
    t = SSM_T
    ks = jnp.arange(t + 1, dtype=F32)
    dirs = []
    for d in range(2):
        lam = lax.complex(lp["ssm_lam_re"][d].astype(F32), lp["ssm_lam_im"][d].astype(F32))
        dt = jnp.exp(lp["ssm_log_dt"][d].astype(F32))[:, None]
        a_bar = jnp.exp(lam * dt)
        b_bar = ((a_bar - 1.0) / lam)[..., None] * lax.complex(lp["ssm_b_re"][d].astype(F32),
                                                               lp["ssm_b_im"][d].astype(F32))
        c_mat = lax.complex(lp["ssm_c_re"][d].astype(F32), lp["ssm_c_im"][d].astype(F32))
        pw = jnp.exp((lam * dt)[None] * ks[:, None, None].astype(jnp.complex64))
        kern = jnp.real(jnp.einsum("gop,kgp,gpi->gkoi", c_mat, pw[:t], b_bar))
        dirs.append((pw, b_bar, c_mat, kern))
    (pw_f, bb_f, cm_f, k_f), (pw_b, bb_b, cm_b, k_b) = dirs
    eye2 = jnp.eye(2, dtype=F32)
    ch, pw2 = SSM_CH, 2 * SSM_CH
    hi = lax.Precision.HIGHEST

    def pair_bd(x):
        r, c = x.shape[1:]
        return jnp.einsum("pgrc,gh->pgrhc", x.reshape(N_PAIR, 2, r, c), eye2.astype(x.dtype)).reshape(N_PAIR, 2 * r, 2 * c)

    def pair_vec(x):
        return x.reshape(x.shape[0], N_PAIR, 2 * P_C).transpose(1, 0, 2)

    def lag_blocks(kern):
        x = kern.transpose(0, 1, 3, 2).reshape(N_PAIR, 2, t, ch, ch)
        return jnp.einsum("pglic,gh->plgihc", x, eye2).reshape(N_PAIR, t, pw2, pw2)
    kp_f, kp_b = lag_blocks(k_f), lag_blocks(k_b)
    d_blk = pair_bd(lp["ssm_d"].astype(F32)[:, :, None] * jnp.eye(ch, dtype=F32)[None])
    center = (kp_f[:, 0] + kp_b[:, 0] + d_blk)[:, None]
    band = jnp.concatenate([kp_b[:, :0:-1], center, kp_f[:, 1:]], axis=1)
    band = band.transpose(0, 2, 1, 3).reshape(N_PAIR, pw2, (2 * t - 1) * pw2)
    m_p = jnp.concatenate([band[:, :, (t - 1 - s) * pw2:(t - 1 - s) * pw2 + SSM_ROW] for s in range(t)], axis=1)

    def inject(pw_sel, b_bar):
        x1 = jnp.repeat(pair_vec(pw_sel), pw2, axis=1)
        x2 = jnp.tile(pair_bd(b_bar.transpose(0, 2, 1)), (1, t, 1))
        return x1 * x2
    g_f = inject(pw_f[t - 1 - jnp.arange(t)], bb_f)
    g_b = inject(pw_b[jnp.arange(t)], bb_b)
    g_p = jnp.concatenate([jnp.real(g_f), jnp.imag(g_f), jnp.real(g_b), jnp.imag(g_b)], axis=2)

    lane = jnp.arange(SSM_ROW)
    exp_t = (jnp.arange(t)[:, None] == lane[None, :] // pw2).astype(F32)
    exp_c = (jnp.arange(pw2)[:, None] == lane[None, :] % pw2).astype(F32)

    def widen(x, e):
        f = lambda v: jnp.einsum("pqk,kx->pqx", v, e, precision=hi)
        return lax.complex(f(jnp.real(x)), f(jnp.imag(x)))

    def readout(pw_sel, c_mat):
        y1 = widen(pair_vec(pw_sel).transpose(0, 2, 1), exp_t)
        y2 = widen(pair_bd(c_mat.transpose(0, 2, 1)), exp_c)
        return y1 * y2
    z_f = readout(pw_f[1 + jnp.arange(t)], cm_f)
    z_b = readout(pw_b[t - jnp.arange(t)], cm_b)
    cc_p = jnp.concatenate([jnp.real(z_f), -jnp.imag(z_f), jnp.real(z_b), -jnp.imag(z_b)], axis=1)
    a16 = jnp.stack([jnp.real(pw_f[t]), jnp.imag(pw_f[t]), jnp.real(pw_b[t]), jnp.imag(pw_b[t])], axis=0)
    a16 = a16.reshape(4, N_PAIR, 2 * P_C).transpose(1, 0, 2)
    return dict(m=m_p.astype(BF16), g=g_p.astype(BF16), cc=cc_p.astype(BF16), a16=a16)


def _ssm_state_rows(s_re, s_im):
    bsz = s_re.shape[0]
    parts = [s_re[:, 0], s_im[:, 0], s_re[:, 1], s_im[:, 1]]
    st = jnp.stack([p.reshape(bsz, N_PAIR, 2 * P_C) for p in parts], axis=2)
    return st.transpose(1, 0, 2, 3).reshape(N_PAIR, bsz, 8 * P_C).astype(F32)


def _ssm_state_unrows(fin):
    npair, bsz, _ = fin.shape
    st = fin.reshape(npair, bsz, 4, 2, P_C).transpose(1, 2, 0, 3, 4).reshape(bsz, 4, G_C, P_C)
    return jnp.stack([st[:, 0], st[:, 2]], axis=1), jnp.stack([st[:, 1], st[:, 3]], axis=1)


def _route(scores, bias):
    tm = scores.shape[1]
    biased = scores + bias
    iota8 = lax.broadcasted_iota(jnp.int32, (PER_GROUP, tm), 0)
    grp = [biased[PER_GROUP * g:PER_GROUP * (g + 1)] for g in range(N_EXP_GROUPS)]
    gscore = []
    for v in grp:
        m1 = jnp.max(v, axis=0, keepdims=True)
        first = jnp.min(jnp.where(v == m1, iota8, PER_GROUP), axis=0, keepdims=True)
        m2 = jnp.max(jnp.where(iota8 == first, -jnp.inf, v), axis=0, keepdims=True)
        gscore.append(m1 + m2)
    masked = []
    for g in range(N_EXP_GROUPS):
        rank = jnp.zeros((1, tm), jnp.int32)
        for o in range(N_EXP_GROUPS):
            if o == g:
                continue
            ahead = (gscore[o] >= gscore[g]) if o < g else (gscore[o] > gscore[g])
            rank = rank + jnp.where(ahead, 1, 0)
        masked.append(jnp.where(rank < TOPK_GROUPS, grp[g], -jnp.inf))
    chosen = [None] * N_EXP_GROUPS
    for _ in range(TOP_K):
        best = masked[0]
        for v in masked[1:]:
            best = jnp.maximum(best, v)
        best = jnp.max(best, axis=0, keepdims=True)
        first = jnp.full((1, tm), N_EXPERTS, jnp.int32)
        for g, v in enumerate(masked):
            cand = jnp.min(jnp.where(v == best, iota8 + PER_GROUP * g, N_EXPERTS), axis=0, keepdims=True)
            first = jnp.minimum(first, cand)
        for g in range(N_EXP_GROUPS):
            hit = (iota8 + PER_GROUP * g) == first
            chosen[g] = hit if chosen[g] is None else (chosen[g] | hit)
            masked[g] = jnp.where(hit, -jnp.inf, masked[g])
    w = [jnp.where(chosen[g], scores[PER_GROUP * g:PER_GROUP * (g + 1)], 0.0) for g in range(N_EXP_GROUPS)]
    wsum = w[0]
    for v in w[1:]:
        wsum = wsum + v
    wsum = jnp.sum(wsum, axis=0, keepdims=True)
    return jnp.concatenate([v / wsum * ROUTED_SCALE for v in w], axis=0)


def _post_body(x_ref, oa_ref, ob_ref, y_ref, mod_ref, wglu_ref, wout_ref, g2_ref, wrh_ref, wrl_ref, br_ref,
               x1_ref, h2_ref, gate_ref, y_scr):
    d = D_MODEL
    tm = x_ref.shape[1]
    pw = 2 * SSM_CH
    for t in range(SSM_T):
        for blk in range(W_C // LANES):
            piece = jnp.concatenate([y_ref[blk * (LANES // pw) + pp, :, t * pw:(t + 1) * pw].astype(F32)
                                     for pp in range(LANES // pw)], axis=1)
            y_scr[blk, pl.ds(t, tm // SSM_T, stride=SSM_T), :] = piece
    g = jax.nn.gelu(jnp.concatenate([y_scr[blk] for blk in range(W_C // LANES)], axis=1))
    oc = g * jax.nn.sigmoid(_dot(g.astype(BF16), wglu_ref[...]))
    mix = (_dot(oa_ref[0], wout_ref[0:W_A]) + _dot(ob_ref[0], wout_ref[W_A:W_A + W_B])
           + _dot(oc.astype(BF16), wout_ref[W_A + W_B:]))
    mod = mod_ref[0]
    x1 = x_ref[0] + mod[:, 2 * d:3 * d] * mix
    x1_ref[0] = x1
    xn = x1 * lax.rsqrt(jnp.mean(x1 * x1, axis=-1, keepdims=True) + EPS) * g2_ref[...]
    h2 = xn * (1.0 + mod[:, 4 * d:5 * d]) + mod[:, 3 * d:4 * d]
    h_hi, h_lo = _split_bf16(h2)
    h2_ref[0] = h_hi
    logits = _dot_nt(wrh_ref[...], h_hi) + _dot_nt(wrh_ref[...], h_lo) + _dot_nt(wrl_ref[...], h_hi)
    gate_ref[0] = _route(jax.nn.sigmoid(logits), br_ref[...]).T


def _post_mix(x, oa, ob, y, mod, w_glu, w_out, g2, wr_hi, wr_lo, b_r):
    bsz, seq, d = x.shape
    tm = next(t for t in (1024, 512, 256) if seq % t == 0)
    bm = mod.shape[0]
    mod_idx = (lambda b, i: (b, 0, 0)) if bm > 1 else (lambda b, i: (0, 0, 0))
    const2 = lambda b, i: (0, 0)
    tok = lambda w: pl.BlockSpec((1, tm, w), lambda b, i: (b, i, 0))
    nt = seq // tm
    return pl.pallas_call(
        _post_body,
        grid=(bsz, nt),
        in_specs=[tok(d), tok(W_A), tok(W_B),
                  pl.BlockSpec((N_PAIR, tm // SSM_T, SSM_ROW), lambda b, i: (0, b * nt + i, 0)),
                  pl.BlockSpec((1, 1, 6 * d), mod_idx),
                  pl.BlockSpec((W_C, W_C), const2),
                  pl.BlockSpec((d, d), const2),
                  pl.BlockSpec((1, d), const2),
                  pl.BlockSpec((N_EXPERTS, d), const2),
                  pl.BlockSpec((N_EXPERTS, d), const2),
                  pl.BlockSpec((N_EXPERTS, 1), const2)],
        out_specs=[tok(d), tok(d), tok(N_EXPERTS)],
        out_shape=[jax.ShapeDtypeStruct((bsz, seq, d), F32),
                   jax.ShapeDtypeStruct((bsz, seq, d), BF16),
                   jax.ShapeDtypeStruct((bsz, seq, N_EXPERTS), F32)],
        scratch_shapes=[pltpu.VMEM((W_C // LANES, tm, LANES), F32)],
        compiler_params=_cparams("parallel", "parallel"),
        name="post_mix",
    )(x, oa, ob, y, mod, w_glu, w_out, g2, wr_hi, wr_lo, b_r)


def _moe_body(x1_ref, h_ref, gate_ref, g2_ref, w1_ref, w3_ref, w2_ref, c1_ref, c32_ref, ex_ref,
              s1_ref, s3_ref, s2_ref, o_ref, acc_ref, h8_ref, hs_ref):
    j = pl.program_id(1)

    @pl.when(j == 0)
    def _():
        h = h_ref[...]
        a = _dot(h, s1_ref[...])
        acc_ref[...] = _dot((a * jax.nn.sigmoid(a) * _dot(h, s3_ref[...])).astype(BF16), s2_ref[...])
        hf = h.astype(F32)
        sc = jnp.maximum(jnp.max(jnp.abs(hf), axis=-1, keepdims=True), F8_TINY) * (1.0 / F8_RANGE)
        hs_ref[...] = sc
        h8_ref[...] = (hf * (1.0 / sc)).astype(F8)

    ne = w1_ref.shape[0]
    h8 = h8_ref[...]
    hs = hs_ref[...]
    a = _dot(h8, jnp.concatenate([w1_ref[e] for e in range(ne)], axis=1)) * c1_ref[...] * hs
    b = _dot(h8, jnp.concatenate([w3_ref[e] for e in range(ne)], axis=1))
    gexp = _dot(jnp.concatenate(_split_bf16(gate_ref[...]), axis=1), ex_ref[...])
    hid = a * jax.nn.sigmoid(a) * b * gexp * c32_ref[...]
    sc = jnp.maximum(jnp.max(jnp.abs(hid), axis=-1, keepdims=True), F8_TINY) * (1.0 / F8_RANGE)
    w2 = w2_ref[...].reshape(ne * F_EXP, w2_ref.shape[-1])
    acc_ref[...] += _dot((hid * (1.0 / sc)).astype(F8), w2) * (sc * hs)

    @pl.when(j == pl.num_programs(1) - 1)
    def _():
        o_ref[...] = x1_ref[...] + g2_ref[0] * acc_ref[...]


def _moe(x1, h2, gates, mod, seq, ew, layer, expand, ws1, ws3, ws2):
    tokens, d = x1.shape
    bm = mod.shape[0]
    span = seq if bm > 1 else tokens
    tm = next(t for t in (1024, 512, 256) if span % t == 0)
    per_b = seq // tm if bm > 1 else 1
    mod_idx = (lambda i, j: (i // per_b, 0, 5)) if bm > 1 else (lambda i, j: (0, 0, 5))
    ne = 8
    fc = ne * F_EXP
    hidden = ew["w2"].shape[1] * F_EXP
    const2 = lambda i, j: (0, 0)
    chunk_row = pl.BlockSpec((None, 1, fc), lambda i, j: (layer, 0, j))
    return pl.pallas_call(
        _moe_body,
        grid=(tokens // tm, hidden // fc),
        in_specs=[pl.BlockSpec((tm, d), lambda i, j: (i, 0)),
                  pl.BlockSpec((tm, d), lambda i, j: (i, 0)),
                  pl.BlockSpec((tm, N_EXPERTS), lambda i, j: (i, 0)),
                  pl.BlockSpec((1, 1, d), mod_idx),
                  pl.BlockSpec((None, ne, d, F_EXP), lambda i, j: (layer, j, 0, 0)),
                  pl.BlockSpec((None, ne, d, F_EXP), lambda i, j: (layer, j, 0, 0)),
                  pl.BlockSpec((None, ne, F_EXP, d), lambda i, j: (layer, j, 0, 0)),
                  chunk_row, chunk_row,
                  pl.BlockSpec((2 * N_EXPERTS, fc), lambda i, j: (0, j)),
                  pl.BlockSpec((d, F_SHARED), const2),
                  pl.BlockSpec((d, F_SHARED), const2),
                  pl.BlockSpec((F_SHARED, d), const2)],
        out_specs=pl.BlockSpec((tm, d), lambda i, j: (i, 0)),
        out_shape=jax.ShapeDtypeStruct((tokens, d), F32),
        scratch_shapes=[pltpu.VMEM((tm, d), F32), pltpu.VMEM((tm, d), F8), pltpu.VMEM((tm, 1), F32)],
        compiler_params=_cparams("parallel", "arbitrary"),
        name="moe",
    )(x1, h2, gates, mod, ew["w1"], ew["w3"], ew["w2"], ew["c1"], ew["c32"], expand, ws1, ws3, ws2)


def _expert_fp8_body(w_ref, q_ref, c_ref):
    for e in range(w_ref.shape[1]):
        w = w_ref[0, e]
        top = jnp.max(jnp.max(jnp.abs(w), axis=0, keepdims=True), axis=1, keepdims=True)
        sc = jnp.maximum(top, F8_TINY) * (1.0 / F8_RANGE)
        q_ref[0, e] = (w * (1.0 / sc)).astype(F8)
        c_ref[0, :, e * F_EXP:(e + 1) * F_EXP] = jnp.broadcast_to(sc, (1, F_EXP))


def _expert_fp8(w):
    depth, ne, r, c = w.shape
    blk = 8
    return pl.pallas_call(
        _expert_fp8_body,
        grid=(depth, ne // blk),
        in_specs=[pl.BlockSpec((1, blk, r, c), lambda l, i: (l, i, 0, 0))],
        out_specs=[pl.BlockSpec((1, blk, r, c), lambda l, i: (l, i, 0, 0)),
                   pl.BlockSpec((1, 1, blk * F_EXP), lambda l, i: (l, 0, i))],
        out_shape=[jax.ShapeDtypeStruct((depth, ne, r, c), F8), jax.ShapeDtypeStruct((depth, 1, ne * F_EXP), F32)],
        compiler_params=_cparams("parallel", "parallel"),
        name="expert_fp8",
    )(w)


def _prep_experts(p):
    w1, c1 = _expert_fp8(p["w_e1"].astype(F32))
    w3, c3 = _expert_fp8(p["w_e3"].astype(F32))
    w2, c2 = _expert_fp8(p["w_e2"].astype(F32))
    return dict(w1=w1, w3=w3, w2=w2, c1=c1, c32=c3 * c2)


def _rope_tables(seq):
    pos = jnp.arange(seq)
    row = (pos // GRID_W).astype(F32)[:, None]
    colp = (pos % GRID_W).astype(F32)[:, None]
    lane = jnp.arange(LANES)

    def table(width):
        half, quarter = width // 2, width // 4
        i = lane % width
        freq = ROPE_BASE ** (-(2.0 * (i % quarter).astype(F32)) / half)
        ang = jnp.where((i // half) == 0, row, colp) * freq[None, :]
        sign = jnp.where((i % half) < quarter, -1.0, 1.0)
        return jnp.cos(ang), jnp.sin(ang) * sign[None, :]

    ca, sa = table(HD_A)
    cb, sb = table(DC_B)
    return ca, sa, cb, sb


def _prep_layer(p):
    d = D_MODEL
    w_in = p["w_in"]
    src = jnp.arange(W_A)
    dst = (src // HD_A) * LANES + (src // HD_A // GQ_A) * HD_A + src % HD_A
    place = (dst[:, None] == jnp.arange(QA_COLS)[None, :]).astype(F32)
    qa_pad = jnp.dot(w_in[:, :W_A].astype(F32), place, precision=lax.Precision.HIGHEST)
    w_in_p = jnp.concatenate([qa_pad, w_in[:, W_A:].astype(F32)], axis=1).astype(BF16)
    gains = jnp.stack([jnp.tile(p["q_norm_a"], LANES // HD_A) * (HD_A ** -0.5),
                       jnp.tile(p["k_norm_a"], LANES // HD_A),
                       jnp.tile(p["q_norm_b"], LANES // DC_B) * (DC_B ** -0.5 * LOG2E),
                       jnp.tile(p["k_norm_b"], LANES // DC_B)], axis=0).astype(F32)
    lp = {k: p[k] for k in ("ssm_lam_re", "ssm_lam_im", "ssm_log_dt", "ssm_b_re", "ssm_b_im",
                            "ssm_c_re", "ssm_c_im", "ssm_d")}
    wr_hi, wr_lo = _split_bf16(p["w_router"].T.astype(F32))
    return dict(
        w_in_p=w_in_p, gains=gains,
        g1=p["norm1_g"].reshape(1, d).astype(F32), g2=p["norm2_g"].reshape(1, d).astype(F32),
        sink=p["sink_a"].astype(F32), lam_b=p["lam_b"].astype(F32),
        subln=jnp.tile(p["subln_b"], LANES // HD_B).reshape(1, LANES).astype(F32),
        ssm=_ssm_matrices(lp),
        w_glu=p["w_glu"].astype(BF16), w_out=p["w_out"].astype(BF16),
        wr_hi=wr_hi, wr_lo=wr_lo, b_r=p["b_router"].reshape(N_EXPERTS, 1).astype(F32),
        ws1=p["w_s1"].astype(BF16), ws3=p["w_s3"].astype(BF16), ws2=p["w_s2"].astype(BF16),
    )


def _trunk_layer(x, mod, lw, consts, ctx):
    bsz, seq, d = x.shape
    latent = ctx is not None
    rope = consts["rope"] if latent else None
    kv_dtype = BF16 if latent else F32
    grp = 1 if latent else next(g for g in (4, 2, 1) if bsz % g == 0)
    tok = lambda a: a.reshape(bsz // grp, grp * seq, a.shape[-1])
    per_seq = lambda a: a.reshape(bsz, seq, a.shape[-1])
    qa, ka, va, qb, kb, vb, u = _inproj(tok(x), mod, lw["g1"], lw["w_in_p"], lw["gains"],
                                        consts["seg64"], consts["seg32"], rope, kv_dtype)
    qa, ka, va, qb, kb, vb = (per_seq(a) for a in (qa, ka, va, qb, kb, vb))
    if latent:
        oa = _attn_a(qa, ka, va, lw["sink"], (ctx["ak"], ctx["av"]))
        ob = _attn_b(qb, [kb, ctx["bk"]], [vb, ctx["bv"]], lw["lam_b"], lw["subln"], lw["lam_init"])
        h0 = ctx["h0"]
    else:
        oa = _attn_a(qa, ka, va, lw["sink"], None)
        ob = _attn_b(qb, [kb], [vb], lw["lam_b"], lw["subln"], lw["lam_init"])
        h0 = jnp.zeros((N_PAIR, bsz, 8 * P_C), F32)
    y_rows, fin = _ssm(u, lw["ssm"], h0, bsz)
    x1, h2, gates = _post_mix(tok(x), tok(oa), tok(ob), y_rows, mod, lw["w_glu"], lw["w_out"], lw["g2"],
                                lw["wr_hi"], lw["wr_lo"], lw["b_r"])
    out = _moe(x1.reshape(bsz * seq, d), h2.reshape(bsz * seq, d), gates.reshape(bsz * seq, N_EXPERTS), mod, seq,
               consts["experts"], lw["layer"], consts["expand"], lw["ws1"], lw["ws3"], lw["ws2"])
    return out.reshape(bsz, seq, d), (ka, va, kb, vb, fin)


def kernel(x_prompt, x_sample, cache_a_k, cache_a_v, cache_b_k, cache_b_v, state_ssm_re, state_ssm_im, c, c_ctx, norm1_g, norm2_g, w_ada, b_ada, w_in, q_norm_a, k_norm_a, sink_a, q_norm_b, k_norm_b, lam_b, subln_b, ssm_lam_re, ssm_lam_im, ssm_log_dt, ssm_b_re, ssm_b_im, ssm_c_re, ssm_c_im, ssm_d, w_glu, w_out, w_router, b_router, w_e1, w_e3, w_e2, w_s1, w_s3, w_s2):
    p = dict(norm1_g=norm1_g, norm2_g=norm2_g, w_in=w_in, q_norm_a=q_norm_a, k_norm_a=k_norm_a, sink_a=sink_a,
             q_norm_b=q_norm_b, k_norm_b=k_norm_b, lam_b=lam_b, subln_b=subln_b,
             ssm_lam_re=ssm_lam_re, ssm_lam_im=ssm_lam_im, ssm_log_dt=ssm_log_dt, ssm_b_re=ssm_b_re,
             ssm_b_im=ssm_b_im, ssm_c_re=ssm_c_re, ssm_c_im=ssm_c_im, ssm_d=ssm_d, w_glu=w_glu, w_out=w_out,
             w_router=w_router, b_router=b_router, w_e1=w_e1, w_e3=w_e3, w_e2=w_e2,
             w_s1=w_s1, w_s3=w_s3, w_s2=w_s2)
    depth = w_in.shape[0]
    bsz, seq, d = x_prompt.shape
    dbsz, dseq, _ = x_sample.shape
    past = cache_a_k.shape[3]

    mod_rows = 16
    cvec = jnp.concatenate([c.astype(F32), c_ctx.astype(F32)[None],
                            jnp.zeros((mod_rows - dbsz - 1, d), F32)], axis=0)
    mods = _modulation(cvec, w_ada.astype(F32), b_ada.astype(F32))

    lane = jnp.arange(LANES)
    hidden = N_EXPERTS * F_EXP
    consts = dict(
        rope=_rope_tables(dseq),
        seg64=(lane[:, None] // HD_A == lane[None, :] // HD_A).astype(BF16),
        seg32=(lane[:, None] // DC_B == lane[None, :] // DC_B).astype(BF16),
        expand=(jnp.arange(2 * N_EXPERTS)[:, None] % N_EXPERTS == jnp.arange(hidden)[None, :] // F_EXP).astype(BF16),
    )

    xp, xs = x_prompt, x_sample
    ak, av, bk, bv, sre, sim = [], [], [], [], [], []
    prepared = jax.vmap(_prep_layer)(p)
    consts["experts"] = _prep_experts(p)
    for l in range(depth):
        lw = jax.tree.map(lambda v: v[l], prepared)
        lw["layer"] = l
        lw["lam_init"] = 0.8 - 0.6 * math.exp(-0.3 * l)
        mod_lat = mods[l, :dbsz][:, None, :]
        mod_ctx = mods[l, dbsz:dbsz + 1][:, None, :]
        xp, (k_a, v_a, k_b, v_b, fin) = _trunk_layer(xp, mod_ctx, lw, consts, None)
        ak.append(k_a.reshape(bsz, seq, KV_A, HD_A).transpose(0, 2, 1, 3))
        av.append(v_a.reshape(bsz, seq, KV_A, HD_A).transpose(0, 2, 1, 3))
        bk.append(k_b.reshape(bsz, seq, H_B, 2, DC_B).transpose(0, 2, 3, 1, 4))
        bv.append(v_b.reshape(bsz, seq, H_B, HD_B).transpose(0, 2, 1, 3))
        f_re, f_im = _ssm_state_unrows(fin)
        sre.append(f_re)
        sim.append(f_im)
        ctx = dict(
            ak=cache_a_k[:, l].transpose(0, 2, 1, 3).reshape(dbsz, past, KV_A * HD_A).astype(BF16),
            av=cache_a_v[:, l].transpose(0, 2, 1, 3).reshape(dbsz, past, KV_A * HD_A).astype(BF16),
            bk=cache_b_k[:, l].transpose(0, 3, 1, 2, 4).reshape(dbsz, past, W_B).astype(BF16),
            bv=cache_b_v[:, l].transpose(0, 2, 1, 3).reshape(dbsz, past, W_B).astype(BF16),
            h0=_ssm_state_rows(state_ssm_re[:, l], state_ssm_im[:, l]),
        )
        xs, _ = _trunk_layer(xs, mod_lat, lw, consts, ctx)
    return (xp, xs, jnp.stack(ak, axis=1), jnp.stack(av, axis=1), jnp.stack(bk, axis=1),
            jnp.stack(bv, axis=1), jnp.stack(sre, axis=1), jnp.stack(sim, axis=1))
```

```python
import functools
import math

import jax
import jax.numpy as jnp
from jax import lax
from jax.experimental import pallas as pl
from jax.experimental.pallas import tpu as pltpu

F32 = jnp.float32
BF16 = jnp.bfloat16
F8 = jnp.float8_e4m3fn
F8_RANGE = 384.0
F8_TINY = 1e-30

D_MODEL = 1024
GRID_W = 64
BLOCK = 128
H_A, KV_A, HD_A = 6, 2, 64
GQ_A = H_A // KV_A
W_A = H_A * HD_A
H_B, HD_B = 4, 64
DC_B = HD_B // 2
W_B = H_B * HD_B
SSM_CH = 16
W_C = D_MODEL - W_A - W_B
G_C = W_C // SSM_CH
P_C = 64
N_EXPERTS, TOP_K, F_EXP, F_SHARED = 64, 6, 128, 256
N_EXP_GROUPS, TOPK_GROUPS = 8, 4
PER_GROUP = N_EXPERTS // N_EXP_GROUPS
ROUTED_SCALE = 2.5
ROPE_BASE = 10000.0
EPS = 1e-6
NEG = -1e30
LOG2E = 1.4426950408889634

LANES = 128
SSM_T = 16
N_PAIR = G_C // 2
SSM_ROW = 2 * SSM_T * SSM_CH
QA_COLS = H_A * LANES
IN_COLS_P = QA_COLS + 2 * KV_A * HD_A + 3 * W_B + W_C
VMEM_LIMIT = 56 << 20
A_STEP_BLOCKS = 16
INPROJ_ROW_BLOCK = 256
POST_ROW_BLOCK = 512


def _cparams(*sem):
    return pltpu.CompilerParams(dimension_semantics=sem, vmem_limit_bytes=VMEM_LIMIT)


def _dot(a, b):
    return jnp.dot(a, b, preferred_element_type=F32)


def _dot_nt(a, b):
    return lax.dot_general(a, b, (((1,), (1,)), ((), ())), preferred_element_type=F32)


def _split_bf16(x):
    hi = x.astype(BF16)
    lo = (x - hi.astype(F32)).astype(BF16)
    return hi, lo


def _mod_body(c_ref, w_ref, b_ref, o_ref):
    c = c_ref[...]
    s = c * jax.nn.sigmoid(c)
    s_hi, s_lo = _split_bf16(s)
    w_hi, w_lo = _split_bf16(w_ref[0])
    o_ref[0] = _dot(s_hi, w_hi) + _dot(s_lo, w_hi) + _dot(s_hi, w_lo) + b_ref[0]


def _modulation(cvec, w_ada, b_ada):
    depth, d, n = w_ada.shape
    rows = cvec.shape[0]
    tn = 768
    return pl.pallas_call(
        _mod_body,
        grid=(depth, n // tn),
        in_specs=[pl.BlockSpec((rows, d), lambda l, j: (0, 0)),
                  pl.BlockSpec((1, d, tn), lambda l, j: (l, 0, j)),
                  pl.BlockSpec((1, 1, tn), lambda l, j: (l, 0, j))],
        out_specs=pl.BlockSpec((1, rows, tn), lambda l, j: (l, 0, j)),
        out_shape=jax.ShapeDtypeStruct((depth, rows, n), F32),
        compiler_params=_cparams("parallel", "parallel"),
        name="adaln_mod",
    )(cvec, w_ada, b_ada.reshape(depth, 1, n))


def _inproj_body(*refs, latent):
    if latent:
        (x_ref, mod_ref, g1_ref, w_ref, gains_ref, s64_ref, s32_ref, ca_ref, sa_ref, cb_ref, sb_ref,
         qa_ref, ka_ref, va_ref, qb_ref, kb_ref, vb_ref, u_ref, u_scr) = refs
    else:
        (x_ref, mod_ref, g1_ref, w_ref, gains_ref, s64_ref, s32_ref,
         qa_ref, ka_ref, va_ref, qb_ref, kb_ref, vb_ref, u_ref, u_scr) = refs
    d = D_MODEL
    mod = mod_ref[0]
    gains = gains_ref[...]
    tm = x_ref.shape[1]
    rb = min(INPROJ_ROW_BLOCK, tm)
    lane = lax.broadcasted_iota(jnp.int32, (rb, LANES), 1)
    first_a = (lane % 32) < 16
    first_b = (lane % 16) < 8
    pw = 2 * SSM_CH

    def normed(xb, seg_ref, inv_n, gain):
        ss = _dot((xb * xb).astype(BF16), seg_ref[...])
        return xb * lax.rsqrt(ss * inv_n + EPS) * gain

    for r0 in range(0, tm, rb):
        rs = slice(r0, r0 + rb)
        x = x_ref[0, rs]
        xn = x * lax.rsqrt(jnp.mean(x * x, axis=-1, keepdims=True) + EPS) * g1_ref[...]
        h = xn * (1.0 + mod[:, d:2 * d]) + mod[:, 0:d]
        acc = _dot(h.astype(BF16), w_ref[...])

        def rope_a(y):
            if not latent:
                return y
            sw = jnp.where(first_a, pltpu.roll(y, LANES - 16, 1), pltpu.roll(y, 16, 1))
            return y * ca_ref[rs] + sw * sa_ref[rs]

        def rope_b(y):
            if not latent:
                return y
            sw = jnp.where(first_b, pltpu.roll(y, LANES - 8, 1), pltpu.roll(y, 8, 1))
            return y * cb_ref[rs] + sw * sb_ref[rs]

        off = 0
        for b in range(H_A):
            y = normed(acc[:, off:off + LANES], s64_ref, 1.0 / HD_A, gains[0:1])
            qa_ref[0, rs, b * LANES:(b + 1) * LANES] = rope_a(y).astype(qa_ref.dtype)
            off += LANES
        y = normed(acc[:, off:off + LANES], s64_ref, 1.0 / HD_A, gains[1:2])
        ka_ref[0, rs] = rope_a(y).astype(ka_ref.dtype)
        off += LANES
        va_ref[0, rs] = acc[:, off:off + LANES].astype(va_ref.dtype)
        off += LANES
        for b in range(W_B // LANES):
            y = normed(acc[:, off:off + LANES], s32_ref, 1.0 / DC_B, gains[2:3])
            qb_ref[0, rs, b * LANES:(b + 1) * LANES] = rope_b(y).astype(qb_ref.dtype)
            off += LANES
        for b in range(W_B // LANES):
            y = normed(acc[:, off:off + LANES], s32_ref, 1.0 / DC_B, gains[3:4])
            kb_ref[0, rs, b * LANES:(b + 1) * LANES] = rope_b(y).astype(kb_ref.dtype)
            off += LANES
        vb_ref[0, rs] = acc[:, off:off + W_B].astype(vb_ref.dtype)
        off += W_B
        for blk in range(W_C // LANES):
            u_scr[blk, rs] = acc[:, off + blk * LANES:off + (blk + 1) * LANES]
        crows = slice(r0 // SSM_T, (r0 + rb) // SSM_T)
        for t in range(SSM_T):
            for blk in range(W_C // LANES):
                xt = u_scr[blk, pl.ds(r0 + t, rb // SSM_T, stride=SSM_T), :]
                for pp in range(LANES // pw):
                    u_ref[blk * (LANES // pw) + pp, crows, t * pw:(t + 1) * pw] = (
                        xt[:, pp * pw:(pp + 1) * pw].astype(u_ref.dtype))


def _inproj(x, mod, g1, w_in_p, gains, seg64, seg32, rope, kv_dtype):
    bsz, seq, d = x.shape
    latent = rope is not None
    tm = next(t for t in (1024, 512, 256) if seq % t == 0)
    bm = mod.shape[0]
    mod_idx = (lambda b, i: (b, 0, 0)) if bm > 1 else (lambda b, i: (0, 0, 0))
    const2 = lambda b, i: (0, 0)
    tok = lambda w: pl.BlockSpec((1, tm, w), lambda b, i: (b, i, 0))
    in_specs = [tok(d),
                pl.BlockSpec((1, 1, 6 * d), mod_idx),
                pl.BlockSpec((1, d), const2),
                pl.BlockSpec((d, IN_COLS_P), const2),
                pl.BlockSpec((4, LANES), const2),
                pl.BlockSpec((LANES, LANES), const2),
                pl.BlockSpec((LANES, LANES), const2)]
    args = [x, mod, g1, w_in_p, gains, seg64, seg32]
    if latent:
        in_specs += [pl.BlockSpec((tm, LANES), lambda b, i: (i, 0))] * 4
        args += list(rope)
    widths = (QA_COLS, KV_A * HD_A, KV_A * HD_A, W_B, W_B, W_B)
    dtypes = (BF16, kv_dtype, kv_dtype, BF16, kv_dtype, kv_dtype)
    nt = seq // tm
    rows = tm // SSM_T
    u_spec = pl.BlockSpec((N_PAIR, rows, SSM_ROW), lambda b, i: (0, b * nt + i, 0))
    u_shape = jax.ShapeDtypeStruct((N_PAIR, bsz * seq // SSM_T, SSM_ROW), BF16)
    return pl.pallas_call(
        functools.partial(_inproj_body, latent=latent),
        grid=(bsz, nt),
        in_specs=in_specs,
        out_specs=[tok(w) for w in widths] + [u_spec],
        out_shape=[jax.ShapeDtypeStruct((bsz, seq, w), dt) for w, dt in zip(widths, dtypes)] + [u_shape],
        scratch_shapes=[pltpu.VMEM((W_C // LANES, tm, LANES), F32)],
        compiler_params=_cparams("parallel", "parallel"),
        name="inproj_latent" if latent else "inproj_ctx",
    )(*args)


def _attn_a_body(sink_ref, q_ref, *refs, latent, nblk, nstep):
    o_ref = refs[-1]
    nk = (len(refs) - 1) // 2
    ks = [r[0].astype(BF16) for r in refs[:nk]]
    vs = [r[0].astype(BF16) for r in refs[nk:2 * nk]]
    rows = GQ_A * BLOCK
    rowi = lax.broadcasted_iota(jnp.int32, (rows, 1), 0)
    lane = lax.broadcasted_iota(jnp.int32, (BLOCK, LANES), 1)
    if latent:
        cols = 3 * BLOCK + ks[3].shape[0]
        r = lax.broadcasted_iota(jnp.int32, (rows, cols), 0) & (BLOCK - 1)
        c = lax.broadcasted_iota(jnp.int32, (rows, cols), 1)
        own_k = [ks[1][t * BLOCK:(t + 1) * BLOCK] for t in range(nstep)]
        own_v = [vs[1][t * BLOCK:(t + 1) * BLOCK] for t in range(nstep)]
        band_k = [ks[0]] + own_k + [ks[2]]
        band_v = [vs[0]] + own_v + [vs[2]]
    for t in range(nstep):
        qrows = slice(t * BLOCK, (t + 1) * BLOCK)
        if latent:
            kcat = jnp.concatenate(band_k[t:t + 3] + [ks[3]], axis=0)
            vcat = jnp.concatenate(band_v[t:t + 3] + [vs[3]], axis=0)
            qblk = pl.program_id(1) * nstep + t
            p_off = jnp.where(qblk > 0, 0, 2 * BLOCK)
            n_off = jnp.where(qblk < nblk - 1, 0, 2 * BLOCK)
            prev_ok = (c >= r + p_off) | (c >= BLOCK)
            next_ok = ((c - 2 * BLOCK + n_off) <= r) | (c < 2 * BLOCK) | (c >= 3 * BLOCK)
            valid = prev_ok & next_ok
        else:
            kcat, vcat = ks[0], vs[0]
        heads = []
        for j in range(KV_A):
            q3 = jnp.concatenate([q_ref[0, qrows, (GQ_A * j + g) * LANES:(GQ_A * j + g + 1) * LANES]
                                  for g in range(GQ_A)], axis=0)
            s = _dot_nt(q3, kcat)
            if latent:
                s = jnp.where(valid, s, NEG)
            sink = jnp.where(rowi < BLOCK, sink_ref[GQ_A * j],
                             jnp.where(rowi < 2 * BLOCK, sink_ref[GQ_A * j + 1], sink_ref[GQ_A * j + 2]))
            m = jnp.maximum(jnp.max(s, axis=-1, keepdims=True), sink)
            e = jnp.exp(s - m)
            den = jnp.sum(e, axis=-1, keepdims=True) + jnp.exp(sink - m)
            o = _dot(e.astype(BF16), vcat) / den
            for g in range(GQ_A):
                heads.append((j, o[g * BLOCK:(g + 1) * BLOCK]))
        for blk in range(H_A // 2):
            (j0, o0), (j1, o1) = heads[2 * blk], heads[2 * blk + 1]
            lo = o0 if j0 == 0 else pltpu.roll(o0, HD_A, 1)
            hi = o1 if j1 == 1 else pltpu.roll(o1, HD_A, 1)
            o_ref[0, qrows, blk * LANES:(blk + 1) * LANES] = jnp.where(lane < HD_A, lo, hi).astype(o_ref.dtype)


def _attn_a(qa, ka, va, sink, ctx_kv):
    bsz, seq, _ = qa.shape
    nblk = seq // BLOCK
    latent = ctx_kv is not None
    kvw = KV_A * HD_A
    nb = A_STEP_BLOCKS if nblk % A_STEP_BLOCKS == 0 else nblk
    if latent:
        past = ctx_kv[0].shape[1]
        band = [pl.BlockSpec((1, BLOCK, kvw), lambda b, i: (b, jnp.maximum(nb * i - 1, 0), 0)),
                pl.BlockSpec((1, nb * BLOCK, kvw), lambda b, i: (b, i, 0)),
                pl.BlockSpec((1, BLOCK, kvw), lambda b, i: (b, jnp.minimum(nb * i + nb, nblk - 1), 0)),
                pl.BlockSpec((1, past, kvw), lambda b, i: (b, 0, 0))]
        kv_specs = band + band
        kv_args = [ka, ka, ka, ctx_kv[0], va, va, va, ctx_kv[1]]
    else:
        kv_specs = [pl.BlockSpec((1, seq, kvw), lambda b, i: (b, 0, 0))] * 2
        kv_args = [ka, va]
    return pl.pallas_call(
        functools.partial(_attn_a_body, latent=latent, nblk=nblk, nstep=nb),
        grid=(bsz, nblk // nb),
        in_specs=[pl.BlockSpec(memory_space=pltpu.SMEM),
                  pl.BlockSpec((1, nb * BLOCK, QA_COLS), lambda b, i: (b, i, 0))] + kv_specs,
        out_specs=pl.BlockSpec((1, nb * BLOCK, W_A), lambda b, i: (b, i, 0)),
        out_shape=jax.ShapeDtypeStruct((bsz, seq, W_A), BF16),
        compiler_params=_cparams("parallel", "parallel"),
        name="attn_a_latent" if latent else "attn_a_ctx",
    )(sink, qa, *kv_args)


def _attn_b_body(lam_ref, gain_ref, q_ref, *refs, part_lens, lam_init, kc):
    npart = len(part_lens)
    k_refs, v_refs = refs[:npart], refs[npart:2 * npart]
    o_ref, s_scr, vm_scr = refs[2 * npart:]
    tq = q_ref.shape[1]
    chunks = []
    col = 0
    for p, plen in enumerate(part_lens):
        step = min(kc, plen)
        for start in range(0, plen, step):
            chunks.append((p, start, col, step))
            col += step

    @pl.when(pl.program_id(2) == 0)
    def _():
        off = 0
        for p, plen in enumerate(part_lens):
            v = v_refs[p][0].astype(BF16)
            lane_v = lax.broadcasted_iota(jnp.int32, (plen, LANES), 1)
            for h in range(2):
                own = (lane_v >= h * HD_B) & (lane_v < (h + 1) * HD_B)
                ones = jnp.where(lane_v == (1 - h) * HD_B, 1.0, 0.0).astype(BF16)
                vm_scr[h, off:off + plen, :] = jnp.where(own, v, ones)
            off += plen

    lv = lam_ref[...]
    lam = (jnp.exp(jnp.sum(lv[0:1] * lv[1:2], axis=-1, keepdims=True))
           - jnp.exp(jnp.sum(lv[2:3] * lv[3:4], axis=-1, keepdims=True)) + lam_init)
    q = q_ref[0]
    lane_q = lax.broadcasted_iota(jnp.int32, (tq, LANES), 1)
    total = jnp.zeros((tq, LANES), F32)
    for h in range(2):
        qc = [jnp.where((lane_q >= h * HD_B + c * DC_B) & (lane_q < h * HD_B + (c + 1) * DC_B), q, jnp.zeros_like(q))
              for c in range(2)]
        rows = [slice(c * tq, (c + 1) * tq) for c in range(2)]
        macc = [None, None]
        for p, start, col, step in chunks:
            kch = k_refs[p][0, start:start + step, :].astype(BF16)
            for c in range(2):
                s = _dot_nt(qc[c], kch)
                s_scr[rows[c], col:col + step] = s
                for j in range(step // LANES):
                    t = s[:, j * LANES:(j + 1) * LANES]
                    macc[c] = t if macc[c] is None else jnp.maximum(macc[c], t)
        m = [jnp.max(macc[c], axis=-1, keepdims=True) for c in range(2)]
        acc = [jnp.zeros((tq, LANES), F32) for _ in range(2)]
        for p, start, col, step in chunks:
            vch = vm_scr[h, col:col + step, :]
            for c in range(2):
                e = jnp.exp2(s_scr[rows[c], col:col + step] - m[c]).astype(BF16)
                acc[c] = acc[c] + _dot(e, vch)
        o2 = [acc[c] / jnp.sum(jnp.where(lane_q == (1 - h) * HD_B, acc[c], 0.0), axis=-1, keepdims=True)
              for c in range(2)]
        own = (lane_q >= h * HD_B) & (lane_q < (h + 1) * HD_B)
        total = total + jnp.where(own, o2[0] - lam * o2[1], 0.0)
    sq = total * total
    ss_lo = jnp.sum(jnp.where(lane_q < HD_B, sq, 0.0), axis=-1, keepdims=True)
    ss_hi = jnp.sum(jnp.where(lane_q >= HD_B, sq, 0.0), axis=-1, keepdims=True)
    rinv = jnp.where(lane_q < HD_B, lax.rsqrt(ss_lo * (1.0 / HD_B) + EPS), lax.rsqrt(ss_hi * (1.0 / HD_B) + EPS))
    o_ref[0] = (total * rinv * gain_ref[...] * (1.0 - lam_init)).astype(o_ref.dtype)


def _attn_b(qb, k_parts, v_parts, lam_b, gain, lam_init):
    bsz, seq, _ = qb.shape
    tq = next(t for t in (1024, 512, 256) if seq % t == 0)
    part_lens = tuple(k.shape[1] for k in k_parts)
    lk = sum(part_lens)
    kv_specs = [pl.BlockSpec((1, n, LANES), lambda b, hp, i: (b, 0, hp)) for n in part_lens]
    return pl.pallas_call(
        functools.partial(_attn_b_body, part_lens=part_lens, lam_init=lam_init, kc=512),
        grid=(bsz, W_B // LANES, seq // tq),
        in_specs=[pl.BlockSpec((4, DC_B), lambda b, hp, i: (0, 0)),
                  pl.BlockSpec((1, LANES), lambda b, hp, i: (0, 0)),
                  pl.BlockSpec((1, tq, LANES), lambda b, hp, i: (b, i, hp))] + kv_specs + kv_specs,
        out_specs=pl.BlockSpec((1, tq, LANES), lambda b, hp, i: (b, i, hp)),
        out_shape=jax.ShapeDtypeStruct((bsz, seq, W_B), BF16),
        scratch_shapes=[pltpu.VMEM((2 * tq, lk), F32), pltpu.VMEM((2, lk, LANES), BF16)],
        compiler_params=_cparams("parallel", "parallel", "arbitrary"),
        name="attn_b_latent" if len(k_parts) > 1 else "attn_b_ctx",
    )(lam_b, gain, qb, *k_parts, *v_parts)


def _ssm_body(u_ref, m_ref, g_ref, cc_ref, a_ref, h0_ref, y_ref, fin_ref, s_scr, h_scr, *, nb, nc):
    u = u_ref[0]
    col = lambda k: slice(k * LANES, (k + 1) * LANES)
    s = _dot(u, g_ref[0])
    for k in range(4):
        s_scr[k] = s[:, col(k)]
    a = a_ref[0]
    afr, afi, abr, abi = (jnp.broadcast_to(a[k:k + 1], (nb, LANES)) for k in range(4))
    h0 = h0_ref[0]

    def step(c, carry):
        fr, fi, br, bi = carry
        rf = pl.ds(c, nb, stride=nc)
        rb = pl.ds(nc - 1 - c, nb, stride=nc)
        h_scr[0, rf, :] = fr
        h_scr[1, rf, :] = fi
        h_scr[2, rb, :] = br
        h_scr[3, rb, :] = bi
        nfr = afr * fr - afi * fi + s_scr[0, rf, :]
        nfi = afr * fi + afi * fr + s_scr[1, rf, :]
        nbr = abr * br - abi * bi + s_scr[2, rb, :]
        nbi = abr * bi + abi * br + s_scr[3, rb, :]
        return nfr, nfi, nbr, nbi

    fin = lax.fori_loop(0, nc, step, tuple(h0[:, col(k)] for k in range(4)), unroll=16)
    for k in range(4):
        fin_ref[0, :, col(k)] = fin[k]
    hin = jnp.concatenate([h_scr[k] for k in range(4)], axis=1).astype(BF16)
    y = _dot(u, m_ref[0]) + _dot(hin, cc_ref[0])
    y_ref[0] = y.astype(y_ref.dtype)


def _ssm(u_rows, mats, h0, nb):
    npair, rows, w = u_rows.shape
    nc = rows // nb
    mat_spec = pl.BlockSpec((1, w, w), lambda p: (p, 0, 0))
    return pl.pallas_call(
        functools.partial(_ssm_body, nb=nb, nc=nc),
        grid=(npair,),
        in_specs=[pl.BlockSpec((1, rows, w), lambda p: (p, 0, 0)), mat_spec, mat_spec, mat_spec,
                  pl.BlockSpec((1, 4, LANES), lambda p: (p, 0, 0)),
                  pl.BlockSpec((1, nb, w), lambda p: (p, 0, 0))],
        out_specs=[pl.BlockSpec((1, rows, w), lambda p: (p, 0, 0)),
                   pl.BlockSpec((1, nb, w), lambda p: (p, 0, 0))],
        out_shape=[jax.ShapeDtypeStruct((npair, rows, w), BF16),
                   jax.ShapeDtypeStruct((npair, nb, w), F32)],
        scratch_shapes=[pltpu.VMEM((4, rows, LANES), F32), pltpu.VMEM((4, rows, LANES), F32)],
        compiler_params=_cparams("parallel"),
        name="ssm_scan",
    )(u_rows, mats["m"], mats["g"], mats["cc"], mats["a16"], h0)


def _ssm_matrices(lp):
    t = SSM_T
    ks = jnp.arange(t + 1, dtype=F32)
    dirs = []
    for d in range(2):
        lam = lax.complex(lp["ssm_lam_re"][d].astype(F32), lp["ssm_lam_im"][d].astype(F32))
        dt = jnp.exp(lp["ssm_log_dt"][d].astype(F32))[:, None]
        a_bar = jnp.exp(lam * dt)
        b_bar = ((a_bar - 1.0) / lam)[..., None] * lax.complex(lp["ssm_b_re"][d].astype(F32),
                                                               lp["ssm_b_im"][d].astype(F32))
        c_mat = lax.complex(lp["ssm_c_re"][d].astype(F32), lp["ssm_c_im"][d].astype(F32))
        pw = jnp.exp((lam * dt)[None] * ks[:, None, None].astype(jnp.complex64))
        kern = jnp.real(jnp.einsum("gop,kgp,gpi->gkoi", c_mat, pw[:t], b_bar))
        dirs.append((pw, b_bar, c_mat, kern))
    (pw_f, bb_f, cm_f, k_f), (pw_b, bb_b, cm_b, k_b) = dirs
    eye2 = jnp.eye(2, dtype=F32)
    ch, pw2 = SSM_CH, 2 * SSM_CH
    hi = lax.Precision.HIGHEST

    def pair_bd(x):
        r, c = x.shape[1:]
        return jnp.einsum("pgrc,gh->pgrhc", x.reshape(N_PAIR, 2, r, c), eye2.astype(x.dtype)).reshape(N_PAIR, 2 * r, 2 * c)

    def pair_vec(x):
        return x.reshape(x.shape[0], N_PAIR, 2 * P_C).transpose(1, 0, 2)

    def lag_blocks(kern):
        x = kern.transpose(0, 1, 3, 2).reshape(N_PAIR, 2, t, ch, ch)
        return jnp.einsum("pglic,gh->plgihc", x, eye2).reshape(N_PAIR, t, pw2, pw2)
    kp_f, kp_b = lag_blocks(k_f), lag_blocks(k_b)
    d_blk = pair_bd(lp["ssm_d"].astype(F32)[:, :, None] * jnp.eye(ch, dtype=F32)[None])
    center = (kp_f[:, 0] + kp_b[:, 0] + d_blk)[:, None]
    band = jnp.concatenate([kp_b[:, :0:-1], center, kp_f[:, 1:]], axis=1)
    band = band.transpose(0, 2, 1, 3).reshape(N_PAIR, pw2, (2 * t - 1) * pw2)
    m_p = jnp.concatenate([band[:, :, (t - 1 - s) * pw2:(t - 1 - s) * pw2 + SSM_ROW] for s in range(t)], axis=1)

    def inject(pw_sel, b_bar):
        x1 = jnp.repeat(pair_vec(pw_sel), pw2, axis=1)
        x2 = jnp.tile(pair_bd(b_bar.transpose(0, 2, 1)), (1, t, 1))
        return x1 * x2
    g_f = inject(pw_f[t - 1 - jnp.arange(t)], bb_f)
    g_b = inject(pw_b[jnp.arange(t)], bb_b)
    g_p = jnp.concatenate([jnp.real(g_f), jnp.imag(g_f), jnp.real(g_b), jnp.imag(g_b)], axis=2)

    lane = jnp.arange(SSM_ROW)
    exp_t = (jnp.arange(t)[:, None] == lane[None, :] // pw2).astype(F32)
    exp_c = (jnp.arange(pw2)[:, None] == lane[None, :] % pw2).astype(F32)

    def widen(x, e):
        f = lambda v: jnp.einsum("pqk,kx->pqx", v, e, precision=hi)
        return lax.complex(f(jnp.real(x)), f(jnp.imag(x)))

    def readout(pw_sel, c_mat):
        y1 = widen(pair_vec(pw_sel).transpose(0, 2, 1), exp_t)
        y2 = widen(pair_bd(c_mat.transpose(0, 2, 1)), exp_c)
        return y1 * y2
    z_f = readout(pw_f[1 + jnp.arange(t)], cm_f)
    z_b = readout(pw_b[t - jnp.arange(t)], cm_b)
    cc_p = jnp.concatenate([jnp.real(z_f), -jnp.imag(z_f), jnp.real(z_b), -jnp.imag(z_b)], axis=1)
    a16 = jnp.stack([jnp.real(pw_f[t]), jnp.imag(pw_f[t]), jnp.real(pw_b[t]), jnp.imag(pw_b[t])], axis=0)
    a16 = a16.reshape(4, N_PAIR, 2 * P_C).transpose(1, 0, 2)
    return dict(m=m_p.astype(BF16), g=g_p.astype(BF16), cc=cc_p.astype(BF16), a16=a16)


def _ssm_state_rows(s_re, s_im):
    bsz = s_re.shape[0]
    parts = [s_re[:, 0], s_im[:, 0], s_re[:, 1], s_im[:, 1]]
    st = jnp.stack([p.reshape(bsz, N_PAIR, 2 * P_C) for p in parts], axis=2)
    return st.transpose(1, 0, 2, 3).reshape(N_PAIR, bsz, 8 * P_C).astype(F32)


def _ssm_state_unrows(fin):
    npair, bsz, _ = fin.shape
    st = fin.reshape(npair, bsz, 4, 2, P_C).transpose(1, 2, 0, 3, 4).reshape(bsz, 4, G_C, P_C)
    return jnp.stack([st[:, 0], st[:, 2]], axis=1), jnp.stack([st[:, 1], st[:, 3]], axis=1)


def _route(scores, bias):
    tm = scores.shape[1]
    biased = scores + bias
    iota8 = lax.broadcasted_iota(jnp.int32, (PER_GROUP, tm), 0)
    grp = [biased[PER_GROUP * g:PER_GROUP * (g + 1)] for g in range(N_EXP_GROUPS)]
    gscore = []
    for v in grp:
        m1 = jnp.max(v, axis=0, keepdims=True)
        first = jnp.min(jnp.where(v == m1, iota8, PER_GROUP), axis=0, keepdims=True)
        m2 = jnp.max(jnp.where(iota8 == first, -jnp.inf, v), axis=0, keepdims=True)
        gscore.append(m1 + m2)
    masked = []
    for g in range(N_EXP_GROUPS):
        rank = jnp.zeros((1, tm), jnp.int32)
        for o in range(N_EXP_GROUPS):
            if o == g:
                continue
            ahead = (gscore[o] >= gscore[g]) if o < g else (gscore[o] > gscore[g])
            rank = rank + jnp.where(ahead, 1, 0)
        masked.append(jnp.where(rank < TOPK_GROUPS, grp[g], -jnp.inf))
    chosen = [None] * N_EXP_GROUPS
    for _ in range(TOP_K):
        best = masked[0]
        for v in masked[1:]:
            best = jnp.maximum(best, v)
        best = jnp.max(best, axis=0, keepdims=True)
        first = jnp.full((1, tm), N_EXPERTS, jnp.int32)
        for g, v in enumerate(masked):
            cand = jnp.min(jnp.where(v == best, iota8 + PER_GROUP * g, N_EXPERTS), axis=0, keepdims=True)
            first = jnp.minimum(first, cand)
        for g in range(N_EXP_GROUPS):
            hit = (iota8 + PER_GROUP * g) == first
            chosen[g] = hit if chosen[g] is None else (chosen[g] | hit)
            masked[g] = jnp.where(hit, -jnp.inf, masked[g])
    w = [jnp.where(chosen[g], scores[PER_GROUP * g:PER_GROUP * (g + 1)], 0.0) for g in range(N_EXP_GROUPS)]
    wsum = w[0]
    for v in w[1:]:
        wsum = wsum + v
    wsum = jnp.sum(wsum, axis=0, keepdims=True)
    return jnp.concatenate([v / wsum * ROUTED_SCALE for v in w], axis=0)


def _post_body(x_ref, oa_ref, ob_ref, y_ref, mod_ref, wglu_ref, wout_ref, g2_ref, wrh_ref, wrl_ref, br_ref,
               x1_ref, h2_ref, gate_ref, y_scr):
    d = D_MODEL
    tm = x_ref.shape[1]
    rb = min(POST_ROW_BLOCK, tm)
    pw = 2 * SSM_CH
    mod = mod_ref[0]
    for r0 in range(0, tm, rb):
        rs = slice(r0, r0 + rb)
        crows = slice(r0 // SSM_T, (r0 + rb) // SSM_T)
        for t in range(SSM_T):
            for blk in range(W_C // LANES):
                piece = jnp.concatenate([y_ref[blk * (LANES // pw) + pp, crows, t * pw:(t + 1) * pw].astype(F32)
                                         for pp in range(LANES // pw)], axis=1)
                y_scr[blk, pl.ds(r0 + t, rb // SSM_T, stride=SSM_T), :] = piece
        g = jax.nn.gelu(jnp.concatenate([y_scr[blk, rs] for blk in range(W_C // LANES)], axis=1))
        oc = g * jax.nn.sigmoid(_dot(g.astype(BF16), wglu_ref[...]))
        mix = (_dot(oa_ref[0, rs], wout_ref[0:W_A]) + _dot(ob_ref[0, rs], wout_ref[W_A:W_A + W_B])
               + _dot(oc.astype(BF16), wout_ref[W_A + W_B:]))
        x1 = x_ref[0, rs] + mod[:, 2 * d:3 * d] * mix
        x1_ref[0, rs] = x1
        xn = x1 * lax.rsqrt(jnp.mean(x1 * x1, axis=-1, keepdims=True) + EPS) * g2_ref[...]
        h2 = xn * (1.0 + mod[:, 4 * d:5 * d]) + mod[:, 3 * d:4 * d]
        h_hi, h_lo = _split_bf16(h2)
        h2_ref[0, rs] = h_hi
        logits = (_dot_nt(wrh_ref[...], h_hi) + _dot_nt(wrh_ref[...], h_lo)
                  + _dot_nt(wrl_ref[...], h_hi))
        gate_ref[0, rs] = _route(jax.nn.sigmoid(logits), br_ref[...]).T


def _post_mix(x, oa, ob, y, mod, w_glu, w_out, g2, wr_hi, wr_lo, b_r):
    bsz, seq, d = x.shape
    tm = next(t for t in (1024, 512, 256) if seq % t == 0)
    bm = mod.shape[0]
    mod_idx = (lambda b, i: (b, 0, 0)) if bm > 1 else (lambda b, i: (0, 0, 0))
    const2 = lambda b, i: (0, 0)
    tok = lambda w: pl.BlockSpec((1, tm, w), lambda b, i: (b, i, 0))
    nt = seq // tm
    return pl.pallas_call(
        _post_body,
        grid=(bsz, nt),
        in_specs=[tok(d), tok(W_A), tok(W_B),
                  pl.BlockSpec((N_PAIR, tm // SSM_T, SSM_ROW), lambda b, i: (0, b * nt + i, 0)),
                  pl.BlockSpec((1, 1, 6 * d), mod_idx),
                  pl.BlockSpec((W_C, W_C), const2),
                  pl.BlockSpec((d, d), const2),
                  pl.BlockSpec((1, d), const2),
                  pl.BlockSpec((N_EXPERTS, d), const2),
                  pl.BlockSpec((N_EXPERTS, d), const2),
                  pl.BlockSpec((N_EXPERTS, 1), const2)],
        out_specs=[tok(d), tok(d), tok(N_EXPERTS)],
        out_shape=[jax.ShapeDtypeStruct((bsz, seq, d), F32),
                   jax.ShapeDtypeStruct((bsz, seq, d), BF16),
                   jax.ShapeDtypeStruct((bsz, seq, N_EXPERTS), F32)],
        scratch_shapes=[pltpu.VMEM((W_C // LANES, tm, LANES), F32)],
        compiler_params=_cparams("parallel", "parallel"),
        name="post_mix",
    )(x, oa, ob, y, mod, w_glu, w_out, g2, wr_hi, wr_lo, b_r)


def _moe_body(x1_ref, h_ref, gate_ref, g2_ref, w1_ref, w3_ref, w2_ref, c1_ref, c32_ref, ex_ref,
              s1_ref, s3_ref, s2_ref, o_ref, acc_ref, h8_ref, hs_ref):
    j = pl.program_id(1)

    @pl.when(j == 0)
    def _():
        h = h_ref[...]
        a = _dot(h, s1_ref[...])
        acc_ref[...] = _dot((a * jax.nn.sigmoid(a) * _dot(h, s3_ref[...])).astype(BF16), s2_ref[...])
        hf = h.astype(F32)
        sc = jnp.maximum(jnp.max(jnp.abs(hf), axis=-1, keepdims=True), F8_TINY) * (1.0 / F8_RANGE)
        hs_ref[...] = sc
        h8_ref[...] = (hf * (1.0 / sc)).astype(F8)

    ne = w1_ref.shape[0]
    h8 = h8_ref[...]
    hs = hs_ref[...]
    a = _dot(h8, jnp.concatenate([w1_ref[e] for e in range(ne)], axis=1)) * c1_ref[...] * hs
    b = _dot(h8, jnp.concatenate([w3_ref[e] for e in range(ne)], axis=1))
    gexp = _dot(jnp.concatenate(_split_bf16(gate_ref[...]), axis=1), ex_ref[...])
    hid = a * jax.nn.sigmoid(a) * b * gexp * c32_ref[...]
    sc = jnp.maximum(jnp.max(jnp.abs(hid), axis=-1, keepdims=True), F8_TINY) * (1.0 / F8_RANGE)
    w2 = w2_ref[...].reshape(ne * F_EXP, w2_ref.shape[-1])
    acc_ref[...] += _dot((hid * (1.0 / sc)).astype(F8), w2) * (sc * hs)

    @pl.when(j == pl.num_programs(1) - 1)
    def _():
        o_ref[...] = x1_ref[...] + g2_ref[0] * acc_ref[...]


def _moe(x1, h2, gates, mod, seq, ew, layer, expand, ws1, ws3, ws2):
    tokens, d = x1.shape
    bm = mod.shape[0]
    span = seq if bm > 1 else tokens
    tm = next(t for t in (1024, 512, 256) if span % t == 0)
    per_b = seq // tm if bm > 1 else 1
    mod_idx = (lambda i, j: (i // per_b, 0, 5)) if bm > 1 else (lambda i, j: (0, 0, 5))
    ne = 8
    fc = ne * F_EXP
    hidden = ew["w2"].shape[1] * F_EXP
    const2 = lambda i, j: (0, 0)
    chunk_row = pl.BlockSpec((None, 1, fc), lambda i, j: (layer, 0, j))
    return pl.pallas_call(
        _moe_body,
        grid=(tokens // tm, hidden // fc),
        in_specs=[pl.BlockSpec((tm, d), lambda i, j: (i, 0)),
                  pl.BlockSpec((tm, d), lambda i, j: (i, 0)),
                  pl.BlockSpec((tm, N_EXPERTS), lambda i, j: (i, 0)),
                  pl.BlockSpec((1, 1, d), mod_idx),
                  pl.BlockSpec((None, ne, d, F_EXP), lambda i, j: (layer, j, 0, 0)),
                  pl.BlockSpec((None, ne, d, F_EXP), lambda i, j: (layer, j, 0, 0)),
                  pl.BlockSpec((None, ne, F_EXP, d), lambda i, j: (layer, j, 0, 0)),
                  chunk_row, chunk_row,
                  pl.BlockSpec((2 * N_EXPERTS, fc), lambda i, j: (0, j)),
                  pl.BlockSpec((d, F_SHARED), const2),
                  pl.BlockSpec((d, F_SHARED), const2),
                  pl.BlockSpec((F_SHARED, d), const2)],
        out_specs=pl.BlockSpec((tm, d), lambda i, j: (i, 0)),
        out_shape=jax.ShapeDtypeStruct((tokens, d), F32),
        scratch_shapes=[pltpu.VMEM((tm, d), F32), pltpu.VMEM((tm, d), F8), pltpu.VMEM((tm, 1), F32)],
        compiler_params=_cparams("parallel", "arbitrary"),
        name="moe",
    )(x1, h2, gates, mod, ew["w1"], ew["w3"], ew["w2"], ew["c1"], ew["c32"], expand, ws1, ws3, ws2)


def _expert_fp8_body(w_ref, q_ref, c_ref):
    for e in range(w_ref.shape[1]):
        w = w_ref[0, e]
        top = jnp.max(jnp.max(jnp.abs(w), axis=0, keepdims=True), axis=1, keepdims=True)
        sc = jnp.maximum(top, F8_TINY) * (1.0 / F8_RANGE)
        q_ref[0, e] = (w * (1.0 / sc)).astype(F8)
        c_ref[0, :, e * F_EXP:(e + 1) * F_EXP] = jnp.broadcast_to(sc, (1, F_EXP))


def _expert_fp8(w):
    depth, ne, r, c = w.shape
    blk = 8
    return pl.pallas_call(
        _expert_fp8_body,
        grid=(depth, ne // blk),
        in_specs=[pl.BlockSpec((1, blk, r, c), lambda l, i: (l, i, 0, 0))],
        out_specs=[pl.BlockSpec((1, blk, r, c), lambda l, i: (l, i, 0, 0)),
                   pl.BlockSpec((1, 1, blk * F_EXP), lambda l, i: (l, 0, i))],
        out_shape=[jax.ShapeDtypeStruct((depth, ne, r, c), F8), jax.ShapeDtypeStruct((depth, 1, ne * F_EXP), F32)],
        compiler_params=_cparams("parallel", "parallel"),
        name="expert_fp8",
    )(w)


def _prep_experts(p):
    w1, c1 = _expert_fp8(p["w_e1"].astype(F32))
    w3, c3 = _expert_fp8(p["w_e3"].astype(F32))
    w2, c2 = _expert_fp8(p["w_e2"].astype(F32))
    return dict(w1=w1, w3=w3, w2=w2, c1=c1, c32=c3 * c2)


def _rope_tables(seq):
    pos = jnp.arange(seq)
    row = (pos // GRID_W).astype(F32)[:, None]
    colp = (pos % GRID_W).astype(F32)[:, None]
    lane = jnp.arange(LANES)

    def table(width):
        half, quarter = width // 2, width // 4
        i = lane % width
        freq = ROPE_BASE ** (-(2.0 * (i % quarter).astype(F32)) / half)
        ang = jnp.where((i // half) == 0, row, colp) * freq[None, :]
        sign = jnp.where((i % half) < quarter, -1.0, 1.0)
        return jnp.cos(ang), jnp.sin(ang) * sign[None, :]

    ca, sa = table(HD_A)
    cb, sb = table(DC_B)
    return ca, sa, cb, sb


def _prep_layer(p):
    d = D_MODEL
    w_in = p["w_in"]
    src = jnp.arange(W_A)
    dst = (src // HD_A) * LANES + (src // HD_A // GQ_A) * HD_A + src % HD_A
    place = (dst[:, None] == jnp.arange(QA_COLS)[None, :]).astype(F32)
    qa_pad = jnp.dot(w_in[:, :W_A].astype(F32), place, precision=lax.Precision.HIGHEST)
    w_in_p = jnp.concatenate([qa_pad, w_in[:, W_A:].astype(F32)], axis=1).astype(BF16)
    gains = jnp.stack([jnp.tile(p["q_norm_a"], LANES // HD_A) * (HD_A ** -0.5),
                       jnp.tile(p["k_norm_a"], LANES // HD_A),
                       jnp.tile(p["q_norm_b"], LANES // DC_B) * (DC_B ** -0.5 * LOG2E),
                       jnp.tile(p["k_norm_b"], LANES // DC_B)], axis=0).astype(F32)
    lp = {k: p[k] for k in ("ssm_lam_re", "ssm_lam_im", "ssm_log_dt", "ssm_b_re", "ssm_b_im",
                            "ssm_c_re", "ssm_c_im", "ssm_d")}
    wr_hi, wr_lo = _split_bf16(p["w_router"].T.astype(F32))
    return dict(
        w_in_p=w_in_p, gains=gains,
        g1=p["norm1_g"].reshape(1, d).astype(F32), g2=p["norm2_g"].reshape(1, d).astype(F32),
        sink=p["sink_a"].astype(F32), lam_b=p["lam_b"].astype(F32),
        subln=jnp.tile(p["subln_b"], LANES // HD_B).reshape(1, LANES).astype(F32),
        ssm=_ssm_matrices(lp),
        w_glu=p["w_glu"].astype(BF16), w_out=p["w_out"].astype(BF16),
        wr_hi=wr_hi, wr_lo=wr_lo, b_r=p["b_router"].reshape(N_EXPERTS, 1).astype(F32),
        ws1=p["w_s1"].astype(BF16), ws3=p["w_s3"].astype(BF16), ws2=p["w_s2"].astype(BF16),
    )


def _trunk_layer(x, mod, lw, consts, ctx):
    bsz, seq, d = x.shape
    latent = ctx is not None
    rope = consts["rope"] if latent else None
    kv_dtype = BF16 if latent else F32
    grp = 1 if latent else next(g for g in (4, 2, 1) if bsz % g == 0)
    tok = lambda a: a.reshape(bsz // grp, grp * seq, a.shape[-1])
    per_seq = lambda a: a.reshape(bsz, seq, a.shape[-1])
    qa, ka, va, qb, kb, vb, u = _inproj(tok(x), mod, lw["g1"], lw["w_in_p"], lw["gains"],
                                        consts["seg64"], consts["seg32"], rope, kv_dtype)
    qa, ka, va, qb, kb, vb = (per_seq(a) for a in (qa, ka, va, qb, kb, vb))
    if latent:
        oa = _attn_a(qa, ka, va, lw["sink"], (ctx["ak"], ctx["av"]))
        ob = _attn_b(qb, [kb, ctx["bk"]], [vb, ctx["bv"]], lw["lam_b"], lw["subln"], lw["lam_init"])
        h0 = ctx["h0"]
    else:
        oa = _attn_a(qa, ka, va, lw["sink"], None)
        ob = _attn_b(qb, [kb], [vb], lw["lam_b"], lw["subln"], lw["lam_init"])
        h0 = jnp.zeros((N_PAIR, bsz, 8 * P_C), F32)
    y_rows, fin = _ssm(u, lw["ssm"], h0, bsz)
    x1, h2, gates = _post_mix(tok(x), tok(oa), tok(ob), y_rows, mod, lw["w_glu"], lw["w_out"], lw["g2"],
                                lw["wr_hi"], lw["wr_lo"], lw["b_r"])
    out = _moe(x1.reshape(bsz * seq, d), h2.reshape(bsz * seq, d), gates.reshape(bsz * seq, N_EXPERTS), mod, seq,
               consts["experts"], lw["layer"], consts["expand"], lw["ws1"], lw["ws3"], lw["ws2"])
    return out.reshape(bsz, seq, d), (ka, va, kb, vb, fin)


def kernel(x_prompt, x_sample, cache_a_k, cache_a_v, cache_b_k, cache_b_v, state_ssm_re, state_ssm_im, c, c_ctx, norm1_g, norm2_g, w_ada, b_ada, w_in, q_norm_a, k_norm_a, sink_a, q_norm_b, k_norm_b, lam_b, subln_b, ssm_lam_re, ssm_lam_im, ssm_log_dt, ssm_b_re, ssm_b_im, ssm_c_re, ssm_c_im, ssm_d, w_glu, w_out, w_router, b_router, w_e1, w_e3, w_e2, w_s1, w_s3, w_s2):
    p = dict(norm1_g=norm1_g, norm2_g=norm2_g, w_in=w_in, q_norm_a=q_norm_a, k_norm_a=k_norm_a, sink_a=sink_a,
             q_norm_b=q_norm_b, k_norm_b=k_norm_b, lam_b=lam_b, subln_b=subln_b,
             ssm_lam_re=ssm_lam_re, ssm_lam_im=ssm_lam_im, ssm_log_dt=ssm_log_dt, ssm_b_re=ssm_b_re,
             ssm_b_im=ssm_b_im, ssm_c_re=ssm_c_re, ssm_c_im=ssm_c_im, ssm_d=ssm_d, w_glu=w_glu, w_out=w_out,
             w_router=w_router, b_router=b_router, w_e1=w_e1, w_e3=w_e3, w_e2=w_e2,
             w_s1=w_s1, w_s3=w_s3, w_s2=w_s2)
    depth = w_in.shape[0]
    bsz, seq, d = x_prompt.shape
    dbsz, dseq, _ = x_sample.shape
    past = cache_a_k.shape[3]

    mod_rows = 16
    cvec = jnp.concatenate([c.astype(F32), c_ctx.astype(F32)[None],
                            jnp.zeros((mod_rows - dbsz - 1, d), F32)], axis=0)
    mods = _modulation(cvec, w_ada.astype(F32), b_ada.astype(F32))

    lane = jnp.arange(LANES)
    hidden = N_EXPERTS * F_EXP
    consts = dict(
        rope=_rope_tables(dseq),
        seg64=(lane[:, None] // HD_A == lane[None, :] // HD_A).astype(BF16),
        seg32=(lane[:, None] // DC_B == lane[None, :] // DC_B).astype(BF16),
        expand=(jnp.arange(2 * N_EXPERTS)[:, None] % N_EXPERTS == jnp.arange(hidden)[None, :] // F_EXP).astype(BF16),
    )

    xp, xs = x_prompt, x_sample
    ak, av, bk, bv, sre, sim = [], [], [], [], [], []
    prepared = jax.vmap(_prep_layer)(p)
    consts["experts"] = _prep_experts(p)
    for l in range(depth):
        lw = jax.tree.map(lambda v: v[l], prepared)
        lw["layer"] = l
        lw["lam_init"] = 0.8 - 0.6 * math.exp(-0.3 * l)
        mod_lat = mods[l, :dbsz][:, None, :]
        mod_ctx = mods[l, dbsz:dbsz + 1][:, None, :]
        xp, (k_a, v_a, k_b, v_b, fin) = _trunk_layer(xp, mod_ctx, lw, consts, None)
        ak.append(k_a.reshape(bsz, seq, KV_A, HD_A).transpose(0, 2, 1, 3))
        av.append(v_a.reshape(bsz, seq, KV_A, HD_A).transpose(0, 2, 1, 3))
        bk.append(k_b.reshape(bsz, seq, H_B, 2, DC_B).transpose(0, 2, 3, 1, 4))
        bv.append(v_b.reshape(bsz, seq, H_B, HD_B).transpose(0, 2, 1, 3))
        f_re, f_im = _ssm_state_unrows(fin)
        sre.append(f_re)
        sim.append(f_im)
        ctx = dict(
            ak=cache_a_k[:, l].transpose(0, 2, 1, 3).reshape(dbsz, past, KV_A * HD_A).astype(BF16),
            av=cache_a_v[:, l].transpose(0, 2, 1, 3).reshape(dbsz, past, KV_A * HD_A).astype(BF16),
            bk=cache_b_k[:, l].transpose(0, 3, 1, 2, 4).reshape(dbsz, past, W_B).astype(BF16),
            bv=cache_b_v[:, l].transpose(0, 2, 1, 3).reshape(dbsz, past, W_B).astype(BF16),
            h0=_ssm_state_rows(state_ssm_re[:, l], state_ssm_im[:, l]),
        )
        xs, _ = _trunk_layer(xs, mod_lat, lw, consts, ctx)
    return (xp, xs, jnp.stack(ak, axis=1), jnp.stack(av, axis=1), jnp.stack(bk, axis=1),
            jnp.stack(bv, axis=1), jnp.stack(sre, axis=1), jnp.stack(sim, axis=1))
```
